```python
import jax, jax.numpy as jnp
from jax import lax
import numpy as np

D_MODEL = 4096
BATCH = 2
SEQ = 4096
DEPTH = 1

HEAD_DIM = 128
ROPE_DIM = HEAD_DIM // 4
ROPE_THETA = 500000.0
RMS_EPS = 1e-6
NEG_INF = -1e30

NSA_HEADS = (D_MODEL // 2) // HEAD_DIM
NSA_GROUPS = 4
NSA_HPG = NSA_HEADS // NSA_GROUPS
CMP_LEN = 32
CMP_STRIDE = 16
CMP_HIDDEN = 256
SEL_BLOCK = 64
SEL_TOP = 16
WINDOW = 512
Q_BLOCK = 128

RWKV_HEAD = 64
RWKV_DIM = D_MODEL // 2
RWKV_HEADS = RWKV_DIM // RWKV_HEAD
DECAY_LORA = 96
AAA_LORA = 96
GATE_LORA = 256
GN_EPS = 64e-5
RWKV_COLS = 3 * RWKV_DIM + DECAY_LORA + AAA_LORA + GATE_LORA

NSA_Q_COLS = NSA_HEADS * HEAD_DIM
NSA_KV_COLS = 6 * NSA_GROUPS * HEAD_DIM
NSA_GATE_COLS = 3 * NSA_HEADS
MERGE_COLS = 2 * D_MODEL
IN_COLS = NSA_Q_COLS + NSA_KV_COLS + NSA_GATE_COLS + RWKV_COLS + MERGE_COLS

XATTN_HEADS = 4
MEM_TOKENS = 256

PEER_HEADS = 8
PEER_KEYS = 128
PEER_EXPERTS = PEER_KEYS * PEER_KEYS
PEER_QDIM = 256
PEER_TOPK = 16
TOKEN_BLOCK = 128

kernel_name = 'hybrid_nsa_rwkv7_peer_block'


def rmsnorm(x, g):
    xf = x.astype(jnp.float32)
    y = xf * lax.rsqrt(jnp.mean(xf * xf, axis=-1, keepdims=True) + RMS_EPS)
    return (y * g.astype(jnp.float32)).astype(x.dtype)


def rope(x, pos):
    half = ROPE_DIM // 2
    inv = ROPE_THETA ** (-jnp.arange(half, dtype=jnp.float32) / half)
    ang = pos.astype(jnp.float32)[..., None, None] * inv
    cos, sin = jnp.cos(ang), jnp.sin(ang)
    xf = x.astype(jnp.float32)
    x1, x2, xp = xf[..., :half], xf[..., half:ROPE_DIM], xf[..., ROPE_DIM:]
    out = jnp.concatenate([x1 * cos - x2 * sin, x2 * cos + x1 * sin, xp], axis=-1)
    return out.astype(x.dtype)


def masked_softmax(s, valid):
    return jax.nn.softmax(jnp.where(valid, s, NEG_INF), axis=-1)


def nsa_mixer(q, kv, gate_logits, pos, cmp_pe, cmp_w1, cmp_w2):
    B, S = q.shape[:2]
    G, n, dh = NSA_GROUPS, NSA_HPG, HEAD_DIM
    scale = dh ** -0.5
    kc_in, vc_in, ks, vs, kw, vw = [t.reshape(B, S, G, dh) for t in jnp.split(kv, 6, axis=-1)]
    ks, kw = rope(ks, pos), rope(kw, pos)
    qg = q.reshape(B, S, G, n, dh)
    t = jnp.arange(S)

    n_cmp = (S - CMP_LEN) // CMP_STRIDE + 1
    tok = jnp.arange(n_cmp)[:, None] * CMP_STRIDE + jnp.arange(CMP_LEN)[None, :]
    cend = tok[:, -1]

    def compress(z, pe_i, w1_i, w2_i):
        blocks = z[:, tok] + pe_i[None, None, :, None, :]
        blocks = blocks.transpose(0, 1, 3, 2, 4).reshape(B, n_cmp, G, CMP_LEN * dh)
        return jax.nn.gelu(blocks @ w1_i) @ w2_i

    kc = rope(compress(kc_in, cmp_pe[0], cmp_w1[0], cmp_w2[0]), pos[:, cend])
    vc = compress(vc_in, cmp_pe[1], cmp_w1[1], cmp_w2[1])
    s_c = jnp.einsum('bsgnd,bcgd->bgnsc', qg, kc).astype(jnp.float32) * scale
    valid_c = cend[None, :] <= t[:, None]
    p_c = masked_softmax(s_c, valid_c) * jnp.any(valid_c, axis=-1)[:, None].astype(jnp.float32)
    o_c = jnp.einsum('bgnsc,bcgd->bsgnd', p_c.astype(vc.dtype), vc)

    n_sel = S // SEL_BLOCK
    r_sel, r_cmp = SEL_BLOCK // CMP_STRIDE, CMP_LEN // CMP_STRIDE
    tgt = (r_sel * jnp.arange(n_sel)[:, None, None] + jnp.arange(r_sel)[None, :, None]
           - jnp.arange(r_cmp)[None, None, :])
    cmap = (jnp.arange(n_cmp)[:, None, None, None] == tgt[None]).sum((2, 3)).astype(jnp.float32)
    imp = jnp.einsum('bgnsc,cj->bgsj', p_c, cmap)
    jb = jnp.arange(n_sel)[None, :]
    cur = (t // SEL_BLOCK)[:, None]
    forced = (jb == 0) | (jb == cur) | (jb == cur - 1)
    imp = jnp.where(forced, 1e6, imp)
    imp = jnp.where(jb > cur, -1e6, imp)
    k_sel = min(SEL_TOP, n_sel)
    _, sel_idx = lax.top_k(imp, k_sel)

    nqb = S // Q_BLOCK
    ks_blk = ks.reshape(B, n_sel, SEL_BLOCK, G, dh).transpose(0, 3, 1, 2, 4)
    vs_blk = vs.reshape(B, n_sel, SEL_BLOCK, G, dh).transpose(0, 3, 1, 2, 4)
    q_b = qg.reshape(B, nqb, Q_BLOCK, G, n, dh).transpose(1, 0, 2, 3, 4, 5)
    i_b = sel_idx.reshape(B, G, nqb, Q_BLOCK, k_sel).transpose(2, 0, 1, 3, 4)
    bi = jnp.arange(B)[:, None, None, None]
    gi = jnp.arange(G)[None, :, None, None]

    def sel_block(args):
        q_c, idx_c, c = args
        kg = ks_blk[bi, gi, idx_c]
        vg = vs_blk[bi, gi, idx_c]
        s = jnp.einsum('bqgnd,bgqkld->bgnqkl', q_c, kg).astype(jnp.float32) * scale
        tq = c * Q_BLOCK + jnp.arange(Q_BLOCK)
        kpos = idx_c[..., None] * SEL_BLOCK + jnp.arange(SEL_BLOCK)
        valid = (kpos <= tq[None, None, :, None, None])[:, :, None]
        s = jnp.where(valid, s, NEG_INF)
        p = jax.nn.softmax(s.reshape(B, G, n, Q_BLOCK, -1), axis=-1).reshape(s.shape)
        return jnp.einsum('bgnqkl,bgqkld->bqgnd', p.astype(vg.dtype), vg)

    o_s = lax.map(sel_block, (q_b, i_b, jnp.arange(nqb)))
    o_s = o_s.transpose(1, 0, 2, 3, 4, 5).reshape(B, S, G, n, dh)

    nwin = WINDOW // Q_BLOCK

    def band(z):
        zp = jnp.pad(z, ((0, 0), (WINDOW, 0), (0, 0), (0, 0))).reshape(B, nqb + nwin, Q_BLOCK, G, dh)
        return jnp.concatenate([zp[:, i:i + nqb] for i in range(nwin + 1)], axis=2)

    kwb, vwb = band(kw), band(vw)
    qc = qg.reshape(B, nqb, Q_BLOCK, G, n, dh)
    s_w = jnp.einsum('bcqgnd,bckgd->bcgnqk', qc, kwb).astype(jnp.float32) * scale
    tq = jnp.arange(nqb)[:, None] * Q_BLOCK + jnp.arange(Q_BLOCK)[None, :]
    kp = jnp.arange(nqb)[:, None] * Q_BLOCK - WINDOW + jnp.arange((nwin + 1) * Q_BLOCK)[None, :]
    valid_w = ((kp[:, None, :] >= 0) & (kp[:, None, :] <= tq[:, :, None])
               & (tq[:, :, None] - kp[:, None, :] < WINDOW))
    p_w = masked_softmax(s_w, valid_w[None, :, None, None])
    o_w = jnp.einsum('bcgnqk,bckgd->bcqgnd', p_w.astype(vwb.dtype), vwb).reshape(B, S, G, n, dh)

    gts = jax.nn.sigmoid(gate_logits.astype(jnp.float32)).reshape(B, S, G, n, 3).astype(q.dtype)
    o = gts[..., 0, None] * o_c + gts[..., 1, None] * o_s + gts[..., 2, None] * o_w
    return o.reshape(B, S, NSA_HEADS * dh)


def rwkv7_mixer(z, mu, w0, w2, a0, a2, g2, k_k, k_a, r_k, ln_w, ln_b):
    B, S, _ = z.shape
    f32 = jnp.float32
    z = z.astype(f32)
    z_prev = jnp.pad(z, ((0, 0), (1, 0), (0, 0)))[:, :S]
    z = z + (z_prev - z) * mu
    cuts = np.cumsum([RWKV_DIM, RWKV_DIM, RWKV_DIM, DECAY_LORA, AAA_LORA]).tolist()
    r, k, v, zw, za, zg = jnp.split(z, cuts, axis=-1)
    w_log = -jax.nn.softplus(-(w0 + jnp.tanh(zw) @ w2)) - 0.5
    decay = jnp.exp(-jnp.exp(w_log))
    a = jax.nn.sigmoid(a0 + za @ a2)
    g = jax.nn.sigmoid(zg) @ g2
    hs = lambda u: u.reshape(B, S, RWKV_HEADS, RWKV_HEAD)
    kk = hs(k * k_k)
    kk = kk / jnp.maximum(jnp.sqrt(jnp.sum(kk * kk, axis=-1, keepdims=True)), 1e-12)
    k = k * (1.0 + (a - 1.0) * k_a)
    r, k, v, decay, a = hs(r), hs(k), hs(v), hs(decay), hs(a)

    def step(state, inp):
        r_t, w_t, k_t, v_t, a_t, b_t = inp
        sa = jnp.einsum('bhvk,bhk->bhv', state, a_t)
        state = (state * w_t[:, :, None, :] + sa[..., None] * b_t[:, :, None, :]
                 + v_t[..., None] * k_t[:, :, None, :])
        return state, jnp.einsum('bhvk,bhk->bhv', state, r_t)

    tm = lambda u: jnp.swapaxes(u, 0, 1)
    state0 = jnp.zeros((B, RWKV_HEADS, RWKV_HEAD, RWKV_HEAD), f32)
    _, y = lax.scan(step, state0, (tm(r), tm(decay), tm(k), tm(v), tm(-kk), tm(kk * a)))
    y = tm(y)
    mean = jnp.mean(y, axis=-1, keepdims=True)
    var = jnp.mean(jnp.square(y - mean), axis=-1, keepdims=True)
    y = ((y - mean) * lax.rsqrt(var + GN_EPS)).reshape(B, S, RWKV_DIM) * ln_w + ln_b
    bonus = jnp.sum(r * k * r_k, axis=-1, keepdims=True) * v
    return (y + bonus.reshape(B, S, RWKV_DIM)) * g


def cross_attn(h, m, wq, wkv, wo):
    B, S, _ = h.shape
    M = m.shape[1]
    q = (h @ wq).reshape(B, S, XATTN_HEADS, HEAD_DIM)
    kv = (m @ wkv).reshape(B, M, 2, XATTN_HEADS, HEAD_DIM)
    k, v = kv[:, :, 0], kv[:, :, 1]
    s = jnp.einsum('bshd,bmhd->bhsm', q, k).astype(jnp.float32) * HEAD_DIM ** -0.5
    p = jax.nn.softmax(s, axis=-1)
    o = jnp.einsum('bhsm,bmhd->bshd', p.astype(v.dtype), v).reshape(B, S, XATTN_HEADS * HEAD_DIM)
    return o @ wo


def peer(h, wq, keys, u_tab, v_tab):
    B, S, D = h.shape
    q = (h @ wq).reshape(B, S, PEER_HEADS, 2, PEER_QDIM // 2)
    s = jnp.einsum('bshcd,hckd->bshck', q, keys).astype(jnp.float32)
    s1, i1 = lax.top_k(s[..., 0, :], PEER_TOPK)
    s2, i2 = lax.top_k(s[..., 1, :], PEER_TOPK)
    cand = (s1[..., :, None] + s2[..., None, :]).reshape(B, S, PEER_HEADS, PEER_TOPK * PEER_TOPK)
    cidx = (i1[..., :, None] * PEER_KEYS + i2[..., None, :]).reshape(B, S, PEER_HEADS, PEER_TOPK * PEER_TOPK)
    top_s, pos = lax.top_k(cand, PEER_TOPK)
    eidx = jnp.take_along_axis(cidx, pos, axis=-1)
    gate = jax.nn.softmax(top_s, axis=-1)
    nb = (B * S) // TOKEN_BLOCK
    hb = h.reshape(nb, TOKEN_BLOCK, D)
    eb = eidx.reshape(nb, TOKEN_BLOCK, PEER_HEADS, PEER_TOPK)
    gb = gate.reshape(nb, TOKEN_BLOCK, PEER_HEADS, PEER_TOPK)

    def blk(args):
        h_c, e_c, g_c = args
        u = u_tab[e_c]
        act = jax.nn.gelu(jnp.einsum('td,thkd->thk', h_c, u).astype(jnp.float32)) * g_c
        v = v_tab[e_c]
        return jnp.einsum('thk,thkd->td', act.astype(v.dtype), v)

    return lax.map(blk, (hb, eb, gb)).reshape(B, S, D)


def setup_inputs(seed: int = 0) -> dict:
    key = jax.random.key(seed)
    ks = iter(list(jax.random.split(key, 48)))
    nrm = lambda shape, scale: jax.random.normal(next(ks), shape, jnp.float32) * scale
    uni = lambda shape, lo, hi: jax.random.uniform(next(ks), shape, jnp.float32, lo, hi)
    L, D = DEPTH, D_MODEL
    gain = lambda shape: 1.0 + nrm(shape, 0.02)
    x = nrm((BATCH, SEQ, D), 1.0)
    mem = nrm((BATCH, MEM_TOKENS, D), 1.0)
    offset = jax.random.randint(next(ks), (BATCH, 1), 0, 1024)
    positions = (offset + jnp.arange(SEQ, dtype=jnp.int32)[None, :]).astype(jnp.int32)
    return {
        'x': x,
        'mem': mem,
        'positions': positions,
        'norm_mix': gain((L, D)),
        'w_in': nrm((L, D, IN_COLS), D ** -0.5),
        'nsa_cmp_pe': nrm((L, 2, CMP_LEN, HEAD_DIM), 0.02),
        'nsa_cmp_w1': nrm((L, 2, CMP_LEN * HEAD_DIM, CMP_HIDDEN), (CMP_LEN * HEAD_DIM) ** -0.5),
        'nsa_cmp_w2': nrm((L, 2, CMP_HIDDEN, HEAD_DIM), CMP_HIDDEN ** -0.5),
        'rwkv_mu': uni((L, RWKV_COLS), 0.0, 1.0),
        'rwkv_w0': uni((L, RWKV_DIM), -5.5, -0.5),
        'rwkv_w2': nrm((L, DECAY_LORA, RWKV_DIM), 0.5 * DECAY_LORA ** -0.5),
        'rwkv_a0': nrm((L, RWKV_DIM), 0.1),
        'rwkv_a2': nrm((L, AAA_LORA, RWKV_DIM), 0.5 * AAA_LORA ** -0.5),
        'rwkv_g2': nrm((L, GATE_LORA, RWKV_DIM), GATE_LORA ** -0.5),
        'rwkv_k_k': 0.85 + nrm((L, RWKV_DIM), 0.02),
        'rwkv_k_a': 1.0 + nrm((L, RWKV_DIM), 0.02),
        'rwkv_r_k': nrm((L, RWKV_HEADS, RWKV_HEAD), 0.1),
        'rwkv_ln_w': gain((L, RWKV_DIM)),
        'rwkv_ln_b': nrm((L, RWKV_DIM), 0.02),
        'w_proj_nsa': nrm((L, NSA_HEADS * HEAD_DIM, D), (NSA_HEADS * HEAD_DIM) ** -0.5),
        'w_proj_rwkv': nrm((L, RWKV_DIM, D), RWKV_DIM ** -0.5),
        'w_out': nrm((L, D, D), 0.5 * D ** -0.5),
        'norm_xattn': gain((L, D)),
        'norm_mem': gain((L, D)),
        'xattn_wq': nrm((L, D, XATTN_HEADS * HEAD_DIM), D ** -0.5),
        'xattn_wkv': nrm((L, D, 2 * XATTN_HEADS * HEAD_DIM), D ** -0.5),
        'xattn_wo': nrm((L, XATTN_HEADS * HEAD_DIM, D), 0.5 * (XATTN_HEADS * HEAD_DIM) ** -0.5),
        'norm_ffn': gain((L, D)),
        'peer_wq': nrm((L, D, PEER_HEADS * PEER_QDIM), D ** -0.5),
        'peer_keys': nrm((L, PEER_HEADS, 2, PEER_KEYS, PEER_QDIM // 2), (PEER_QDIM // 2) ** -0.5),
        'peer_u': nrm((L, PEER_EXPERTS, D), D ** -0.5),
        'peer_v': nrm((L, PEER_EXPERTS, D), 0.3),
        'norm_final': gain((D,)),
    }


def reference(x, mem, positions, norm_mix, w_in, nsa_cmp_pe, nsa_cmp_w1, nsa_cmp_w2,
              rwkv_mu, rwkv_w0, rwkv_w2, rwkv_a0, rwkv_a2, rwkv_g2, rwkv_k_k, rwkv_k_a,
              rwkv_r_k, rwkv_ln_w, rwkv_ln_b, w_proj_nsa, w_proj_rwkv, w_out,
              norm_xattn, norm_mem, xattn_wq, xattn_wkv, xattn_wo,
              norm_ffn, peer_wq, peer_keys, peer_u, peer_v, norm_final):
    B, S, _ = x.shape
    cuts = np.cumsum([NSA_Q_COLS, NSA_KV_COLS, NSA_GATE_COLS, RWKV_COLS]).tolist()
    for l in range(DEPTH):
        h = rmsnorm(x, norm_mix[l])
        z = h @ w_in[l]
        zq, zkv, zg_nsa, z_rwkv, z_merge = jnp.split(z, cuts, axis=-1)
        q = rope(zq.reshape(B, S, NSA_HEADS, HEAD_DIM), positions)
        o_nsa = nsa_mixer(q, zkv, zg_nsa, positions, nsa_cmp_pe[l], nsa_cmp_w1[l], nsa_cmp_w2[l])
        o_rwkv = rwkv7_mixer(z_rwkv, rwkv_mu[l], rwkv_w0[l], rwkv_w2[l], rwkv_a0[l], rwkv_a2[l],
                             rwkv_g2[l], rwkv_k_k[l], rwkv_k_a[l], rwkv_r_k[l],
                             rwkv_ln_w[l], rwkv_ln_b[l]).astype(x.dtype)
        gates = jax.nn.sigmoid(z_merge.astype(jnp.float32)).astype(x.dtype)
        g_a, g_b = gates[..., :D_MODEL], gates[..., D_MODEL:]
        merged = g_a * (o_nsa @ w_proj_nsa[l]) + g_b * (o_rwkv @ w_proj_rwkv[l])
        x = x + merged @ w_out[l]
        h = rmsnorm(x, norm_xattn[l])
        m = rmsnorm(mem, norm_mem[l])
        x = x + cross_attn(h, m, xattn_wq[l], xattn_wkv[l], xattn_wo[l])
        h = rmsnorm(x, norm_ffn[l])
        x = x + peer(h, peer_wq[l], peer_keys[l], peer_u[l], peer_v[l])
    return rmsnorm(x, norm_final)
```

```python
import functools

import jax
import jax.numpy as jnp
from jax import lax
from jax.experimental import pallas as pl
from jax.experimental.pallas import tpu as pltpu

F32 = jnp.float32
BF16 = jnp.bfloat16
HI = lax.Precision.HIGHEST

LANES = 128
HEAD_DIM = 128
ROPE_DIM = HEAD_DIM // 4
ROPE_THETA = 500000.0
RMS_EPS = 1e-6
NEG_INF = -1e30

NSA_GROUPS = 4
NSA_HPG = 4
CMP_LEN = 32
CMP_STRIDE = 16
SEL_BLOCK = 64
SEL_TOP = 16
WINDOW = 512
Q_BLOCK = 128

RWKV_HEAD = 64
DECAY_LORA = 96
AAA_LORA = 96
GATE_LORA = 256
GN_EPS = 64e-5
RWKV_CHUNK = 128

XATTN_HEADS = 4
PEER_HEADS = 8
PEER_KEYS = 128
PEER_TOPK = 16

VMEM_LIMIT = 56 * 1024 * 1024


def _cparams(sem):
    return pltpu.CompilerParams(dimension_semantics=sem, vmem_limit_bytes=VMEM_LIMIT)


def _gelu(x):
    return 0.5 * x * (1.0 + jnp.tanh(0.7978845608028654 * (x + 0.044715 * x * x * x)))


def _softplus(x):
    return jnp.maximum(x, 0.0) + jnp.log(1.0 + jnp.exp(-jnp.abs(x)))


def _dot(a, b):
    return jnp.dot(a, b, preferred_element_type=F32)


def _dot_nt(a, b):
    return lax.dot_general(a, b, (((1,), (1,)), ((), ())), preferred_element_type=F32)


def _dotf(a, b):
    return jnp.dot(a, b, preferred_element_type=F32, precision=HI)


def _dotf_nt(a, b):
    return lax.dot_general(a, b, (((1,), (1,)), ((), ())), preferred_element_type=F32, precision=HI)


def _rope_apply(a, c, sa, sb):
    return a * c + pltpu.roll(a, LANES - ROPE_DIM // 2, 1) * sa + pltpu.roll(a, ROPE_DIM // 2, 1) * sb


def _rope_tables(pos):
    half = ROPE_DIM // 2
    inv = ROPE_THETA ** (-jnp.arange(half, dtype=F32) / half)
    ang = pos.astype(F32)[..., None] * inv
    cos, sin = jnp.cos(ang), jnp.sin(ang)
    shp = pos.shape + (HEAD_DIM - ROPE_DIM,)
    c = jnp.concatenate([cos, cos, jnp.ones(shp, F32)], axis=-1)
    sa = jnp.concatenate([-sin, jnp.zeros(pos.shape + (HEAD_DIM - half,), F32)], axis=-1)
    sb = jnp.concatenate([jnp.zeros(pos.shape + (half,), F32), sin, jnp.zeros(shp, F32)], axis=-1)
    return c, sa, sb


def _rmsnorm_kernel(x_ref, g_ref, o_ref):
    x = x_ref[...].astype(F32)
    ms = jnp.mean(x * x, axis=-1, keepdims=True)
    o_ref[...] = (x * lax.rsqrt(ms + RMS_EPS) * g_ref[...]).astype(o_ref.dtype)


def rmsnorm(x2d, g, out_dtype, tm=256):
    M, D = x2d.shape
    tm = min(tm, M)
    return pl.pallas_call(
        _rmsnorm_kernel,
        grid=(M // tm,),
        in_specs=[pl.BlockSpec((tm, D), lambda i: (i, 0)), pl.BlockSpec((1, D), lambda i: (0, 0))],
        out_specs=pl.BlockSpec((tm, D), lambda i: (i, 0)),
        out_shape=jax.ShapeDtypeStruct((M, D), out_dtype),
        compiler_params=_cparams(("parallel",)),
        name="rmsnorm",
    )(x2d, g.reshape(1, D).astype(F32))


def _mm_kernel(*refs, nk, mode, n_extra):
    x_ref, w_ref = refs[0], refs[1]
    extras = refs[2:2 + n_extra]
    o_ref = refs[2 + n_extra]

    def epilogue(acc):
        if mode == "plain":
            o_ref[...] = acc.astype(o_ref.dtype)
        elif mode == "sigmoid":
            o_ref[...] = jax.nn.sigmoid(acc).astype(o_ref.dtype)
        elif mode == "residual":
            o_ref[...] = (extras[0][...] + acc).astype(o_ref.dtype)
        elif mode in ("heads", "heads_sigmoid", "heads_rope"):
            nh = acc.shape[1] // LANES
            for j in range(nh):
                a = acc[:, j * LANES:(j + 1) * LANES]
                if mode == "heads_sigmoid":
                    a = jax.nn.sigmoid(a)
                elif mode == "heads_rope":
                    a = _rope_apply(a, extras[0][...], extras[1][...], extras[2][...])
                o_ref[0, j] = a.astype(o_ref.dtype)
        else:
            raise ValueError(mode)

    if nk == 1:
        epilogue(_dot(x_ref[...], w_ref[...]))
    else:
        acc_ref = refs[-1]
        k = pl.program_id(2)

        @pl.when(k == 0)
        def _():
            acc_ref[...] = jnp.zeros_like(acc_ref)

        acc_ref[...] += _dot(x_ref[...], w_ref[...])

        @pl.when(k == nk - 1)
        def _():
            epilogue(acc_ref[...])


def matmul(x, w, *, mode="plain", extras=(), out_dtype=F32, tm=1024, tn=512, tk=None, seq=None, name="mm"):
    M, K = x.shape
    N = w.shape[1]
    tm, tn = min(tm, M), min(tn, N)
    if mode.startswith("heads"):
        tm = min(tm, seq)
    tk = K if tk is None else min(tk, K)
    nk = K // tk
    assert M % tm == 0 and N % tn == 0 and K % tk == 0
    grid = (M // tm, N // tn, nk)
    in_specs = [pl.BlockSpec((tm, tk), lambda i, j, k: (i, k)), pl.BlockSpec((tk, tn), lambda i, j, k: (k, j))]
    if mode == "residual":
        in_specs.append(pl.BlockSpec((tm, tn), lambda i, j, k: (i, j)))
    elif mode == "heads_rope":
        in_specs += [pl.BlockSpec((tm, LANES), lambda i, j, k: (i, 0))] * 3
    if mode.startswith("heads"):
        assert seq % tm == 0 and tn % LANES == 0
        spb = seq // tm
        out_shape = jax.ShapeDtypeStruct((M // seq, N // LANES, seq, LANES), out_dtype)
        out_spec = pl.BlockSpec((1, tn // LANES, tm, LANES), lambda i, j, k: (i // spb, j, i % spb, 0))
    else:
        out_shape = jax.ShapeDtypeStruct((M, N), out_dtype)
        out_spec = pl.BlockSpec((tm, tn), lambda i, j, k: (i, j))
    scratch = [pltpu.VMEM((tm, tn), F32)] if nk > 1 else []
    return pl.pallas_call(
        functools.partial(_mm_kernel, nk=nk, mode=mode, n_extra=len(extras)),
        grid=grid,
        in_specs=in_specs,
        out_specs=out_spec,
        out_shape=out_shape,
        scratch_shapes=scratch,
        compiler_params=_cparams(("parallel", "parallel", "arbitrary")),
        name=name,
    )(x, w, *extras)


def _compress_kernel(x_ref, w1_ref, w2_ref, pe_ref, c_ref, sa_ref, sb_ref, o_ref):
    half = x_ref.shape[-1]
    x = x_ref[0, 0]
    a = _dot(x, w1_ref[0, :half, :])
    b = _dot(x, w1_ref[0, half:, :])
    pet = _dot(pe_ref[0], w1_ref[0])
    n = a.shape[0]
    hid = a + pltpu.roll(b, n - 1, 0) + pet[0:1]
    out = _dot(_gelu(hid).astype(BF16), w2_ref[0])
    out = _rope_apply(out, c_ref[0, 0], sa_ref[0, 0], sb_ref[0, 0])
    o_ref[0, 0, 0] = out.astype(o_ref.dtype)


def nsa_compress(kvp, w1, w2, pe8, tabs):
    B, _, S, dh = kvp.shape
    G = NSA_GROUPS
    nch = S // CMP_STRIDE
    xv = kvp.reshape(B, kvp.shape[1], nch, CMP_STRIDE * dh)
    hidden = w1.shape[-1]
    tab_spec = pl.BlockSpec((1, 1, nch, LANES), lambda w, b, g: (w, b, 0, 0))
    return pl.pallas_call(
        _compress_kernel,
        grid=(2, B, G),
        in_specs=[
            pl.BlockSpec((1, 1, nch, CMP_STRIDE * dh), lambda w, b, g: (b, w * G + g, 0, 0)),
            pl.BlockSpec((1, CMP_LEN * dh, hidden), lambda w, b, g: (w, 0, 0)),
            pl.BlockSpec((1, hidden, dh), lambda w, b, g: (w, 0, 0)),
            pl.BlockSpec((1, 8, CMP_LEN * dh), lambda w, b, g: (w, 0, 0)),
            tab_spec, tab_spec, tab_spec,
        ],
        out_specs=pl.BlockSpec((1, 1, 1, nch, dh), lambda w, b, g: (w, b, g, 0, 0)),
        out_shape=jax.ShapeDtypeStruct((2, B, G, nch, dh), BF16),
        compiler_params=_cparams(("parallel", "parallel", "parallel")),
        name="nsa_compress",
    )(xv, w1, w2, pe8, *tabs)


def _nsa_kernel(q_ref, ks_ref, kw_ref, vs_ref, vw_ref, kc_ref, vc_ref, gt_ref, cmap_ref, e_ref, o_ref, *, ck):
    n = NSA_HPG
    nq = n * Q_BLOCK
    qb = pl.program_id(2)
    t0 = qb * Q_BLOCK
    scale = HEAD_DIM ** -0.5
    nsel = ks_ref.shape[2] // SEL_BLOCK
    q = q_ref[0].reshape(nq, HEAD_DIM)
    tq = t0 + lax.broadcasted_iota(jnp.int32, (Q_BLOCK, 1), 0)

    def masked_softmax(s, ok):
        kc = s.shape[-1]
        s3 = jnp.where(ok[None], s.reshape(n, Q_BLOCK, kc), NEG_INF)
        m = jnp.max(s3, axis=-1, keepdims=True)
        p = jnp.exp(s3 - m)
        return p / jnp.sum(p, axis=-1, keepdims=True)

    kc = kc_ref[0, 0, 0]
    vc = vc_ref[0, 0, 0]
    ncmp = kc.shape[0]
    s = _dot_nt(q, kc) * scale
    cidx = lax.broadcasted_iota(jnp.int32, (Q_BLOCK, ncmp), 1)
    valid_c = (cidx < ncmp - 1) & (cidx * CMP_STRIDE + (CMP_LEN - 1) <= tq)
    p = masked_softmax(s, valid_c)
    p = p * (tq >= CMP_LEN - 1).astype(F32)[None]
    o_c = _dot(p.reshape(nq, ncmp).astype(BF16), vc)

    psum = p[0] + p[1] + p[2] + p[3]
    hi = psum.astype(BF16)
    lo = (psum - hi.astype(F32)).astype(BF16)
    imp = _dot(hi, cmap_ref[...]) + _dot(lo, cmap_ref[...])
    j = lax.broadcasted_iota(jnp.int32, (Q_BLOCK, LANES), 1)
    cur = tq // SEL_BLOCK
    imp = jnp.where(j == 0, 1e6, imp)
    imp = jnp.where(j == cur, 1e6, imp)
    imp = jnp.where(j == cur - 1, 1e6, imp)
    imp = jnp.where(j > cur, -1e6, imp)
    imp = jnp.where(j >= nsel, -3e6, imp)

    imp_t = imp.T
    jr = lax.broadcasted_iota(jnp.int32, (LANES, Q_BLOCK), 0)
    rank = jnp.zeros((LANES, Q_BLOCK), F32)
    for i in range(min(nsel, LANES)):
        row = imp_t[i:i + 1, :]
        rank = rank + jnp.where(row > imp_t, 1.0, jnp.where(row == imp_t, jnp.where(jr > i, 1.0, 0.0), 0.0))
    sel = jnp.where(rank < float(SEL_TOP), 1.0, 0.0).T.astype(BF16)

    n_ch = (t0 + Q_BLOCK + ck - 1) // ck

    def body(kb, carry):
        m_i, l_i, acc = carry
        k0 = pl.multiple_of(kb * ck, ck)
        kblk = ks_ref[0, 0, pl.ds(k0, ck), :]
        vblk = vs_ref[0, 0, pl.ds(k0, ck), :]
        s = _dot_nt(q, kblk) * scale
        em = _dot(sel, e_ref[kb])
        kpos = k0 + lax.broadcasted_iota(jnp.int32, (Q_BLOCK, ck), 1)
        ok = (kpos <= tq) & (em > 0.5)
        s3 = jnp.where(ok[None], s.reshape(n, Q_BLOCK, ck), NEG_INF).reshape(nq, ck)
        m_new = jnp.maximum(m_i, jnp.max(s3, axis=-1, keepdims=True))
        alpha = jnp.exp(m_i - m_new)
        pp = jnp.exp(s3 - m_new)
        l_new = alpha * l_i + jnp.sum(pp, axis=-1, keepdims=True)
        acc_new = alpha * acc + _dot(pp.astype(BF16), vblk)
        return m_new, l_new, acc_new

    init = (jnp.full((nq, 1), NEG_INF, F32), jnp.zeros((nq, 1), F32), jnp.zeros((nq, HEAD_DIM), F32))
    _, l_f, acc_f = lax.fori_loop(0, n_ch, body, init)
    o_s = acc_f / l_f

    wlen = min(WINDOW + Q_BLOCK, kw_ref.shape[2])
    w0 = pl.multiple_of(jnp.maximum(t0 + Q_BLOCK - wlen, 0), Q_BLOCK)
    kblk = kw_ref[0, 0, pl.ds(w0, wlen), :]
    vblk = vw_ref[0, 0, pl.ds(w0, wlen), :]
    s = _dot_nt(q, kblk) * scale
    kpos = w0 + lax.broadcasted_iota(jnp.int32, (Q_BLOCK, wlen), 1)
    ok = (kpos <= tq) & (tq - kpos < WINDOW)
    pw = masked_softmax(s, ok)
    o_w = _dot(pw.reshape(nq, wlen).astype(BF16), vblk)

    gt = gt_ref[0, 0]
    for h in range(n):
        r = slice(h * Q_BLOCK, (h + 1) * Q_BLOCK)
        o = (gt[:, 3 * h:3 * h + 1] * o_c[r] + gt[:, 3 * h + 1:3 * h + 2] * o_s[r]
             + gt[:, 3 * h + 2:3 * h + 3] * o_w[r])
        o_ref[0, :, h * HEAD_DIM:(h + 1) * HEAD_DIM] = o.astype(o_ref.dtype)


def nsa_attention(qkr, kvp, cmp, gates, ck=512):
    B, _, S, dh = qkr.shape
    G, n = NSA_GROUPS, NSA_HPG
    ncmp = cmp.shape[3]
    nsel = S // SEL_BLOCK
    ck = min(ck, S)
    r_sel, r_cmp = SEL_BLOCK // CMP_STRIDE, CMP_LEN // CMP_STRIDE
    tgt = (r_sel * jnp.arange(nsel)[:, None, None] + jnp.arange(r_sel)[None, :, None]
           - jnp.arange(r_cmp)[None, None, :])
    cmap = (jnp.arange(ncmp - 1)[:, None, None, None] == tgt[None]).sum((2, 3)).astype(F32)
    cmap = jnp.pad(cmap, ((0, 1), (0, LANES - nsel))).astype(BF16)
    e = (jnp.arange(S)[None, :] // SEL_BLOCK == jnp.arange(LANES)[:, None]).astype(BF16)
    e3 = e.reshape(LANES, S // ck, ck).transpose(1, 0, 2)

    def head(h0):
        return pl.BlockSpec((1, 1, S, dh), lambda b, g, i: (b, h0 + g, 0, 0))

    return pl.pallas_call(
        functools.partial(_nsa_kernel, ck=ck),
        grid=(B, G, S // Q_BLOCK),
        in_specs=[
            pl.BlockSpec((1, n, Q_BLOCK, dh), lambda b, g, i: (b, g, i, 0)),
            head(16), head(20), head(8), head(12),
            pl.BlockSpec((1, 1, 1, ncmp, dh), lambda b, g, i: (0, b, g, 0, 0)),
            pl.BlockSpec((1, 1, 1, ncmp, dh), lambda b, g, i: (1, b, g, 0, 0)),
            pl.BlockSpec((1, 1, Q_BLOCK, LANES), lambda b, g, i: (b, g, i, 0)),
            pl.BlockSpec((ncmp, LANES), lambda b, g, i: (0, 0)),
            pl.BlockSpec((S // ck, LANES, ck), lambda b, g, i: (0, 0, 0)),
        ],
        out_specs=pl.BlockSpec((1, Q_BLOCK, n * dh), lambda b, g, i: (b, i, g)),
        out_shape=jax.ShapeDtypeStruct((B, S, G * n * dh), BF16),
        compiler_params=_cparams(("parallel", "parallel", "arbitrary")),
        name="nsa_attention",
    )(qkr, qkr, qkr, kvp, kvp, cmp, cmp, gates, cmap, e3)


def _rwkv_feat_kernel(z_ref, zp_ref, mu_ref, o_ref):
    i = pl.program_id(1)
    z = z_ref[0]
    prev = zp_ref[0, 7:8, :] * (i > 0).astype(F32)
    row = lax.broadcasted_iota(jnp.int32, z.shape, 0)
    sh = jnp.where(row == 0, prev, pltpu.roll(z, 1, 0))
    zs = z + (sh - z) * mu_ref[...]
    o_ref[0, :, 0:128] = jnp.tanh(zs[:, 0:128])
    o_ref[0, :, 128:256] = zs[:, 128:256]
    o_ref[0, :, 256:] = jax.nn.sigmoid(zs[:, 256:])


def rwkv_features(z_lora, mu_lora, tm=512):
    B, S, W = z_lora.shape
    tm = min(tm, S)
    return pl.pallas_call(
        _rwkv_feat_kernel,
        grid=(B, S // tm),
        in_specs=[pl.BlockSpec((1, tm, W), lambda b, i: (b, i, 0)),
                  pl.BlockSpec((1, 8, W), lambda b, i: (b, jnp.maximum(i * (tm // 8) - 1, 0), 0)),
                  pl.BlockSpec((1, W), lambda b, i: (0, 0))],
        out_specs=pl.BlockSpec((1, tm, W), lambda b, i: (b, i, 0)),
        out_shape=jax.ShapeDtypeStruct((B, S, W), F32),
        compiler_params=_cparams(("parallel", "arbitrary")),
        name="rwkv_features",
    )(z_lora, z_lora, mu_lora)


def _rwkv_kernel(r_ref, k_ref, v_ref, f_ref, par_ref, w2_ref, a2_ref, g2_ref, o_ref, st_ref, prev_ref):
    C = RWKV_CHUNK
    c = pl.program_id(2)

    @pl.when(c == 0)
    def _():
        st_ref[...] = jnp.zeros_like(st_ref)
        prev_ref[...] = jnp.zeros_like(prev_ref)

    par = par_ref[...]
    mu_r, mu_k, mu_v = par[0:1], par[1:2], par[2:3]
    w0, a0, k_k, k_a, r_k, ln_w, ln_b = par[3:4], par[4:5], par[5:6], par[6:7], par[7:8], par[8:9], par[9:10]

    row = lax.broadcasted_iota(jnp.int32, (C, C), 0)
    col = lax.broadcasted_iota(jnp.int32, (C, C), 1)
    trow = lax.broadcasted_iota(jnp.int32, (C, LANES), 0)
    lane = lax.broadcasted_iota(jnp.int32, (C, LANES), 1)
    lr = lax.broadcasted_iota(jnp.int32, (LANES, LANES), 0)
    lc = lax.broadcasted_iota(jnp.int32, (LANES, LANES), 1)

    def shifted(z_ref, idx, mu):
        z = z_ref[0]
        sh = jnp.where(trow == 0, prev_ref[idx:idx + 1, :], pltpu.roll(z, 1, 0))
        prev_ref[idx:idx + 1, :] = z[C - 1:C, :]
        return z + (sh - z) * mu

    r = shifted(r_ref, 0, mu_r)
    k = shifted(k_ref, 1, mu_k)
    v = shifted(v_ref, 2, mu_v)
    f = f_ref[0]
    w_log = -_softplus(-(w0 + _dotf(f[:, 0:128], w2_ref[...]))) - 0.5
    logd = -jnp.exp(w_log)
    a = jax.nn.sigmoid(a0 + _dotf(f[:, 128:256], a2_ref[...]))
    g = _dotf(f[:, 256:], g2_ref[...])

    bd = jnp.where((lr // RWKV_HEAD) == (lc // RWKV_HEAD), 1.0, 0.0)
    kk = k * k_k
    kk = kk / jnp.maximum(jnp.sqrt(_dotf(kk * kk, bd)), 1e-12)
    k2 = k * (1.0 + (a - 1.0) * k_a)

    tri_incl = jnp.where(col <= row, 1.0, 0.0)
    cum = _dotf(tri_incl, logd)
    cum_end = cum[C - 1:C, :]
    e_neg = jnp.exp(-cum)
    a_t = -kk * jnp.exp(cum - logd)
    b_t = kk * a * e_neg
    k_t = k2 * e_neg
    r_t = r * jnp.exp(cum)
    e_end = jnp.exp(cum_end - cum)
    b_e = kk * a * e_end
    k_e = k2 * e_end
    p_end = jnp.exp(cum_end)

    strict = col < row
    eye = jnp.where(col == row, 1.0, 0.0)
    rhs = jnp.concatenate([b_t, k_t], axis=0)
    y = jnp.zeros((C, LANES), F32)
    for h in range(LANES // RWKV_HEAD):
        mh = (lane // RWKV_HEAD) == h
        am = jnp.where(mh, a_t, 0.0)
        rm = jnp.where(mh, r_t, 0.0)
        vm = jnp.where(mh, v, 0.0)
        bem = jnp.where(mh, b_e, 0.0)
        kem = jnp.where(mh, k_e, 0.0)
        amat = _dotf_nt(jnp.concatenate([am, rm], axis=0), rhs)
        l_ab = jnp.where(strict, amat[0:C, 0:C], 0.0)
        l_ak = jnp.where(strict, amat[0:C, C:2 * C], 0.0)
        l_rb = jnp.where(col <= row, amat[C:2 * C, 0:C], 0.0)
        l_rk = jnp.where(col <= row, amat[C:2 * C, C:2 * C], 0.0)
        x = eye + l_ab
        pw = l_ab
        n_sq = max(C.bit_length() - 2, 0)
        for _ in range(n_sq):
            pw = _dotf(pw, pw)
            x = x + _dotf(x, pw)
        st = st_ref[h]
        u = _dotf(x, _dotf_nt(am, st) + _dotf(l_ak, vm))
        y = y + _dotf_nt(rm, st) + _dotf(l_rb, u) + _dotf(l_rk, vm)
        st_ref[h] = st * p_end + _dotf(u.T, bem) + _dotf(vm.T, kem)

    inv_n = 1.0 / RWKV_HEAD
    mean = _dotf(y, bd) * inv_n
    d = y - mean
    var = _dotf(d * d, bd) * inv_n
    yn = d * lax.rsqrt(var + GN_EPS) * ln_w + ln_b
    bonus = _dotf(r * k2 * r_k, bd) * v
    o_ref[0] = ((yn + bonus) * g).astype(o_ref.dtype)


def rwkv_scan(z_rkv, feat, par, w2p, a2p, g2):
    B, S, R3 = z_rkv.shape
    R = R3 // 3
    npair = R // LANES
    C = RWKV_CHUNK
    assert S % C == 0

    def col(off):
        return pl.BlockSpec((1, C, LANES), lambda b, p, c: (b, c, off + p))

    return pl.pallas_call(
        _rwkv_kernel,
        grid=(B, npair, S // C),
        in_specs=[
            col(0), col(npair), col(2 * npair),
            pl.BlockSpec((1, C, feat.shape[-1]), lambda b, p, c: (b, c, 0)),
            pl.BlockSpec((16, LANES), lambda b, p, c: (0, p)),
            pl.BlockSpec((LANES, LANES), lambda b, p, c: (0, p)),
            pl.BlockSpec((LANES, LANES), lambda b, p, c: (0, p)),
            pl.BlockSpec((GATE_LORA, LANES), lambda b, p, c: (0, p)),
        ],
        out_specs=pl.BlockSpec((1, C, LANES), lambda b, p, c: (b, c, p)),
        out_shape=jax.ShapeDtypeStruct((B, S, R), BF16),
        scratch_shapes=[pltpu.VMEM((LANES // RWKV_HEAD, LANES, LANES), F32), pltpu.VMEM((8, LANES), F32)],
        compiler_params=_cparams(("parallel", "parallel", "arbitrary")),
        name="rwkv_scan",
    )(z_rkv, z_rkv, z_rkv, feat, par, w2p, a2p, g2)


def _merge_kernel(on_ref, or_ref, wn_ref, wr_ref, ga_ref, gb_ref, o_ref):
    pa = _dot(on_ref[...], wn_ref[...])
    pb = _dot(or_ref[...], wr_ref[...])
    o_ref[...] = (ga_ref[...].astype(F32) * pa + gb_ref[...].astype(F32) * pb).astype(o_ref.dtype)


def merge_mixers(o_nsa, o_rwkv, wn, wr, gates, tm=1024, tn=512):
    M, Kn = o_nsa.shape
    Kr = o_rwkv.shape[1]
    D = wn.shape[1]
    tm, tn = min(tm, M), min(tn, D)
    nb = D // tn
    return pl.pallas_call(
        _merge_kernel,
        grid=(M // tm, nb),
        in_specs=[
            pl.BlockSpec((tm, Kn), lambda i, j: (i, 0)),
            pl.BlockSpec((tm, Kr), lambda i, j: (i, 0)),
            pl.BlockSpec((Kn, tn), lambda i, j: (0, j)),
            pl.BlockSpec((Kr, tn), lambda i, j: (0, j)),
            pl.BlockSpec((tm, tn), lambda i, j: (i, j)),
            pl.BlockSpec((tm, tn), lambda i, j: (i, nb + j)),
        ],
        out_specs=pl.BlockSpec((tm, tn), lambda i, j: (i, j)),
        out_shape=jax.ShapeDtypeStruct((M, D), BF16),
        compiler_params=_cparams(("parallel", "parallel")),
        name="merge_mixers",
    )(o_nsa, o_rwkv, wn, wr, gates, gates)


def _xattn_kernel(q_ref, kv_ref, wo_ref, x_ref, o_ref):
    nh = XATTN_HEADS
    scale = HEAD_DIM ** -0.5
    outs = []
    for h in range(nh):
        q = q_ref[:, h * HEAD_DIM:(h + 1) * HEAD_DIM]
        k = kv_ref[0, :, h * HEAD_DIM:(h + 1) * HEAD_DIM]
        v = kv_ref[0, :, (nh + h) * HEAD_DIM:(nh + h + 1) * HEAD_DIM]
        s = _dot_nt(q, k) * scale
        m = jnp.max(s, axis=-1, keepdims=True)
        p = jnp.exp(s - m)
        p = p / jnp.sum(p, axis=-1, keepdims=True)
        outs.append(_dot(p.astype(BF16), v).astype(BF16))
    o = jnp.concatenate(outs, axis=-1)
    o_ref[...] = x_ref[...] + _dot(o, wo_ref[...])


def cross_attention(q, kv, wo, x, seq, tm=256):
    T, D = x.shape
    tm = min(tm, seq)
    spb = seq // tm
    Mm = kv.shape[1]
    return pl.pallas_call(
        _xattn_kernel,
        grid=(T // tm,),
        in_specs=[
            pl.BlockSpec((tm, q.shape[1]), lambda i: (i, 0)),
            pl.BlockSpec((1, Mm, kv.shape[2]), lambda i: (i // spb, 0, 0)),
            pl.BlockSpec(wo.shape, lambda i: (0, 0)),
            pl.BlockSpec((tm, D), lambda i: (i, 0)),
        ],
        out_specs=pl.BlockSpec((tm, D), lambda i: (i, 0)),
        out_shape=jax.ShapeDtypeStruct((T, D), F32),
        compiler_params=_cparams(("parallel",)),
        name="cross_attention",
    )(q, kv, wo, x)


def _top16(x):
    n = x.shape[0]
    iota = lax.broadcasted_iota(jnp.int32, x.shape, 0).astype(F32)
    cur = x
    rank = jnp.full(x.shape, float(PEER_TOPK), F32)
    vals = []
    for r in range(PEER_TOPK):
        m = jnp.max(cur, axis=0, keepdims=True)
        idx = jnp.min(jnp.where(cur == m, iota, float(n)), axis=0, keepdims=True)
        hit = iota == idx
        rank = jnp.where(hit, float(r), rank)
        cur = jnp.where(hit, -jnp.inf, cur)
        vals.append(m)
    return jnp.concatenate(vals, axis=0), rank


def _peer_route_kernel(q_ref, keys_ref, o_ref):
    K = PEER_TOPK
    q = q_ref[...]
    half = q.shape[1] // 2
    s1 = _dotf_nt(keys_ref[0, 0], q[:, :half])
    s2 = _dotf_nt(keys_ref[0, 1], q[:, half:])
    v1, rank1 = _top16(s1)
    v2, rank2 = _top16(s2)
    cand = jnp.concatenate([v1[i:i + 1] + v2 for i in range(K)], axis=0)
    top_s, crank = _top16(cand)
    chosen = jnp.where(crank < float(K), 1.0, 0.0)
    z = jnp.sum(jnp.exp(top_s - top_s[0:1]), axis=0, keepdims=True)
    jn = jnp.zeros_like(s1)
    for i in range(K):
        j_i = jnp.sum(chosen[i * K:(i + 1) * K], axis=0, keepdims=True)
        jn = jnp.where(rank1 == float(i), j_i, jn)
    o_ref[0, 0] = jnp.exp(s1 - v1[0:1]) / z
    o_ref[0, 1] = jnp.exp(s2 - v2[0:1])
    o_ref[0, 2] = jn
    o_ref[0, 3] = rank2


def peer_route(q, keys, tm=256):
    T = q.shape[0]
    H, _, nkeys, hd = keys.shape
    tm = min(tm, T)
    return pl.pallas_call(
        _peer_route_kernel,
        grid=(T // tm, H),
        in_specs=[pl.BlockSpec((tm, 2 * hd), lambda i, h: (i, h)),
                  pl.BlockSpec((1, 2, nkeys, hd), lambda i, h: (h, 0, 0, 0))],
        out_specs=pl.BlockSpec((1, 4, nkeys, tm), lambda i, h: (h, 0, 0, i)),
        out_shape=jax.ShapeDtypeStruct((H, 4, nkeys, T), F32),
        compiler_params=_cparams(("parallel", "parallel")),
        name="peer_route",
    )(q, keys)


def _peer_act_kernel(h_ref, u_ref, r_ref, o_ref):
    nk = PEER_KEYS
    ec = u_ref.shape[0]
    j = pl.program_id(1)
    hu = _dot_nt(u_ref[...], h_ref[...])
    for al in range(ec // nk):
        a = j * (ec // nk) + al
        w = None
        for hd in range(r_ref.shape[0]):
            e1 = r_ref[hd, 0, pl.ds(a, 1), :]
            jn = r_ref[hd, 2, pl.ds(a, 1), :]
            g = jnp.where(r_ref[hd, 3] < jn, e1 * r_ref[hd, 1], 0.0)
            w = g if w is None else w + g
        act = _gelu(hu[al * nk:(al + 1) * nk, :]) * w
        o_ref[:, al * nk:(al + 1) * nk] = act.T.astype(o_ref.dtype)


def peer_activations(h, u_tab, route, tm=512, ec=512):
    T, D = h.shape
    E = u_tab.shape[0]
    tm, ec = min(tm, T), min(ec, E)
    H, _, nkeys, _ = route.shape
    return pl.pallas_call(
        _peer_act_kernel,
        grid=(T // tm, E // ec),
        in_specs=[pl.BlockSpec((tm, D), lambda i, j: (i, 0)),
                  pl.BlockSpec((ec, D), lambda i, j: (j, 0)),
                  pl.BlockSpec((H, 4, nkeys, tm), lambda i, j: (0, 0, 0, i))],
        out_specs=pl.BlockSpec((tm, ec), lambda i, j: (i, j)),
        out_shape=jax.ShapeDtypeStruct((T, E), BF16),
        compiler_params=_cparams(("parallel", "arbitrary")),
        name="peer_activations",
    )(h, u_tab, route)


def kernel(x, mem, positions, norm_mix, w_in, nsa_cmp_pe, nsa_cmp_w1, nsa_cmp_w2, rwkv_mu, rwkv_w0, rwkv_w2, rwkv_a0, rwkv_a2, rwkv_g2, rwkv_k_k, rwkv_k_a, rwkv_r_k, rwkv_ln_w, rwkv_ln_b, w_proj_nsa, w_proj_rwkv, w_out, norm_xattn, norm_mem, xattn_wq, xattn_wkv, xattn_wo, norm_ffn, peer_wq, peer_keys, peer_u, peer_v, norm_final):
    B, S, D = x.shape
    T = B * S
    depth = w_in.shape[0]
    G, n, dh = NSA_GROUPS, NSA_HPG, HEAD_DIM
    nq_cols = G * n * dh
    nkv_cols = 6 * G * dh
    ngate = 3 * G * n
    R = rwkv_w0.shape[1]
    c_q, c_kv, c_g = nq_cols, nq_cols + nkv_cols, nq_cols + nkv_cols + ngate
    c_r = c_g + 3 * R + DECAY_LORA + AAA_LORA + GATE_LORA
    gd = G * dh

    pos_flat = positions.reshape(T)
    rope_tabs = _rope_tables(pos_flat)
    cend = jnp.arange(S // CMP_STRIDE) * CMP_STRIDE + (CMP_LEN - 1)
    ctab = _rope_tables(positions[:, jnp.minimum(cend, S - 1)])
    ident = _rope_tables(jnp.zeros_like(positions[:, :S // CMP_STRIDE]))
    cmp_tabs = [jnp.stack([a, b]) for a, b in zip(ctab, ident)]

    xf = x.reshape(T, D)
    for l in range(depth):
        w = w_in[l]
        kv = w[:, c_q:c_kv]
        w_rope = jnp.concatenate([w[:, :c_q], kv[:, 2 * gd:3 * gd], kv[:, 4 * gd:5 * gd]], axis=1).astype(BF16)
        w_plain = jnp.concatenate([kv[:, 0:2 * gd], kv[:, 3 * gd:4 * gd], kv[:, 5 * gd:6 * gd]], axis=1).astype(BF16)
        w_gate = jnp.pad(w[:, c_kv:c_g].reshape(D, G, 3 * n), ((0, 0), (0, 0), (0, LANES - 3 * n)))
        w_gate = w_gate.reshape(D, G * LANES).astype(BF16)
        w_rkv = w[:, c_g:c_g + 3 * R].astype(BF16)
        wl = w[:, c_g + 3 * R:c_r]
        padl = lambda a, n_: jnp.pad(a, ((0, 0), (0, n_ - a.shape[1])))
        w_lora = jnp.concatenate([padl(wl[:, :DECAY_LORA], LANES), padl(wl[:, DECAY_LORA:DECAY_LORA + AAA_LORA], LANES),
                                  wl[:, DECAY_LORA + AAA_LORA:]], axis=1).astype(BF16)
        w_merge = w[:, c_r:].astype(BF16)

        h = rmsnorm(xf, norm_mix[l], BF16)
        qkr = matmul(h, w_rope, mode="heads_rope", extras=rope_tabs, out_dtype=BF16, seq=S, name="proj_rope")
        kvp = matmul(h, w_plain, mode="heads", out_dtype=BF16, seq=S, name="proj_kv")
        gates_nsa = matmul(h, w_gate, mode="heads_sigmoid", out_dtype=F32, seq=S, name="proj_gate")
        z_rkv = matmul(h, w_rkv, out_dtype=F32, name="proj_rkv")
        z_lora = matmul(h, w_lora, out_dtype=F32, name="proj_lora")
        gates_merge = matmul(h, w_merge, mode="sigmoid", out_dtype=BF16, name="proj_merge")

        pe8 = jnp.broadcast_to(nsa_cmp_pe[l].reshape(2, 1, CMP_LEN * dh), (2, 8, CMP_LEN * dh)).astype(BF16)
        cmp = nsa_compress(kvp, nsa_cmp_w1[l].astype(BF16), nsa_cmp_w2[l].astype(BF16), pe8, cmp_tabs)
        o_nsa = nsa_attention(qkr, kvp, cmp, gates_nsa).reshape(T, nq_cols)

        mu = rwkv_mu[l]
        mul = mu[3 * R:]
        pad1 = lambda a, n_: jnp.pad(a, (0, n_ - a.shape[0]))
        mu_lora = jnp.concatenate([pad1(mul[:DECAY_LORA], LANES), pad1(mul[DECAY_LORA:DECAY_LORA + AAA_LORA], LANES),
                                   mul[DECAY_LORA + AAA_LORA:]]).reshape(1, -1)
        feat = rwkv_features(z_lora.reshape(B, S, -1), mu_lora)
        par = jnp.stack([mu[:R], mu[R:2 * R], mu[2 * R:3 * R], rwkv_w0[l], rwkv_a0[l], rwkv_k_k[l], rwkv_k_a[l],
                         rwkv_r_k[l].reshape(R), rwkv_ln_w[l], rwkv_ln_b[l]])
        par = jnp.pad(par, ((0, 16 - par.shape[0]), (0, 0)))
        w2p = jnp.pad(rwkv_w2[l], ((0, LANES - DECAY_LORA), (0, 0)))
        a2p = jnp.pad(rwkv_a2[l], ((0, LANES - AAA_LORA), (0, 0)))
        o_rwkv = rwkv_scan(z_rkv.reshape(B, S, 3 * R), feat, par, w2p, a2p, rwkv_g2[l]).reshape(T, R)

        merged = merge_mixers(o_nsa, o_rwkv, w_proj_nsa[l].astype(BF16), w_proj_rwkv[l].astype(BF16), gates_merge)
        xf = matmul(merged, w_out[l].astype(BF16), mode="residual", extras=(xf,), out_dtype=F32, name="out_proj")

        h = rmsnorm(xf, norm_xattn[l], BF16)
        m = rmsnorm(mem.reshape(-1, D), norm_mem[l], BF16)
        q = matmul(h, xattn_wq[l].astype(BF16), out_dtype=BF16, name="xattn_q")
        kvm = matmul(m, xattn_wkv[l].astype(BF16), out_dtype=BF16, name="xattn_kv").reshape(B, mem.shape[1], -1)
        xf = cross_attention(q, kvm, xattn_wo[l].astype(BF16), xf, S)

        h = rmsnorm(xf, norm_ffn[l], BF16)
        pq = matmul(h, peer_wq[l].astype(BF16), out_dtype=F32, name="peer_q")
        route = peer_route(pq, peer_keys[l])
        act = peer_activations(h, peer_u[l].astype(BF16), route)
        xf = matmul(act, peer_v[l].astype(BF16), mode="residual", extras=(xf,), out_dtype=F32,
                    tm=1024, tn=1024, tk=1024, name="peer_out")

    return rmsnorm(xf, norm_final, F32).reshape(B, S, D)
```

```python
import functools

import jax
import jax.numpy as jnp
from jax import lax
from jax.experimental import pallas as pl
from jax.experimental.pallas import tpu as pltpu

F32 = jnp.float32
BF16 = jnp.bfloat16
HI = lax.Precision.HIGHEST

LANES = 128
HEAD_DIM = 128
ROPE_DIM = HEAD_DIM // 4
ROPE_THETA = 500000.0
RMS_EPS = 1e-6
NEG_INF = -1e30

NSA_GROUPS = 4
NSA_HPG = 4
CMP_LEN = 32
CMP_STRIDE = 16
SEL_BLOCK = 64
SEL_TOP = 16
WINDOW = 512
Q_BLOCK = 128

RWKV_HEAD = 64
DECAY_LORA = 96
AAA_LORA = 96
GATE_LORA = 256
GN_EPS = 64e-5
RWKV_CHUNK = 128

XATTN_HEADS = 4
PEER_HEADS = 8
PEER_KEYS = 128
PEER_TOPK = 16

VMEM_LIMIT = 56 * 1024 * 1024


def _cparams(sem):
    return pltpu.CompilerParams(dimension_semantics=sem, vmem_limit_bytes=VMEM_LIMIT)


def _gelu(x):
    return 0.5 * x * (1.0 + jnp.tanh(0.7978845608028654 * (x + 0.044715 * x * x * x)))


def _softplus(x):
    return jnp.maximum(x, 0.0) + jnp.log(1.0 + jnp.exp(-jnp.abs(x)))


def _dot(a, b):
    return jnp.dot(a, b, preferred_element_type=F32)


def _dot_nt(a, b):
    return lax.dot_general(a, b, (((1,), (1,)), ((), ())), preferred_element_type=F32)


def _dotf_nt(a, b):
    return lax.dot_general(a, b, (((1,), (1,)), ((), ())), preferred_element_type=F32, precision=HI)


def _split(x):
    hi = x.astype(BF16)
    return hi, (x - hi.astype(F32)).astype(BF16)


def _exact(x):
    return x.astype(BF16), None


_DN = {"nn": (((1,), (0,)), ((), ())), "nt": (((1,), (1,)), ((), ())), "tn": (((0,), (0,)), ((), ()))}


def _dot3(a, b, form="nn"):
    d = lambda x, y: lax.dot_general(x, y, _DN[form], preferred_element_type=F32)
    out = d(a[0], b[0])
    if a[1] is not None:
        out = out + d(a[1], b[0])
    if b[1] is not None:
        out = out + d(a[0], b[1])
    return out


def _rope_apply(a, c, sa, sb):
    return a * c + pltpu.roll(a, LANES - ROPE_DIM // 2, 1) * sa + pltpu.roll(a, ROPE_DIM // 2, 1) * sb


def _rope_tables(pos):
    half = ROPE_DIM // 2
    inv = ROPE_THETA ** (-jnp.arange(half, dtype=F32) / half)
    ang = pos.astype(F32)[..., None] * inv
    cos, sin = jnp.cos(ang), jnp.sin(ang)
    shp = pos.shape + (HEAD_DIM - ROPE_DIM,)
    c = jnp.concatenate([cos, cos, jnp.ones(shp, F32)], axis=-1)
    sa = jnp.concatenate([-sin, jnp.zeros(pos.shape + (HEAD_DIM - half,), F32)], axis=-1)
    sb = jnp.concatenate([jnp.zeros(pos.shape + (half,), F32), sin, jnp.zeros(shp, F32)], axis=-1)
    return c, sa, sb


def _rmsnorm_kernel(x_ref, g_ref, o_ref):
    x = x_ref[...].astype(F32)
    ms = jnp.mean(x * x, axis=-1, keepdims=True)
    o_ref[...] = (x * lax.rsqrt(ms + RMS_EPS) * g_ref[...]).astype(o_ref.dtype)


def rmsnorm(x2d, g, out_dtype, tm=256):
    M, D = x2d.shape
    tm = min(tm, M)
    return pl.pallas_call(
        _rmsnorm_kernel,
        grid=(M // tm,),
        in_specs=[pl.BlockSpec((tm, D), lambda i: (i, 0)), pl.BlockSpec((1, D), lambda i: (0, 0))],
        out_specs=pl.BlockSpec((tm, D), lambda i: (i, 0)),
        out_shape=jax.ShapeDtypeStruct((M, D), out_dtype),
        compiler_params=_cparams(("parallel",)),
        name="rmsnorm",
    )(x2d, g.reshape(1, D).astype(F32))


def _mm_kernel(*refs, nk, mode, n_extra):
    x_ref, w_ref = refs[0], refs[1]
    extras = refs[2:2 + n_extra]
    o_ref = refs[2 + n_extra]

    def epilogue(acc):
        if mode == "plain":
            o_ref[...] = acc.astype(o_ref.dtype)
        elif mode == "sigmoid":
            o_ref[...] = jax.nn.sigmoid(acc).astype(o_ref.dtype)
        elif mode == "residual":
            o_ref[...] = (extras[0][...] + acc).astype(o_ref.dtype)
        elif mode in ("heads", "heads_sigmoid", "heads_rope"):
            nh = acc.shape[1] // LANES
            for j in range(nh):
                a = acc[:, j * LANES:(j + 1) * LANES]
                if mode == "heads_sigmoid":
                    a = jax.nn.sigmoid(a)
                elif mode == "heads_rope":
                    a = _rope_apply(a, extras[0][...], extras[1][...], extras[2][...])
                o_ref[0, j] = a.astype(o_ref.dtype)
        else:
            raise ValueError(mode)

    if nk == 1:
        epilogue(_dot(x_ref[...], w_ref[...]))
    else:
        acc_ref = refs[-1]
        k = pl.program_id(2)

        @pl.when(k == 0)
        def _():
            acc_ref[...] = jnp.zeros_like(acc_ref)

        acc_ref[...] += _dot(x_ref[...], w_ref[...])

        @pl.when(k == nk - 1)
        def _():
            epilogue(acc_ref[...])


def matmul(x, w, *, mode="plain", extras=(), out_dtype=F32, tm=1024, tn=512, tk=None, seq=None, name="mm"):
    M, K = x.shape
    N = w.shape[1]
    tm, tn = min(tm, M), min(tn, N)
    if mode.startswith("heads"):
        tm = min(tm, seq)
    tk = K if tk is None else min(tk, K)
    nk = K // tk
    assert M % tm == 0 and N % tn == 0 and K % tk == 0
    grid = (M // tm, N // tn, nk)
    in_specs = [pl.BlockSpec((tm, tk), lambda i, j, k: (i, k)), pl.BlockSpec((tk, tn), lambda i, j, k: (k, j))]
    if mode == "residual":
        in_specs.append(pl.BlockSpec((tm, tn), lambda i, j, k: (i, j)))
    elif mode == "heads_rope":
        in_specs += [pl.BlockSpec((tm, LANES), lambda i, j, k: (i, 0))] * 3
    if mode.startswith("heads"):
        assert seq % tm == 0 and tn % LANES == 0
        spb = seq // tm
        out_shape = jax.ShapeDtypeStruct((M // seq, N // LANES, seq, LANES), out_dtype)
        out_spec = pl.BlockSpec((1, tn // LANES, tm, LANES), lambda i, j, k: (i // spb, j, i % spb, 0))
    else:
        out_shape = jax.ShapeDtypeStruct((M, N), out_dtype)
        out_spec = pl.BlockSpec((tm, tn), lambda i, j, k: (i, j))
    scratch = [pltpu.VMEM((tm, tn), F32)] if nk > 1 else []
    return pl.pallas_call(
        functools.partial(_mm_kernel, nk=nk, mode=mode, n_extra=len(extras)),
        grid=grid,
        in_specs=in_specs,
        out_specs=out_spec,
        out_shape=out_shape,
        scratch_shapes=scratch,
        compiler_params=_cparams(("parallel", "parallel", "arbitrary")),
        name=name,
    )(x, w, *extras)


def _compress_kernel(x_ref, w1_ref, w2_ref, pe_ref, c_ref, sa_ref, sb_ref, o_ref):
    half = x_ref.shape[-1]
    x = x_ref[0, 0]
    a = _dot(x, w1_ref[0, :half, :])
    b = _dot(x, w1_ref[0, half:, :])
    pet = _dot(pe_ref[0], w1_ref[0])
    n = a.shape[0]
    hid = a + pltpu.roll(b, n - 1, 0) + pet[0:1]
    out = _dot(_gelu(hid).astype(BF16), w2_ref[0])
    out = _rope_apply(out, c_ref[0, 0], sa_ref[0, 0], sb_ref[0, 0])
    o_ref[0, 0, 0] = out.astype(o_ref.dtype)


def nsa_compress(kvp, w1, w2, pe8, tabs):
    B, _, S, dh = kvp.shape
    G = NSA_GROUPS
    nch = S // CMP_STRIDE
    xv = kvp.reshape(B, kvp.shape[1], nch, CMP_STRIDE * dh)
    hidden = w1.shape[-1]
    tab_spec = pl.BlockSpec((1, 1, nch, LANES), lambda w, b, g: (w, b, 0, 0))
    return pl.pallas_call(
        _compress_kernel,
        grid=(2, B, G),
        in_specs=[
            pl.BlockSpec((1, 1, nch, CMP_STRIDE * dh), lambda w, b, g: (b, w * G + g, 0, 0)),
            pl.BlockSpec((1, CMP_LEN * dh, hidden), lambda w, b, g: (w, 0, 0)),
            pl.BlockSpec((1, hidden, dh), lambda w, b, g: (w, 0, 0)),
            pl.BlockSpec((1, 8, CMP_LEN * dh), lambda w, b, g: (w, 0, 0)),
            tab_spec, tab_spec, tab_spec,
        ],
        out_specs=pl.BlockSpec((1, 1, 1, nch, dh), lambda w, b, g: (w, b, g, 0, 0)),
        out_shape=jax.ShapeDtypeStruct((2, B, G, nch, dh), BF16),
        compiler_params=_cparams(("parallel", "parallel", "parallel")),
        name="nsa_compress",
    )(xv, w1, w2, pe8, *tabs)


def _nsa_kernel(q_ref, ks_ref, kw_ref, vs_ref, vw_ref, kc_ref, vc_ref, gt_ref, cmap_ref, e_ref, o_ref, *, ck):
    n = NSA_HPG
    nq = n * Q_BLOCK
    qb = pl.program_id(2)
    t0 = qb * Q_BLOCK
    c2 = (HEAD_DIM ** -0.5) * 1.4426950408889634
    nsel = ks_ref.shape[2] // SEL_BLOCK
    q = q_ref[0].reshape(nq, HEAD_DIM)
    tq = t0 + lax.broadcasted_iota(jnp.int32, (Q_BLOCK, 1), 0)

    def biased(s, ok):
        kc = s.shape[-1]
        bias = jnp.where(ok, 0.0, NEG_INF)
        return (s.reshape(n, Q_BLOCK, kc) + bias[None]).reshape(nq, kc)

    def softmax(sb):
        m = jnp.max(sb, axis=-1, keepdims=True)
        p = jnp.exp2((sb - m) * c2)
        return p / jnp.sum(p, axis=-1, keepdims=True)

    kc = kc_ref[0, 0, 0]
    vc = vc_ref[0, 0, 0]
    ncmp = kc.shape[0]
    cidx = lax.broadcasted_iota(jnp.int32, (Q_BLOCK, ncmp), 1)
    valid_c = (cidx < ncmp - 1) & (cidx * CMP_STRIDE + (CMP_LEN - 1) <= tq)
    p = softmax(biased(_dot_nt(q, kc), valid_c)).reshape(n, Q_BLOCK, ncmp)
    p = p * (tq >= CMP_LEN - 1).astype(F32)[None]
    o_c = _dot(p.reshape(nq, ncmp).astype(BF16), vc)

    psum = p[0] + p[1] + p[2] + p[3]
    hi = psum.astype(BF16)
    lo = (psum - hi.astype(F32)).astype(BF16)
    imp = _dot(hi, cmap_ref[...]) + _dot(lo, cmap_ref[...])
    j = lax.broadcasted_iota(jnp.int32, (Q_BLOCK, LANES), 1)
    cur = tq // SEL_BLOCK
    imp = jnp.where(j == 0, 1e6, imp)
    imp = jnp.where(j == cur, 1e6, imp)
    imp = jnp.where(j == cur - 1, 1e6, imp)
    imp = jnp.where(j > cur, -1e6, imp)
    imp = jnp.where(j >= nsel, -3e6, imp)

    nrow = min(((nsel + 7) // 8) * 8, LANES)
    imp_t = imp.T[0:nrow]
    jr = lax.broadcasted_iota(jnp.int32, (nrow, Q_BLOCK), 0)
    rank = jnp.zeros((nrow, Q_BLOCK), F32)
    for i in range(min(nsel, LANES)):
        row = imp_t[i:i + 1, :]
        rank = rank + jnp.where(row > imp_t, 1.0, jnp.where(row == imp_t, jnp.where(jr > i, 1.0, 0.0), 0.0))
    sel_t = jnp.where(rank < float(SEL_TOP), 1.0, 0.0)
    if nrow < LANES:
        sel_t = jnp.concatenate([sel_t, jnp.zeros((LANES - nrow, Q_BLOCK), F32)], axis=0)
    sel = sel_t.T.astype(BF16)

    n_ch = (t0 + Q_BLOCK + ck - 1) // ck
    ones = jnp.ones((ck, LANES), BF16)

    def body(kb, carry):
        m_i, l_i, acc = carry
        k0 = pl.multiple_of(kb * ck, ck)
        kblk = ks_ref[0, 0, pl.ds(k0, ck), :]
        vblk = vs_ref[0, 0, pl.ds(k0, ck), :]
        em = _dot(sel, e_ref[kb])
        kpos = k0 + lax.broadcasted_iota(jnp.int32, (Q_BLOCK, ck), 1)
        sb = biased(_dot_nt(q, kblk), (kpos <= tq) & (em > 0.5))
        m_new = jnp.maximum(m_i, jnp.max(sb, axis=-1, keepdims=True))
        alpha = jnp.exp2((m_i - m_new) * c2)
        pp = jnp.exp2(((sb - m_new) * c2).astype(BF16))
        l_new = alpha * l_i + _dot(pp, ones)
        acc_new = alpha * acc + _dot(pp, vblk)
        return m_new, l_new, acc_new

    init = (jnp.full((nq, 1), NEG_INF, F32), jnp.zeros((nq, LANES), F32), jnp.zeros((nq, HEAD_DIM), F32))
    _, l_f, acc_f = lax.fori_loop(0, n_ch, body, init)
    o_s = acc_f / l_f

    wlen = min(WINDOW + Q_BLOCK, kw_ref.shape[2])
    w0 = pl.multiple_of(jnp.maximum(t0 + Q_BLOCK - wlen, 0), Q_BLOCK)
    kblk = kw_ref[0, 0, pl.ds(w0, wlen), :]
    vblk = vw_ref[0, 0, pl.ds(w0, wlen), :]
    kpos = w0 + lax.broadcasted_iota(jnp.int32, (Q_BLOCK, wlen), 1)
    pw = softmax(biased(_dot_nt(q, kblk), (kpos <= tq) & (tq - kpos < WINDOW)))
    o_w = _dot(pw.astype(BF16), vblk)

    gt = gt_ref[0, 0]
    for h in range(n):
        r = slice(h * Q_BLOCK, (h + 1) * Q_BLOCK)
        o = (gt[:, 3 * h:3 * h + 1] * o_c[r] + gt[:, 3 * h + 1:3 * h + 2] * o_s[r]
             + gt[:, 3 * h + 2:3 * h + 3] * o_w[r])
        o_ref[0, :, h * HEAD_DIM:(h + 1) * HEAD_DIM] = o.astype(o_ref.dtype)


def nsa_attention(qkr, kvp, cmp, gates, ck=512):
    B, _, S, dh = qkr.shape
    G, n = NSA_GROUPS, NSA_HPG
    ncmp = cmp.shape[3]
    nsel = S // SEL_BLOCK
    ck = min(ck, S)
    r_sel, r_cmp = SEL_BLOCK // CMP_STRIDE, CMP_LEN // CMP_STRIDE
    tgt = (r_sel * jnp.arange(nsel)[:, None, None] + jnp.arange(r_sel)[None, :, None]
           - jnp.arange(r_cmp)[None, None, :])
    cmap = (jnp.arange(ncmp - 1)[:, None, None, None] == tgt[None]).sum((2, 3)).astype(F32)
    cmap = jnp.pad(cmap, ((0, 1), (0, LANES - nsel))).astype(BF16)
    e = (jnp.arange(S)[None, :] // SEL_BLOCK == jnp.arange(LANES)[:, None]).astype(BF16)
    e3 = e.reshape(LANES, S // ck, ck).transpose(1, 0, 2)

    def head(h0):
        return pl.BlockSpec((1, 1, S, dh), lambda b, g, i: (b, h0 + g, 0, 0))

    return pl.pallas_call(
        functools.partial(_nsa_kernel, ck=ck),
        grid=(B, G, S // Q_BLOCK),
        in_specs=[
            pl.BlockSpec((1, n, Q_BLOCK, dh), lambda b, g, i: (b, g, i, 0)),
            head(16), head(20), head(8), head(12),
            pl.BlockSpec((1, 1, 1, ncmp, dh), lambda b, g, i: (0, b, g, 0, 0)),
            pl.BlockSpec((1, 1, 1, ncmp, dh), lambda b, g, i: (1, b, g, 0, 0)),
            pl.BlockSpec((1, 1, Q_BLOCK, LANES), lambda b, g, i: (b, g, i, 0)),
            pl.BlockSpec((ncmp, LANES), lambda b, g, i: (0, 0)),
            pl.BlockSpec((S // ck, LANES, ck), lambda b, g, i: (0, 0, 0)),
        ],
        out_specs=pl.BlockSpec((1, Q_BLOCK, n * dh), lambda b, g, i: (b, i, g)),
        out_shape=jax.ShapeDtypeStruct((B, S, G * n * dh), BF16),
        compiler_params=_cparams(("parallel", "parallel", "arbitrary")),
        name="nsa_attention",
    )(qkr, qkr, qkr, kvp, kvp, cmp, cmp, gates, cmap, e3)


def _rwkv_feat_kernel(z_ref, zp_ref, mu_ref, o_ref):
    i = pl.program_id(1)
    z = z_ref[0]
    prev = zp_ref[0, 7:8, :] * (i > 0).astype(F32)
    row = lax.broadcasted_iota(jnp.int32, z.shape, 0)
    sh = jnp.where(row == 0, prev, pltpu.roll(z, 1, 0))
    zs = z + (sh - z) * mu_ref[...]
    o_ref[0, :, 0:128] = jnp.tanh(zs[:, 0:128])
    o_ref[0, :, 128:256] = zs[:, 128:256]
    o_ref[0, :, 256:] = jax.nn.sigmoid(zs[:, 256:])


def rwkv_features(z_lora, mu_lora, tm=512):
    B, S, W = z_lora.shape
    tm = min(tm, S)
    return pl.pallas_call(
        _rwkv_feat_kernel,
        grid=(B, S // tm),
        in_specs=[pl.BlockSpec((1, tm, W), lambda b, i: (b, i, 0)),
                  pl.BlockSpec((1, 8, W), lambda b, i: (b, jnp.maximum(i * (tm // 8) - 1, 0), 0)),
                  pl.BlockSpec((1, W), lambda b, i: (0, 0))],
        out_specs=pl.BlockSpec((1, tm, W), lambda b, i: (b, i, 0)),
        out_shape=jax.ShapeDtypeStruct((B, S, W), F32),
        compiler_params=_cparams(("parallel", "arbitrary")),
        name="rwkv_features",
    )(z_lora, z_lora, mu_lora)


def _rwkv_kernel(r_ref, k_ref, v_ref, f_ref, par_ref, w2_ref, a2_ref, g2_ref, o_ref, st_ref, prev_ref):
    C = RWKV_CHUNK
    W = r_ref.shape[-1]
    npair = W // LANES
    c = pl.program_id(2)

    @pl.when(c == 0)
    def _():
        st_ref[...] = jnp.zeros_like(st_ref)
        prev_ref[...] = jnp.zeros_like(prev_ref)

    par = par_ref[...]
    mu_r, mu_k, mu_v = par[0:1], par[1:2], par[2:3]
    w0, a0, k_k, k_a, r_k, ln_w, ln_b = par[3:4], par[4:5], par[5:6], par[6:7], par[7:8], par[8:9], par[9:10]

    row = lax.broadcasted_iota(jnp.int32, (C, C), 0)
    col = lax.broadcasted_iota(jnp.int32, (C, C), 1)
    trow = lax.broadcasted_iota(jnp.int32, (C, W), 0)
    lane = lax.broadcasted_iota(jnp.int32, (C, LANES), 1)
    lr = lax.broadcasted_iota(jnp.int32, (LANES, LANES), 0)
    lc = lax.broadcasted_iota(jnp.int32, (LANES, LANES), 1)

    def shifted(z_ref, idx, mu):
        z = z_ref[0]
        sh = jnp.where(trow == 0, prev_ref[idx:idx + 1, :], pltpu.roll(z, 1, 0))
        prev_ref[idx:idx + 1, :] = z[C - 1:C, :]
        return z + (sh - z) * mu

    r = shifted(r_ref, 0, mu_r)
    k = shifted(k_ref, 1, mu_k)
    v = shifted(v_ref, 2, mu_v)
    f = f_ref[0]
    w_log = -_softplus(-(w0 + _dot3(_split(f[:, 0:128]), _split(w2_ref[...])))) - 0.5
    logd = -jnp.exp(w_log)
    a = jax.nn.sigmoid(a0 + _dot3(_split(f[:, 128:256]), _split(a2_ref[...])))
    g = _dot3(_split(f[:, 256:]), _split(g2_ref[...]))

    bd = _exact(jnp.where((lr // RWKV_HEAD) == (lc // RWKV_HEAD), 1.0, 0.0))

    def head_sum(x):
        return jnp.concatenate([_dot3(_split(x[:, p * LANES:(p + 1) * LANES]), bd) for p in range(npair)], axis=1)

    kk = k * k_k
    kk = kk / jnp.maximum(jnp.sqrt(head_sum(kk * kk)), 1e-12)
    k2 = k * (1.0 + (a - 1.0) * k_a)

    tri_incl = _exact(jnp.where(col <= row, 1.0, 0.0))
    cum = _dot3(tri_incl, _split(logd))
    cum_end = cum[C - 1:C, :]
    e_neg = jnp.exp(-cum)
    kka = kk * a
    a_t = -kk * jnp.exp(cum - logd)
    b_t = kka * e_neg
    k_t = k2 * e_neg
    r_t = r * jnp.exp(cum)
    e_end = jnp.exp(cum_end - cum)
    b_e = kka * e_end
    k_e = k2 * e_end
    p_end = jnp.exp(cum_end)

    strict = col < row
    incl = col <= row
    eye = jnp.where(col == row, 1.0, 0.0)
    n_sq = C.bit_length() - 2
    hpp = LANES // RWKV_HEAD
    heads = [(p, h) for p in range(npair) for h in range(hpp)]
    hd = []
    for p, h in heads:
        sl = slice(p * LANES, (p + 1) * LANES)
        mh = (lane // RWKV_HEAD) == h
        s_am = _split(jnp.where(mh, a_t[:, sl], 0.0))
        s_rm = _split(jnp.where(mh, r_t[:, sl], 0.0))
        hd.append(dict(
            sl=sl, s_am=s_am, s_rm=s_rm,
            s_vm=_split(jnp.where(mh, v[:, sl], 0.0)),
            s_bem=_split(jnp.where(mh, b_e[:, sl], 0.0)),
            s_kem=_split(jnp.where(mh, k_e[:, sl], 0.0)),
            lhs=(jnp.concatenate([s_am[0], s_rm[0]], axis=0), jnp.concatenate([s_am[1], s_rm[1]], axis=0)),
            rhs=_split(jnp.concatenate([b_t[:, sl], k_t[:, sl]], axis=0)) if h == 0 else hd[-1]["rhs"]))
    for d in hd:
        amat = _dot3(d["lhs"], d["rhs"], "nt")
        l_ab = jnp.where(strict, amat[0:C, 0:C], 0.0)
        d["l_ak"] = _split(jnp.where(strict, amat[0:C, C:2 * C], 0.0))
        d["l_rb"] = jnp.where(incl, amat[C:2 * C, 0:C], 0.0).astype(BF16)
        d["l_rk"] = jnp.where(incl, amat[C:2 * C, C:2 * C], 0.0).astype(BF16)
        d["x"] = eye + l_ab
        d["s_pw"] = _split(l_ab)
    for _ in range(n_sq):
        for d in hd:
            d["s_pw"] = _split(_dot3(d["s_pw"], d["s_pw"]))
        for d in hd:
            d["x"] = d["x"] + _dot3(_split(d["x"]), d["s_pw"])
    for i, d in enumerate(hd):
        d["st"] = st_ref[i]
        d["s_st"] = _split(d["st"])
        d["t"] = _split(_dot3(d["s_am"], d["s_st"], "nt") + _dot3(d["l_ak"], d["s_vm"]))
    for d in hd:
        d["s_u"] = _split(_dot3(_split(d["x"]), d["t"]))
    for i, d in enumerate(hd):
        st_ref[i] = d["st"] * p_end[:, d["sl"]] + _dot3(d["s_u"], d["s_bem"], "tn") + _dot3(d["s_vm"], d["s_kem"], "tn")
    ys = []
    for d in hd:
        ys.append(_dot_nt(d["s_rm"][0], d["s_st"][0]) + _dot(d["l_rb"], d["s_u"][0]) + _dot(d["l_rk"], d["s_vm"][0]))
    y = jnp.concatenate([sum(ys[hpp * p + 1:hpp * (p + 1)], ys[hpp * p]) for p in range(npair)], axis=1)

    inv_n = 1.0 / RWKV_HEAD
    mean = head_sum(y) * inv_n
    d = y - mean
    var = head_sum(d * d) * inv_n
    yn = d * lax.rsqrt(var + GN_EPS) * ln_w + ln_b
    bonus = head_sum(r * k2 * r_k) * v
    o_ref[0] = ((yn + bonus) * g).astype(o_ref.dtype)


def rwkv_scan(z_rkv, feat, par, w2p, a2p, g2, pairs_per_step=4):
    B, S, R3 = z_rkv.shape
    R = R3 // 3
    W = pairs_per_step * LANES
    nblk = R // W
    C = RWKV_CHUNK
    assert S % C == 0 and R % W == 0

    def col(off):
        return pl.BlockSpec((1, C, W), lambda b, p, c: (b, c, off + p))

    return pl.pallas_call(
        _rwkv_kernel,
        grid=(B, nblk, S // C),
        in_specs=[
            col(0), col(nblk), col(2 * nblk),
            pl.BlockSpec((1, C, feat.shape[-1]), lambda b, p, c: (b, c, 0)),
            pl.BlockSpec((16, W), lambda b, p, c: (0, p)),
            pl.BlockSpec((LANES, W), lambda b, p, c: (0, p)),
            pl.BlockSpec((LANES, W), lambda b, p, c: (0, p)),
            pl.BlockSpec((GATE_LORA, W), lambda b, p, c: (0, p)),
        ],
        out_specs=pl.BlockSpec((1, C, W), lambda b, p, c: (b, c, p)),
        out_shape=jax.ShapeDtypeStruct((B, S, R), BF16),
        scratch_shapes=[pltpu.VMEM((W // RWKV_HEAD, LANES, LANES), F32), pltpu.VMEM((8, W), F32)],
        compiler_params=_cparams(("parallel", "parallel", "arbitrary")),
        name="rwkv_scan",
    )(z_rkv, z_rkv, z_rkv, feat, par, w2p, a2p, g2)


def _merge_kernel(on_ref, or_ref, wn_ref, wr_ref, ga_ref, gb_ref, o_ref):
    pa = _dot(on_ref[...], wn_ref[...])
    pb = _dot(or_ref[...], wr_ref[...])
    o_ref[...] = (ga_ref[...].astype(F32) * pa + gb_ref[...].astype(F32) * pb).astype(o_ref.dtype)


def merge_mixers(o_nsa, o_rwkv, wn, wr, gates, tm=1024, tn=512):
    M, Kn = o_nsa.shape
    Kr = o_rwkv.shape[1]
    D = wn.shape[1]
    tm, tn = min(tm, M), min(tn, D)
    nb = D // tn
    return pl.pallas_call(
        _merge_kernel,
        grid=(M // tm, nb),
        in_specs=[
            pl.BlockSpec((tm, Kn), lambda i, j: (i, 0)),
            pl.BlockSpec((tm, Kr), lambda i, j: (i, 0)),
            pl.BlockSpec((Kn, tn), lambda i, j: (0, j)),
            pl.BlockSpec((Kr, tn), lambda i, j: (0, j)),
            pl.BlockSpec((tm, tn), lambda i, j: (i, j)),
            pl.BlockSpec((tm, tn), lambda i, j: (i, nb + j)),
        ],
        out_specs=pl.BlockSpec((tm, tn), lambda i, j: (i, j)),
        out_shape=jax.ShapeDtypeStruct((M, D), BF16),
        compiler_params=_cparams(("parallel", "parallel")),
        name="merge_mixers",
    )(o_nsa, o_rwkv, wn, wr, gates, gates)


def _xattn_kernel(q_ref, kv_ref, wo_ref, x_ref, o_ref):
    nh = XATTN_HEADS
    scale = HEAD_DIM ** -0.5
    outs = []
    for h in range(nh):
        q = q_ref[:, h * HEAD_DIM:(h + 1) * HEAD_DIM]
        k = kv_ref[0, :, h * HEAD_DIM:(h + 1) * HEAD_DIM]
        v = kv_ref[0, :, (nh + h) * HEAD_DIM:(nh + h + 1) * HEAD_DIM]
        s = _dot_nt(q, k) * scale
        m = jnp.max(s, axis=-1, keepdims=True)
        p = jnp.exp(s - m)
        p = p / jnp.sum(p, axis=-1, keepdims=True)
        outs.append(_dot(p.astype(BF16), v).astype(BF16))
    o = jnp.concatenate(outs, axis=-1)
    o_ref[...] = x_ref[...] + _dot(o, wo_ref[...])


def cross_attention(q, kv, wo, x, seq, tm=256):
    T, D = x.shape
    tm = min(tm, seq)
    spb = seq // tm
    Mm = kv.shape[1]
    return pl.pallas_call(
        _xattn_kernel,
        grid=(T // tm,),
        in_specs=[
            pl.BlockSpec((tm, q.shape[1]), lambda i: (i, 0)),
            pl.BlockSpec((1, Mm, kv.shape[2]), lambda i: (i // spb, 0, 0)),
            pl.BlockSpec(wo.shape, lambda i: (0, 0)),
            pl.BlockSpec((tm, D), lambda i: (i, 0)),
        ],
        out_specs=pl.BlockSpec((tm, D), lambda i: (i, 0)),
        out_shape=jax.ShapeDtypeStruct((T, D), F32),
        compiler_params=_cparams(("parallel",)),
        name="cross_attention",
    )(q, kv, wo, x)


def _top16(x):
    n = x.shape[0]
    iota = lax.broadcasted_iota(jnp.int32, x.shape, 0).astype(F32)
    cur = x
    rank = jnp.full(x.shape, float(PEER_TOPK), F32)
    vals = []
    for r in range(PEER_TOPK):
        m = jnp.max(cur, axis=0, keepdims=True)
        idx = jnp.min(jnp.where(cur == m, iota, float(n)), axis=0, keepdims=True)
        hit = iota == idx
        rank = jnp.where(hit, float(r), rank)
        cur = jnp.where(hit, -jnp.inf, cur)
        vals.append(m)
    return jnp.concatenate(vals, axis=0), rank


def _peer_route_kernel(q_ref, keys_ref, row_ref, pl_ref):
    K = PEER_TOPK
    q = q_ref[...]
    half = q.shape[1] // 2
    s1 = _dotf_nt(keys_ref[0, 0], q[:, :half])
    s2 = _dotf_nt(keys_ref[0, 1], q[:, half:])
    v1, rank1 = _top16(s1)
    v2, rank2 = _top16(s2)
    cand = jnp.concatenate([v1[i:i + 1] + v2 for i in range(K)], axis=0)
    top_s, crank = _top16(cand)
    chosen = jnp.where(crank < float(K), 1.0, 0.0)
    z = jnp.sum(jnp.exp(top_s - top_s[0:1]), axis=0, keepdims=True)
    jn = jnp.zeros_like(s1)
    for i in range(K):
        j_i = jnp.sum(chosen[i * K:(i + 1) * K], axis=0, keepdims=True)
        jn = jnp.where(rank1 == float(i), j_i, jn)
    row_ref[0, 0] = jnp.exp(s1 - v1[0:1]) / z
    row_ref[0, 1] = jn
    pl_ref[0, 0] = jnp.exp(s2 - v2[0:1]).astype(pl_ref.dtype)
    pl_ref[0, 1] = rank2.astype(pl_ref.dtype)


def peer_route(q, keys, tm=256):
    T = q.shape[0]
    H, _, nkeys, hd = keys.shape
    tm = min(tm, T)
    spec = pl.BlockSpec((1, 2, nkeys, tm), lambda i, h: (h, 0, 0, i))
    return pl.pallas_call(
        _peer_route_kernel,
        grid=(T // tm, H),
        in_specs=[pl.BlockSpec((tm, 2 * hd), lambda i, h: (i, h)),
                  pl.BlockSpec((1, 2, nkeys, hd), lambda i, h: (h, 0, 0, 0))],
        out_specs=[spec, spec],
        out_shape=[jax.ShapeDtypeStruct((H, 2, nkeys, T), F32), jax.ShapeDtypeStruct((H, 2, nkeys, T), BF16)],
        compiler_params=_cparams(("parallel", "parallel")),
        name="peer_route",
    )(q, keys)


def _peer_act_kernel(h_ref, u_ref, row_ref, pl_ref, o_ref):
    nk = PEER_KEYS
    ec = u_ref.shape[0]
    j = pl.program_id(1)
    hu = _dot_nt(u_ref[...], h_ref[...])
    zero = jnp.zeros((), o_ref.dtype)
    for al in range(ec // nk):
        a = j * (ec // nk) + al
        w = None
        for hd in range(row_ref.shape[0]):
            e1 = row_ref[hd, 0, pl.ds(a, 1), :].astype(o_ref.dtype)
            jn = row_ref[hd, 1, pl.ds(a, 1), :].astype(o_ref.dtype)
            g = jnp.where(pl_ref[hd, 1] < jn, e1 * pl_ref[hd, 0], zero)
            w = g if w is None else w + g
        act = _gelu(hu[al * nk:(al + 1) * nk, :].astype(o_ref.dtype)) * w
        o_ref[:, al * nk:(al + 1) * nk] = act.T


def peer_activations(h, u_tab, rows, planes, tm=512, ec=512):
    T, D = h.shape
    E = u_tab.shape[0]
    tm, ec = min(tm, T), min(ec, E)
    H, _, nkeys, _ = rows.shape
    rspec = pl.BlockSpec((H, 2, nkeys, tm), lambda i, j: (0, 0, 0, i))
    return pl.pallas_call(
        _peer_act_kernel,
        grid=(T // tm, E // ec),
        in_specs=[pl.BlockSpec((tm, D), lambda i, j: (i, 0)),
                  pl.BlockSpec((ec, D), lambda i, j: (j, 0)),
                  rspec, rspec],
        out_specs=pl.BlockSpec((tm, ec), lambda i, j: (i, j)),
        out_shape=jax.ShapeDtypeStruct((T, E), BF16),
        compiler_params=_cparams(("parallel", "arbitrary")),
        name="peer_activations",
    )(h, u_tab, rows, planes)


def kernel(x, mem, positions, norm_mix, w_in, nsa_cmp_pe, nsa_cmp_w1, nsa_cmp_w2, rwkv_mu, rwkv_w0, rwkv_w2, rwkv_a0, rwkv_a2, rwkv_g2, rwkv_k_k, rwkv_k_a, rwkv_r_k, rwkv_ln_w, rwkv_ln_b, w_proj_nsa, w_proj_rwkv, w_out, norm_xattn, norm_mem, xattn_wq, xattn_wkv, xattn_wo, norm_ffn, peer_wq, peer_keys, peer_u, peer_v, norm_final):
    B, S, D = x.shape
    T = B * S
    depth = w_in.shape[0]
    G, n, dh = NSA_GROUPS, NSA_HPG, HEAD_DIM
    nq_cols = G * n * dh
    nkv_cols = 6 * G * dh
    ngate = 3 * G * n
    R = rwkv_w0.shape[1]
    c_q, c_kv, c_g = nq_cols, nq_cols + nkv_cols, nq_cols + nkv_cols + ngate
    c_r = c_g + 3 * R + DECAY_LORA + AAA_LORA + GATE_LORA
    gd = G * dh

    pos_flat = positions.reshape(T)
    rope_tabs = _rope_tables(pos_flat)
    cend = jnp.arange(S // CMP_STRIDE) * CMP_STRIDE + (CMP_LEN - 1)
    ctab = _rope_tables(positions[:, jnp.minimum(cend, S - 1)])
    ident = _rope_tables(jnp.zeros_like(positions[:, :S // CMP_STRIDE]))
    cmp_tabs = [jnp.stack([a, b]) for a, b in zip(ctab, ident)]

    xf = x.reshape(T, D)
    for l in range(depth):
        w = w_in[l]
        kv = w[:, c_q:c_kv]
        w_rope = jnp.concatenate([w[:, :c_q], kv[:, 2 * gd:3 * gd], kv[:, 4 * gd:5 * gd]], axis=1).astype(BF16)
        w_plain = jnp.concatenate([kv[:, 0:2 * gd], kv[:, 3 * gd:4 * gd], kv[:, 5 * gd:6 * gd]], axis=1).astype(BF16)
        w_gate = jnp.pad(w[:, c_kv:c_g].reshape(D, G, 3 * n), ((0, 0), (0, 0), (0, LANES - 3 * n)))
        w_gate = w_gate.reshape(D, G * LANES).astype(BF16)
        w_rkv = w[:, c_g:c_g + 3 * R].astype(BF16)
        wl = w[:, c_g + 3 * R:c_r]
        padl = lambda a, n_: jnp.pad(a, ((0, 0), (0, n_ - a.shape[1])))
        w_lora = jnp.concatenate([padl(wl[:, :DECAY_LORA], LANES), padl(wl[:, DECAY_LORA:DECAY_LORA + AAA_LORA], LANES),
                                  wl[:, DECAY_LORA + AAA_LORA:]], axis=1).astype(BF16)
        w_merge = w[:, c_r:].astype(BF16)

        h = rmsnorm(xf, norm_mix[l], BF16)
        qkr = matmul(h, w_rope, mode="heads_rope", extras=rope_tabs, out_dtype=BF16, seq=S, name="proj_rope")
        kvp = matmul(h, w_plain, mode="heads", out_dtype=BF16, seq=S, name="proj_kv")
        gates_nsa = matmul(h, w_gate, mode="heads_sigmoid", out_dtype=F32, seq=S, name="proj_gate")
        z_rkv = matmul(h, w_rkv, out_dtype=F32, name="proj_rkv")
        z_lora = matmul(h, w_lora, out_dtype=F32, name="proj_lora")
        gates_merge = matmul(h, w_merge, mode="sigmoid", out_dtype=BF16, name="proj_merge")

        pe8 = jnp.broadcast_to(nsa_cmp_pe[l].reshape(2, 1, CMP_LEN * dh), (2, 8, CMP_LEN * dh)).astype(BF16)
        cmp = nsa_compress(kvp, nsa_cmp_w1[l].astype(BF16), nsa_cmp_w2[l].astype(BF16), pe8, cmp_tabs)
        o_nsa = nsa_attention(qkr, kvp, cmp, gates_nsa).reshape(T, nq_cols)

        mu = rwkv_mu[l]
        mul = mu[3 * R:]
        pad1 = lambda a, n_: jnp.pad(a, (0, n_ - a.shape[0]))
        mu_lora = jnp.concatenate([pad1(mul[:DECAY_LORA], LANES), pad1(mul[DECAY_LORA:DECAY_LORA + AAA_LORA], LANES),
                                   mul[DECAY_LORA + AAA_LORA:]]).reshape(1, -1)
        feat = rwkv_features(z_lora.reshape(B, S, -1), mu_lora)
        par = jnp.stack([mu[:R], mu[R:2 * R], mu[2 * R:3 * R], rwkv_w0[l], rwkv_a0[l], rwkv_k_k[l], rwkv_k_a[l],
                         rwkv_r_k[l].reshape(R), rwkv_ln_w[l], rwkv_ln_b[l]])
        par = jnp.pad(par, ((0, 16 - par.shape[0]), (0, 0)))
        w2p = jnp.pad(rwkv_w2[l], ((0, LANES - DECAY_LORA), (0, 0)))
        a2p = jnp.pad(rwkv_a2[l], ((0, LANES - AAA_LORA), (0, 0)))
        o_rwkv = rwkv_scan(z_rkv.reshape(B, S, 3 * R), feat, par, w2p, a2p, rwkv_g2[l]).reshape(T, R)

        merged = merge_mixers(o_nsa, o_rwkv, w_proj_nsa[l].astype(BF16), w_proj_rwkv[l].astype(BF16), gates_merge)
        xf = matmul(merged, w_out[l].astype(BF16), mode="residual", extras=(xf,), out_dtype=F32, name="out_proj")

        h = rmsnorm(xf, norm_xattn[l], BF16)
        m = rmsnorm(mem.reshape(-1, D), norm_mem[l], BF16)
        q = matmul(h, xattn_wq[l].astype(BF16), out_dtype=BF16, name="xattn_q")
        kvm = matmul(m, xattn_wkv[l].astype(BF16), out_dtype=BF16, name="xattn_kv").reshape(B, mem.shape[1], -1)
        xf = cross_attention(q, kvm, xattn_wo[l].astype(BF16), xf, S)

        h = rmsnorm(xf, norm_ffn[l], BF16)
        pq = matmul(h, peer_wq[l].astype(BF16), out_dtype=F32, name="peer_q")
        rows, planes = peer_route(pq, peer_keys[l])
        act = peer_activations(h, peer_u[l].astype(BF16), rows, planes)
        xf = matmul(act, peer_v[l].astype(BF16), mode="residual", extras=(xf,), out_dtype=F32,
                    tm=1024, tn=1024, tk=1024, name="peer_out")

    return rmsnorm(xf, norm_final, F32).reshape(B, S, D)
```

```python
import functools

import jax
import jax.numpy as jnp
from jax import lax
from jax.experimental import pallas as pl
from jax.experimental.pallas import tpu as pltpu

F32 = jnp.float32
BF16 = jnp.bfloat16
HI = lax.Precision.HIGHEST

LANES = 128
HEAD_DIM = 128
ROPE_DIM = HEAD_DIM // 4
ROPE_THETA = 500000.0
RMS_EPS = 1e-6
NEG_INF = -1e30

NSA_GROUPS = 4
NSA_HPG = 4
CMP_LEN = 32
CMP_STRIDE = 16
SEL_BLOCK = 64
SEL_TOP = 16
WINDOW = 512
Q_BLOCK = 128

RWKV_HEAD = 64
DECAY_LORA = 96
AAA_LORA = 96
GATE_LORA = 256
GN_EPS = 64e-5
RWKV_CHUNK = 128

XATTN_HEADS = 4
PEER_HEADS = 8
PEER_KEYS = 128
PEER_TOPK = 16

VMEM_LIMIT = 56 * 1024 * 1024


def _cparams(sem):
    return pltpu.CompilerParams(dimension_semantics=sem, vmem_limit_bytes=VMEM_LIMIT)


def _gelu(x):
    return 0.5 * x * (1.0 + jnp.tanh(0.7978845608028654 * (x + 0.044715 * x * x * x)))


def _softplus(x):
    return jnp.maximum(x, 0.0) + jnp.log(1.0 + jnp.exp(-jnp.abs(x)))


def _dot(a, b):
    return jnp.dot(a, b, preferred_element_type=F32)


def _dot_nt(a, b):
    return lax.dot_general(a, b, (((1,), (1,)), ((), ())), preferred_element_type=F32)


def _dotf_nt(a, b):
    return lax.dot_general(a, b, (((1,), (1,)), ((), ())), preferred_element_type=F32, precision=HI)


def _split(x):
    hi = x.astype(BF16)
    return hi, (x - hi.astype(F32)).astype(BF16)


def _exact(x):
    return x.astype(BF16), None


_DN = {"nn": (((1,), (0,)), ((), ())), "nt": (((1,), (1,)), ((), ())), "tn": (((0,), (0,)), ((), ()))}


def _dot3(a, b, form="nn"):
    d = lambda x, y: lax.dot_general(x, y, _DN[form], preferred_element_type=F32)
    out = d(a[0], b[0])
    if a[1] is not None:
        out = out + d(a[1], b[0])
    if b[1] is not None:
        out = out + d(a[0], b[1])
    return out


def _rope_apply(a, c, sa, sb):
    return a * c + pltpu.roll(a, LANES - ROPE_DIM // 2, 1) * sa + pltpu.roll(a, ROPE_DIM // 2, 1) * sb


def _rope_tables(pos):
    half = ROPE_DIM // 2
    inv = ROPE_THETA ** (-jnp.arange(half, dtype=F32) / half)
    ang = pos.astype(F32)[..., None] * inv
    cos, sin = jnp.cos(ang), jnp.sin(ang)
    shp = pos.shape + (HEAD_DIM - ROPE_DIM,)
    c = jnp.concatenate([cos, cos, jnp.ones(shp, F32)], axis=-1)
    sa = jnp.concatenate([-sin, jnp.zeros(pos.shape + (HEAD_DIM - half,), F32)], axis=-1)
    sb = jnp.concatenate([jnp.zeros(pos.shape + (half,), F32), sin, jnp.zeros(shp, F32)], axis=-1)
    return c, sa, sb


def _rmsnorm_kernel(x_ref, g_ref, o_ref):
    x = x_ref[...].astype(F32)
    ms = jnp.mean(x * x, axis=-1, keepdims=True)
    o_ref[...] = (x * lax.rsqrt(ms + RMS_EPS) * g_ref[...]).astype(o_ref.dtype)


def rmsnorm(x2d, g, out_dtype, tm=256):
    M, D = x2d.shape
    tm = min(tm, M)
    return pl.pallas_call(
        _rmsnorm_kernel,
        grid=(M // tm,),
        in_specs=[pl.BlockSpec((tm, D), lambda i: (i, 0)), pl.BlockSpec((1, D), lambda i: (0, 0))],
        out_specs=pl.BlockSpec((tm, D), lambda i: (i, 0)),
        out_shape=jax.ShapeDtypeStruct((M, D), out_dtype),
        compiler_params=_cparams(("parallel",)),
        name="rmsnorm",
    )(x2d, g.reshape(1, D).astype(F32))


def _mm_kernel(*refs, nk, mode, n_extra):
    x_ref, w_ref = refs[0], refs[1]
    extras = refs[2:2 + n_extra]
    o_ref = refs[2 + n_extra]

    def epilogue(acc):
        if mode == "plain":
            o_ref[...] = acc.astype(o_ref.dtype)
        elif mode == "sigmoid":
            o_ref[...] = jax.nn.sigmoid(acc).astype(o_ref.dtype)
        elif mode == "residual":
            o_ref[...] = (extras[0][...] + acc).astype(o_ref.dtype)
        elif mode in ("heads", "heads_sigmoid", "heads_rope"):
            nh = acc.shape[1] // LANES
            for j in range(nh):
                a = acc[:, j * LANES:(j + 1) * LANES]
                if mode == "heads_sigmoid":
                    a = jax.nn.sigmoid(a)
                elif mode == "heads_rope":
                    a = _rope_apply(a, extras[0][...], extras[1][...], extras[2][...])
                o_ref[0, j] = a.astype(o_ref.dtype)
        else:
            raise ValueError(mode)

    if nk == 1:
        epilogue(_dot(x_ref[...], w_ref[...]))
    else:
        acc_ref = refs[-1]
        k = pl.program_id(2)

        @pl.when(k == 0)
        def _():
            acc_ref[...] = jnp.zeros_like(acc_ref)

        acc_ref[...] += _dot(x_ref[...], w_ref[...])

        @pl.when(k == nk - 1)
        def _():
            epilogue(acc_ref[...])


def matmul(x, w, *, mode="plain", extras=(), out_dtype=F32, tm=1024, tn=512, tk=None, seq=None, name="mm"):
    M, K = x.shape
    N = w.shape[1]
    tm, tn = min(tm, M), min(tn, N)
    if mode.startswith("heads"):
        tm = min(tm, seq)
    tk = K if tk is None else min(tk, K)
    nk = K // tk
    assert M % tm == 0 and N % tn == 0 and K % tk == 0
    grid = (M // tm, N // tn, nk)
    in_specs = [pl.BlockSpec((tm, tk), lambda i, j, k: (i, k)), pl.BlockSpec((tk, tn), lambda i, j, k: (k, j))]
    if mode == "residual":
        in_specs.append(pl.BlockSpec((tm, tn), lambda i, j, k: (i, j)))
    elif mode == "heads_rope":
        in_specs += [pl.BlockSpec((tm, LANES), lambda i, j, k: (i, 0))] * 3
    if mode.startswith("heads"):
        assert seq % tm == 0 and tn % LANES == 0
        spb = seq // tm
        out_shape = jax.ShapeDtypeStruct((M // seq, N // LANES, seq, LANES), out_dtype)
        out_spec = pl.BlockSpec((1, tn // LANES, tm, LANES), lambda i, j, k: (i // spb, j, i % spb, 0))
    else:
        out_shape = jax.ShapeDtypeStruct((M, N), out_dtype)
        out_spec = pl.BlockSpec((tm, tn), lambda i, j, k: (i, j))
    scratch = [pltpu.VMEM((tm, tn), F32)] if nk > 1 else []
    return pl.pallas_call(
        functools.partial(_mm_kernel, nk=nk, mode=mode, n_extra=len(extras)),
        grid=grid,
        in_specs=in_specs,
        out_specs=out_spec,
        out_shape=out_shape,
        scratch_shapes=scratch,
        compiler_params=_cparams(("parallel", "parallel", "arbitrary")),
        name=name,
    )(x, w, *extras)


def _compress_kernel(x_ref, w1_ref, w2_ref, pe_ref, c_ref, sa_ref, sb_ref, o_ref):
    half = x_ref.shape[-1]
    x = x_ref[0, 0]
    a = _dot(x, w1_ref[0, :half, :])
    b = _dot(x, w1_ref[0, half:, :])
    pet = _dot(pe_ref[0], w1_ref[0])
    n = a.shape[0]
    hid = a + pltpu.roll(b, n - 1, 0) + pet[0:1]
    out = _dot(_gelu(hid).astype(BF16), w2_ref[0])
    out = _rope_apply(out, c_ref[0, 0], sa_ref[0, 0], sb_ref[0, 0])
    o_ref[0, 0, 0] = out.astype(o_ref.dtype)


def nsa_compress(kvp, w1, w2, pe8, tabs):
    B, _, S, dh = kvp.shape
    G = NSA_GROUPS
    nch = S // CMP_STRIDE
    xv = kvp.reshape(B, kvp.shape[1], nch, CMP_STRIDE * dh)
    hidden = w1.shape[-1]
    tab_spec = pl.BlockSpec((1, 1, nch, LANES), lambda w, b, g: (w, b, 0, 0))
    return pl.pallas_call(
        _compress_kernel,
        grid=(2, B, G),
        in_specs=[
            pl.BlockSpec((1, 1, nch, CMP_STRIDE * dh), lambda w, b, g: (b, w * G + g, 0, 0)),
            pl.BlockSpec((1, CMP_LEN * dh, hidden), lambda w, b, g: (w, 0, 0)),
            pl.BlockSpec((1, hidden, dh), lambda w, b, g: (w, 0, 0)),
            pl.BlockSpec((1, 8, CMP_LEN * dh), lambda w, b, g: (w, 0, 0)),
            tab_spec, tab_spec, tab_spec,
        ],
        out_specs=pl.BlockSpec((1, 1, 1, nch, dh), lambda w, b, g: (w, b, g, 0, 0)),
        out_shape=jax.ShapeDtypeStruct((2, B, G, nch, dh), BF16),
        compiler_params=_cparams(("parallel", "parallel", "parallel")),
        name="nsa_compress",
    )(xv, w1, w2, pe8, *tabs)


def _nsa_kernel(q_ref, ks_ref, kw_ref, vs_ref, vw_ref, kc_ref, vc_ref, gt_ref, cmap_ref, e_ref, o_ref, *, ck):
    n = NSA_HPG
    nq = n * Q_BLOCK
    qb = pl.program_id(2)
    t0 = qb * Q_BLOCK
    c2 = (HEAD_DIM ** -0.5) * 1.4426950408889634
    nsel = ks_ref.shape[2] // SEL_BLOCK
    q = q_ref[0].reshape(nq, HEAD_DIM)
    tq = t0 + lax.broadcasted_iota(jnp.int32, (Q_BLOCK, 1), 0)

    def biased(s, ok):
        kc = s.shape[-1]
        bias = jnp.where(ok, 0.0, NEG_INF)
        return (s.reshape(n, Q_BLOCK, kc) + bias[None]).reshape(nq, kc)

    def softmax(sb):
        m = jnp.max(sb, axis=-1, keepdims=True)
        p = jnp.exp2((sb - m) * c2)
        return p / jnp.sum(p, axis=-1, keepdims=True)

    kc = kc_ref[0, 0, 0]
    vc = vc_ref[0, 0, 0]
    ncmp = kc.shape[0]
    cidx = lax.broadcasted_iota(jnp.int32, (Q_BLOCK, ncmp), 1)
    valid_c = (cidx < ncmp - 1) & (cidx * CMP_STRIDE + (CMP_LEN - 1) <= tq)
    p = softmax(biased(_dot_nt(q, kc), valid_c)).reshape(n, Q_BLOCK, ncmp)
    p = p * (tq >= CMP_LEN - 1).astype(F32)[None]
    o_c = _dot(p.reshape(nq, ncmp).astype(BF16), vc)

    psum = p[0] + p[1] + p[2] + p[3]
    hi = psum.astype(BF16)
    lo = (psum - hi.astype(F32)).astype(BF16)
    imp = _dot(hi, cmap_ref[...]) + _dot(lo, cmap_ref[...])
    j = lax.broadcasted_iota(jnp.int32, (Q_BLOCK, LANES), 1)
    cur = tq // SEL_BLOCK
    imp = jnp.where(j == 0, 1e6, imp)
    imp = jnp.where(j == cur, 1e6, imp)
    imp = jnp.where(j == cur - 1, 1e6, imp)
    imp = jnp.where(j > cur, -1e6, imp)
    imp = jnp.where(j >= nsel, -3e6, imp)

    nrow = min(((nsel + 7) // 8) * 8, LANES)
    imp_t = imp.T[0:nrow]
    jr = lax.broadcasted_iota(jnp.int32, (nrow, Q_BLOCK), 0)
    rank = jnp.zeros((nrow, Q_BLOCK), F32)
    for i in range(min(nsel, LANES)):
        row = imp_t[i:i + 1, :]
        rank = rank + jnp.where(row > imp_t, 1.0, jnp.where(row == imp_t, jnp.where(jr > i, 1.0, 0.0), 0.0))
    sel_t = jnp.where(rank < float(SEL_TOP), 1.0, 0.0)
    if nrow < LANES:
        sel_t = jnp.concatenate([sel_t, jnp.zeros((LANES - nrow, Q_BLOCK), F32)], axis=0)
    sel = sel_t.T.astype(BF16)

    n_ch = (t0 + Q_BLOCK + ck - 1) // ck
    ones = jnp.ones((ck, LANES), BF16)

    def body(kb, carry):
        m_i, l_i, acc = carry
        k0 = pl.multiple_of(kb * ck, ck)
        kblk = ks_ref[0, 0, pl.ds(k0, ck), :]
        vblk = vs_ref[0, 0, pl.ds(k0, ck), :]
        em = _dot(sel, e_ref[kb])
        kpos = k0 + lax.broadcasted_iota(jnp.int32, (Q_BLOCK, ck), 1)
        sb = biased(_dot_nt(q, kblk), (kpos <= tq) & (em > 0.5))
        m_new = jnp.maximum(m_i, jnp.max(sb, axis=-1, keepdims=True))
        alpha = jnp.exp2((m_i - m_new) * c2)
        pp = jnp.exp2(((sb - m_new) * c2).astype(BF16))
        l_new = alpha * l_i + _dot(pp, ones)
        acc_new = alpha * acc + _dot(pp, vblk)
        return m_new, l_new, acc_new

    init = (jnp.full((nq, 1), NEG_INF, F32), jnp.zeros((nq, LANES), F32), jnp.zeros((nq, HEAD_DIM), F32))
    _, l_f, acc_f = lax.fori_loop(0, n_ch, body, init)
    o_s = acc_f / l_f

    wlen = min(WINDOW + Q_BLOCK, kw_ref.shape[2])
    w0 = pl.multiple_of(jnp.maximum(t0 + Q_BLOCK - wlen, 0), Q_BLOCK)
    kblk = kw_ref[0, 0, pl.ds(w0, wlen), :]
    vblk = vw_ref[0, 0, pl.ds(w0, wlen), :]
    kpos = w0 + lax.broadcasted_iota(jnp.int32, (Q_BLOCK, wlen), 1)
    pw = softmax(biased(_dot_nt(q, kblk), (kpos <= tq) & (tq - kpos < WINDOW)))
    o_w = _dot(pw.astype(BF16), vblk)

    gt = gt_ref[0, 0]
    for h in range(n):
        r = slice(h * Q_BLOCK, (h + 1) * Q_BLOCK)
        o = (gt[:, 3 * h:3 * h + 1] * o_c[r] + gt[:, 3 * h + 1:3 * h + 2] * o_s[r]
             + gt[:, 3 * h + 2:3 * h + 3] * o_w[r])
        o_ref[0, :, h * HEAD_DIM:(h + 1) * HEAD_DIM] = o.astype(o_ref.dtype)


def nsa_attention(qkr, kvp, cmp, gates, ck=512):
    B, _, S, dh = qkr.shape
    G, n = NSA_GROUPS, NSA_HPG
    ncmp = cmp.shape[3]
    nsel = S // SEL_BLOCK
    ck = min(ck, S)
    r_sel, r_cmp = SEL_BLOCK // CMP_STRIDE, CMP_LEN // CMP_STRIDE
    tgt = (r_sel * jnp.arange(nsel)[:, None, None] + jnp.arange(r_sel)[None, :, None]
           - jnp.arange(r_cmp)[None, None, :])
    cmap = (jnp.arange(ncmp - 1)[:, None, None, None] == tgt[None]).sum((2, 3)).astype(F32)
    cmap = jnp.pad(cmap, ((0, 1), (0, LANES - nsel))).astype(BF16)
    e = (jnp.arange(S)[None, :] // SEL_BLOCK == jnp.arange(LANES)[:, None]).astype(BF16)
    e3 = e.reshape(LANES, S // ck, ck).transpose(1, 0, 2)

    def head(h0):
        return pl.BlockSpec((1, 1, S, dh), lambda b, g, i: (b, h0 + g, 0, 0))

    return pl.pallas_call(
        functools.partial(_nsa_kernel, ck=ck),
        grid=(B, G, S // Q_BLOCK),
        in_specs=[
            pl.BlockSpec((1, n, Q_BLOCK, dh), lambda b, g, i: (b, g, i, 0)),
            head(16), head(20), head(8), head(12),
            pl.BlockSpec((1, 1, 1, ncmp, dh), lambda b, g, i: (0, b, g, 0, 0)),
            pl.BlockSpec((1, 1, 1, ncmp, dh), lambda b, g, i: (1, b, g, 0, 0)),
            pl.BlockSpec((1, 1, Q_BLOCK, LANES), lambda b, g, i: (b, g, i, 0)),
            pl.BlockSpec((ncmp, LANES), lambda b, g, i: (0, 0)),
            pl.BlockSpec((S // ck, LANES, ck), lambda b, g, i: (0, 0, 0)),
        ],
        out_specs=pl.BlockSpec((1, Q_BLOCK, n * dh), lambda b, g, i: (b, i, g)),
        out_shape=jax.ShapeDtypeStruct((B, S, G * n * dh), BF16),
        compiler_params=_cparams(("parallel", "parallel", "arbitrary")),
        name="nsa_attention",
    )(qkr, qkr, qkr, kvp, kvp, cmp, cmp, gates, cmap, e3)


def _rwkv_feat_kernel(z_ref, zp_ref, mu_ref, o_ref):
    i = pl.program_id(1)
    z = z_ref[0]
    prev = zp_ref[0, 7:8, :] * (i > 0).astype(F32)
    row = lax.broadcasted_iota(jnp.int32, z.shape, 0)
    sh = jnp.where(row == 0, prev, pltpu.roll(z, 1, 0))
    zs = z + (sh - z) * mu_ref[...]
    o_ref[0, :, 0:128] = jnp.tanh(zs[:, 0:128])
    o_ref[0, :, 128:256] = zs[:, 128:256]
    o_ref[0, :, 256:] = jax.nn.sigmoid(zs[:, 256:])


def rwkv_features(z_lora, mu_lora, tm=512):
    B, S, W = z_lora.shape
    tm = min(tm, S)
    return pl.pallas_call(
        _rwkv_feat_kernel,
        grid=(B, S // tm),
        in_specs=[pl.BlockSpec((1, tm, W), lambda b, i: (b, i, 0)),
                  pl.BlockSpec((1, 8, W), lambda b, i: (b, jnp.maximum(i * (tm // 8) - 1, 0), 0)),
                  pl.BlockSpec((1, W), lambda b, i: (0, 0))],
        out_specs=pl.BlockSpec((1, tm, W), lambda b, i: (b, i, 0)),
        out_shape=jax.ShapeDtypeStruct((B, S, W), F32),
        compiler_params=_cparams(("parallel", "arbitrary")),
        name="rwkv_features",
    )(z_lora, z_lora, mu_lora)


def _rwkv_kernel(r_ref, k_ref, v_ref, f_ref, par_ref, w2_ref, a2_ref, g2_ref, o_ref, st_ref, prev_ref):
    C = RWKV_CHUNK
    W = r_ref.shape[-1]
    npair = W // LANES
    c = pl.program_id(2)

    @pl.when(c == 0)
    def _():
        st_ref[...] = jnp.zeros_like(st_ref)
        prev_ref[...] = jnp.zeros_like(prev_ref)

    par = par_ref[...]
    mu_r, mu_k, mu_v = par[0:1], par[1:2], par[2:3]
    w0, a0, k_k, k_a, r_k, ln_w, ln_b = par[3:4], par[4:5], par[5:6], par[6:7], par[7:8], par[8:9], par[9:10]

    row = lax.broadcasted_iota(jnp.int32, (C, C), 0)
    col = lax.broadcasted_iota(jnp.int32, (C, C), 1)
    trow = lax.broadcasted_iota(jnp.int32, (C, W), 0)
    lane = lax.broadcasted_iota(jnp.int32, (C, LANES), 1)
    lr = lax.broadcasted_iota(jnp.int32, (LANES, LANES), 0)
    lc = lax.broadcasted_iota(jnp.int32, (LANES, LANES), 1)

    def shifted(z_ref, idx, mu):
        z = z_ref[0]
        sh = jnp.where(trow == 0, prev_ref[idx:idx + 1, :], pltpu.roll(z, 1, 0))
        prev_ref[idx:idx + 1, :] = z[C - 1:C, :]
        return z + (sh - z) * mu

    r = shifted(r_ref, 0, mu_r)
    k = shifted(k_ref, 1, mu_k)
    v = shifted(v_ref, 2, mu_v)
    f = f_ref[0]
    w_log = -_softplus(-(w0 + _dot3(_split(f[:, 0:128]), _split(w2_ref[...])))) - 0.5
    logd = -jnp.exp(w_log)
    a = jax.nn.sigmoid(a0 + _dot3(_split(f[:, 128:256]), _split(a2_ref[...])))
    g = _dot3(_split(f[:, 256:]), _split(g2_ref[...]))

    bd = _exact(jnp.where((lr // RWKV_HEAD) == (lc // RWKV_HEAD), 1.0, 0.0))

    def head_sum(x):
        return jnp.concatenate([_dot3(_split(x[:, p * LANES:(p + 1) * LANES]), bd) for p in range(npair)], axis=1)

    kk = k * k_k
    kk = kk / jnp.maximum(jnp.sqrt(head_sum(kk * kk)), 1e-12)
    k2 = k * (1.0 + (a - 1.0) * k_a)

    tri_incl = _exact(jnp.where(col <= row, 1.0, 0.0))
    cum = _dot3(tri_incl, _split(logd))
    cum_end = cum[C - 1:C, :]
    e_neg = jnp.exp(-cum)
    kka = kk * a
    a_t = -kk * jnp.exp(cum - logd)
    b_t = kka * e_neg
    k_t = k2 * e_neg
    r_t = r * jnp.exp(cum)
    e_end = jnp.exp(cum_end - cum)
    b_e = kka * e_end
    k_e = k2 * e_end
    p_end = jnp.exp(cum_end)

    strict = col < row
    incl = col <= row
    eye = jnp.where(col == row, 1.0, 0.0)
    n_sq = C.bit_length() - 2
    hpp = LANES // RWKV_HEAD
    heads = [(p, h) for p in range(npair) for h in range(hpp)]
    hd = []
    for p, h in heads:
        sl = slice(p * LANES, (p + 1) * LANES)
        mh = (lane // RWKV_HEAD) == h
        s_am = _split(jnp.where(mh, a_t[:, sl], 0.0))
        s_rm = _split(jnp.where(mh, r_t[:, sl], 0.0))
        hd.append(dict(
            sl=sl, s_am=s_am, s_rm=s_rm,
            s_vm=_split(jnp.where(mh, v[:, sl], 0.0)),
            s_bem=_split(jnp.where(mh, b_e[:, sl], 0.0)),
            s_kem=_split(jnp.where(mh, k_e[:, sl], 0.0)),
            lhs=(jnp.concatenate([s_am[0], s_rm[0]], axis=0), jnp.concatenate([s_am[1], s_rm[1]], axis=0)),
            rhs=_split(jnp.concatenate([b_t[:, sl], k_t[:, sl]], axis=0)) if h == 0 else hd[-1]["rhs"]))
    for d in hd:
        amat = _dot3(d["lhs"], d["rhs"], "nt")
        l_ab = jnp.where(strict, amat[0:C, 0:C], 0.0)
        d["l_ak"] = _split(jnp.where(strict, amat[0:C, C:2 * C], 0.0))
        d["l_rb"] = jnp.where(incl, amat[C:2 * C, 0:C], 0.0).astype(BF16)
        d["l_rk"] = jnp.where(incl, amat[C:2 * C, C:2 * C], 0.0).astype(BF16)
        d["x"] = eye + l_ab
        d["s_pw"] = _split(l_ab)
    for i in range(n_sq):
        for d in hd:
            d["s_pw"] = _split(_dot3(d["s_pw"], d["s_pw"])) if i == 0 else (_dot(d["s_pw"][0], d["s_pw"][0]).astype(BF16), None)
        for d in hd:
            d["x"] = d["x"] + (_dot3(_split(d["x"]), d["s_pw"]) if i == 0 else _dot(d["x"].astype(BF16), d["s_pw"][0]))
    for i, d in enumerate(hd):
        d["st"] = st_ref[i]
        d["s_st"] = _split(d["st"])
        d["t"] = _split(_dot3(d["s_am"], d["s_st"], "nt") + _dot3(d["l_ak"], d["s_vm"]))
    for d in hd:
        d["s_u"] = _split(_dot3(_split(d["x"]), d["t"]))
    for i, d in enumerate(hd):
        st_ref[i] = d["st"] * p_end[:, d["sl"]] + _dot3(d["s_u"], d["s_bem"], "tn") + _dot3(d["s_vm"], d["s_kem"], "tn")
    ys = []
    for d in hd:
        ys.append(_dot_nt(d["s_rm"][0], d["s_st"][0]) + _dot(d["l_rb"], d["s_u"][0]) + _dot(d["l_rk"], d["s_vm"][0]))
    y = jnp.concatenate([sum(ys[hpp * p + 1:hpp * (p + 1)], ys[hpp * p]) for p in range(npair)], axis=1)

    inv_n = 1.0 / RWKV_HEAD
    mean = head_sum(y) * inv_n
    d = y - mean
    var = head_sum(d * d) * inv_n
    yn = d * lax.rsqrt(var + GN_EPS) * ln_w + ln_b
    bonus = head_sum(r * k2 * r_k) * v
    o_ref[0] = ((yn + bonus) * g).astype(o_ref.dtype)


def rwkv_scan(z_rkv, feat, par, w2p, a2p, g2, pairs_per_step=8):
    B, S, R3 = z_rkv.shape
    R = R3 // 3
    W = pairs_per_step * LANES
    nblk = R // W
    C = RWKV_CHUNK
    assert S % C == 0 and R % W == 0

    def col(off):
        return pl.BlockSpec((1, C, W), lambda b, p, c: (b, c, off + p))

    return pl.pallas_call(
        _rwkv_kernel,
        grid=(B, nblk, S // C),
        in_specs=[
            col(0), col(nblk), col(2 * nblk),
            pl.BlockSpec((1, C, feat.shape[-1]), lambda b, p, c: (b, c, 0)),
            pl.BlockSpec((16, W), lambda b, p, c: (0, p)),
            pl.BlockSpec((LANES, W), lambda b, p, c: (0, p)),
            pl.BlockSpec((LANES, W), lambda b, p, c: (0, p)),
            pl.BlockSpec((GATE_LORA, W), lambda b, p, c: (0, p)),
        ],
        out_specs=pl.BlockSpec((1, C, W), lambda b, p, c: (b, c, p)),
        out_shape=jax.ShapeDtypeStruct((B, S, R), BF16),
        scratch_shapes=[pltpu.VMEM((W // RWKV_HEAD, LANES, LANES), F32), pltpu.VMEM((8, W), F32)],
        compiler_params=_cparams(("parallel", "parallel", "arbitrary")),
        name="rwkv_scan",
    )(z_rkv, z_rkv, z_rkv, feat, par, w2p, a2p, g2)


def _merge_kernel(on_ref, or_ref, wn_ref, wr_ref, ga_ref, gb_ref, o_ref):
    pa = _dot(on_ref[...], wn_ref[...])
    pb = _dot(or_ref[...], wr_ref[...])
    o_ref[...] = (ga_ref[...].astype(F32) * pa + gb_ref[...].astype(F32) * pb).astype(o_ref.dtype)


def merge_mixers(o_nsa, o_rwkv, wn, wr, gates, tm=1024, tn=512):
    M, Kn = o_nsa.shape
    Kr = o_rwkv.shape[1]
    D = wn.shape[1]
    tm, tn = min(tm, M), min(tn, D)
    nb = D // tn
    return pl.pallas_call(
        _merge_kernel,
        grid=(M // tm, nb),
        in_specs=[
            pl.BlockSpec((tm, Kn), lambda i, j: (i, 0)),
            pl.BlockSpec((tm, Kr), lambda i, j: (i, 0)),
            pl.BlockSpec((Kn, tn), lambda i, j: (0, j)),
            pl.BlockSpec((Kr, tn), lambda i, j: (0, j)),
            pl.BlockSpec((tm, tn), lambda i, j: (i, j)),
            pl.BlockSpec((tm, tn), lambda i, j: (i, nb + j)),
        ],
        out_specs=pl.BlockSpec((tm, tn), lambda i, j: (i, j)),
        out_shape=jax.ShapeDtypeStruct((M, D), BF16),
        compiler_params=_cparams(("parallel", "parallel")),
        name="merge_mixers",
    )(o_nsa, o_rwkv, wn, wr, gates, gates)


def _xattn_kernel(q_ref, kv_ref, wo_ref, x_ref, o_ref):
    nh = XATTN_HEADS
    scale = HEAD_DIM ** -0.5
    outs = []
    for h in range(nh):
        q = q_ref[:, h * HEAD_DIM:(h + 1) * HEAD_DIM]
        k = kv_ref[0, :, h * HEAD_DIM:(h + 1) * HEAD_DIM]
        v = kv_ref[0, :, (nh + h) * HEAD_DIM:(nh + h + 1) * HEAD_DIM]
        s = _dot_nt(q, k) * scale
        m = jnp.max(s, axis=-1, keepdims=True)
        p = jnp.exp(s - m)
        p = p / jnp.sum(p, axis=-1, keepdims=True)
        outs.append(_dot(p.astype(BF16), v).astype(BF16))
    o = jnp.concatenate(outs, axis=-1)
    o_ref[...] = x_ref[...] + _dot(o, wo_ref[...])


def cross_attention(q, kv, wo, x, seq, tm=256):
    T, D = x.shape
    tm = min(tm, seq)
    spb = seq // tm
    Mm = kv.shape[1]
    return pl.pallas_call(
        _xattn_kernel,
        grid=(T // tm,),
        in_specs=[
            pl.BlockSpec((tm, q.shape[1]), lambda i: (i, 0)),
            pl.BlockSpec((1, Mm, kv.shape[2]), lambda i: (i // spb, 0, 0)),
            pl.BlockSpec(wo.shape, lambda i: (0, 0)),
            pl.BlockSpec((tm, D), lambda i: (i, 0)),
        ],
        out_specs=pl.BlockSpec((tm, D), lambda i: (i, 0)),
        out_shape=jax.ShapeDtypeStruct((T, D), F32),
        compiler_params=_cparams(("parallel",)),
        name="cross_attention",
    )(q, kv, wo, x)


def _top16(x, iota=None, n=None):
    if iota is None:
        n = x.shape[0]
        iota = lax.broadcasted_iota(jnp.int32, x.shape, 0).astype(F32)
    cur = x
    rank = jnp.full(x.shape, float(PEER_TOPK), F32)
    vals = []
    for r in range(PEER_TOPK):
        m = jnp.max(cur, axis=0, keepdims=True)
        idx = jnp.min(jnp.where(cur == m, iota, float(n)), axis=0, keepdims=True)
        hit = iota == idx
        rank = jnp.where(hit, float(r), rank)
        cur = jnp.where(hit, -jnp.inf, cur)
        vals.append(m)
    return jnp.concatenate(vals, axis=0), rank


def _peer_route_kernel(q_ref, keys_ref, row_ref, pl_ref):
    K = PEER_TOPK
    q = q_ref[...]
    half = q.shape[1] // 2
    s1 = _dotf_nt(keys_ref[0, 0], q[:, :half])
    s2 = _dotf_nt(keys_ref[0, 1], q[:, half:])
    v1, rank1 = _top16(s1)
    v2, rank2 = _top16(s2)
    tmq = q.shape[0]
    r8 = lax.broadcasted_iota(jnp.int32, (8, tmq), 0)
    ninf = -jnp.inf
    lo8 = v2[0:8]
    tiles = [
        (v1[0:1] + lo8, r8),
        (v1[0:1] + v2[8:16], 8 + r8),
        (v1[1:2] + lo8, 16 + r8),
        (jnp.where(r8 < 5, v1[2:3] + lo8, ninf), 32 + r8),
        (jnp.where(r8 < 7, jnp.where(r8 < 4, v1[3:4], v1[4:5]) + jnp.where(r8 < 4, lo8, pltpu.roll(lo8, 4, 0)), ninf),
         jnp.where(r8 < 4, 48 + r8, 60 + r8)),
        (jnp.where(r8 < 6, jnp.where(r8 < 2, v1[5:6], jnp.where(r8 < 4, v1[6:7], v1[7:8]))
                   + jnp.where(r8 % 2 == 0, v2[0:1], v2[1:2]), ninf),
         80 + (r8 // 2) * 16 + r8 % 2),
        (v1[8:16] + v2[0:1], (8 + r8) * 16),
    ]
    cand = jnp.concatenate([t for t, _ in tiles], axis=0)
    cidx = jnp.concatenate([i for _, i in tiles], axis=0).astype(F32)
    top_s, crank = _top16(cand, cidx, K * K)
    chosen = jnp.where(crank < float(K), 1.0, 0.0)
    z = jnp.sum(jnp.exp(top_s - top_s[0:1]), axis=0, keepdims=True)
    ch = [chosen[8 * t:8 * (t + 1)] for t in range(len(tiles))]
    colsum = lambda x: jnp.sum(x, axis=0, keepdims=True)
    upper = (r8 < 4).astype(F32)
    j_rows = [colsum(ch[0]) + colsum(ch[1]), colsum(ch[2]), colsum(ch[3]),
              colsum(ch[4] * upper), colsum(ch[4] * (1.0 - upper))]
    j_rows += [ch[5][2 * m:2 * m + 1] + ch[5][2 * m + 1:2 * m + 2] for m in range(3)]
    j_rows += [ch[6][m:m + 1] for m in range(8)]
    jn = jnp.zeros_like(s1)
    for i in range(K):
        jn = jnp.where(rank1 == float(i), j_rows[i], jn)
    row_ref[0, 0] = jnp.exp(s1 - v1[0:1]) / z
    row_ref[0, 1] = jn
    pl_ref[0, 0] = jnp.exp(s2 - v2[0:1]).astype(pl_ref.dtype)
    pl_ref[0, 1] = rank2.astype(pl_ref.dtype)


def peer_route(q, keys, tm=256):
    T = q.shape[0]
    H, _, nkeys, hd = keys.shape
    tm = min(tm, T)
    spec = pl.BlockSpec((1, 2, nkeys, tm), lambda i, h: (h, 0, 0, i))
    return pl.pallas_call(
        _peer_route_kernel,
        grid=(T // tm, H),
        in_specs=[pl.BlockSpec((tm, 2 * hd), lambda i, h: (i, h)),
                  pl.BlockSpec((1, 2, nkeys, hd), lambda i, h: (h, 0, 0, 0))],
        out_specs=[spec, spec],
        out_shape=[jax.ShapeDtypeStruct((H, 2, nkeys, T), F32), jax.ShapeDtypeStruct((H, 2, nkeys, T), BF16)],
        compiler_params=_cparams(("parallel", "parallel")),
        name="peer_route",
    )(q, keys)


def _peer_act_kernel(h_ref, u_ref, row_ref, pl_ref, o_ref):
    nk = PEER_KEYS
    ec = u_ref.shape[0]
    j = pl.program_id(1)
    hu = _dot_nt(u_ref[...], h_ref[...])
    zero = jnp.zeros((), o_ref.dtype)
    for al in range(ec // nk):
        a = j * (ec // nk) + al
        w = None
        for hd in range(row_ref.shape[0]):
            e1 = row_ref[hd, 0, pl.ds(a, 1), :].astype(o_ref.dtype)
            jn = row_ref[hd, 1, pl.ds(a, 1), :].astype(o_ref.dtype)
            g = jnp.where(pl_ref[hd, 1] < jn, e1 * pl_ref[hd, 0], zero)
            w = g if w is None else w + g
        act = _gelu(hu[al * nk:(al + 1) * nk, :].astype(o_ref.dtype)) * w
        o_ref[:, al * nk:(al + 1) * nk] = act.T


def peer_activations(h, u_tab, rows, planes, tm=512, ec=512):
    T, D = h.shape
    E = u_tab.shape[0]
    tm, ec = min(tm, T), min(ec, E)
    H, _, nkeys, _ = rows.shape
    rspec = pl.BlockSpec((H, 2, nkeys, tm), lambda i, j: (0, 0, 0, i))
    return pl.pallas_call(
        _peer_act_kernel,
        grid=(T // tm, E // ec),
        in_specs=[pl.BlockSpec((tm, D), lambda i, j: (i, 0)),
                  pl.BlockSpec((ec, D), lambda i, j: (j, 0)),
                  rspec, rspec],
        out_specs=pl.BlockSpec((tm, ec), lambda i, j: (i, j)),
        out_shape=jax.ShapeDtypeStruct((T, E), BF16),
        compiler_params=_cparams(("parallel", "arbitrary")),
        name="peer_activations",
    )(h, u_tab, rows, planes)


def kernel(x, mem, positions, norm_mix, w_in, nsa_cmp_pe, nsa_cmp_w1, nsa_cmp_w2, rwkv_mu, rwkv_w0, rwkv_w2, rwkv_a0, rwkv_a2, rwkv_g2, rwkv_k_k, rwkv_k_a, rwkv_r_k, rwkv_ln_w, rwkv_ln_b, w_proj_nsa, w_proj_rwkv, w_out, norm_xattn, norm_mem, xattn_wq, xattn_wkv, xattn_wo, norm_ffn, peer_wq, peer_keys, peer_u, peer_v, norm_final):
    B, S, D = x.shape
    T = B * S
    depth = w_in.shape[0]
    G, n, dh = NSA_GROUPS, NSA_HPG, HEAD_DIM
    nq_cols = G * n * dh
    nkv_cols = 6 * G * dh
    ngate = 3 * G * n
    R = rwkv_w0.shape[1]
    c_q, c_kv, c_g = nq_cols, nq_cols + nkv_cols, nq_cols + nkv_cols + ngate
    c_r = c_g + 3 * R + DECAY_LORA + AAA_LORA + GATE_LORA
    gd = G * dh

    pos_flat = positions.reshape(T)
    rope_tabs = _rope_tables(pos_flat)
    cend = jnp.arange(S // CMP_STRIDE) * CMP_STRIDE + (CMP_LEN - 1)
    ctab = _rope_tables(positions[:, jnp.minimum(cend, S - 1)])
    ident = _rope_tables(jnp.zeros_like(positions[:, :S // CMP_STRIDE]))
    cmp_tabs = [jnp.stack([a, b]) for a, b in zip(ctab, ident)]

    xf = x.reshape(T, D)
    for l in range(depth):
        w = w_in[l]
        kv = w[:, c_q:c_kv]
        w_rope = jnp.concatenate([w[:, :c_q], kv[:, 2 * gd:3 * gd], kv[:, 4 * gd:5 * gd]], axis=1).astype(BF16)
        w_plain = jnp.concatenate([kv[:, 0:2 * gd], kv[:, 3 * gd:4 * gd], kv[:, 5 * gd:6 * gd]], axis=1).astype(BF16)
        w_gate = jnp.pad(w[:, c_kv:c_g].reshape(D, G, 3 * n), ((0, 0), (0, 0), (0, LANES - 3 * n)))
        w_gate = w_gate.reshape(D, G * LANES).astype(BF16)
        w_rkv = w[:, c_g:c_g + 3 * R].astype(BF16)
        wl = w[:, c_g + 3 * R:c_r]
        padl = lambda a, n_: jnp.pad(a, ((0, 0), (0, n_ - a.shape[1])))
        w_lora = jnp.concatenate([padl(wl[:, :DECAY_LORA], LANES), padl(wl[:, DECAY_LORA:DECAY_LORA + AAA_LORA], LANES),
                                  wl[:, DECAY_LORA + AAA_LORA:]], axis=1).astype(BF16)
        w_merge = w[:, c_r:].astype(BF16)

        h = rmsnorm(xf, norm_mix[l], BF16)
        qkr = matmul(h, w_rope, mode="heads_rope", extras=rope_tabs, out_dtype=BF16, seq=S, name="proj_rope")
        kvp = matmul(h, w_plain, mode="heads", out_dtype=BF16, seq=S, name="proj_kv")
        gates_nsa = matmul(h, w_gate, mode="heads_sigmoid", out_dtype=F32, seq=S, name="proj_gate")
        z_rkv = matmul(h, w_rkv, out_dtype=F32, name="proj_rkv")
        z_lora = matmul(h, w_lora, out_dtype=F32, name="proj_lora")
        gates_merge = matmul(h, w_merge, mode="sigmoid", out_dtype=BF16, name="proj_merge")

        pe8 = jnp.broadcast_to(nsa_cmp_pe[l].reshape(2, 1, CMP_LEN * dh), (2, 8, CMP_LEN * dh)).astype(BF16)
        cmp = nsa_compress(kvp, nsa_cmp_w1[l].astype(BF16), nsa_cmp_w2[l].astype(BF16), pe8, cmp_tabs)
        o_nsa = nsa_attention(qkr, kvp, cmp, gates_nsa).reshape(T, nq_cols)

        mu = rwkv_mu[l]
        mul = mu[3 * R:]
        pad1 = lambda a, n_: jnp.pad(a, (0, n_ - a.shape[0]))
        mu_lora = jnp.concatenate([pad1(mul[:DECAY_LORA], LANES), pad1(mul[DECAY_LORA:DECAY_LORA + AAA_LORA], LANES),
                                   mul[DECAY_LORA + AAA_LORA:]]).reshape(1, -1)
        feat = rwkv_features(z_lora.reshape(B, S, -1), mu_lora)
        par = jnp.stack([mu[:R], mu[R:2 * R], mu[2 * R:3 * R], rwkv_w0[l], rwkv_a0[l], rwkv_k_k[l], rwkv_k_a[l],
                         rwkv_r_k[l].reshape(R), rwkv_ln_w[l], rwkv_ln_b[l]])
        par = jnp.pad(par, ((0, 16 - par.shape[0]), (0, 0)))
        w2p = jnp.pad(rwkv_w2[l], ((0, LANES - DECAY_LORA), (0, 0)))
        a2p = jnp.pad(rwkv_a2[l], ((0, LANES - AAA_LORA), (0, 0)))
        o_rwkv = rwkv_scan(z_rkv.reshape(B, S, 3 * R), feat, par, w2p, a2p, rwkv_g2[l]).reshape(T, R)

        merged = merge_mixers(o_nsa, o_rwkv, w_proj_nsa[l].astype(BF16), w_proj_rwkv[l].astype(BF16), gates_merge)
        xf = matmul(merged, w_out[l].astype(BF16), mode="residual", extras=(xf,), out_dtype=F32, name="out_proj")

        h = rmsnorm(xf, norm_xattn[l], BF16)
        m = rmsnorm(mem.reshape(-1, D), norm_mem[l], BF16)
        q = matmul(h, xattn_wq[l].astype(BF16), out_dtype=BF16, name="xattn_q")
        kvm = matmul(m, xattn_wkv[l].astype(BF16), out_dtype=BF16, name="xattn_kv").reshape(B, mem.shape[1], -1)
        xf = cross_attention(q, kvm, xattn_wo[l].astype(BF16), xf, S)

        h = rmsnorm(xf, norm_ffn[l], BF16)
        pq = matmul(h, peer_wq[l].astype(BF16), out_dtype=F32, name="peer_q")
        rows, planes = peer_route(pq, peer_keys[l])
        act = peer_activations(h, peer_u[l].astype(BF16), rows, planes)
        xf = matmul(act, peer_v[l].astype(BF16), mode="residual", extras=(xf,), out_dtype=F32,
                    tm=1024, tn=1024, tk=1024, name="peer_out")

    return rmsnorm(xf, norm_final, F32).reshape(B, S, D)
```

```python
import functools

import jax
import jax.numpy as jnp
from jax import lax
from jax.experimental import pallas as pl
from jax.experimental.pallas import tpu as pltpu

F32 = jnp.float32
BF16 = jnp.bfloat16
HI = lax.Precision.HIGHEST

LANES = 128
HEAD_DIM = 128
ROPE_DIM = HEAD_DIM // 4
ROPE_THETA = 500000.0
RMS_EPS = 1e-6
NEG_INF = -1e30

NSA_GROUPS = 4
NSA_HPG = 4
CMP_LEN = 32
CMP_STRIDE = 16
SEL_BLOCK = 64
SEL_TOP = 16
WINDOW = 512
Q_BLOCK = 128

RWKV_HEAD = 64
DECAY_LORA = 96
AAA_LORA = 96
GATE_LORA = 256
GN_EPS = 64e-5
RWKV_CHUNK = 128

XATTN_HEADS = 4
PEER_HEADS = 8
PEER_KEYS = 128
PEER_TOPK = 16

VMEM_LIMIT = 56 * 1024 * 1024


def _cparams(sem, flags=None):
    return pltpu.CompilerParams(dimension_semantics=sem, vmem_limit_bytes=VMEM_LIMIT, flags=flags)


def _gelu(x):
    return 0.5 * x * (1.0 + jnp.tanh(0.7978845608028654 * (x + 0.044715 * x * x * x)))


def _softplus(x):
    return jnp.maximum(x, 0.0) + jnp.log(1.0 + jnp.exp(-jnp.abs(x)))


def _dot(a, b):
    return jnp.dot(a, b, preferred_element_type=F32)


def _dot_nt(a, b):
    return lax.dot_general(a, b, (((1,), (1,)), ((), ())), preferred_element_type=F32)


def _dotf_nt(a, b):
    return lax.dot_general(a, b, (((1,), (1,)), ((), ())), preferred_element_type=F32, precision=HI)


def _split(x):
    hi = x.astype(BF16)
    return hi, (x - hi.astype(F32)).astype(BF16)


def _exact(x):
    return x.astype(BF16), None


_DN = {"nn": (((1,), (0,)), ((), ())), "nt": (((1,), (1,)), ((), ())), "tn": (((0,), (0,)), ((), ()))}


def _dot3(a, b, form="nn"):
    d = lambda x, y: lax.dot_general(x, y, _DN[form], preferred_element_type=F32)
    out = d(a[0], b[0])
    if a[1] is not None:
        out = out + d(a[1], b[0])
    if b[1] is not None:
        out = out + d(a[0], b[1])
    return out


def _rope_apply(a, c, sa, sb):
    return a * c + pltpu.roll(a, LANES - ROPE_DIM // 2, 1) * sa + pltpu.roll(a, ROPE_DIM // 2, 1) * sb


def _rope_tables(pos):
    half = ROPE_DIM // 2
    inv = ROPE_THETA ** (-jnp.arange(half, dtype=F32) / half)
    ang = pos.astype(F32)[..., None] * inv
    cos, sin = jnp.cos(ang), jnp.sin(ang)
    shp = pos.shape + (HEAD_DIM - ROPE_DIM,)
    c = jnp.concatenate([cos, cos, jnp.ones(shp, F32)], axis=-1)
    sa = jnp.concatenate([-sin, jnp.zeros(pos.shape + (HEAD_DIM - half,), F32)], axis=-1)
    sb = jnp.concatenate([jnp.zeros(pos.shape + (half,), F32), sin, jnp.zeros(shp, F32)], axis=-1)
    return c, sa, sb


def _rmsnorm_kernel(x_ref, g_ref, o_ref, *ot_ref):
    x = x_ref[...].astype(F32)
    ms = jnp.mean(x * x, axis=-1, keepdims=True)
    y = x * lax.rsqrt(ms + RMS_EPS) * g_ref[...]
    o_ref[...] = y.astype(o_ref.dtype)
    if ot_ref:
        ot_ref[0][...] = y.T.astype(ot_ref[0].dtype)


def rmsnorm(x2d, g, out_dtype, tm=256, with_transpose=False):
    M, D = x2d.shape
    tm = min(tm, M)
    out_specs = [pl.BlockSpec((tm, D), lambda i: (i, 0))]
    out_shape = [jax.ShapeDtypeStruct((M, D), out_dtype)]
    if with_transpose:
        out_specs.append(pl.BlockSpec((D, tm), lambda i: (0, i)))
        out_shape.append(jax.ShapeDtypeStruct((D, M), out_dtype))
    out = pl.pallas_call(
        _rmsnorm_kernel,
        grid=(M // tm,),
        in_specs=[pl.BlockSpec((tm, D), lambda i: (i, 0)), pl.BlockSpec((1, D), lambda i: (0, 0))],
        out_specs=out_specs,
        out_shape=out_shape,
        compiler_params=_cparams(("parallel",)),
        name="rmsnorm",
    )(x2d, g.reshape(1, D).astype(F32))
    return out if with_transpose else out[0]


def _mm_kernel(*refs, nk, mode, n_extra):
    x_ref, w_ref = refs[0], refs[1]
    extras = refs[2:2 + n_extra]
    o_ref = refs[2 + n_extra]

    def epilogue(acc):
        if mode == "plain":
            o_ref[...] = acc.astype(o_ref.dtype)
        elif mode == "sigmoid":
            o_ref[...] = jax.nn.sigmoid(acc).astype(o_ref.dtype)
        elif mode == "residual":
            o_ref[...] = (extras[0][...] + acc).astype(o_ref.dtype)
        elif mode in ("heads", "heads_sigmoid", "heads_rope"):
            nh = acc.shape[1] // LANES
            for j in range(nh):
                a = acc[:, j * LANES:(j + 1) * LANES]
                if mode == "heads_sigmoid":
                    a = jax.nn.sigmoid(a)
                elif mode == "heads_rope":
                    a = _rope_apply(a, extras[0][...], extras[1][...], extras[2][...])
                o_ref[0, j] = a.astype(o_ref.dtype)
        else:
            raise ValueError(mode)

    if nk == 1:
        epilogue(_dot(x_ref[...], w_ref[...]))
    else:
        acc_ref = refs[-1]
        k = pl.program_id(2)

        @pl.when(k == 0)
        def _():
            acc_ref[...] = jnp.zeros_like(acc_ref)

        acc_ref[...] += _dot(x_ref[...], w_ref[...])

        @pl.when(k == nk - 1)
        def _():
            epilogue(acc_ref[...])


def matmul(x, w, *, mode="plain", extras=(), out_dtype=F32, tm=1024, tn=512, tk=None, seq=None, name="mm",
           wcols=None):
    M, K = x.shape
    wcol0, N = (0, w.shape[1]) if wcols is None else wcols
    tm, tn = min(tm, M), min(tn, N)
    assert wcol0 % tn == 0
    wblk0 = wcol0 // tn
    if mode.startswith("heads"):
        tm = min(tm, seq)
    tk = K if tk is None else min(tk, K)
    nk = K // tk
    assert M % tm == 0 and N % tn == 0 and K % tk == 0
    grid = (M // tm, N // tn, nk)
    in_specs = [pl.BlockSpec((tm, tk), lambda i, j, k: (i, k)), pl.BlockSpec((tk, tn), lambda i, j, k: (k, wblk0 + j))]
    if mode == "residual":
        in_specs.append(pl.BlockSpec((tm, tn), lambda i, j, k: (i, j)))
    elif mode == "heads_rope":
        in_specs += [pl.BlockSpec((tm, LANES), lambda i, j, k: (i, 0))] * 3
    if mode.startswith("heads"):
        assert seq % tm == 0 and tn % LANES == 0
        spb = seq // tm
        out_shape = jax.ShapeDtypeStruct((M // seq, N // LANES, seq, LANES), out_dtype)
        out_spec = pl.BlockSpec((1, tn // LANES, tm, LANES), lambda i, j, k: (i // spb, j, i % spb, 0))
    else:
        out_shape = jax.ShapeDtypeStruct((M, N), out_dtype)
        out_spec = pl.BlockSpec((tm, tn), lambda i, j, k: (i, j))
    scratch = [pltpu.VMEM((tm, tn), F32)] if nk > 1 else []
    return pl.pallas_call(
        functools.partial(_mm_kernel, nk=nk, mode=mode, n_extra=len(extras)),
        grid=grid,
        in_specs=in_specs,
        out_specs=out_spec,
        out_shape=out_shape,
        scratch_shapes=scratch,
        compiler_params=_cparams(("parallel", "parallel", "arbitrary")),
        name=name,
    )(x, w, *extras)


def _repack_kernel(tab_ref, *refs):
    nsub = (len(refs) - 1) // 2
    o_ref = refs[-1]
    j = pl.program_id(1)
    lane = lax.broadcasted_iota(jnp.int32, refs[0].shape, 1)
    for s in range(nsub):
        sh = tab_ref[1, j * nsub + s]
        valid = tab_ref[2, j * nsub + s]
        amt = lax.rem(LANES - sh, LANES)
        lo = pltpu.roll(refs[2 * s][...], amt, 1)
        hi = pltpu.roll(refs[2 * s + 1][...], amt, 1)
        out = jnp.where(lane < LANES - sh, lo, hi)
        o_ref[:, s * LANES:(s + 1) * LANES] = jnp.where(lane < valid, out, 0.0).astype(o_ref.dtype)


def repack_columns(w, segments, tr=1024, nsub=4):
    D, NC = w.shape
    ndst = len(segments)
    assert ndst % nsub == 0 and D % tr == 0
    last = (NC + LANES - 1) // LANES - 1
    tab = jnp.asarray([[c // LANES for c, _ in segments], [c % LANES for c, _ in segments], [v for _, v in segments]],
                      jnp.int32)
    specs = []
    for s in range(nsub):
        specs.append(pl.BlockSpec((tr, LANES), lambda i, j, t, s=s: (i, t[0, j * nsub + s])))
        specs.append(pl.BlockSpec((tr, LANES), lambda i, j, t, s=s: (i, jnp.minimum(t[0, j * nsub + s] + 1, last))))
    grid_spec = pltpu.PrefetchScalarGridSpec(
        num_scalar_prefetch=1, grid=(D // tr, ndst // nsub), in_specs=specs,
        out_specs=pl.BlockSpec((tr, LANES * nsub), lambda i, j, t: (i, j)))
    return pl.pallas_call(
        _repack_kernel, grid_spec=grid_spec, out_shape=jax.ShapeDtypeStruct((D, ndst * LANES), BF16),
        compiler_params=_cparams(("parallel", "arbitrary")), name="repack_w_in",
    )(tab, *([w] * (2 * nsub)))


def _compress_kernel(x_ref, w1_ref, w2_ref, pe_ref, c_ref, sa_ref, sb_ref, o_ref):
    half = x_ref.shape[-1]
    x = x_ref[0, 0]
    a = _dot(x, w1_ref[0, :half, :])
    b = _dot(x, w1_ref[0, half:, :])
    pet = _dot(pe_ref[0], w1_ref[0])
    n = a.shape[0]
    hid = a + pltpu.roll(b, n - 1, 0) + pet[0:1]
    out = _dot(_gelu(hid).astype(BF16), w2_ref[0])
    out = _rope_apply(out, c_ref[0, 0], sa_ref[0, 0], sb_ref[0, 0])
    o_ref[0, 0, 0] = out.astype(o_ref.dtype)


def nsa_compress(kvp, w1, w2, pe8, tabs):
    B, _, S, dh = kvp.shape
    G = NSA_GROUPS
    nch = S // CMP_STRIDE
    xv = kvp.reshape(B, kvp.shape[1], nch, CMP_STRIDE * dh)
    hidden = w1.shape[-1]
    tab_spec = pl.BlockSpec((1, 1, nch, LANES), lambda w, b, g: (w, b, 0, 0))
    return pl.pallas_call(
        _compress_kernel,
        grid=(2, B, G),
        in_specs=[
            pl.BlockSpec((1, 1, nch, CMP_STRIDE * dh), lambda w, b, g: (b, w * G + g, 0, 0)),
            pl.BlockSpec((1, CMP_LEN * dh, hidden), lambda w, b, g: (w, 0, 0)),
            pl.BlockSpec((1, hidden, dh), lambda w, b, g: (w, 0, 0)),
            pl.BlockSpec((1, 8, CMP_LEN * dh), lambda w, b, g: (w, 0, 0)),
            tab_spec, tab_spec, tab_spec,
        ],
        out_specs=pl.BlockSpec((1, 1, 1, nch, dh), lambda w, b, g: (w, b, g, 0, 0)),
        out_shape=jax.ShapeDtypeStruct((2, B, G, nch, dh), BF16),
        compiler_params=_cparams(("parallel", "parallel", "parallel")),
        name="nsa_compress",
    )(xv, w1, w2, pe8, *tabs)


def _nsa_kernel(q_ref, ks_ref, kw_ref, vs_ref, vw_ref, kc_ref, vc_ref, gt_ref, cmap_ref, e_ref, o_ref, *, ck):
    n = NSA_HPG
    nq = n * Q_BLOCK
    qb = pl.program_id(2)
    t0 = qb * Q_BLOCK
    c2 = (HEAD_DIM ** -0.5) * 1.4426950408889634
    nsel = ks_ref.shape[2] // SEL_BLOCK
    q = q_ref[0].reshape(nq, HEAD_DIM)
    tq = t0 + lax.broadcasted_iota(jnp.int32, (Q_BLOCK, 1), 0)

    def biased(s, ok):
        kc = s.shape[-1]
        bias = jnp.where(ok, 0.0, NEG_INF)
        return (s.reshape(n, Q_BLOCK, kc) + bias[None]).reshape(nq, kc)

    def softmax(sb):
        m = jnp.max(sb, axis=-1, keepdims=True)
        p = jnp.exp2((sb - m) * c2)
        return p / jnp.sum(p, axis=-1, keepdims=True)

    kc = kc_ref[0, 0, 0]
    vc = vc_ref[0, 0, 0]
    ncmp = kc.shape[0]
    cidx = lax.broadcasted_iota(jnp.int32, (Q_BLOCK, ncmp), 1)
    valid_c = (cidx < ncmp - 1) & (cidx * CMP_STRIDE + (CMP_LEN - 1) <= tq)
    p = softmax(biased(_dot_nt(q, kc), valid_c)).reshape(n, Q_BLOCK, ncmp)
    p = p * (tq >= CMP_LEN - 1).astype(F32)[None]
    o_c = _dot(p.reshape(nq, ncmp).astype(BF16), vc)

    psum = p[0] + p[1] + p[2] + p[3]
    hi = psum.astype(BF16)
    lo = (psum - hi.astype(F32)).astype(BF16)
    imp = _dot(hi, cmap_ref[...]) + _dot(lo, cmap_ref[...])
    j = lax.broadcasted_iota(jnp.int32, (Q_BLOCK, LANES), 1)
    cur = tq // SEL_BLOCK
    imp = jnp.where(j == 0, 1e6, imp)
    imp = jnp.where(j == cur, 1e6, imp)
    imp = jnp.where(j == cur - 1, 1e6, imp)
    imp = jnp.where(j > cur, -1e6, imp)
    imp = jnp.where(j >= nsel, -3e6, imp)

    nrow = min(((nsel + 7) // 8) * 8, LANES)
    imp_t = imp.T[0:nrow]
    jr = lax.broadcasted_iota(jnp.int32, (nrow, Q_BLOCK), 0)
    rank = jnp.zeros((nrow, Q_BLOCK), F32)
    for i in range(min(nsel, LANES)):
        row = imp_t[i:i + 1, :]
        rank = rank + jnp.where(row > imp_t, 1.0, jnp.where(row == imp_t, jnp.where(jr > i, 1.0, 0.0), 0.0))
    sel_t = jnp.where(rank < float(SEL_TOP), 1.0, 0.0)
    if nrow < LANES:
        sel_t = jnp.concatenate([sel_t, jnp.zeros((LANES - nrow, Q_BLOCK), F32)], axis=0)
    sel = sel_t.T.astype(BF16)

    n_ch = (t0 + Q_BLOCK + ck - 1) // ck
    ones = jnp.ones((ck, LANES), BF16)

    def body(kb, carry):
        m_i, l_i, acc = carry
        k0 = pl.multiple_of(kb * ck, ck)
        kblk = ks_ref[0, 0, pl.ds(k0, ck), :]
        vblk = vs_ref[0, 0, pl.ds(k0, ck), :]
        em = _dot(sel, e_ref[kb])
        kpos = k0 + lax.broadcasted_iota(jnp.int32, (Q_BLOCK, ck), 1)
        sb = biased(_dot_nt(q, kblk), (kpos <= tq) & (em > 0.5))
        m_new = jnp.maximum(m_i, jnp.max(sb, axis=-1, keepdims=True))
        alpha = jnp.exp2((m_i - m_new) * c2)
        pp = jnp.exp2(((sb - m_new) * c2).astype(BF16))
        l_new = alpha * l_i + _dot(pp, ones)
        acc_new = alpha * acc + _dot(pp, vblk)
        return m_new, l_new, acc_new

    init = (jnp.full((nq, 1), NEG_INF, F32), jnp.zeros((nq, LANES), F32), jnp.zeros((nq, HEAD_DIM), F32))
    _, l_f, acc_f = lax.fori_loop(0, n_ch, body, init)
    o_s = acc_f / l_f

    wlen = min(WINDOW + Q_BLOCK, kw_ref.shape[2])
    w0 = pl.multiple_of(jnp.maximum(t0 + Q_BLOCK - wlen, 0), Q_BLOCK)
    kblk = kw_ref[0, 0, pl.ds(w0, wlen), :]
    vblk = vw_ref[0, 0, pl.ds(w0, wlen), :]
    kpos = w0 + lax.broadcasted_iota(jnp.int32, (Q_BLOCK, wlen), 1)
    pw = softmax(biased(_dot_nt(q, kblk), (kpos <= tq) & (tq - kpos < WINDOW)))
    o_w = _dot(pw.astype(BF16), vblk)

    gt = gt_ref[0, 0]
    for h in range(n):
        r = slice(h * Q_BLOCK, (h + 1) * Q_BLOCK)
        o = (gt[:, 3 * h:3 * h + 1] * o_c[r] + gt[:, 3 * h + 1:3 * h + 2] * o_s[r]
             + gt[:, 3 * h + 2:3 * h + 3] * o_w[r])
        o_ref[0, :, h * HEAD_DIM:(h + 1) * HEAD_DIM] = o.astype(o_ref.dtype)


def nsa_attention(qkr, kvp, cmp, gates, ck=512):
    B, _, S, dh = qkr.shape
    G, n = NSA_GROUPS, NSA_HPG
    ncmp = cmp.shape[3]
    nsel = S // SEL_BLOCK
    ck = min(ck, S)
    r_sel, r_cmp = SEL_BLOCK // CMP_STRIDE, CMP_LEN // CMP_STRIDE
    tgt = (r_sel * jnp.arange(nsel)[:, None, None] + jnp.arange(r_sel)[None, :, None]
           - jnp.arange(r_cmp)[None, None, :])
    cmap = (jnp.arange(ncmp - 1)[:, None, None, None] == tgt[None]).sum((2, 3)).astype(F32)
    cmap = jnp.pad(cmap, ((0, 1), (0, LANES - nsel))).astype(BF16)
    e = (jnp.arange(S)[None, :] // SEL_BLOCK == jnp.arange(LANES)[:, None]).astype(BF16)
    e3 = e.reshape(LANES, S // ck, ck).transpose(1, 0, 2)

    def head(h0):
        return pl.BlockSpec((1, 1, S, dh), lambda b, g, i: (b, h0 + g, 0, 0))

    return pl.pallas_call(
        functools.partial(_nsa_kernel, ck=ck),
        grid=(B, G, S // Q_BLOCK),
        in_specs=[
            pl.BlockSpec((1, n, Q_BLOCK, dh), lambda b, g, i: (b, g, i, 0)),
            head(16), head(20), head(8), head(12),
            pl.BlockSpec((1, 1, 1, ncmp, dh), lambda b, g, i: (0, b, g, 0, 0)),
            pl.BlockSpec((1, 1, 1, ncmp, dh), lambda b, g, i: (1, b, g, 0, 0)),
            pl.BlockSpec((1, 1, Q_BLOCK, LANES), lambda b, g, i: (b, g, i, 0)),
            pl.BlockSpec((ncmp, LANES), lambda b, g, i: (0, 0)),
            pl.BlockSpec((S // ck, LANES, ck), lambda b, g, i: (0, 0, 0)),
        ],
        out_specs=pl.BlockSpec((1, Q_BLOCK, n * dh), lambda b, g, i: (b, i, g)),
        out_shape=jax.ShapeDtypeStruct((B, S, G * n * dh), BF16),
        compiler_params=_cparams(("parallel", "parallel", "arbitrary")),
        name="nsa_attention",
    )(qkr, qkr, qkr, kvp, kvp, cmp, cmp, gates, cmap, e3)


def _rwkv_feat_kernel(z_ref, zp_ref, mu_ref, o_ref):
    i = pl.program_id(1)
    z = z_ref[0]
    prev = zp_ref[0, 7:8, :] * (i > 0).astype(F32)
    row = lax.broadcasted_iota(jnp.int32, z.shape, 0)
    sh = jnp.where(row == 0, prev, pltpu.roll(z, 1, 0))
    zs = z + (sh - z) * mu_ref[...]
    o_ref[0, :, 0:128] = jnp.tanh(zs[:, 0:128])
    o_ref[0, :, 128:256] = zs[:, 128:256]
    o_ref[0, :, 256:] = jax.nn.sigmoid(zs[:, 256:])


def rwkv_features(z_lora, mu_lora, tm=512):
    B, S, W = z_lora.shape
    tm = min(tm, S)
    return pl.pallas_call(
        _rwkv_feat_kernel,
        grid=(B, S // tm),
        in_specs=[pl.BlockSpec((1, tm, W), lambda b, i: (b, i, 0)),
                  pl.BlockSpec((1, 8, W), lambda b, i: (b, jnp.maximum(i * (tm // 8) - 1, 0), 0)),
                  pl.BlockSpec((1, W), lambda b, i: (0, 0))],
        out_specs=pl.BlockSpec((1, tm, W), lambda b, i: (b, i, 0)),
        out_shape=jax.ShapeDtypeStruct((B, S, W), F32),
        compiler_params=_cparams(("parallel", "arbitrary")),
        name="rwkv_features",
    )(z_lora, z_lora, mu_lora)


def _rwkv_kernel(r_ref, k_ref, v_ref, f_ref, par_ref, w2_ref, a2_ref, g2_ref, o_ref, st_ref, prev_ref):
    C = RWKV_CHUNK
    W = r_ref.shape[-1]
    npair = W // LANES
    c = pl.program_id(2)

    @pl.when(c == 0)
    def _():
        st_ref[...] = jnp.zeros_like(st_ref)
        prev_ref[...] = jnp.zeros_like(prev_ref)

    par = par_ref[...]
    mu_r, mu_k, mu_v = par[0:1], par[1:2], par[2:3]
    w0, a0, k_k, k_a, r_k, ln_w, ln_b = par[3:4], par[4:5], par[5:6], par[6:7], par[7:8], par[8:9], par[9:10]

    row = lax.broadcasted_iota(jnp.int32, (C, C), 0)
    col = lax.broadcasted_iota(jnp.int32, (C, C), 1)
    trow = lax.broadcasted_iota(jnp.int32, (C, W), 0)
    lane = lax.broadcasted_iota(jnp.int32, (C, LANES), 1)
    lr = lax.broadcasted_iota(jnp.int32, (LANES, LANES), 0)
    lc = lax.broadcasted_iota(jnp.int32, (LANES, LANES), 1)

    def shifted(z_ref, idx, mu):
        z = z_ref[0]
        sh = jnp.where(trow == 0, prev_ref[idx:idx + 1, :], pltpu.roll(z, 1, 0))
        prev_ref[idx:idx + 1, :] = z[C - 1:C, :]
        return z + (sh - z) * mu

    r = shifted(r_ref, 0, mu_r)
    k = shifted(k_ref, 1, mu_k)
    v = shifted(v_ref, 2, mu_v)
    f = f_ref[0]
    w_log = -_softplus(-(w0 + _dot3(_split(f[:, 0:128]), _split(w2_ref[...])))) - 0.5
    logd = -jnp.exp(w_log)
    a = jax.nn.sigmoid(a0 + _dot3(_split(f[:, 128:256]), _split(a2_ref[...])))
    g = _dot3(_split(f[:, 256:]), _split(g2_ref[...]))

    bd = _exact(jnp.where((lr // RWKV_HEAD) == (lc // RWKV_HEAD), 1.0, 0.0))

    def head_sum(x, passes=2):
        sp = _split if passes == 2 else (lambda z: (z.astype(BF16), None))
        return jnp.concatenate([_dot3(sp(x[:, p * LANES:(p + 1) * LANES]), bd) for p in range(npair)], axis=1)

    kk = k * k_k
    kk = kk / jnp.maximum(jnp.sqrt(head_sum(kk * kk)), 1e-12)
    k2 = k * (1.0 + (a - 1.0) * k_a)

    tri_incl = _exact(jnp.where(col <= row, 1.0, 0.0))
    cum = _dot3(tri_incl, _split(logd))
    cum_end = cum[C - 1:C, :]
    e_neg = jnp.exp(-cum)
    kka = kk * a
    a_t = -kk * jnp.exp(cum - logd)
    b_t = kka * e_neg
    k_t = k2 * e_neg
    r_t = r * jnp.exp(cum)
    e_end = jnp.exp(cum_end - cum)
    b_e = kka * e_end
    k_e = k2 * e_end
    p_end = jnp.exp(cum_end)

    strict = col < row
    incl = col <= row
    eye = jnp.where(col == row, 1.0, 0.0)
    n_sq = C.bit_length() - 2
    hpp = LANES // RWKV_HEAD
    heads = [(p, h) for p in range(npair) for h in range(hpp)]
    hd = []
    for p, h in heads:
        sl = slice(p * LANES, (p + 1) * LANES)
        mh = (lane // RWKV_HEAD) == h
        s_am = _split(jnp.where(mh, a_t[:, sl], 0.0))
        s_rm = _split(jnp.where(mh, r_t[:, sl], 0.0))
        hd.append(dict(
            sl=sl, s_am=s_am, s_rm=s_rm,
            s_vm=_split(jnp.where(mh, v[:, sl], 0.0)),
            s_bem=_split(jnp.where(mh, b_e[:, sl], 0.0)),
            s_kem=_split(jnp.where(mh, k_e[:, sl], 0.0)),
            rhs=_split(jnp.concatenate([b_t[:, sl], k_t[:, sl]], axis=0)) if h == 0 else hd[-1]["rhs"]))
    for d in hd:
        amat = _dot3(d["s_am"], d["rhs"], "nt")
        rmat = _dot_nt(d["s_rm"][0], d["rhs"][0])
        l_ab = jnp.where(strict, amat[:, 0:C], 0.0)
        d["l_ak"] = _split(jnp.where(strict, amat[:, C:2 * C], 0.0))
        d["l_rb"] = jnp.where(incl, rmat[:, 0:C], 0.0).astype(BF16)
        d["l_rk"] = jnp.where(incl, rmat[:, C:2 * C], 0.0).astype(BF16)
        d["x"] = eye + l_ab
        d["s_pw"] = _split(l_ab)
    for i in range(n_sq):
        for d in hd:
            d["s_pw"] = _split(_dot3(d["s_pw"], d["s_pw"])) if i == 0 else (_dot(d["s_pw"][0], d["s_pw"][0]).astype(BF16), None)
        for d in hd:
            d["x"] = d["x"] + (_dot3(_split(d["x"]), d["s_pw"]) if i == 0 else _dot(d["x"].astype(BF16), d["s_pw"][0]))
    for i, d in enumerate(hd):
        d["st"] = st_ref[i]
        d["s_st"] = _split(d["st"])
        d["t"] = _split(_dot3(d["s_am"], d["s_st"], "nt") + _dot3(d["l_ak"], d["s_vm"]))
    for d in hd:
        d["s_u"] = _split(_dot3(_split(d["x"]), d["t"]))
    for i, d in enumerate(hd):
        st_ref[i] = d["st"] * p_end[:, d["sl"]] + _dot3(d["s_u"], d["s_bem"], "tn") + _dot3(d["s_vm"], d["s_kem"], "tn")
    ys = []
    for d in hd:
        ys.append(_dot_nt(d["s_rm"][0], d["s_st"][0]) + _dot(d["l_rb"], d["s_u"][0]) + _dot(d["l_rk"], d["s_vm"][0]))
    y = jnp.concatenate([sum(ys[hpp * p + 1:hpp * (p + 1)], ys[hpp * p]) for p in range(npair)], axis=1)

    inv_n = 1.0 / RWKV_HEAD
    mean = head_sum(y) * inv_n
    d = y - mean
    var = head_sum(d * d) * inv_n
    yn = d * lax.rsqrt(var + GN_EPS) * ln_w + ln_b
    bonus = head_sum(r * k2 * r_k, passes=1) * v
    o_ref[0] = ((yn + bonus) * g).astype(o_ref.dtype)


def rwkv_scan(z_rkv, feat, par, w2p, a2p, g2, pairs_per_step=8):
    B, S, R3 = z_rkv.shape
    R = R3 // 3
    W = pairs_per_step * LANES
    nblk = R // W
    C = RWKV_CHUNK
    assert S % C == 0 and R % W == 0

    def col(off):
        return pl.BlockSpec((1, C, W), lambda b, p, c: (b, c, off + p))

    return pl.pallas_call(
        _rwkv_kernel,
        grid=(B, nblk, S // C),
        in_specs=[
            col(0), col(nblk), col(2 * nblk),
            pl.BlockSpec((1, C, feat.shape[-1]), lambda b, p, c: (b, c, 0)),
            pl.BlockSpec((16, W), lambda b, p, c: (0, p)),
            pl.BlockSpec((LANES, W), lambda b, p, c: (0, p)),
            pl.BlockSpec((LANES, W), lambda b, p, c: (0, p)),
            pl.BlockSpec((GATE_LORA, W), lambda b, p, c: (0, p)),
        ],
        out_specs=pl.BlockSpec((1, C, W), lambda b, p, c: (b, c, p)),
        out_shape=jax.ShapeDtypeStruct((B, S, R), BF16),
        scratch_shapes=[pltpu.VMEM((W // RWKV_HEAD, LANES, LANES), F32), pltpu.VMEM((8, W), F32)],
        compiler_params=_cparams(("parallel", "parallel", "arbitrary")),
        name="rwkv_scan",
    )(z_rkv, z_rkv, z_rkv, feat, par, w2p, a2p, g2)


def _merge_kernel(on_ref, or_ref, wn_ref, wr_ref, ga_ref, gb_ref, o_ref):
    pa = _dot(on_ref[...], wn_ref[...])
    pb = _dot(or_ref[...], wr_ref[...])
    o_ref[...] = (ga_ref[...].astype(F32) * pa + gb_ref[...].astype(F32) * pb).astype(o_ref.dtype)


def merge_mixers(o_nsa, o_rwkv, wn, wr, gates, tm=1024, tn=512):
    M, Kn = o_nsa.shape
    Kr = o_rwkv.shape[1]
    D = wn.shape[1]
    tm, tn = min(tm, M), min(tn, D)
    nb = D // tn
    return pl.pallas_call(
        _merge_kernel,
        grid=(M // tm, nb),
        in_specs=[
            pl.BlockSpec((tm, Kn), lambda i, j: (i, 0)),
            pl.BlockSpec((tm, Kr), lambda i, j: (i, 0)),
            pl.BlockSpec((Kn, tn), lambda i, j: (0, j)),
            pl.BlockSpec((Kr, tn), lambda i, j: (0, j)),
            pl.BlockSpec((tm, tn), lambda i, j: (i, j)),
            pl.BlockSpec((tm, tn), lambda i, j: (i, nb + j)),
        ],
        out_specs=pl.BlockSpec((tm, tn), lambda i, j: (i, j)),
        out_shape=jax.ShapeDtypeStruct((M, D), BF16),
        compiler_params=_cparams(("parallel", "parallel")),
        name="merge_mixers",
    )(o_nsa, o_rwkv, wn, wr, gates, gates)


def _xattn_kernel(q_ref, kv_ref, wo_ref, x_ref, o_ref):
    nh = XATTN_HEADS
    scale = HEAD_DIM ** -0.5
    outs = []
    for h in range(nh):
        q = q_ref[:, h * HEAD_DIM:(h + 1) * HEAD_DIM]
        k = kv_ref[0, :, h * HEAD_DIM:(h + 1) * HEAD_DIM]
        v = kv_ref[0, :, (nh + h) * HEAD_DIM:(nh + h + 1) * HEAD_DIM]
        s = _dot_nt(q, k) * scale
        m = jnp.max(s, axis=-1, keepdims=True)
        p = jnp.exp(s - m)
        p = p / jnp.sum(p, axis=-1, keepdims=True)
        outs.append(_dot(p.astype(BF16), v).astype(BF16))
    o = jnp.concatenate(outs, axis=-1)
    o_ref[...] = x_ref[...] + _dot(o, wo_ref[...])


def cross_attention(q, kv, wo, x, seq, tm=256):
    T, D = x.shape
    tm = min(tm, seq)
    spb = seq // tm
    Mm = kv.shape[1]
    return pl.pallas_call(
        _xattn_kernel,
        grid=(T // tm,),
        in_specs=[
            pl.BlockSpec((tm, q.shape[1]), lambda i: (i, 0)),
            pl.BlockSpec((1, Mm, kv.shape[2]), lambda i: (i // spb, 0, 0)),
            pl.BlockSpec(wo.shape, lambda i: (0, 0)),
            pl.BlockSpec((tm, D), lambda i: (i, 0)),
        ],
        out_specs=pl.BlockSpec((tm, D), lambda i: (i, 0)),
        out_shape=jax.ShapeDtypeStruct((T, D), F32),
        compiler_params=_cparams(("parallel",)),
        name="cross_attention",
    )(q, kv, wo, x)


def _top16(x, iota=None, n=None):
    if iota is None:
        n = x.shape[0]
        iota = lax.broadcasted_iota(jnp.int32, x.shape, 0).astype(F32)
    cur = x
    rank = jnp.full(x.shape, float(PEER_TOPK), F32)
    vals = []
    for r in range(PEER_TOPK):
        m = jnp.max(cur, axis=0, keepdims=True)
        idx = jnp.min(jnp.where(cur == m, iota, float(n)), axis=0, keepdims=True)
        hit = iota == idx
        rank = jnp.where(hit, float(r), rank)
        cur = jnp.where(hit, -jnp.inf, cur)
        vals.append(m)
    return jnp.concatenate(vals, axis=0), rank


def _peer_route_kernel(q_ref, keys_ref, row_ref, pl_ref):
    K = PEER_TOPK
    q = q_ref[...]
    half = q.shape[1] // 2
    s1 = _dotf_nt(keys_ref[0, 0], q[:, :half])
    s2 = _dotf_nt(keys_ref[0, 1], q[:, half:])
    v1, rank1 = _top16(s1)
    v2, rank2 = _top16(s2)
    tmq = q.shape[0]
    r8 = lax.broadcasted_iota(jnp.int32, (8, tmq), 0)
    ninf = -jnp.inf
    lo8 = v2[0:8]
    tiles = [
        (v1[0:1] + lo8, r8),
        (v1[0:1] + v2[8:16], 8 + r8),
        (v1[1:2] + lo8, 16 + r8),
        (jnp.where(r8 < 5, v1[2:3] + lo8, ninf), 32 + r8),
        (jnp.where(r8 < 7, jnp.where(r8 < 4, v1[3:4], v1[4:5]) + jnp.where(r8 < 4, lo8, pltpu.roll(lo8, 4, 0)), ninf),
         jnp.where(r8 < 4, 48 + r8, 60 + r8)),
        (jnp.where(r8 < 6, jnp.where(r8 < 2, v1[5:6], jnp.where(r8 < 4, v1[6:7], v1[7:8]))
                   + jnp.where(r8 % 2 == 0, v2[0:1], v2[1:2]), ninf),
         80 + (r8 // 2) * 16 + r8 % 2),
        (v1[8:16] + v2[0:1], (8 + r8) * 16),
    ]
    cand = jnp.concatenate([t for t, _ in tiles], axis=0)
    cidx = jnp.concatenate([i for _, i in tiles], axis=0).astype(F32)
    top_s, crank = _top16(cand, cidx, K * K)
    chosen = jnp.where(crank < float(K), 1.0, 0.0)
    z = jnp.sum(jnp.exp(top_s - top_s[0:1]), axis=0, keepdims=True)
    ch = [chosen[8 * t:8 * (t + 1)] for t in range(len(tiles))]
    colsum = lambda x: jnp.sum(x, axis=0, keepdims=True)
    upper = (r8 < 4).astype(F32)
    j_rows = [colsum(ch[0]) + colsum(ch[1]), colsum(ch[2]), colsum(ch[3]),
              colsum(ch[4] * upper), colsum(ch[4] * (1.0 - upper))]
    j_rows += [ch[5][2 * m:2 * m + 1] + ch[5][2 * m + 1:2 * m + 2] for m in range(3)]
    j_rows += [ch[6][m:m + 1] for m in range(8)]
    jn = jnp.zeros_like(s1)
    for i in range(K):
        jn = jnp.where(rank1 == float(i), j_rows[i], jn)
    row_ref[0, 0] = jnp.exp(s1 - v1[0:1]) / z
    row_ref[0, 1] = jn
    pl_ref[0, 0] = jnp.exp(s2 - v2[0:1]).astype(pl_ref.dtype)
    pl_ref[0, 1] = rank2.astype(pl_ref.dtype)


def peer_route(q, keys, tm=256):
    T = q.shape[0]
    H, _, nkeys, hd = keys.shape
    tm = min(tm, T)
    spec = pl.BlockSpec((1, 2, nkeys, tm), lambda i, h: (h, 0, 0, i))
    return pl.pallas_call(
        _peer_route_kernel,
        grid=(T // tm, H),
        in_specs=[pl.BlockSpec((tm, 2 * hd), lambda i, h: (i, h)),
                  pl.BlockSpec((1, 2, nkeys, hd), lambda i, h: (h, 0, 0, 0))],
        out_specs=[spec, spec],
        out_shape=[jax.ShapeDtypeStruct((H, 2, nkeys, T), F32), jax.ShapeDtypeStruct((H, 2, nkeys, T), BF16)],
        compiler_params=_cparams(("parallel", "parallel")),
        name="peer_route",
    )(q, keys)


def _peer_act_kernel(h_ref, u_ref, row_ref, pl_ref, o_ref):
    nk = PEER_KEYS
    ec = u_ref.shape[0]
    j = pl.program_id(1)
    hu = _dot(u_ref[...], h_ref[...])
    zero = jnp.zeros((), o_ref.dtype)
    for al in range(ec // nk):
        a = j * (ec // nk) + al
        w = None
        for hd in range(row_ref.shape[0]):
            e1 = row_ref[hd, 0, pl.ds(a, 1), :].astype(o_ref.dtype)
            jn = row_ref[hd, 1, pl.ds(a, 1), :].astype(o_ref.dtype)
            g = jnp.where(pl_ref[hd, 1] < jn, e1 * pl_ref[hd, 0], zero)
            w = g if w is None else w + g
        act = _gelu(hu[al * nk:(al + 1) * nk, :].astype(o_ref.dtype)) * w
        o_ref[:, al * nk:(al + 1) * nk] = act.T


def peer_activations(h_t, u_tab, rows, planes, tm=512, ec=512):
    D, T = h_t.shape
    E = u_tab.shape[0]
    tm, ec = min(tm, T), min(ec, E)
    H, _, nkeys, _ = rows.shape
    rspec = pl.BlockSpec((H, 2, nkeys, tm), lambda i, j: (0, 0, 0, i))
    return pl.pallas_call(
        _peer_act_kernel,
        grid=(T // tm, E // ec),
        in_specs=[pl.BlockSpec((D, tm), lambda i, j: (0, i)),
                  pl.BlockSpec((ec, D), lambda i, j: (j, 0)),
                  rspec, rspec],
        out_specs=pl.BlockSpec((tm, ec), lambda i, j: (i, j)),
        out_shape=jax.ShapeDtypeStruct((T, E), BF16),
        compiler_params=_cparams(("parallel", "arbitrary")),
        name="peer_activations",
    )(h_t, u_tab, rows, planes)


def kernel(x, mem, positions, norm_mix, w_in, nsa_cmp_pe, nsa_cmp_w1, nsa_cmp_w2, rwkv_mu, rwkv_w0, rwkv_w2, rwkv_a0, rwkv_a2, rwkv_g2, rwkv_k_k, rwkv_k_a, rwkv_r_k, rwkv_ln_w, rwkv_ln_b, w_proj_nsa, w_proj_rwkv, w_out, norm_xattn, norm_mem, xattn_wq, xattn_wkv, xattn_wo, norm_ffn, peer_wq, peer_keys, peer_u, peer_v, norm_final):
    B, S, D = x.shape
    T = B * S
    depth = w_in.shape[0]
    G, n, dh = NSA_GROUPS, NSA_HPG, HEAD_DIM
    nq_cols = G * n * dh
    nkv_cols = 6 * G * dh
    ngate = 3 * G * n
    R = rwkv_w0.shape[1]
    c_q, c_kv, c_g = nq_cols, nq_cols + nkv_cols, nq_cols + nkv_cols + ngate
    c_r = c_g + 3 * R + DECAY_LORA + AAA_LORA + GATE_LORA
    gd = G * dh

    pos_flat = positions.reshape(T)
    rope_tabs = _rope_tables(pos_flat)
    cend = jnp.arange(S // CMP_STRIDE) * CMP_STRIDE + (CMP_LEN - 1)
    ctab = _rope_tables(positions[:, jnp.minimum(cend, S - 1)])
    ident = _rope_tables(jnp.zeros_like(positions[:, :S // CMP_STRIDE]))
    cmp_tabs = [jnp.stack([a, b]) for a, b in zip(ctab, ident)]

    xf = x.reshape(T, D)
    for l in range(depth):
        segs, groups = [], []

        def piece(*ranges):
            first = len(segs) * LANES
            for c0, ncol in ranges:
                segs.extend((c0 + c, min(LANES, ncol - c)) for c in range(0, ncol, LANES))
            groups.append((first, len(segs) * LANES - first))

        piece((0, c_q), (c_q + 2 * gd, gd), (c_q + 4 * gd, gd))
        piece((c_q, 2 * gd), (c_q + 3 * gd, gd), (c_q + 5 * gd, gd))
        piece(*[(c_kv + 3 * n * g, 3 * n) for g in range(G)])
        piece((c_g, 3 * R))
        piece((c_g + 3 * R, DECAY_LORA), (c_g + 3 * R + DECAY_LORA, AAA_LORA), (c_g + 3 * R + DECAY_LORA + AAA_LORA, GATE_LORA))
        piece((c_r, 2 * D))
        wp = repack_columns(w_in[l], segs)
        g_rope, g_plain, g_gate, g_rkv, g_lora, g_merge = groups

        h = rmsnorm(xf, norm_mix[l], BF16)
        qkr = matmul(h, wp, wcols=g_rope, mode="heads_rope", extras=rope_tabs, out_dtype=BF16, seq=S, name="proj_rope")
        kvp = matmul(h, wp, wcols=g_plain, mode="heads", out_dtype=BF16, seq=S, name="proj_kv")
        gates_nsa = matmul(h, wp, wcols=g_gate, mode="heads_sigmoid", out_dtype=F32, seq=S, name="proj_gate")
        z_rkv = matmul(h, wp, wcols=g_rkv, out_dtype=F32, name="proj_rkv")
        z_lora = matmul(h, wp, wcols=g_lora, out_dtype=F32, name="proj_lora")
        gates_merge = matmul(h, wp, wcols=g_merge, mode="sigmoid", out_dtype=BF16, name="proj_merge")

        pe8 = jnp.broadcast_to(nsa_cmp_pe[l].reshape(2, 1, CMP_LEN * dh), (2, 8, CMP_LEN * dh)).astype(BF16)
        cmp = nsa_compress(kvp, nsa_cmp_w1[l].astype(BF16), nsa_cmp_w2[l].astype(BF16), pe8, cmp_tabs)
        o_nsa = nsa_attention(qkr, kvp, cmp, gates_nsa).reshape(T, nq_cols)

        mu = rwkv_mu[l]
        mul = mu[3 * R:]
        pad1 = lambda a, n_: jnp.pad(a, (0, n_ - a.shape[0]))
        mu_lora = jnp.concatenate([pad1(mul[:DECAY_LORA], LANES), pad1(mul[DECAY_LORA:DECAY_LORA + AAA_LORA], LANES),
                                   mul[DECAY_LORA + AAA_LORA:]]).reshape(1, -1)
        feat = rwkv_features(z_lora.reshape(B, S, -1), mu_lora)
        par = jnp.stack([mu[:R], mu[R:2 * R], mu[2 * R:3 * R], rwkv_w0[l], rwkv_a0[l], rwkv_k_k[l], rwkv_k_a[l],
                         rwkv_r_k[l].reshape(R), rwkv_ln_w[l], rwkv_ln_b[l]])
        par = jnp.pad(par, ((0, 16 - par.shape[0]), (0, 0)))
        w2p = jnp.pad(rwkv_w2[l], ((0, LANES - DECAY_LORA), (0, 0)))
        a2p = jnp.pad(rwkv_a2[l], ((0, LANES - AAA_LORA), (0, 0)))
        o_rwkv = rwkv_scan(z_rkv.reshape(B, S, 3 * R), feat, par, w2p, a2p, rwkv_g2[l]).reshape(T, R)

        merged = merge_mixers(o_nsa, o_rwkv, w_proj_nsa[l].astype(BF16), w_proj_rwkv[l].astype(BF16), gates_merge)
        xf = matmul(merged, w_out[l].astype(BF16), mode="residual", extras=(xf,), out_dtype=F32, name="out_proj")

        h = rmsnorm(xf, norm_xattn[l], BF16)
        m = rmsnorm(mem.reshape(-1, D), norm_mem[l], BF16)
        q = matmul(h, xattn_wq[l].astype(BF16), out_dtype=BF16, name="xattn_q")
        kvm = matmul(m, xattn_wkv[l].astype(BF16), out_dtype=BF16, name="xattn_kv").reshape(B, mem.shape[1], -1)
        xf = cross_attention(q, kvm, xattn_wo[l].astype(BF16), xf, S)

        h, h_t = rmsnorm(xf, norm_ffn[l], BF16, with_transpose=True)
        pq = matmul(h, peer_wq[l].astype(BF16), out_dtype=F32, name="peer_q")
        rows, planes = peer_route(pq, peer_keys[l])
        act = peer_activations(h_t, peer_u[l].astype(BF16), rows, planes)
        xf = matmul(act, peer_v[l].astype(BF16), mode="residual", extras=(xf,), out_dtype=F32,
                    tm=1024, tn=1024, tk=1024, name="peer_out")

    return rmsnorm(xf, norm_final, F32).reshape(B, S, D)
```

```python
import functools

import jax
import jax.numpy as jnp
from jax import lax
from jax.experimental import pallas as pl
from jax.experimental.pallas import tpu as pltpu

F32 = jnp.float32
BF16 = jnp.bfloat16
HI = lax.Precision.HIGHEST

LANES = 128
HEAD_DIM = 128
ROPE_DIM = HEAD_DIM // 4
ROPE_THETA = 500000.0
RMS_EPS = 1e-6
NEG_INF = -1e30

NSA_GROUPS = 4
NSA_HPG = 4
CMP_LEN = 32
CMP_STRIDE = 16
SEL_BLOCK = 64
SEL_TOP = 16
WINDOW = 512
Q_BLOCK = 128

RWKV_HEAD = 64
DECAY_LORA = 96
AAA_LORA = 96
GATE_LORA = 256
GN_EPS = 64e-5
RWKV_CHUNK = 128

XATTN_HEADS = 4
PEER_HEADS = 8
PEER_KEYS = 128
PEER_TOPK = 16

VMEM_LIMIT = 56 * 1024 * 1024


def _cparams(sem, flags=None):
    return pltpu.CompilerParams(dimension_semantics=sem, vmem_limit_bytes=VMEM_LIMIT, flags=flags)


def _gelu(x):
    return 0.5 * x * (1.0 + jnp.tanh(0.7978845608028654 * (x + 0.044715 * x * x * x)))


def _softplus(x):
    return jnp.maximum(x, 0.0) + jnp.log(1.0 + jnp.exp(-jnp.abs(x)))


def _dot(a, b):
    return jnp.dot(a, b, preferred_element_type=F32)


def _dot_nt(a, b):
    return lax.dot_general(a, b, (((1,), (1,)), ((), ())), preferred_element_type=F32)


def _dotf_nt(a, b):
    return lax.dot_general(a, b, (((1,), (1,)), ((), ())), preferred_element_type=F32, precision=HI)


def _split(x):
    hi = x.astype(BF16)
    return hi, (x - hi.astype(F32)).astype(BF16)


def _exact(x):
    return x.astype(BF16), None


_DN = {"nn": (((1,), (0,)), ((), ())), "nt": (((1,), (1,)), ((), ())), "tn": (((0,), (0,)), ((), ()))}


def _dot3(a, b, form="nn"):
    d = lambda x, y: lax.dot_general(x, y, _DN[form], preferred_element_type=F32)
    out = d(a[0], b[0])
    if a[1] is not None:
        out = out + d(a[1], b[0])
    if b[1] is not None:
        out = out + d(a[0], b[1])
    return out


def _rope_apply(a, c, sa, sb):
    return a * c + pltpu.roll(a, LANES - ROPE_DIM // 2, 1) * sa + pltpu.roll(a, ROPE_DIM // 2, 1) * sb


def _rope_tables(pos):
    half = ROPE_DIM // 2
    inv = ROPE_THETA ** (-jnp.arange(half, dtype=F32) / half)
    ang = pos.astype(F32)[..., None] * inv
    cos, sin = jnp.cos(ang), jnp.sin(ang)
    shp = pos.shape + (HEAD_DIM - ROPE_DIM,)
    c = jnp.concatenate([cos, cos, jnp.ones(shp, F32)], axis=-1)
    sa = jnp.concatenate([-sin, jnp.zeros(pos.shape + (HEAD_DIM - half,), F32)], axis=-1)
    sb = jnp.concatenate([jnp.zeros(pos.shape + (half,), F32), sin, jnp.zeros(shp, F32)], axis=-1)
    return c, sa, sb


def _rmsnorm_kernel(x_ref, g_ref, o_ref, *ot_ref):
    x = x_ref[...].astype(F32)
    ms = jnp.mean(x * x, axis=-1, keepdims=True)
    y = x * lax.rsqrt(ms + RMS_EPS) * g_ref[...]
    o_ref[...] = y.astype(o_ref.dtype)
    if ot_ref:
        ot_ref[0][...] = y.T.astype(ot_ref[0].dtype)


def rmsnorm(x2d, g, out_dtype, tm=256, with_transpose=False):
    M, D = x2d.shape
    tm = min(tm, M)
    out_specs = [pl.BlockSpec((tm, D), lambda i: (i, 0))]
    out_shape = [jax.ShapeDtypeStruct((M, D), out_dtype)]
    if with_transpose:
        out_specs.append(pl.BlockSpec((D, tm), lambda i: (0, i)))
        out_shape.append(jax.ShapeDtypeStruct((D, M), out_dtype))
    out = pl.pallas_call(
        _rmsnorm_kernel,
        grid=(M // tm,),
        in_specs=[pl.BlockSpec((tm, D), lambda i: (i, 0)), pl.BlockSpec((1, D), lambda i: (0, 0))],
        out_specs=out_specs,
        out_shape=out_shape,
        compiler_params=_cparams(("parallel",)),
        name="rmsnorm",
    )(x2d, g.reshape(1, D).astype(F32))
    return out if with_transpose else out[0]


def _mm_kernel(*refs, nk, mode, n_extra, w_t):
    x_ref, w_ref = refs[0], refs[1]
    extras = refs[2:2 + n_extra]
    o_ref = refs[2 + n_extra]

    def product():
        w = w_ref[...]
        if w.dtype != x_ref.dtype:
            w = w.astype(x_ref.dtype)
        return _dot_nt(x_ref[...], w) if w_t else _dot(x_ref[...], w)

    def epilogue(acc):
        if mode == "plain":
            o_ref[...] = acc.astype(o_ref.dtype)
        elif mode == "sigmoid":
            o_ref[...] = jax.nn.sigmoid(acc).astype(o_ref.dtype)
        elif mode == "residual":
            o_ref[...] = (extras[0][...] + acc).astype(o_ref.dtype)
        elif mode in ("heads", "heads_sigmoid", "heads_rope"):
            nh = acc.shape[1] // LANES
            for j in range(nh):
                a = acc[:, j * LANES:(j + 1) * LANES]
                if mode == "heads_sigmoid":
                    a = jax.nn.sigmoid(a)
                elif mode == "heads_rope":
                    a = _rope_apply(a, extras[0][...], extras[1][...], extras[2][...])
                o_ref[0, j] = a.astype(o_ref.dtype)
        else:
            raise ValueError(mode)

    if nk == 1:
        epilogue(product())
    else:
        acc_ref = refs[-1]
        k = pl.program_id(2)

        @pl.when(k == 0)
        def _():
            acc_ref[...] = jnp.zeros_like(acc_ref)

        acc_ref[...] += product()

        @pl.when(k == nk - 1)
        def _():
            epilogue(acc_ref[...])


def matmul(x, w, *, mode="plain", extras=(), out_dtype=F32, tm=1024, tn=512, tk=None, seq=None, name="mm",
           w_t=False, wcols=None):
    M, K = x.shape
    wcol0, N = (0, w.shape[0 if w_t else 1]) if wcols is None else wcols
    tm, tn = min(tm, M), min(tn, N)
    assert wcol0 % tn == 0
    wblk0 = wcol0 // tn
    if mode.startswith("heads"):
        tm = min(tm, seq)
    tk = K if tk is None else min(tk, K)
    nk = K // tk
    assert M % tm == 0 and N % tn == 0 and K % tk == 0
    grid = (M // tm, N // tn, nk)
    if w_t:
        w_spec = pl.BlockSpec((tn, tk), lambda i, j, k: (wblk0 + j, k))
    else:
        w_spec = pl.BlockSpec((tk, tn), lambda i, j, k: (k, wblk0 + j))
    in_specs = [pl.BlockSpec((tm, tk), lambda i, j, k: (i, k)), w_spec]
    if mode == "residual":
        in_specs.append(pl.BlockSpec((tm, tn), lambda i, j, k: (i, j)))
    elif mode == "heads_rope":
        in_specs += [pl.BlockSpec((tm, LANES), lambda i, j, k: (i, 0))] * 3
    if mode.startswith("heads"):
        assert seq % tm == 0 and tn % LANES == 0
        spb = seq // tm
        out_shape = jax.ShapeDtypeStruct((M // seq, N // LANES, seq, LANES), out_dtype)
        out_spec = pl.BlockSpec((1, tn // LANES, tm, LANES), lambda i, j, k: (i // spb, j, i % spb, 0))
    else:
        out_shape = jax.ShapeDtypeStruct((M, N), out_dtype)
        out_spec = pl.BlockSpec((tm, tn), lambda i, j, k: (i, j))
    scratch = [pltpu.VMEM((tm, tn), F32)] if nk > 1 else []
    return pl.pallas_call(
        functools.partial(_mm_kernel, nk=nk, mode=mode, n_extra=len(extras), w_t=w_t),
        grid=grid,
        in_specs=in_specs,
        out_specs=out_spec,
        out_shape=out_shape,
        scratch_shapes=scratch,
        compiler_params=_cparams(("parallel", "parallel", "arbitrary")),
        name=name,
    )(x, w, *extras)


SUBLANES = 8


def _repack_kernel(tab_ref, w_ref, o_ref):
    o_ref[...] = w_ref[0].astype(o_ref.dtype)


def repack_rows(wt3, l, pieces, tr=512):
    _, NC, D = wt3.shape
    assert all(r0 % SUBLANES == 0 and nrow % tr == 0 for r0, nrow in pieces)
    starts = [r0 + r for r0, nrow in pieces for r in range(0, nrow, tr)]
    tab = jnp.asarray([s // SUBLANES for s in starts], jnp.int32)
    grid_spec = pltpu.PrefetchScalarGridSpec(
        num_scalar_prefetch=1, grid=(len(starts),),
        in_specs=[pl.BlockSpec((pl.Element(1), pl.Element(tr), pl.Element(D)),
                               lambda j, t: (l, t[j] * SUBLANES, 0))],
        out_specs=pl.BlockSpec((tr, D), lambda j, t: (j, 0)))
    return pl.pallas_call(
        _repack_kernel, grid_spec=grid_spec, out_shape=jax.ShapeDtypeStruct((len(starts) * tr, D), BF16),
        compiler_params=_cparams(("arbitrary",)), name="repack_w_in",
    )(tab, wt3)


def _compress_kernel(x_ref, w1_ref, w2_ref, pe_ref, c_ref, sa_ref, sb_ref, o_ref):
    half = x_ref.shape[-1]
    x = x_ref[0, 0]
    a = _dot(x, w1_ref[0, :half, :])
    b = _dot(x, w1_ref[0, half:, :])
    pet = _dot(pe_ref[0], w1_ref[0])
    n = a.shape[0]
    hid = a + pltpu.roll(b, n - 1, 0) + pet[0:1]
    out = _dot(_gelu(hid).astype(BF16), w2_ref[0])
    out = _rope_apply(out, c_ref[0, 0], sa_ref[0, 0], sb_ref[0, 0])
    o_ref[0, 0, 0] = out.astype(o_ref.dtype)


def nsa_compress(kvp, w1, w2, pe8, tabs):
    B, _, S, dh = kvp.shape
    G = NSA_GROUPS
    nch = S // CMP_STRIDE
    xv = kvp.reshape(B, kvp.shape[1], nch, CMP_STRIDE * dh)
    hidden = w1.shape[-1]
    tab_spec = pl.BlockSpec((1, 1, nch, LANES), lambda w, b, g: (w, b, 0, 0))
    return pl.pallas_call(
        _compress_kernel,
        grid=(2, B, G),
        in_specs=[
            pl.BlockSpec((1, 1, nch, CMP_STRIDE * dh), lambda w, b, g: (b, w * G + g, 0, 0)),
            pl.BlockSpec((1, CMP_LEN * dh, hidden), lambda w, b, g: (w, 0, 0)),
            pl.BlockSpec((1, hidden, dh), lambda w, b, g: (w, 0, 0)),
            pl.BlockSpec((1, 8, CMP_LEN * dh), lambda w, b, g: (w, 0, 0)),
            tab_spec, tab_spec, tab_spec,
        ],
        out_specs=pl.BlockSpec((1, 1, 1, nch, dh), lambda w, b, g: (w, b, g, 0, 0)),
        out_shape=jax.ShapeDtypeStruct((2, B, G, nch, dh), BF16),
        compiler_params=_cparams(("parallel", "parallel", "parallel")),
        name="nsa_compress",
    )(xv, w1, w2, pe8, *tabs)


def _nsa_kernel(q_ref, ks_ref, kw_ref, vs_ref, vw_ref, kc_ref, vc_ref, gt_ref, cmap_ref, e_ref, o_ref, *, ck):
    n = NSA_HPG
    nq = n * Q_BLOCK
    qb = pl.program_id(2)
    t0 = qb * Q_BLOCK
    c2 = (HEAD_DIM ** -0.5) * 1.4426950408889634
    nsel = ks_ref.shape[2] // SEL_BLOCK
    q = q_ref[0].reshape(nq, HEAD_DIM)
    tq = t0 + lax.broadcasted_iota(jnp.int32, (Q_BLOCK, 1), 0)

    def biased(s, ok):
        kc = s.shape[-1]
        bias = jnp.where(ok, 0.0, NEG_INF)
        return (s.reshape(n, Q_BLOCK, kc) + bias[None]).reshape(nq, kc)

    def softmax(sb):
        m = jnp.max(sb, axis=-1, keepdims=True)
        p = jnp.exp2((sb - m) * c2)
        return p / jnp.sum(p, axis=-1, keepdims=True)

    kc = kc_ref[0, 0, 0]
    vc = vc_ref[0, 0, 0]
    ncmp = kc.shape[0]
    cidx = lax.broadcasted_iota(jnp.int32, (Q_BLOCK, ncmp), 1)
    valid_c = (cidx < ncmp - 1) & (cidx * CMP_STRIDE + (CMP_LEN - 1) <= tq)
    p = softmax(biased(_dot_nt(q, kc), valid_c)).reshape(n, Q_BLOCK, ncmp)
    p = p * (tq >= CMP_LEN - 1).astype(F32)[None]
    o_c = _dot(p.reshape(nq, ncmp).astype(BF16), vc)

    psum = p[0] + p[1] + p[2] + p[3]
    hi = psum.astype(BF16)
    lo = (psum - hi.astype(F32)).astype(BF16)
    imp = _dot(hi, cmap_ref[...]) + _dot(lo, cmap_ref[...])
    j = lax.broadcasted_iota(jnp.int32, (Q_BLOCK, LANES), 1)
    cur = tq // SEL_BLOCK
    imp = jnp.where(j == 0, 1e6, imp)
    imp = jnp.where(j == cur, 1e6, imp)
    imp = jnp.where(j == cur - 1, 1e6, imp)
    imp = jnp.where(j > cur, -1e6, imp)
    imp = jnp.where(j >= nsel, -3e6, imp)

    nrow = min(((nsel + 7) // 8) * 8, LANES)
    imp_t = imp.T[0:nrow]
    jr = lax.broadcasted_iota(jnp.int32, (nrow, Q_BLOCK), 0)
    rank = jnp.zeros((nrow, Q_BLOCK), F32)
    for i in range(min(nsel, LANES)):
        row = imp_t[i:i + 1, :]
        rank = rank + jnp.where(row > imp_t, 1.0, jnp.where(row == imp_t, jnp.where(jr > i, 1.0, 0.0), 0.0))
    sel_t = jnp.where(rank < float(SEL_TOP), 1.0, 0.0)
    if nrow < LANES:
        sel_t = jnp.concatenate([sel_t, jnp.zeros((LANES - nrow, Q_BLOCK), F32)], axis=0)
    sel = sel_t.T.astype(BF16)

    n_ch = (t0 + Q_BLOCK + ck - 1) // ck
    ones = jnp.ones((ck, LANES), BF16)

    def body(kb, carry):
        m_i, l_i, acc = carry
        k0 = pl.multiple_of(kb * ck, ck)
        kblk = ks_ref[0, 0, pl.ds(k0, ck), :]
        vblk = vs_ref[0, 0, pl.ds(k0, ck), :]
        em = _dot(sel, e_ref[kb])
        kpos = k0 + lax.broadcasted_iota(jnp.int32, (Q_BLOCK, ck), 1)
        sb = biased(_dot_nt(q, kblk), (kpos <= tq) & (em > 0.5))
        m_new = jnp.maximum(m_i, jnp.max(sb, axis=-1, keepdims=True))
        alpha = jnp.exp2((m_i - m_new) * c2)
        pp = jnp.exp2(((sb - m_new) * c2).astype(BF16))
        l_new = alpha * l_i + _dot(pp, ones)
        acc_new = alpha * acc + _dot(pp, vblk)
        return m_new, l_new, acc_new

    init = (jnp.full((nq, 1), NEG_INF, F32), jnp.zeros((nq, LANES), F32), jnp.zeros((nq, HEAD_DIM), F32))
    _, l_f, acc_f = lax.fori_loop(0, n_ch, body, init)
    o_s = acc_f / l_f

    wlen = min(WINDOW + Q_BLOCK, kw_ref.shape[2])
    w0 = pl.multiple_of(jnp.maximum(t0 + Q_BLOCK - wlen, 0), Q_BLOCK)
    kblk = kw_ref[0, 0, pl.ds(w0, wlen), :]
    vblk = vw_ref[0, 0, pl.ds(w0, wlen), :]
    kpos = w0 + lax.broadcasted_iota(jnp.int32, (Q_BLOCK, wlen), 1)
    pw = softmax(biased(_dot_nt(q, kblk), (kpos <= tq) & (tq - kpos < WINDOW)))
    o_w = _dot(pw.astype(BF16), vblk)

    gt = gt_ref[0, 0]
    for h in range(n):
        r = slice(h * Q_BLOCK, (h + 1) * Q_BLOCK)
        o = (gt[:, 3 * h:3 * h + 1] * o_c[r] + gt[:, 3 * h + 1:3 * h + 2] * o_s[r]
             + gt[:, 3 * h + 2:3 * h + 3] * o_w[r])
        o_ref[0, :, h * HEAD_DIM:(h + 1) * HEAD_DIM] = o.astype(o_ref.dtype)


def nsa_attention(qkr, kvp, cmp, gates, ck=512):
    B, _, S, dh = qkr.shape
    G, n = NSA_GROUPS, NSA_HPG
    ncmp = cmp.shape[3]
    nsel = S // SEL_BLOCK
    ck = min(ck, S)
    r_sel, r_cmp = SEL_BLOCK // CMP_STRIDE, CMP_LEN // CMP_STRIDE
    tgt = (r_sel * jnp.arange(nsel)[:, None, None] + jnp.arange(r_sel)[None, :, None]
           - jnp.arange(r_cmp)[None, None, :])
    cmap = (jnp.arange(ncmp - 1)[:, None, None, None] == tgt[None]).sum((2, 3)).astype(F32)
    cmap = jnp.pad(cmap, ((0, 1), (0, LANES - nsel))).astype(BF16)
    e = (jnp.arange(S)[None, :] // SEL_BLOCK == jnp.arange(LANES)[:, None]).astype(BF16)
    e3 = e.reshape(LANES, S // ck, ck).transpose(1, 0, 2)

    def head(h0):
        return pl.BlockSpec((1, 1, S, dh), lambda b, g, i: (b, h0 + g, 0, 0))

    return pl.pallas_call(
        functools.partial(_nsa_kernel, ck=ck),
        grid=(B, G, S // Q_BLOCK),
        in_specs=[
            pl.BlockSpec((1, n, Q_BLOCK, dh), lambda b, g, i: (b, g, i, 0)),
            head(16), head(20), head(8), head(12),
            pl.BlockSpec((1, 1, 1, ncmp, dh), lambda b, g, i: (0, b, g, 0, 0)),
            pl.BlockSpec((1, 1, 1, ncmp, dh), lambda b, g, i: (1, b, g, 0, 0)),
            pl.BlockSpec((1, 1, Q_BLOCK, LANES), lambda b, g, i: (b, g, i, 0)),
            pl.BlockSpec((ncmp, LANES), lambda b, g, i: (0, 0)),
            pl.BlockSpec((S // ck, LANES, ck), lambda b, g, i: (0, 0, 0)),
        ],
        out_specs=pl.BlockSpec((1, Q_BLOCK, n * dh), lambda b, g, i: (b, i, g)),
        out_shape=jax.ShapeDtypeStruct((B, S, G * n * dh), BF16),
        compiler_params=_cparams(("parallel", "parallel", "arbitrary")),
        name="nsa_attention",
    )(qkr, qkr, qkr, kvp, kvp, cmp, cmp, gates, cmap, e3)


def _rwkv_feat_kernel(z_ref, zp_ref, mu_ref, o_ref):
    i = pl.program_id(1)
    z = z_ref[0]
    prev = zp_ref[0, 7:8, :] * (i > 0).astype(F32)
    row = lax.broadcasted_iota(jnp.int32, z.shape, 0)
    sh = jnp.where(row == 0, prev, pltpu.roll(z, 1, 0))
    zs = z + (sh - z) * mu_ref[...]
    o_ref[0, :, 0:128] = jnp.tanh(zs[:, 0:128])
    o_ref[0, :, 128:256] = zs[:, 128:256]
    o_ref[0, :, 256:] = jax.nn.sigmoid(zs[:, 256:])


def rwkv_features(z_lora, mu_lora, tm=512):
    B, S, W = z_lora.shape
    tm = min(tm, S)
    return pl.pallas_call(
        _rwkv_feat_kernel,
        grid=(B, S // tm),
        in_specs=[pl.BlockSpec((1, tm, W), lambda b, i: (b, i, 0)),
                  pl.BlockSpec((1, 8, W), lambda b, i: (b, jnp.maximum(i * (tm // 8) - 1, 0), 0)),
                  pl.BlockSpec((1, W), lambda b, i: (0, 0))],
        out_specs=pl.BlockSpec((1, tm, W), lambda b, i: (b, i, 0)),
        out_shape=jax.ShapeDtypeStruct((B, S, W), F32),
        compiler_params=_cparams(("parallel", "arbitrary")),
        name="rwkv_features",
    )(z_lora, z_lora, mu_lora)


def _rwkv_kernel(r_ref, k_ref, v_ref, f_ref, par_ref, w2_ref, a2_ref, g2_ref, o_ref, st_ref, prev_ref):
    C = RWKV_CHUNK
    W = r_ref.shape[-1]
    npair = W // LANES
    c = pl.program_id(2)

    @pl.when(c == 0)
    def _():
        st_ref[...] = jnp.zeros_like(st_ref)
        prev_ref[...] = jnp.zeros_like(prev_ref)

    par = par_ref[...]
    mu_r, mu_k, mu_v = par[0:1], par[1:2], par[2:3]
    w0, a0, k_k, k_a, r_k, ln_w, ln_b = par[3:4], par[4:5], par[5:6], par[6:7], par[7:8], par[8:9], par[9:10]

    row = lax.broadcasted_iota(jnp.int32, (C, C), 0)
    col = lax.broadcasted_iota(jnp.int32, (C, C), 1)
    trow = lax.broadcasted_iota(jnp.int32, (C, W), 0)
    lane = lax.broadcasted_iota(jnp.int32, (C, LANES), 1)
    lr = lax.broadcasted_iota(jnp.int32, (LANES, LANES), 0)
    lc = lax.broadcasted_iota(jnp.int32, (LANES, LANES), 1)

    def shifted(z_ref, idx, mu):
        z = z_ref[0]
        sh = jnp.where(trow == 0, prev_ref[idx:idx + 1, :], pltpu.roll(z, 1, 0))
        prev_ref[idx:idx + 1, :] = z[C - 1:C, :]
        return z + (sh - z) * mu

    r = shifted(r_ref, 0, mu_r)
    k = shifted(k_ref, 1, mu_k)
    v = shifted(v_ref, 2, mu_v)
    f = f_ref[0]
    w_log = -_softplus(-(w0 + _dot3(_split(f[:, 0:128]), _split(w2_ref[...])))) - 0.5
    logd = -jnp.exp(w_log)
    a = jax.nn.sigmoid(a0 + _dot3(_split(f[:, 128:256]), _split(a2_ref[...])))
    g = _dot3(_split(f[:, 256:]), _split(g2_ref[...]))

    bd = _exact(jnp.where((lr // RWKV_HEAD) == (lc // RWKV_HEAD), 1.0, 0.0))

    def head_sum(x, passes=2):
        sp = _split if passes == 2 else (lambda z: (z.astype(BF16), None))
        return jnp.concatenate([_dot3(sp(x[:, p * LANES:(p + 1) * LANES]), bd) for p in range(npair)], axis=1)

    kk = k * k_k
    kk = kk / jnp.maximum(jnp.sqrt(head_sum(kk * kk)), 1e-12)
    k2 = k * (1.0 + (a - 1.0) * k_a)

    tri_incl = _exact(jnp.where(col <= row, 1.0, 0.0))
    cum = _dot3(tri_incl, _split(logd))
    cum_end = cum[C - 1:C, :]
    e_neg = jnp.exp(-cum)
    kka = kk * a
    a_t = -kk * jnp.exp(cum - logd)
    b_t = kka * e_neg
    k_t = k2 * e_neg
    r_t = r * jnp.exp(cum)
    e_end = jnp.exp(cum_end - cum)
    b_e = kka * e_end
    k_e = k2 * e_end
    p_end = jnp.exp(cum_end)

    strict = col < row
    incl = col <= row
    eye = jnp.where(col == row, 1.0, 0.0)
    n_sq = C.bit_length() - 2
    hpp = LANES // RWKV_HEAD
    heads = [(p, h) for p in range(npair) for h in range(hpp)]
    hd = []
    for p, h in heads:
        sl = slice(p * LANES, (p + 1) * LANES)
        mh = (lane // RWKV_HEAD) == h
        s_am = _split(jnp.where(mh, a_t[:, sl], 0.0))
        s_rm = _split(jnp.where(mh, r_t[:, sl], 0.0))
        hd.append(dict(
            sl=sl, s_am=s_am, s_rm=s_rm,
            s_vm=_split(jnp.where(mh, v[:, sl], 0.0)),
            s_bem=_split(jnp.where(mh, b_e[:, sl], 0.0)),
            s_kem=_split(jnp.where(mh, k_e[:, sl], 0.0)),
            rhs=_split(jnp.concatenate([b_t[:, sl], k_t[:, sl]], axis=0)) if h == 0 else hd[-1]["rhs"]))
    for d in hd:
        amat = _dot3(d["s_am"], d["rhs"], "nt")
        rmat = _dot_nt(d["s_rm"][0], d["rhs"][0])
        l_ab = jnp.where(strict, amat[:, 0:C], 0.0)
        d["l_ak"] = _split(jnp.where(strict, amat[:, C:2 * C], 0.0))
        d["l_rb"] = jnp.where(incl, rmat[:, 0:C], 0.0).astype(BF16)
        d["l_rk"] = jnp.where(incl, rmat[:, C:2 * C], 0.0).astype(BF16)
        d["x"] = eye + l_ab
        d["s_pw"] = _split(l_ab)
    for i in range(n_sq):
        for d in hd:
            d["s_pw"] = _split(_dot3(d["s_pw"], d["s_pw"])) if i == 0 else (_dot(d["s_pw"][0], d["s_pw"][0]).astype(BF16), None)
        for d in hd:
            d["x"] = d["x"] + (_dot3(_split(d["x"]), d["s_pw"]) if i == 0 else _dot(d["x"].astype(BF16), d["s_pw"][0]))
    for i, d in enumerate(hd):
        d["st"] = st_ref[i]
        d["s_st"] = _split(d["st"])
        d["t"] = _split(_dot3(d["s_am"], d["s_st"], "nt") + _dot3(d["l_ak"], d["s_vm"]))
    for d in hd:
        d["s_u"] = _split(_dot3(_split(d["x"]), d["t"]))
    for i, d in enumerate(hd):
        st_ref[i] = d["st"] * p_end[:, d["sl"]] + _dot3(d["s_u"], d["s_bem"], "tn") + _dot3(d["s_vm"], d["s_kem"], "tn")
    ys = []
    for d in hd:
        ys.append(_dot_nt(d["s_rm"][0], d["s_st"][0]) + _dot(d["l_rb"], d["s_u"][0]) + _dot(d["l_rk"], d["s_vm"][0]))
    y = jnp.concatenate([sum(ys[hpp * p + 1:hpp * (p + 1)], ys[hpp * p]) for p in range(npair)], axis=1)

    inv_n = 1.0 / RWKV_HEAD
    mean = head_sum(y) * inv_n
    d = y - mean
    var = head_sum(d * d) * inv_n
    yn = d * lax.rsqrt(var + GN_EPS) * ln_w + ln_b
    bonus = head_sum(r * k2 * r_k, passes=1) * v
    o_ref[0] = ((yn + bonus) * g).astype(o_ref.dtype)


def rwkv_scan(z_rkv, feat, par, w2p, a2p, g2, pairs_per_step=8):
    B, S, R3 = z_rkv.shape
    R = R3 // 3
    W = pairs_per_step * LANES
    nblk = R // W
    C = RWKV_CHUNK
    assert S % C == 0 and R % W == 0

    def col(off):
        return pl.BlockSpec((1, C, W), lambda b, p, c: (b, c, off + p))

    return pl.pallas_call(
        _rwkv_kernel,
        grid=(B, nblk, S // C),
        in_specs=[
            col(0), col(nblk), col(2 * nblk),
            pl.BlockSpec((1, C, feat.shape[-1]), lambda b, p, c: (b, c, 0)),
            pl.BlockSpec((16, W), lambda b, p, c: (0, p)),
            pl.BlockSpec((LANES, W), lambda b, p, c: (0, p)),
            pl.BlockSpec((LANES, W), lambda b, p, c: (0, p)),
            pl.BlockSpec((GATE_LORA, W), lambda b, p, c: (0, p)),
        ],
        out_specs=pl.BlockSpec((1, C, W), lambda b, p, c: (b, c, p)),
        out_shape=jax.ShapeDtypeStruct((B, S, R), BF16),
        scratch_shapes=[pltpu.VMEM((W // RWKV_HEAD, LANES, LANES), F32), pltpu.VMEM((8, W), F32)],
        compiler_params=_cparams(("parallel", "parallel", "arbitrary")),
        name="rwkv_scan",
    )(z_rkv, z_rkv, z_rkv, feat, par, w2p, a2p, g2)


def _merge_kernel(on_ref, or_ref, wn_ref, wr_ref, ga_ref, gb_ref, o_ref):
    pa = _dot(on_ref[...], wn_ref[...].astype(on_ref.dtype))
    pb = _dot(or_ref[...], wr_ref[...].astype(or_ref.dtype))
    o_ref[...] = (ga_ref[...].astype(F32) * pa + gb_ref[...].astype(F32) * pb).astype(o_ref.dtype)


def merge_mixers(o_nsa, o_rwkv, wn, wr, gates, tm=1024, tn=512):
    M, Kn = o_nsa.shape
    Kr = o_rwkv.shape[1]
    D = wn.shape[1]
    tm, tn = min(tm, M), min(tn, D)
    nb = D // tn
    return pl.pallas_call(
        _merge_kernel,
        grid=(M // tm, nb),
        in_specs=[
            pl.BlockSpec((tm, Kn), lambda i, j: (i, 0)),
            pl.BlockSpec((tm, Kr), lambda i, j: (i, 0)),
            pl.BlockSpec((Kn, tn), lambda i, j: (0, j)),
            pl.BlockSpec((Kr, tn), lambda i, j: (0, j)),
            pl.BlockSpec((tm, tn), lambda i, j: (i, j)),
            pl.BlockSpec((tm, tn), lambda i, j: (i, nb + j)),
        ],
        out_specs=pl.BlockSpec((tm, tn), lambda i, j: (i, j)),
        out_shape=jax.ShapeDtypeStruct((M, D), BF16),
        compiler_params=_cparams(("parallel", "parallel")),
        name="merge_mixers",
    )(o_nsa, o_rwkv, wn, wr, gates, gates)


def _xattn_kernel(q_ref, kv_ref, wo_ref, x_ref, o_ref):
    nh = XATTN_HEADS
    scale = HEAD_DIM ** -0.5
    outs = []
    for h in range(nh):
        q = q_ref[:, h * HEAD_DIM:(h + 1) * HEAD_DIM]
        k = kv_ref[0, :, h * HEAD_DIM:(h + 1) * HEAD_DIM]
        v = kv_ref[0, :, (nh + h) * HEAD_DIM:(nh + h + 1) * HEAD_DIM]
        s = _dot_nt(q, k) * scale
        m = jnp.max(s, axis=-1, keepdims=True)
        p = jnp.exp(s - m)
        p = p / jnp.sum(p, axis=-1, keepdims=True)
        outs.append(_dot(p.astype(BF16), v).astype(BF16))
    o = jnp.concatenate(outs, axis=-1)
    o_ref[...] = x_ref[...] + _dot(o, wo_ref[...])


def cross_attention(q, kv, wo, x, seq, tm=256):
    T, D = x.shape
    tm = min(tm, seq)
    spb = seq // tm
    Mm = kv.shape[1]
    return pl.pallas_call(
        _xattn_kernel,
        grid=(T // tm,),
        in_specs=[
            pl.BlockSpec((tm, q.shape[1]), lambda i: (i, 0)),
            pl.BlockSpec((1, Mm, kv.shape[2]), lambda i: (i // spb, 0, 0)),
            pl.BlockSpec(wo.shape, lambda i: (0, 0)),
            pl.BlockSpec((tm, D), lambda i: (i, 0)),
        ],
        out_specs=pl.BlockSpec((tm, D), lambda i: (i, 0)),
        out_shape=jax.ShapeDtypeStruct((T, D), F32),
        compiler_params=_cparams(("parallel",)),
        name="cross_attention",
    )(q, kv, wo, x)


def _top16(x, iota=None, n=None):
    if iota is None:
        n = x.shape[0]
        iota = lax.broadcasted_iota(jnp.int32, x.shape, 0).astype(F32)
    cur = x
    rank = jnp.full(x.shape, float(PEER_TOPK), F32)
    vals = []
    for r in range(PEER_TOPK):
        m = jnp.max(cur, axis=0, keepdims=True)
        idx = jnp.min(jnp.where(cur == m, iota, float(n)), axis=0, keepdims=True)
        hit = iota == idx
        rank = jnp.where(hit, float(r), rank)
        cur = jnp.where(hit, -jnp.inf, cur)
        vals.append(m)
    return jnp.concatenate(vals, axis=0), rank


def _peer_route_kernel(q_ref, keys_ref, row_ref, pl_ref):
    K = PEER_TOPK
    q = q_ref[...]
    half = q.shape[1] // 2
    s1 = _dotf_nt(keys_ref[0, 0], q[:, :half])
    s2 = _dotf_nt(keys_ref[0, 1], q[:, half:])
    v1, rank1 = _top16(s1)
    v2, rank2 = _top16(s2)
    tmq = q.shape[0]
    r8 = lax.broadcasted_iota(jnp.int32, (8, tmq), 0)
    ninf = -jnp.inf
    lo8 = v2[0:8]
    tiles = [
        (v1[0:1] + lo8, r8),
        (v1[0:1] + v2[8:16], 8 + r8),
        (v1[1:2] + lo8, 16 + r8),
        (jnp.where(r8 < 5, v1[2:3] + lo8, ninf), 32 + r8),
        (jnp.where(r8 < 7, jnp.where(r8 < 4, v1[3:4], v1[4:5]) + jnp.where(r8 < 4, lo8, pltpu.roll(lo8, 4, 0)), ninf),
         jnp.where(r8 < 4, 48 + r8, 60 + r8)),
        (jnp.where(r8 < 6, jnp.where(r8 < 2, v1[5:6], jnp.where(r8 < 4, v1[6:7], v1[7:8]))
                   + jnp.where(r8 % 2 == 0, v2[0:1], v2[1:2]), ninf),
         80 + (r8 // 2) * 16 + r8 % 2),
        (v1[8:16] + v2[0:1], (8 + r8) * 16),
    ]
    cand = jnp.concatenate([t for t, _ in tiles], axis=0)
    cidx = jnp.concatenate([i for _, i in tiles], axis=0).astype(F32)
    top_s, crank = _top16(cand, cidx, K * K)
    chosen = jnp.where(crank < float(K), 1.0, 0.0)
    z = jnp.sum(jnp.exp(top_s - top_s[0:1]), axis=0, keepdims=True)
    ch = [chosen[8 * t:8 * (t + 1)] for t in range(len(tiles))]
    colsum = lambda x: jnp.sum(x, axis=0, keepdims=True)
    upper = (r8 < 4).astype(F32)
    j_rows = [colsum(ch[0]) + colsum(ch[1]), colsum(ch[2]), colsum(ch[3]),
              colsum(ch[4] * upper), colsum(ch[4] * (1.0 - upper))]
    j_rows += [ch[5][2 * m:2 * m + 1] + ch[5][2 * m + 1:2 * m + 2] for m in range(3)]
    j_rows += [ch[6][m:m + 1] for m in range(8)]
    jn = jnp.zeros_like(s1)
    for i in range(K):
        jn = jnp.where(rank1 == float(i), j_rows[i], jn)
    row_ref[0, 0] = jnp.exp(s1 - v1[0:1]) / z
    row_ref[0, 1] = jn
    pl_ref[0, 0] = jnp.exp(s2 - v2[0:1]).astype(pl_ref.dtype)
    pl_ref[0, 1] = rank2.astype(pl_ref.dtype)


def peer_route(q, keys, tm=256):
    T = q.shape[0]
    H, _, nkeys, hd = keys.shape
    tm = min(tm, T)
    spec = pl.BlockSpec((1, 2, nkeys, tm), lambda i, h: (h, 0, 0, i))
    return pl.pallas_call(
        _peer_route_kernel,
        grid=(T // tm, H),
        in_specs=[pl.BlockSpec((tm, 2 * hd), lambda i, h: (i, h)),
                  pl.BlockSpec((1, 2, nkeys, hd), lambda i, h: (h, 0, 0, 0))],
        out_specs=[spec, spec],
        out_shape=[jax.ShapeDtypeStruct((H, 2, nkeys, T), F32), jax.ShapeDtypeStruct((H, 2, nkeys, T), BF16)],
        compiler_params=_cparams(("parallel", "parallel")),
        name="peer_route",
    )(q, keys)


def _peer_act_kernel(h_ref, u_ref, row_ref, pl_ref, o_ref):
    nk = PEER_KEYS
    ec = u_ref.shape[0]
    j = pl.program_id(1)
    hu = _dot(u_ref[...], h_ref[...])
    zero = jnp.zeros((), o_ref.dtype)
    for al in range(ec // nk):
        a = j * (ec // nk) + al
        w = None
        for hd in range(row_ref.shape[0]):
            e1 = row_ref[hd, 0, pl.ds(a, 1), :].astype(o_ref.dtype)
            jn = row_ref[hd, 1, pl.ds(a, 1), :].astype(o_ref.dtype)
            g = jnp.where(pl_ref[hd, 1] < jn, e1 * pl_ref[hd, 0], zero)
            w = g if w is None else w + g
        act = _gelu(hu[al * nk:(al + 1) * nk, :].astype(o_ref.dtype)) * w
        o_ref[:, al * nk:(al + 1) * nk] = act.T


def peer_activations(h_t, u_tab, rows, planes, tm=512, ec=512):
    D, T = h_t.shape
    E = u_tab.shape[0]
    tm, ec = min(tm, T), min(ec, E)
    H, _, nkeys, _ = rows.shape
    rspec = pl.BlockSpec((H, 2, nkeys, tm), lambda i, j: (0, 0, 0, i))
    return pl.pallas_call(
        _peer_act_kernel,
        grid=(T // tm, E // ec),
        in_specs=[pl.BlockSpec((D, tm), lambda i, j: (0, i)),
                  pl.BlockSpec((ec, D), lambda i, j: (j, 0)),
                  rspec, rspec],
        out_specs=pl.BlockSpec((tm, ec), lambda i, j: (i, j)),
        out_shape=jax.ShapeDtypeStruct((T, E), BF16),
        compiler_params=_cparams(("parallel", "arbitrary")),
        name="peer_activations",
    )(h_t, u_tab, rows, planes)


def kernel(x, mem, positions, norm_mix, w_in, nsa_cmp_pe, nsa_cmp_w1, nsa_cmp_w2, rwkv_mu, rwkv_w0, rwkv_w2, rwkv_a0, rwkv_a2, rwkv_g2, rwkv_k_k, rwkv_k_a, rwkv_r_k, rwkv_ln_w, rwkv_ln_b, w_proj_nsa, w_proj_rwkv, w_out, norm_xattn, norm_mem, xattn_wq, xattn_wkv, xattn_wo, norm_ffn, peer_wq, peer_keys, peer_u, peer_v, norm_final):
    B, S, D = x.shape
    T = B * S
    depth = w_in.shape[0]
    G, n, dh = NSA_GROUPS, NSA_HPG, HEAD_DIM
    nq_cols = G * n * dh
    nkv_cols = 6 * G * dh
    ngate = 3 * G * n
    R = rwkv_w0.shape[1]
    c_q, c_kv, c_g = nq_cols, nq_cols + nkv_cols, nq_cols + nkv_cols + ngate
    c_r = c_g + 3 * R + DECAY_LORA + AAA_LORA + GATE_LORA
    gd = G * dh

    pos_flat = positions.reshape(T)
    rope_tabs = _rope_tables(pos_flat)
    cend = jnp.arange(S // CMP_STRIDE) * CMP_STRIDE + (CMP_LEN - 1)
    ctab = _rope_tables(positions[:, jnp.minimum(cend, S - 1)])
    ident = _rope_tables(jnp.zeros_like(positions[:, :S // CMP_STRIDE]))
    cmp_tabs = [jnp.stack([a, b]) for a, b in zip(ctab, ident)]

    xf = x.reshape(T, D)
    for l in range(depth):
        w_t = jnp.swapaxes(w_in, 1, 2)
        pieces = [(0, c_q), (c_q + 2 * gd, gd), (c_q + 4 * gd, gd),
                  (c_q, 2 * gd), (c_q + 3 * gd, gd), (c_q + 5 * gd, gd),
                  (c_g, 3 * R), (c_r, 2 * D)]
        wp = repack_rows(w_t, l, pieces)
        g_rope = (0, c_q + 2 * gd)
        g_plain = (g_rope[1], 4 * gd)
        g_rkv = (g_plain[0] + g_plain[1], 3 * R)
        g_merge = (g_rkv[0] + g_rkv[1], 2 * D)
        w_gate = jnp.pad(w_t[l, c_kv:c_g].reshape(G, 3 * n, D), ((0, 0), (0, LANES - 3 * n), (0, 0)))
        w_gate = w_gate.reshape(G * LANES, D).astype(BF16)
        wl = w_t[l, c_g + 3 * R:c_r]
        padl = lambda a, n_: jnp.pad(a, ((0, n_ - a.shape[0]), (0, 0)))
        w_lora = jnp.concatenate([padl(wl[:DECAY_LORA], LANES), padl(wl[DECAY_LORA:DECAY_LORA + AAA_LORA], LANES),
                                  wl[DECAY_LORA + AAA_LORA:]], axis=0).astype(BF16)

        h = rmsnorm(xf, norm_mix[l], BF16)
        proj = functools.partial(matmul, h, w_t=True)
        qkr = proj(wp, wcols=g_rope, mode="heads_rope", extras=rope_tabs, out_dtype=BF16, seq=S, name="proj_rope")
        kvp = proj(wp, wcols=g_plain, mode="heads", out_dtype=BF16, seq=S, name="proj_kv")
        gates_nsa = proj(w_gate, mode="heads_sigmoid", out_dtype=F32, seq=S, name="proj_gate")
        z_rkv = proj(wp, wcols=g_rkv, out_dtype=F32, name="proj_rkv")
        z_lora = proj(w_lora, out_dtype=F32, name="proj_lora")
        gates_merge = proj(wp, wcols=g_merge, mode="sigmoid", out_dtype=BF16, name="proj_merge")

        pe8 = jnp.broadcast_to(nsa_cmp_pe[l].reshape(2, 1, CMP_LEN * dh), (2, 8, CMP_LEN * dh)).astype(BF16)
        cmp = nsa_compress(kvp, nsa_cmp_w1[l].astype(BF16), nsa_cmp_w2[l].astype(BF16), pe8, cmp_tabs)
        o_nsa = nsa_attention(qkr, kvp, cmp, gates_nsa).reshape(T, nq_cols)

        mu = rwkv_mu[l]
        mul = mu[3 * R:]
        pad1 = lambda a, n_: jnp.pad(a, (0, n_ - a.shape[0]))
        mu_lora = jnp.concatenate([pad1(mul[:DECAY_LORA], LANES), pad1(mul[DECAY_LORA:DECAY_LORA + AAA_LORA], LANES),
                                   mul[DECAY_LORA + AAA_LORA:]]).reshape(1, -1)
        feat = rwkv_features(z_lora.reshape(B, S, -1), mu_lora)
        par = jnp.stack([mu[:R], mu[R:2 * R], mu[2 * R:3 * R], rwkv_w0[l], rwkv_a0[l], rwkv_k_k[l], rwkv_k_a[l],
                         rwkv_r_k[l].reshape(R), rwkv_ln_w[l], rwkv_ln_b[l]])
        par = jnp.pad(par, ((0, 16 - par.shape[0]), (0, 0)))
        w2p = jnp.pad(rwkv_w2[l], ((0, LANES - DECAY_LORA), (0, 0)))
        a2p = jnp.pad(rwkv_a2[l], ((0, LANES - AAA_LORA), (0, 0)))
        o_rwkv = rwkv_scan(z_rkv.reshape(B, S, 3 * R), feat, par, w2p, a2p, rwkv_g2[l]).reshape(T, R)

        merged = merge_mixers(o_nsa, o_rwkv, w_proj_nsa[l], w_proj_rwkv[l], gates_merge)
        xf = matmul(merged, w_out[l], mode="residual", extras=(xf,), out_dtype=F32, name="out_proj")

        h = rmsnorm(xf, norm_xattn[l], BF16)
        m = rmsnorm(mem.reshape(-1, D), norm_mem[l], BF16)
        q = matmul(h, xattn_wq[l], out_dtype=BF16, name="xattn_q")
        kvm = matmul(m, xattn_wkv[l], out_dtype=BF16, name="xattn_kv").reshape(B, mem.shape[1], -1)
        xf = cross_attention(q, kvm, xattn_wo[l].astype(BF16), xf, S)

        h, h_t = rmsnorm(xf, norm_ffn[l], BF16, with_transpose=True)
        pq = matmul(h, peer_wq[l], out_dtype=F32, name="peer_q")
        rows, planes = peer_route(pq, peer_keys[l])
        act = peer_activations(h_t, peer_u[l].astype(BF16), rows, planes)
        xf = matmul(act, peer_v[l], mode="residual", extras=(xf,), out_dtype=F32,
                    tm=1024, tn=1024, tk=1024, name="peer_out")

    return rmsnorm(xf, norm_final, F32).reshape(B, S, D)
```

```python
import functools

import jax
import jax.numpy as jnp
from jax import lax
from jax.experimental import pallas as pl
from jax.experimental.pallas import tpu as pltpu

F32 = jnp.float32
BF16 = jnp.bfloat16
HI = lax.Precision.HIGHEST

LANES = 128
HEAD_DIM = 128
ROPE_DIM = HEAD_DIM // 4
ROPE_THETA = 500000.0
RMS_EPS = 1e-6
NEG_INF = -1e30

NSA_GROUPS = 4
NSA_HPG = 4
CMP_LEN = 32
CMP_STRIDE = 16
SEL_BLOCK = 64
SEL_TOP = 16
WINDOW = 512
Q_BLOCK = 128

RWKV_HEAD = 64
DECAY_LORA = 96
AAA_LORA = 96
GATE_LORA = 256
GN_EPS = 64e-5
RWKV_CHUNK = 128

XATTN_HEADS = 4
PEER_HEADS = 8
PEER_KEYS = 128
PEER_TOPK = 16

VMEM_LIMIT = 56 * 1024 * 1024


def _cparams(sem, flags=None):
    return pltpu.CompilerParams(dimension_semantics=sem, vmem_limit_bytes=VMEM_LIMIT, flags=flags)


def _gelu(x):
    return 0.5 * x * (1.0 + jnp.tanh(0.7978845608028654 * (x + 0.044715 * x * x * x)))


def _softplus(x):
    return jnp.maximum(x, 0.0) + jnp.log(1.0 + jnp.exp(-jnp.abs(x)))


def _dot(a, b):
    return jnp.dot(a, b, preferred_element_type=F32)


def _dot_nt(a, b):
    return lax.dot_general(a, b, (((1,), (1,)), ((), ())), preferred_element_type=F32)


def _dotf_nt(a, b):
    return lax.dot_general(a, b, (((1,), (1,)), ((), ())), preferred_element_type=F32, precision=HI)


def _split(x):
    hi = x.astype(BF16)
    return hi, (x - hi.astype(F32)).astype(BF16)


def _exact(x):
    return x.astype(BF16), None


_DN = {"nn": (((1,), (0,)), ((), ())), "nt": (((1,), (1,)), ((), ())), "tn": (((0,), (0,)), ((), ()))}


def _dot3(a, b, form="nn"):
    d = lambda x, y: lax.dot_general(x, y, _DN[form], preferred_element_type=F32)
    out = d(a[0], b[0])
    if a[1] is not None:
        out = out + d(a[1], b[0])
    if b[1] is not None:
        out = out + d(a[0], b[1])
    return out


def _rope_apply(a, c, sa, sb):
    return a * c + pltpu.roll(a, LANES - ROPE_DIM // 2, 1) * sa + pltpu.roll(a, ROPE_DIM // 2, 1) * sb


def _rope_tables(pos):
    half = ROPE_DIM // 2
    inv = ROPE_THETA ** (-jnp.arange(half, dtype=F32) / half)
    ang = pos.astype(F32)[..., None] * inv
    cos, sin = jnp.cos(ang), jnp.sin(ang)
    shp = pos.shape + (HEAD_DIM - ROPE_DIM,)
    c = jnp.concatenate([cos, cos, jnp.ones(shp, F32)], axis=-1)
    sa = jnp.concatenate([-sin, jnp.zeros(pos.shape + (HEAD_DIM - half,), F32)], axis=-1)
    sb = jnp.concatenate([jnp.zeros(pos.shape + (half,), F32), sin, jnp.zeros(shp, F32)], axis=-1)
    return c, sa, sb


def _rmsnorm_kernel(x_ref, g_ref, o_ref, *ot_ref):
    x = x_ref[...].astype(F32)
    ms = jnp.mean(x * x, axis=-1, keepdims=True)
    y = x * lax.rsqrt(ms + RMS_EPS) * g_ref[...]
    o_ref[...] = y.astype(o_ref.dtype)
    if ot_ref:
        ot_ref[0][...] = y.T.astype(ot_ref[0].dtype)


def rmsnorm(x2d, g, out_dtype, tm=256, with_transpose=False):
    M, D = x2d.shape
    tm = min(tm, M)
    out_specs = [pl.BlockSpec((tm, D), lambda i: (i, 0))]
    out_shape = [jax.ShapeDtypeStruct((M, D), out_dtype)]
    if with_transpose:
        out_specs.append(pl.BlockSpec((D, tm), lambda i: (0, i)))
        out_shape.append(jax.ShapeDtypeStruct((D, M), out_dtype))
    out = pl.pallas_call(
        _rmsnorm_kernel,
        grid=(M // tm,),
        in_specs=[pl.BlockSpec((tm, D), lambda i: (i, 0)), pl.BlockSpec((1, D), lambda i: (0, 0))],
        out_specs=out_specs,
        out_shape=out_shape,
        compiler_params=_cparams(("parallel",)),
        name="rmsnorm",
    )(x2d, g.reshape(1, D).astype(F32))
    return out if with_transpose else out[0]


def _mm_kernel(*refs, nk, mode, n_extra, w_t):
    x_ref, w_ref = refs[0], refs[1]
    extras = refs[2:2 + n_extra]
    o_ref = refs[2 + n_extra]

    def product():
        w = w_ref[...]
        if w.dtype != x_ref.dtype:
            w = w.astype(x_ref.dtype)
        return _dot_nt(x_ref[...], w) if w_t else _dot(x_ref[...], w)

    def epilogue(acc):
        if mode == "plain":
            o_ref[...] = acc.astype(o_ref.dtype)
        elif mode == "sigmoid":
            o_ref[...] = jax.nn.sigmoid(acc).astype(o_ref.dtype)
        elif mode == "residual":
            o_ref[...] = (extras[0][...] + acc).astype(o_ref.dtype)
        elif mode in ("heads", "heads_sigmoid", "heads_rope"):
            nh = acc.shape[1] // LANES
            for j in range(nh):
                a = acc[:, j * LANES:(j + 1) * LANES]
                if mode == "heads_sigmoid":
                    a = jax.nn.sigmoid(a)
                elif mode == "heads_rope":
                    a = _rope_apply(a, extras[0][...], extras[1][...], extras[2][...])
                o_ref[0, j] = a.astype(o_ref.dtype)
        elif mode == "heads_t":
            for j in range(acc.shape[1] // LANES):
                for c in range(acc.shape[0] // LANES):
                    o_ref[0, j, c] = acc[c * LANES:(c + 1) * LANES, j * LANES:(j + 1) * LANES].T.astype(o_ref.dtype)
        else:
            raise ValueError(mode)

    if nk == 1:
        epilogue(product())
    else:
        acc_ref = refs[-1]
        k = pl.program_id(2)

        @pl.when(k == 0)
        def _():
            acc_ref[...] = jnp.zeros_like(acc_ref)

        acc_ref[...] += product()

        @pl.when(k == nk - 1)
        def _():
            epilogue(acc_ref[...])


def matmul(x, w, *, mode="plain", extras=(), out_dtype=F32, tm=1024, tn=512, tk=None, seq=None, name="mm",
           w_t=False, wcols=None):
    M, K = x.shape
    wcol0, N = (0, w.shape[0 if w_t else 1]) if wcols is None else wcols
    tm, tn = min(tm, M), min(tn, N)
    assert wcol0 % tn == 0
    wblk0 = wcol0 // tn
    if mode.startswith("heads"):
        tm = min(tm, seq)
    tk = K if tk is None else min(tk, K)
    nk = K // tk
    assert M % tm == 0 and N % tn == 0 and K % tk == 0
    grid = (M // tm, N // tn, nk)
    if w_t:
        w_spec = pl.BlockSpec((tn, tk), lambda i, j, k: (wblk0 + j, k))
    else:
        w_spec = pl.BlockSpec((tk, tn), lambda i, j, k: (k, wblk0 + j))
    in_specs = [pl.BlockSpec((tm, tk), lambda i, j, k: (i, k)), w_spec]
    if mode == "residual":
        in_specs.append(pl.BlockSpec((tm, tn), lambda i, j, k: (i, j)))
    elif mode == "heads_rope":
        in_specs += [pl.BlockSpec((tm, LANES), lambda i, j, k: (i, 0))] * 3
    if mode == "heads_t":
        assert seq % tm == 0 and tn % LANES == 0 and tm % LANES == 0
        spb = seq // tm
        out_shape = jax.ShapeDtypeStruct((M // seq, N // LANES, seq // LANES, LANES, LANES), out_dtype)
        out_spec = pl.BlockSpec((1, tn // LANES, tm // LANES, LANES, LANES),
                                lambda i, j, k: (i // spb, j, i % spb, 0, 0))
    elif mode.startswith("heads"):
        assert seq % tm == 0 and tn % LANES == 0
        spb = seq // tm
        out_shape = jax.ShapeDtypeStruct((M // seq, N // LANES, seq, LANES), out_dtype)
        out_spec = pl.BlockSpec((1, tn // LANES, tm, LANES), lambda i, j, k: (i // spb, j, i % spb, 0))
    else:
        out_shape = jax.ShapeDtypeStruct((M, N), out_dtype)
        out_spec = pl.BlockSpec((tm, tn), lambda i, j, k: (i, j))
    scratch = [pltpu.VMEM((tm, tn), F32)] if nk > 1 else []
    return pl.pallas_call(
        functools.partial(_mm_kernel, nk=nk, mode=mode, n_extra=len(extras), w_t=w_t),
        grid=grid,
        in_specs=in_specs,
        out_specs=out_spec,
        out_shape=out_shape,
        scratch_shapes=scratch,
        compiler_params=_cparams(("parallel", "parallel", "arbitrary")),
        name=name,
    )(x, w, *extras)


SUBLANES = 8


def _repack_kernel(tab_ref, w_ref, o_ref):
    o_ref[...] = w_ref[0].astype(o_ref.dtype)


def repack_rows(wt3, l, pieces, tr=512):
    _, NC, D = wt3.shape
    assert all(r0 % SUBLANES == 0 and nrow % tr == 0 for r0, nrow in pieces)
    starts = [r0 + r for r0, nrow in pieces for r in range(0, nrow, tr)]
    tab = jnp.asarray([s // SUBLANES for s in starts], jnp.int32)
    grid_spec = pltpu.PrefetchScalarGridSpec(
        num_scalar_prefetch=1, grid=(len(starts),),
        in_specs=[pl.BlockSpec((pl.Element(1), pl.Element(tr), pl.Element(D)),
                               lambda j, t: (l, t[j] * SUBLANES, 0))],
        out_specs=pl.BlockSpec((tr, D), lambda j, t: (j, 0)))
    return pl.pallas_call(
        _repack_kernel, grid_spec=grid_spec, out_shape=jax.ShapeDtypeStruct((len(starts) * tr, D), BF16),
        compiler_params=_cparams(("arbitrary",)), name="repack_w_in",
    )(tab, wt3)


def _compress_kernel(x_ref, w1_ref, w2_ref, pe_ref, c_ref, sa_ref, sb_ref, o_ref):
    half = x_ref.shape[-1]
    x = x_ref[0, 0]
    a = _dot(x, w1_ref[0, :half, :])
    b = _dot(x, w1_ref[0, half:, :])
    pet = _dot(pe_ref[0], w1_ref[0])
    n = a.shape[0]
    hid = a + pltpu.roll(b, n - 1, 0) + pet[0:1]
    out = _dot(_gelu(hid).astype(BF16), w2_ref[0])
    out = _rope_apply(out, c_ref[0, 0], sa_ref[0, 0], sb_ref[0, 0])
    o_ref[0, 0, 0] = out.astype(o_ref.dtype)


def nsa_compress(kvp, w1, w2, pe8, tabs):
    B, _, S, dh = kvp.shape
    G = NSA_GROUPS
    nch = S // CMP_STRIDE
    xv = kvp.reshape(B, kvp.shape[1], nch, CMP_STRIDE * dh)
    hidden = w1.shape[-1]
    tab_spec = pl.BlockSpec((1, 1, nch, LANES), lambda w, b, g: (w, b, 0, 0))
    return pl.pallas_call(
        _compress_kernel,
        grid=(2, B, G),
        in_specs=[
            pl.BlockSpec((1, 1, nch, CMP_STRIDE * dh), lambda w, b, g: (b, w * G + g, 0, 0)),
            pl.BlockSpec((1, CMP_LEN * dh, hidden), lambda w, b, g: (w, 0, 0)),
            pl.BlockSpec((1, hidden, dh), lambda w, b, g: (w, 0, 0)),
            pl.BlockSpec((1, 8, CMP_LEN * dh), lambda w, b, g: (w, 0, 0)),
            tab_spec, tab_spec, tab_spec,
        ],
        out_specs=pl.BlockSpec((1, 1, 1, nch, dh), lambda w, b, g: (w, b, g, 0, 0)),
        out_shape=jax.ShapeDtypeStruct((2, B, G, nch, dh), BF16),
        compiler_params=_cparams(("parallel", "parallel", "parallel")),
        name="nsa_compress",
    )(xv, w1, w2, pe8, *tabs)


def _dot_tn(a, b):
    return lax.dot_general(a, b, (((0,), (0,)), ((), ())), preferred_element_type=F32)


NSA_SPLIT = 2


def _nsa_kernel(q_ref, ks_ref, kw_ref, vs_ref, vw_ref, kc_ref, vc_ref, gt_ref, cmap_ref, e_ref, o_ref, *, ck):
    n = NSA_HPG
    ns = NSA_SPLIT
    hps = n // ns
    sq = hps * Q_BLOCK
    streams = range(ns)
    qb = pl.program_id(2)
    t0 = qb * Q_BLOCK
    c2 = (HEAD_DIM ** -0.5) * 1.4426950408889634
    nsel = ks_ref.shape[2] // SEL_BLOCK
    q_t = [jnp.concatenate([q_ref[0, s * hps + h].T for h in range(hps)], axis=1) for s in streams]
    tq = t0 + lax.broadcasted_iota(jnp.int32, (1, Q_BLOCK), 1)

    def heads(x):
        return jnp.concatenate([x] * hps, axis=1)

    def scores(k, ok_t):
        bias = heads(jnp.where(ok_t, 0.0, NEG_INF))
        return [_dot(k, q_t[s]) + bias for s in streams]

    def values(vt_ref, tile0, ntile):
        vt = jnp.concatenate([vt_ref[0, 0, tile0 + i] for i in range(ntile)], axis=1)
        return jnp.concatenate([vt, jnp.ones(vt.shape, vt.dtype)], axis=0)

    kc = kc_ref[0, 0, 0]
    vc = vc_ref[0, 0, 0]
    ncmp = kc.shape[0]
    crow = lax.broadcasted_iota(jnp.int32, (ncmp, Q_BLOCK), 0)
    sb = scores(kc, (crow < ncmp - 1) & (crow * CMP_STRIDE + (CMP_LEN - 1) <= tq))
    p = [jnp.exp2((sb[s] - jnp.max(sb[s], axis=0, keepdims=True)) * c2) for s in streams]
    anyv = heads((tq >= CMP_LEN - 1).astype(F32))
    p = [p[s] * (anyv / jnp.sum(p[s], axis=0, keepdims=True)) for s in streams]
    o_c = [_dot_tn(vc, p[s].astype(BF16)) for s in streams]

    psum = None
    for s in streams:
        for h in range(hps):
            ph = p[s][:, h * Q_BLOCK:(h + 1) * Q_BLOCK]
            psum = ph if psum is None else psum + ph
    hi = psum.astype(BF16)
    lo = (psum - hi.astype(F32)).astype(BF16)
    imp = _dot(cmap_ref[...], hi) + _dot(cmap_ref[...], lo)
    jr = lax.broadcasted_iota(jnp.int32, (LANES, Q_BLOCK), 0)
    cur = tq // SEL_BLOCK
    imp = jnp.where(jr == 0, 1e6, imp)
    imp = jnp.where(jr == cur, 1e6, imp)
    imp = jnp.where(jr == cur - 1, 1e6, imp)
    imp = jnp.where(jr > cur, -1e6, imp)

    nrow = min(((nsel + 7) // 8) * 8, LANES)
    imp = jnp.where(jr >= nsel, -3e6, imp)[0:nrow]
    jrr = jr[0:nrow]
    rank = jnp.zeros((nrow, Q_BLOCK), F32)
    for i in range(min(nsel, LANES)):
        row = imp[i:i + 1, :]
        rank = rank + jnp.where(row > imp, 1.0, jnp.where(row == imp, jnp.where(jrr > i, 1.0, 0.0), 0.0))
    sel = jnp.where(rank < float(SEL_TOP), 1.0, 0.0)
    if nrow < LANES:
        sel = jnp.concatenate([sel, jnp.zeros((LANES - nrow, Q_BLOCK), F32)], axis=0)
    sel = sel.astype(BF16)

    n_ch = (t0 + Q_BLOCK + ck - 1) // ck
    krow = lax.broadcasted_iota(jnp.int32, (ck, Q_BLOCK), 0)

    def body(kb, carry):
        m_i, acc = carry[:ns], carry[ns:]
        k0 = pl.multiple_of(kb * ck, ck)
        kblk = ks_ref[0, 0, pl.ds(k0, ck), :]
        v1 = values(vs_ref, kb * (ck // LANES), ck // LANES)
        em = _dot(e_ref[kb], sel)
        sb = scores(kblk, (k0 + krow <= tq) & (em > 0.5))
        m_new = [jnp.maximum(m_i[s], jnp.max(sb[s], axis=0, keepdims=True)) for s in streams]
        alpha = [jnp.exp2((m_i[s] - m_new[s]) * c2) for s in streams]
        pp = [jnp.exp2(((sb[s] - m_new[s]) * c2).astype(BF16)) for s in streams]
        acc = [alpha[s] * acc[s] + _dot(v1, pp[s]) for s in streams]
        return tuple(m_new) + tuple(acc)

    init = tuple(jnp.full((1, sq), NEG_INF, F32) for _ in streams) + tuple(
        jnp.zeros((2 * HEAD_DIM, sq), F32) for _ in streams)
    acc_f = lax.fori_loop(0, n_ch, body, init)[ns:]
    o_s = [a[0:HEAD_DIM] / a[HEAD_DIM:HEAD_DIM + 1] for a in acc_f]

    wlen = min(WINDOW + Q_BLOCK, kw_ref.shape[2])
    w0 = pl.multiple_of(jnp.maximum(t0 + Q_BLOCK - wlen, 0), Q_BLOCK)
    kblk = kw_ref[0, 0, pl.ds(w0, wlen), :]
    v1 = values(vw_ref, w0 // LANES, wlen // LANES)
    kpos = w0 + lax.broadcasted_iota(jnp.int32, (wlen, Q_BLOCK), 0)
    sb = scores(kblk, (kpos <= tq) & (tq - kpos < WINDOW))
    pw = [jnp.exp2(((sb[s] - jnp.max(sb[s], axis=0, keepdims=True)) * c2).astype(BF16)) for s in streams]
    acc_w = [_dot(v1, pw[s]) for s in streams]
    o_w = [a[0:HEAD_DIM] / a[HEAD_DIM:HEAD_DIM + 1] for a in acc_w]

    gt = gt_ref[0, 0].T
    for h in range(n):
        s = h // hps
        r = slice((h % hps) * Q_BLOCK, (h % hps + 1) * Q_BLOCK)
        o = (gt[3 * h:3 * h + 1] * o_c[s][:, r] + gt[3 * h + 1:3 * h + 2] * o_s[s][:, r]
             + gt[3 * h + 2:3 * h + 3] * o_w[s][:, r])
        o_ref[0, :, h * HEAD_DIM:(h + 1) * HEAD_DIM] = o.T.astype(o_ref.dtype)


def nsa_attention(qkr, vt, cmp, gates, ck=512):
    B, _, S, dh = qkr.shape
    G, n = NSA_GROUPS, NSA_HPG
    ncmp = cmp.shape[3]
    nsel = S // SEL_BLOCK
    ck = min(ck, S)
    r_sel, r_cmp = SEL_BLOCK // CMP_STRIDE, CMP_LEN // CMP_STRIDE
    tgt = (r_sel * jnp.arange(nsel)[:, None, None] + jnp.arange(r_sel)[None, :, None]
           - jnp.arange(r_cmp)[None, None, :])
    cmap = (jnp.arange(ncmp - 1)[:, None, None, None] == tgt[None]).sum((2, 3)).astype(F32)
    cmap = jnp.pad(cmap, ((0, 1), (0, LANES - nsel))).astype(BF16).T
    e = (jnp.arange(S)[:, None] // SEL_BLOCK == jnp.arange(LANES)[None, :]).astype(BF16)
    e3 = e.reshape(S // ck, ck, LANES)

    def head(h0):
        return pl.BlockSpec((1, 1, S, dh), lambda b, g, i: (b, h0 + g, 0, 0))

    def head_t(h0):
        return pl.BlockSpec((1, 1, S // LANES, dh, LANES), lambda b, g, i: (b, h0 + g, 0, 0, 0))

    return pl.pallas_call(
        functools.partial(_nsa_kernel, ck=ck),
        grid=(B, G, S // Q_BLOCK),
        in_specs=[
            pl.BlockSpec((1, n, Q_BLOCK, dh), lambda b, g, i: (b, g, i, 0)),
            head(16), head(20), head_t(0), head_t(G),
            pl.BlockSpec((1, 1, 1, ncmp, dh), lambda b, g, i: (0, b, g, 0, 0)),
            pl.BlockSpec((1, 1, 1, ncmp, dh), lambda b, g, i: (1, b, g, 0, 0)),
            pl.BlockSpec((1, 1, Q_BLOCK, LANES), lambda b, g, i: (b, g, i, 0)),
            pl.BlockSpec((LANES, ncmp), lambda b, g, i: (0, 0)),
            pl.BlockSpec((S // ck, ck, LANES), lambda b, g, i: (0, 0, 0)),
        ],
        out_specs=pl.BlockSpec((1, Q_BLOCK, n * dh), lambda b, g, i: (b, i, g)),
        out_shape=jax.ShapeDtypeStruct((B, S, G * n * dh), BF16),
        compiler_params=_cparams(("parallel", "parallel", "arbitrary")),
        name="nsa_attention",
    )(qkr, qkr, qkr, vt, vt, cmp, cmp, gates, cmap, e3)


def _rwkv_feat_kernel(z_ref, zp_ref, mu_ref, o_ref):
    i = pl.program_id(1)
    z = z_ref[0]
    prev = zp_ref[0, 7:8, :] * (i > 0).astype(F32)
    row = lax.broadcasted_iota(jnp.int32, z.shape, 0)
    sh = jnp.where(row == 0, prev, pltpu.roll(z, 1, 0))
    zs = z + (sh - z) * mu_ref[...]
    o_ref[0, :, 0:128] = jnp.tanh(zs[:, 0:128])
    o_ref[0, :, 128:256] = zs[:, 128:256]
    o_ref[0, :, 256:] = jax.nn.sigmoid(zs[:, 256:])


def rwkv_features(z_lora, mu_lora, tm=512):
    B, S, W = z_lora.shape
    tm = min(tm, S)
    return pl.pallas_call(
        _rwkv_feat_kernel,
        grid=(B, S // tm),
        in_specs=[pl.BlockSpec((1, tm, W), lambda b, i: (b, i, 0)),
                  pl.BlockSpec((1, 8, W), lambda b, i: (b, jnp.maximum(i * (tm // 8) - 1, 0), 0)),
                  pl.BlockSpec((1, W), lambda b, i: (0, 0))],
        out_specs=pl.BlockSpec((1, tm, W), lambda b, i: (b, i, 0)),
        out_shape=jax.ShapeDtypeStruct((B, S, W), F32),
        compiler_params=_cparams(("parallel", "arbitrary")),
        name="rwkv_features",
    )(z_lora, z_lora, mu_lora)


def _rwkv_kernel(r_ref, k_ref, v_ref, f_ref, par_ref, w2_ref, a2_ref, g2_ref, o_ref, st_ref, prev_ref):
    C = RWKV_CHUNK
    W = r_ref.shape[-1]
    npair = W // LANES
    c = pl.program_id(2)

    @pl.when(c == 0)
    def _():
        st_ref[...] = jnp.zeros_like(st_ref)
        prev_ref[...] = jnp.zeros_like(prev_ref)

    par = par_ref[...]
    mu_r, mu_k, mu_v = par[0:1], par[1:2], par[2:3]
    w0, a0, k_k, k_a, r_k, ln_w, ln_b = par[3:4], par[4:5], par[5:6], par[6:7], par[7:8], par[8:9], par[9:10]

    row = lax.broadcasted_iota(jnp.int32, (C, C), 0)
    col = lax.broadcasted_iota(jnp.int32, (C, C), 1)
    trow = lax.broadcasted_iota(jnp.int32, (C, W), 0)
    lane = lax.broadcasted_iota(jnp.int32, (C, LANES), 1)
    lr = lax.broadcasted_iota(jnp.int32, (LANES, LANES), 0)
    lc = lax.broadcasted_iota(jnp.int32, (LANES, LANES), 1)

    def shifted(z_ref, idx, mu):
        z = z_ref[0]
        sh = jnp.where(trow == 0, prev_ref[idx:idx + 1, :], pltpu.roll(z, 1, 0))
        prev_ref[idx:idx + 1, :] = z[C - 1:C, :]
        return z + (sh - z) * mu

    r = shifted(r_ref, 0, mu_r)
    k = shifted(k_ref, 1, mu_k)
    v = shifted(v_ref, 2, mu_v)
    f = f_ref[0]
    w_log = -_softplus(-(w0 + _dot3(_split(f[:, 0:128]), _split(w2_ref[...])))) - 0.5
    logd = -jnp.exp(w_log)
    a = jax.nn.sigmoid(a0 + _dot3(_split(f[:, 128:256]), _split(a2_ref[...])))
    g = _dot3(_split(f[:, 256:]), _split(g2_ref[...]))

    bd = _exact(jnp.where((lr // RWKV_HEAD) == (lc // RWKV_HEAD), 1.0, 0.0))

    def head_sum(x, passes=2):
        sp = _split if passes == 2 else (lambda z: (z.astype(BF16), None))
        return jnp.concatenate([_dot3(sp(x[:, p * LANES:(p + 1) * LANES]), bd) for p in range(npair)], axis=1)

    kk = k * k_k
    kk = kk / jnp.maximum(jnp.sqrt(head_sum(kk * kk)), 1e-12)
    k2 = k * (1.0 + (a - 1.0) * k_a)

    tri_incl = _exact(jnp.where(col <= row, 1.0, 0.0))
    cum = _dot3(tri_incl, _split(logd))
    cum_end = cum[C - 1:C, :]
    e_neg = jnp.exp(-cum)
    kka = kk * a
    a_t = -kk * jnp.exp(cum - logd)
    b_t = kka * e_neg
    k_t = k2 * e_neg
    r_t = r * jnp.exp(cum)
    e_end = jnp.exp(cum_end - cum)
    b_e = kka * e_end
    k_e = k2 * e_end
    p_end = jnp.exp(cum_end)

    strict = col < row
    incl = col <= row
    eye = jnp.where(col == row, 1.0, 0.0)
    n_sq = C.bit_length() - 2
    hpp = LANES // RWKV_HEAD
    heads = [(p, h) for p in range(npair) for h in range(hpp)]
    hd = []
    for p, h in heads:
        sl = slice(p * LANES, (p + 1) * LANES)
        mh = (lane // RWKV_HEAD) == h
        s_am = _split(jnp.where(mh, a_t[:, sl], 0.0))
        s_rm = _split(jnp.where(mh, r_t[:, sl], 0.0))
        hd.append(dict(
            sl=sl, s_am=s_am, s_rm=s_rm,
            s_vm=_split(jnp.where(mh, v[:, sl], 0.0)),
            s_bem=_split(jnp.where(mh, b_e[:, sl], 0.0)),
            s_kem=_split(jnp.where(mh, k_e[:, sl], 0.0)),
            rhs=_split(jnp.concatenate([b_t[:, sl], k_t[:, sl]], axis=0)) if h == 0 else hd[-1]["rhs"]))
    for d in hd:
        amat = _dot3(d["s_am"], d["rhs"], "nt")
        rmat = _dot_nt(d["s_rm"][0], d["rhs"][0])
        l_ab = jnp.where(strict, amat[:, 0:C], 0.0)
        d["l_ak"] = _split(jnp.where(strict, amat[:, C:2 * C], 0.0))
        d["l_rb"] = jnp.where(incl, rmat[:, 0:C], 0.0).astype(BF16)
        d["l_rk"] = jnp.where(incl, rmat[:, C:2 * C], 0.0).astype(BF16)
        d["x"] = eye + l_ab
        d["s_pw"] = _split(l_ab)
    for i in range(n_sq):
        for d in hd:
            d["s_pw"] = _split(_dot3(d["s_pw"], d["s_pw"])) if i == 0 else (_dot(d["s_pw"][0], d["s_pw"][0]).astype(BF16), None)
        for d in hd:
            d["x"] = d["x"] + (_dot3(_split(d["x"]), d["s_pw"]) if i == 0 else _dot(d["x"].astype(BF16), d["s_pw"][0]))
    for i, d in enumerate(hd):
        d["st"] = st_ref[i]
        d["s_st"] = _split(d["st"])
        d["t"] = _split(_dot3(d["s_am"], d["s_st"], "nt") + _dot3(d["l_ak"], d["s_vm"]))
    for d in hd:
        d["s_u"] = _split(_dot3(_split(d["x"]), d["t"]))
    for i, d in enumerate(hd):
        st_ref[i] = d["st"] * p_end[:, d["sl"]] + _dot3(d["s_u"], d["s_bem"], "tn") + _dot3(d["s_vm"], d["s_kem"], "tn")
    ys = []
    for d in hd:
        ys.append(_dot_nt(d["s_rm"][0], d["s_st"][0]) + _dot(d["l_rb"], d["s_u"][0]) + _dot(d["l_rk"], d["s_vm"][0]))
    y = jnp.concatenate([sum(ys[hpp * p + 1:hpp * (p + 1)], ys[hpp * p]) for p in range(npair)], axis=1)

    inv_n = 1.0 / RWKV_HEAD
    mean = head_sum(y) * inv_n
    d = y - mean
    var = head_sum(d * d) * inv_n
    yn = d * lax.rsqrt(var + GN_EPS) * ln_w + ln_b
    bonus = head_sum(r * k2 * r_k, passes=1) * v
    o_ref[0] = ((yn + bonus) * g).astype(o_ref.dtype)


def rwkv_scan(z_rkv, feat, par, w2p, a2p, g2, pairs_per_step=8):
    B, S, R3 = z_rkv.shape
    R = R3 // 3
    W = pairs_per_step * LANES
    nblk = R // W
    C = RWKV_CHUNK
    assert S % C == 0 and R % W == 0

    def col(off):
        return pl.BlockSpec((1, C, W), lambda b, p, c: (b, c, off + p))

    return pl.pallas_call(
        _rwkv_kernel,
        grid=(B, nblk, S // C),
        in_specs=[
            col(0), col(nblk), col(2 * nblk),
            pl.BlockSpec((1, C, feat.shape[-1]), lambda b, p, c: (b, c, 0)),
            pl.BlockSpec((16, W), lambda b, p, c: (0, p)),
            pl.BlockSpec((LANES, W), lambda b, p, c: (0, p)),
            pl.BlockSpec((LANES, W), lambda b, p, c: (0, p)),
            pl.BlockSpec((GATE_LORA, W), lambda b, p, c: (0, p)),
        ],
        out_specs=pl.BlockSpec((1, C, W), lambda b, p, c: (b, c, p)),
        out_shape=jax.ShapeDtypeStruct((B, S, R), BF16),
        scratch_shapes=[pltpu.VMEM((W // RWKV_HEAD, LANES, LANES), F32), pltpu.VMEM((8, W), F32)],
        compiler_params=_cparams(("parallel", "parallel", "arbitrary")),
        name="rwkv_scan",
    )(z_rkv, z_rkv, z_rkv, feat, par, w2p, a2p, g2)


def _merge_kernel(on_ref, or_ref, wn_ref, wr_ref, ga_ref, gb_ref, o_ref):
    pa = _dot(on_ref[...], wn_ref[...].astype(on_ref.dtype))
    pb = _dot(or_ref[...], wr_ref[...].astype(or_ref.dtype))
    o_ref[...] = (ga_ref[...].astype(F32) * pa + gb_ref[...].astype(F32) * pb).astype(o_ref.dtype)


def merge_mixers(o_nsa, o_rwkv, wn, wr, gates, tm=1024, tn=512):
    M, Kn = o_nsa.shape
    Kr = o_rwkv.shape[1]
    D = wn.shape[1]
    tm, tn = min(tm, M), min(tn, D)
    nb = D // tn
    return pl.pallas_call(
        _merge_kernel,
        grid=(M // tm, nb),
        in_specs=[
            pl.BlockSpec((tm, Kn), lambda i, j: (i, 0)),
            pl.BlockSpec((tm, Kr), lambda i, j: (i, 0)),
            pl.BlockSpec((Kn, tn), lambda i, j: (0, j)),
            pl.BlockSpec((Kr, tn), lambda i, j: (0, j)),
            pl.BlockSpec((tm, tn), lambda i, j: (i, j)),
            pl.BlockSpec((tm, tn), lambda i, j: (i, nb + j)),
        ],
        out_specs=pl.BlockSpec((tm, tn), lambda i, j: (i, j)),
        out_shape=jax.ShapeDtypeStruct((M, D), BF16),
        compiler_params=_cparams(("parallel", "parallel")),
        name="merge_mixers",
    )(o_nsa, o_rwkv, wn, wr, gates, gates)


def _xattn_kernel(q_ref, kv_ref, wo_ref, x_ref, o_ref):
    nh = XATTN_HEADS
    scale = HEAD_DIM ** -0.5
    outs = []
    for h in range(nh):
        q = q_ref[:, h * HEAD_DIM:(h + 1) * HEAD_DIM]
        k = kv_ref[0, :, h * HEAD_DIM:(h + 1) * HEAD_DIM]
        v = kv_ref[0, :, (nh + h) * HEAD_DIM:(nh + h + 1) * HEAD_DIM]
        s = _dot_nt(q, k) * scale
        m = jnp.max(s, axis=-1, keepdims=True)
        p = jnp.exp(s - m)
        p = p / jnp.sum(p, axis=-1, keepdims=True)
        outs.append(_dot(p.astype(BF16), v).astype(BF16))
    o = jnp.concatenate(outs, axis=-1)
    o_ref[...] = x_ref[...] + _dot(o, wo_ref[...])


def cross_attention(q, kv, wo, x, seq, tm=256):
    T, D = x.shape
    tm = min(tm, seq)
    spb = seq // tm
    Mm = kv.shape[1]
    return pl.pallas_call(
        _xattn_kernel,
        grid=(T // tm,),
        in_specs=[
            pl.BlockSpec((tm, q.shape[1]), lambda i: (i, 0)),
            pl.BlockSpec((1, Mm, kv.shape[2]), lambda i: (i // spb, 0, 0)),
            pl.BlockSpec(wo.shape, lambda i: (0, 0)),
            pl.BlockSpec((tm, D), lambda i: (i, 0)),
        ],
        out_specs=pl.BlockSpec((tm, D), lambda i: (i, 0)),
        out_shape=jax.ShapeDtypeStruct((T, D), F32),
        compiler_params=_cparams(("parallel",)),
        name="cross_attention",
    )(q, kv, wo, x)


def _top16(x, iota=None, n=None):
    if iota is None:
        n = x.shape[0]
        iota = lax.broadcasted_iota(jnp.int32, x.shape, 0).astype(F32)
    cur = x
    rank = jnp.full(x.shape, float(PEER_TOPK), F32)
    vals = []
    for r in range(PEER_TOPK):
        m = jnp.max(cur, axis=0, keepdims=True)
        idx = jnp.min(jnp.where(cur == m, iota, float(n)), axis=0, keepdims=True)
        hit = iota == idx
        rank = jnp.where(hit, float(r), rank)
        cur = jnp.where(hit, -jnp.inf, cur)
        vals.append(m)
    return jnp.concatenate(vals, axis=0), rank


def _peer_route_kernel(q_ref, keys_ref, row_ref, pl_ref):
    K = PEER_TOPK
    q = q_ref[...]
    half = q.shape[1] // 2
    s1 = _dotf_nt(keys_ref[0, 0], q[:, :half])
    s2 = _dotf_nt(keys_ref[0, 1], q[:, half:])
    v1, rank1 = _top16(s1)
    v2, rank2 = _top16(s2)
    tmq = q.shape[0]
    r8 = lax.broadcasted_iota(jnp.int32, (8, tmq), 0)
    ninf = -jnp.inf
    lo8 = v2[0:8]
    tiles = [
        (v1[0:1] + lo8, r8),
        (v1[0:1] + v2[8:16], 8 + r8),
        (v1[1:2] + lo8, 16 + r8),
        (jnp.where(r8 < 5, v1[2:3] + lo8, ninf), 32 + r8),
        (jnp.where(r8 < 7, jnp.where(r8 < 4, v1[3:4], v1[4:5]) + jnp.where(r8 < 4, lo8, pltpu.roll(lo8, 4, 0)), ninf),
         jnp.where(r8 < 4, 48 + r8, 60 + r8)),
        (jnp.where(r8 < 6, jnp.where(r8 < 2, v1[5:6], jnp.where(r8 < 4, v1[6:7], v1[7:8]))
                   + jnp.where(r8 % 2 == 0, v2[0:1], v2[1:2]), ninf),
         80 + (r8 // 2) * 16 + r8 % 2),
        (v1[8:16] + v2[0:1], (8 + r8) * 16),
    ]
    cand = jnp.concatenate([t for t, _ in tiles], axis=0)
    cidx = jnp.concatenate([i for _, i in tiles], axis=0).astype(F32)
    top_s, crank = _top16(cand, cidx, K * K)
    chosen = jnp.where(crank < float(K), 1.0, 0.0)
    z = jnp.sum(jnp.exp(top_s - top_s[0:1]), axis=0, keepdims=True)
    ch = [chosen[8 * t:8 * (t + 1)] for t in range(len(tiles))]
    colsum = lambda x: jnp.sum(x, axis=0, keepdims=True)
    upper = (r8 < 4).astype(F32)
    j_rows = [colsum(ch[0]) + colsum(ch[1]), colsum(ch[2]), colsum(ch[3]),
              colsum(ch[4] * upper), colsum(ch[4] * (1.0 - upper))]
    j_rows += [ch[5][2 * m:2 * m + 1] + ch[5][2 * m + 1:2 * m + 2] for m in range(3)]
    j_rows += [ch[6][m:m + 1] for m in range(8)]
    jn = jnp.zeros_like(s1)
    for i in range(K):
        jn = jnp.where(rank1 == float(i), j_rows[i], jn)
    row_ref[0, 0] = jnp.exp(s1 - v1[0:1]) / z
    row_ref[0, 1] = jn
    pl_ref[0, 0] = jnp.exp(s2 - v2[0:1]).astype(pl_ref.dtype)
    pl_ref[0, 1] = rank2.astype(pl_ref.dtype)


def peer_route(q, keys, tm=256):
    T = q.shape[0]
    H, _, nkeys, hd = keys.shape
    tm = min(tm, T)
    spec = pl.BlockSpec((1, 2, nkeys, tm), lambda i, h: (h, 0, 0, i))
    return pl.pallas_call(
        _peer_route_kernel,
        grid=(T // tm, H),
        in_specs=[pl.BlockSpec((tm, 2 * hd), lambda i, h: (i, h)),
                  pl.BlockSpec((1, 2, nkeys, hd), lambda i, h: (h, 0, 0, 0))],
        out_specs=[spec, spec],
        out_shape=[jax.ShapeDtypeStruct((H, 2, nkeys, T), F32), jax.ShapeDtypeStruct((H, 2, nkeys, T), BF16)],
        compiler_params=_cparams(("parallel", "parallel")),
        name="peer_route",
    )(q, keys)


def _peer_act_kernel(h_ref, u_ref, row_ref, pl_ref, o_ref):
    nk = PEER_KEYS
    ec = u_ref.shape[0]
    j = pl.program_id(1)
    hu = _dot(u_ref[...], h_ref[...])
    zero = jnp.zeros((), o_ref.dtype)
    for al in range(ec // nk):
        a = j * (ec // nk) + al
        w = None
        for hd in range(row_ref.shape[0]):
            e1 = row_ref[hd, 0, pl.ds(a, 1), :].astype(o_ref.dtype)
            jn = row_ref[hd, 1, pl.ds(a, 1), :].astype(o_ref.dtype)
            g = jnp.where(pl_ref[hd, 1] < jn, e1 * pl_ref[hd, 0], zero)
            w = g if w is None else w + g
        act = _gelu(hu[al * nk:(al + 1) * nk, :].astype(o_ref.dtype)) * w
        o_ref[:, al * nk:(al + 1) * nk] = act.T


def peer_activations(h_t, u_tab, rows, planes, tm=512, ec=512):
    D, T = h_t.shape
    E = u_tab.shape[0]
    tm, ec = min(tm, T), min(ec, E)
    H, _, nkeys, _ = rows.shape
    rspec = pl.BlockSpec((H, 2, nkeys, tm), lambda i, j: (0, 0, 0, i))
    return pl.pallas_call(
        _peer_act_kernel,
        grid=(T // tm, E // ec),
        in_specs=[pl.BlockSpec((D, tm), lambda i, j: (0, i)),
                  pl.BlockSpec((ec, D), lambda i, j: (j, 0)),
                  rspec, rspec],
        out_specs=pl.BlockSpec((tm, ec), lambda i, j: (i, j)),
        out_shape=jax.ShapeDtypeStruct((T, E), BF16),
        compiler_params=_cparams(("parallel", "arbitrary")),
        name="peer_activations",
    )(h_t, u_tab, rows, planes)


def kernel(x, mem, positions, norm_mix, w_in, nsa_cmp_pe, nsa_cmp_w1, nsa_cmp_w2, rwkv_mu, rwkv_w0, rwkv_w2, rwkv_a0, rwkv_a2, rwkv_g2, rwkv_k_k, rwkv_k_a, rwkv_r_k, rwkv_ln_w, rwkv_ln_b, w_proj_nsa, w_proj_rwkv, w_out, norm_xattn, norm_mem, xattn_wq, xattn_wkv, xattn_wo, norm_ffn, peer_wq, peer_keys, peer_u, peer_v, norm_final):
    B, S, D = x.shape
    T = B * S
    depth = w_in.shape[0]
    G, n, dh = NSA_GROUPS, NSA_HPG, HEAD_DIM
    nq_cols = G * n * dh
    nkv_cols = 6 * G * dh
    ngate = 3 * G * n
    R = rwkv_w0.shape[1]
    c_q, c_kv, c_g = nq_cols, nq_cols + nkv_cols, nq_cols + nkv_cols + ngate
    c_r = c_g + 3 * R + DECAY_LORA + AAA_LORA + GATE_LORA
    gd = G * dh

    pos_flat = positions.reshape(T)
    rope_tabs = _rope_tables(pos_flat)
    cend = jnp.arange(S // CMP_STRIDE) * CMP_STRIDE + (CMP_LEN - 1)
    ctab = _rope_tables(positions[:, jnp.minimum(cend, S - 1)])
    ident = _rope_tables(jnp.zeros_like(positions[:, :S // CMP_STRIDE]))
    cmp_tabs = [jnp.stack([a, b]) for a, b in zip(ctab, ident)]

    xf = x.reshape(T, D)
    for l in range(depth):
        w_t = jnp.swapaxes(w_in, 1, 2)
        pieces = [(0, c_q), (c_q + 2 * gd, gd), (c_q + 4 * gd, gd),
                  (c_q, 2 * gd), (c_q + 3 * gd, gd), (c_q + 5 * gd, gd),
                  (c_g, 3 * R), (c_r, 2 * D)]
        wp = repack_rows(w_t, l, pieces)
        g_rope = (0, c_q + 2 * gd)
        g_plain = (g_rope[1], 4 * gd)
        g_rkv = (g_plain[0] + g_plain[1], 3 * R)
        g_merge = (g_rkv[0] + g_rkv[1], 2 * D)
        w_gate = jnp.pad(w_t[l, c_kv:c_g].reshape(G, 3 * n, D), ((0, 0), (0, LANES - 3 * n), (0, 0)))
        w_gate = w_gate.reshape(G * LANES, D)
        wl = w_t[l, c_g + 3 * R:c_r]
        padl = lambda a, n_: jnp.pad(a, ((0, n_ - a.shape[0]), (0, 0)))
        w_lora = jnp.concatenate([padl(wl[:DECAY_LORA], LANES), padl(wl[DECAY_LORA:DECAY_LORA + AAA_LORA], LANES),
                                  wl[DECAY_LORA + AAA_LORA:]], axis=0)

        h = rmsnorm(xf, norm_mix[l], BF16)
        proj = functools.partial(matmul, h, w_t=True)
        qkr = proj(wp, wcols=g_rope, mode="heads_rope", extras=rope_tabs, out_dtype=BF16, seq=S, name="proj_rope")
        kvc = proj(wp, wcols=(g_plain[0], 2 * gd), mode="heads", out_dtype=BF16, seq=S, name="proj_kvc")
        vt = proj(wp, wcols=(g_plain[0] + 2 * gd, 2 * gd), mode="heads_t", out_dtype=BF16, seq=S, name="proj_vt")
        gates_nsa = proj(w_gate, mode="heads_sigmoid", out_dtype=F32, seq=S, name="proj_gate")
        z_rkv = proj(wp, wcols=g_rkv, out_dtype=F32, name="proj_rkv")
        z_lora = proj(w_lora, out_dtype=F32, name="proj_lora")
        gates_merge = proj(wp, wcols=g_merge, mode="sigmoid", out_dtype=BF16, name="proj_merge")

        pe8 = jnp.broadcast_to(nsa_cmp_pe[l].reshape(2, 1, CMP_LEN * dh), (2, 8, CMP_LEN * dh)).astype(BF16)
        cmp = nsa_compress(kvc, nsa_cmp_w1[l].astype(BF16), nsa_cmp_w2[l].astype(BF16), pe8, cmp_tabs)
        o_nsa = nsa_attention(qkr, vt, cmp, gates_nsa).reshape(T, nq_cols)

        mu = rwkv_mu[l]
        mul = mu[3 * R:]
        pad1 = lambda a, n_: jnp.pad(a, (0, n_ - a.shape[0]))
        mu_lora = jnp.concatenate([pad1(mul[:DECAY_LORA], LANES), pad1(mul[DECAY_LORA:DECAY_LORA + AAA_LORA], LANES),
                                   mul[DECAY_LORA + AAA_LORA:]]).reshape(1, -1)
        feat = rwkv_features(z_lora.reshape(B, S, -1), mu_lora)
        par = jnp.stack([mu[:R], mu[R:2 * R], mu[2 * R:3 * R], rwkv_w0[l], rwkv_a0[l], rwkv_k_k[l], rwkv_k_a[l],
                         rwkv_r_k[l].reshape(R), rwkv_ln_w[l], rwkv_ln_b[l]])
        par = jnp.pad(par, ((0, 16 - par.shape[0]), (0, 0)))
        w2p = jnp.pad(rwkv_w2[l], ((0, LANES - DECAY_LORA), (0, 0)))
        a2p = jnp.pad(rwkv_a2[l], ((0, LANES - AAA_LORA), (0, 0)))
        o_rwkv = rwkv_scan(z_rkv.reshape(B, S, 3 * R), feat, par, w2p, a2p, rwkv_g2[l]).reshape(T, R)

        merged = merge_mixers(o_nsa, o_rwkv, w_proj_nsa[l], w_proj_rwkv[l], gates_merge)
        xf = matmul(merged, w_out[l], mode="residual", extras=(xf,), out_dtype=F32, name="out_proj")

        h = rmsnorm(xf, norm_xattn[l], BF16)
        m = rmsnorm(mem.reshape(-1, D), norm_mem[l], BF16)
        q = matmul(h, xattn_wq[l], out_dtype=BF16, name="xattn_q")
        kvm = matmul(m, xattn_wkv[l], out_dtype=BF16, name="xattn_kv").reshape(B, mem.shape[1], -1)
        xf = cross_attention(q, kvm, xattn_wo[l].astype(BF16), xf, S)

        h, h_t = rmsnorm(xf, norm_ffn[l], BF16, with_transpose=True)
        pq = matmul(h, peer_wq[l], out_dtype=F32, name="peer_q")
        rows, planes = peer_route(pq, peer_keys[l])
        act = peer_activations(h_t, peer_u[l].astype(BF16), rows, planes)
        xf = matmul(act, peer_v[l], mode="residual", extras=(xf,), out_dtype=F32,
                    tm=1024, tn=1024, tk=1024, name="peer_out")

    return rmsnorm(xf, norm_final, F32).reshape(B, S, D)
```

```python
import functools

import jax
import jax.numpy as jnp
from jax import lax
from jax.experimental import pallas as pl
from jax.experimental.pallas import tpu as pltpu

F32 = jnp.float32
BF16 = jnp.bfloat16
HI = lax.Precision.HIGHEST

LANES = 128
HEAD_DIM = 128
ROPE_DIM = HEAD_DIM // 4
ROPE_THETA = 500000.0
RMS_EPS = 1e-6
NEG_INF = -1e30

NSA_GROUPS = 4
NSA_HPG = 4
CMP_LEN = 32
CMP_STRIDE = 16
SEL_BLOCK = 64
SEL_TOP = 16
WINDOW = 512
Q_BLOCK = 128

RWKV_HEAD = 64
DECAY_LORA = 96
AAA_LORA = 96
GATE_LORA = 256
GN_EPS = 64e-5
RWKV_CHUNK = 128

XATTN_HEADS = 4
PEER_HEADS = 8
PEER_KEYS = 128
PEER_TOPK = 16

VMEM_LIMIT = 56 * 1024 * 1024


def _cparams(sem, flags=None):
    return pltpu.CompilerParams(dimension_semantics=sem, vmem_limit_bytes=VMEM_LIMIT, flags=flags)


def _gelu(x):
    return 0.5 * x * (1.0 + jnp.tanh(0.7978845608028654 * (x + 0.044715 * x * x * x)))


def _softplus(x):
    return jnp.maximum(x, 0.0) + jnp.log(1.0 + jnp.exp(-jnp.abs(x)))


def _dot(a, b):
    return jnp.dot(a, b, preferred_element_type=F32)


def _dot_nt(a, b):
    return lax.dot_general(a, b, (((1,), (1,)), ((), ())), preferred_element_type=F32)


def _dotf_nt(a, b):
    return lax.dot_general(a, b, (((1,), (1,)), ((), ())), preferred_element_type=F32, precision=HI)


def _split(x):
    hi = x.astype(BF16)
    return hi, (x - hi.astype(F32)).astype(BF16)


def _exact(x):
    return x.astype(BF16), None


_DN = {"nn": (((1,), (0,)), ((), ())), "nt": (((1,), (1,)), ((), ())), "tn": (((0,), (0,)), ((), ()))}


def _dot3(a, b, form="nn"):
    d = lambda x, y: lax.dot_general(x, y, _DN[form], preferred_element_type=F32)
    out = d(a[0], b[0])
    if a[1] is not None:
        out = out + d(a[1], b[0])
    if b[1] is not None:
        out = out + d(a[0], b[1])
    return out


def _rope_apply(a, c, sa, sb):
    return a * c + pltpu.roll(a, LANES - ROPE_DIM // 2, 1) * sa + pltpu.roll(a, ROPE_DIM // 2, 1) * sb


def _rope_tables(pos):
    half = ROPE_DIM // 2
    inv = ROPE_THETA ** (-jnp.arange(half, dtype=F32) / half)
    ang = pos.astype(F32)[..., None] * inv
    cos, sin = jnp.cos(ang), jnp.sin(ang)
    shp = pos.shape + (HEAD_DIM - ROPE_DIM,)
    c = jnp.concatenate([cos, cos, jnp.ones(shp, F32)], axis=-1)
    sa = jnp.concatenate([-sin, jnp.zeros(pos.shape + (HEAD_DIM - half,), F32)], axis=-1)
    sb = jnp.concatenate([jnp.zeros(pos.shape + (half,), F32), sin, jnp.zeros(shp, F32)], axis=-1)
    return c, sa, sb


def _rmsnorm_kernel(x_ref, g_ref, o_ref, *ot_ref):
    x = x_ref[...].astype(F32)
    ms = jnp.mean(x * x, axis=-1, keepdims=True)
    y = x * lax.rsqrt(ms + RMS_EPS) * g_ref[...]
    o_ref[...] = y.astype(o_ref.dtype)
    if ot_ref:
        ot_ref[0][...] = y.T.astype(ot_ref[0].dtype)


def rmsnorm(x2d, g, out_dtype, tm=256, with_transpose=False):
    M, D = x2d.shape
    tm = min(tm, M)
    out_specs = [pl.BlockSpec((tm, D), lambda i: (i, 0))]
    out_shape = [jax.ShapeDtypeStruct((M, D), out_dtype)]
    if with_transpose:
        out_specs.append(pl.BlockSpec((D, tm), lambda i: (0, i)))
        out_shape.append(jax.ShapeDtypeStruct((D, M), out_dtype))
    out = pl.pallas_call(
        _rmsnorm_kernel,
        grid=(M // tm,),
        in_specs=[pl.BlockSpec((tm, D), lambda i: (i, 0)), pl.BlockSpec((1, D), lambda i: (0, 0))],
        out_specs=out_specs,
        out_shape=out_shape,
        compiler_params=_cparams(("parallel",)),
        name="rmsnorm",
    )(x2d, g.reshape(1, D).astype(F32))
    return out if with_transpose else out[0]


def _mm_kernel(*refs, nk, mode, n_extra, w_t):
    x_ref, w_ref = refs[0], refs[1]
    extras = refs[2:2 + n_extra]
    o_ref = refs[2 + n_extra]

    def product():
        w = w_ref[...]
        if w.dtype != x_ref.dtype:
            w = w.astype(x_ref.dtype)
        return _dot_nt(x_ref[...], w) if w_t else _dot(x_ref[...], w)

    def epilogue(acc):
        if mode == "plain":
            o_ref[...] = acc.astype(o_ref.dtype)
        elif mode == "sigmoid":
            o_ref[...] = jax.nn.sigmoid(acc).astype(o_ref.dtype)
        elif mode == "residual":
            o_ref[...] = (extras[0][...] + acc).astype(o_ref.dtype)
        elif mode in ("heads", "heads_sigmoid", "heads_rope"):
            nh = acc.shape[1] // LANES
            for j in range(nh):
                a = acc[:, j * LANES:(j + 1) * LANES]
                if mode == "heads_sigmoid":
                    a = jax.nn.sigmoid(a)
                elif mode == "heads_rope":
                    a = _rope_apply(a, extras[0][...], extras[1][...], extras[2][...])
                o_ref[0, j] = a.astype(o_ref.dtype)
        elif mode == "heads_t":
            for j in range(acc.shape[1] // LANES):
                for c in range(acc.shape[0] // LANES):
                    o_ref[0, j, c] = acc[c * LANES:(c + 1) * LANES, j * LANES:(j + 1) * LANES].T.astype(o_ref.dtype)
        else:
            raise ValueError(mode)

    if nk == 1:
        epilogue(product())
    else:
        acc_ref = refs[-1]
        k = pl.program_id(2)

        @pl.when(k == 0)
        def _():
            acc_ref[...] = jnp.zeros_like(acc_ref)

        acc_ref[...] += product()

        @pl.when(k == nk - 1)
        def _():
            epilogue(acc_ref[...])


def matmul(x, w, *, mode="plain", extras=(), out_dtype=F32, tm=1024, tn=512, tk=None, seq=None, name="mm",
           w_t=False, wcols=None):
    M, K = x.shape
    wcol0, N = (0, w.shape[0 if w_t else 1]) if wcols is None else wcols
    tm, tn = min(tm, M), min(tn, N)
    assert wcol0 % tn == 0
    wblk0 = wcol0 // tn
    if mode.startswith("heads"):
        tm = min(tm, seq)
    tk = K if tk is None else min(tk, K)
    nk = K // tk
    assert M % tm == 0 and N % tn == 0 and K % tk == 0
    grid = (M // tm, N // tn, nk)
    if w_t:
        w_spec = pl.BlockSpec((tn, tk), lambda i, j, k: (wblk0 + j, k))
    else:
        w_spec = pl.BlockSpec((tk, tn), lambda i, j, k: (k, wblk0 + j))
    in_specs = [pl.BlockSpec((tm, tk), lambda i, j, k: (i, k)), w_spec]
    if mode == "residual":
        in_specs.append(pl.BlockSpec((tm, tn), lambda i, j, k: (i, j)))
    elif mode == "heads_rope":
        in_specs += [pl.BlockSpec((tm, LANES), lambda i, j, k: (i, 0))] * 3
    if mode == "heads_t":
        assert seq % tm == 0 and tn % LANES == 0 and tm % LANES == 0
        spb = seq // tm
        out_shape = jax.ShapeDtypeStruct((M // seq, N // LANES, seq // LANES, LANES, LANES), out_dtype)
        out_spec = pl.BlockSpec((1, tn // LANES, tm // LANES, LANES, LANES),
                                lambda i, j, k: (i // spb, j, i % spb, 0, 0))
    elif mode.startswith("heads"):
        assert seq % tm == 0 and tn % LANES == 0
        spb = seq // tm
        out_shape = jax.ShapeDtypeStruct((M // seq, N // LANES, seq, LANES), out_dtype)
        out_spec = pl.BlockSpec((1, tn // LANES, tm, LANES), lambda i, j, k: (i // spb, j, i % spb, 0))
    else:
        out_shape = jax.ShapeDtypeStruct((M, N), out_dtype)
        out_spec = pl.BlockSpec((tm, tn), lambda i, j, k: (i, j))
    scratch = [pltpu.VMEM((tm, tn), F32)] if nk > 1 else []
    return pl.pallas_call(
        functools.partial(_mm_kernel, nk=nk, mode=mode, n_extra=len(extras), w_t=w_t),
        grid=grid,
        in_specs=in_specs,
        out_specs=out_spec,
        out_shape=out_shape,
        scratch_shapes=scratch,
        compiler_params=_cparams(("parallel", "parallel", "arbitrary")),
        name=name,
    )(x, w, *extras)


SUBLANES = 8


def _repack_kernel(tab_ref, w_ref, o_ref):
    o_ref[...] = w_ref[0].astype(o_ref.dtype)


def repack_rows(wt3, l, pieces, tr=512):
    _, NC, D = wt3.shape
    assert all(r0 % SUBLANES == 0 and nrow % tr == 0 for r0, nrow in pieces)
    starts = [r0 + r for r0, nrow in pieces for r in range(0, nrow, tr)]
    tab = jnp.asarray([s // SUBLANES for s in starts], jnp.int32)
    grid_spec = pltpu.PrefetchScalarGridSpec(
        num_scalar_prefetch=1, grid=(len(starts),),
        in_specs=[pl.BlockSpec((pl.Element(1), pl.Element(tr), pl.Element(D)),
                               lambda j, t: (l, t[j] * SUBLANES, 0))],
        out_specs=pl.BlockSpec((tr, D), lambda j, t: (j, 0)))
    return pl.pallas_call(
        _repack_kernel, grid_spec=grid_spec, out_shape=jax.ShapeDtypeStruct((len(starts) * tr, D), BF16),
        compiler_params=_cparams(("arbitrary",)), name="repack_w_in",
    )(tab, wt3)


def _compress_kernel(x_ref, w1_ref, w2_ref, pe_ref, c_ref, sa_ref, sb_ref, o_ref):
    half = x_ref.shape[-1]
    x = x_ref[0, 0]
    a = _dot(x, w1_ref[0, :half, :])
    b = _dot(x, w1_ref[0, half:, :])
    pet = _dot(pe_ref[0], w1_ref[0])
    n = a.shape[0]
    hid = a + pltpu.roll(b, n - 1, 0) + pet[0:1]
    out = _dot(_gelu(hid).astype(BF16), w2_ref[0])
    out = _rope_apply(out, c_ref[0, 0], sa_ref[0, 0], sb_ref[0, 0])
    o_ref[0, 0, 0] = out.astype(o_ref.dtype)


def nsa_compress(kvp, w1, w2, pe8, tabs):
    B, _, S, dh = kvp.shape
    G = NSA_GROUPS
    nch = S // CMP_STRIDE
    xv = kvp.reshape(B, kvp.shape[1], nch, CMP_STRIDE * dh)
    hidden = w1.shape[-1]
    tab_spec = pl.BlockSpec((1, 1, nch, LANES), lambda w, b, g: (w, b, 0, 0))
    return pl.pallas_call(
        _compress_kernel,
        grid=(2, B, G),
        in_specs=[
            pl.BlockSpec((1, 1, nch, CMP_STRIDE * dh), lambda w, b, g: (b, w * G + g, 0, 0)),
            pl.BlockSpec((1, CMP_LEN * dh, hidden), lambda w, b, g: (w, 0, 0)),
            pl.BlockSpec((1, hidden, dh), lambda w, b, g: (w, 0, 0)),
            pl.BlockSpec((1, 8, CMP_LEN * dh), lambda w, b, g: (w, 0, 0)),
            tab_spec, tab_spec, tab_spec,
        ],
        out_specs=pl.BlockSpec((1, 1, 1, nch, dh), lambda w, b, g: (w, b, g, 0, 0)),
        out_shape=jax.ShapeDtypeStruct((2, B, G, nch, dh), BF16),
        compiler_params=_cparams(("parallel", "parallel", "parallel")),
        name="nsa_compress",
    )(xv, w1, w2, pe8, *tabs)


def _dot_tn(a, b):
    return lax.dot_general(a, b, (((0,), (0,)), ((), ())), preferred_element_type=F32)


NSA_SPLIT = 2


def _nsa_kernel(q_ref, ks_ref, kw_ref, vs_ref, vw_ref, kc_ref, vc_ref, gt_ref, cmap_ref, e_ref, o_ref, *, ck):
    n = NSA_HPG
    ns = NSA_SPLIT
    hps = n // ns
    sq = hps * Q_BLOCK
    streams = range(ns)
    qb = pl.program_id(2)
    t0 = qb * Q_BLOCK
    c2 = (HEAD_DIM ** -0.5) * 1.4426950408889634
    nsel = ks_ref.shape[2] // SEL_BLOCK
    q_t = [jnp.concatenate([q_ref[0, s * hps + h].T for h in range(hps)], axis=1) for s in streams]
    tq = t0 + lax.broadcasted_iota(jnp.int32, (1, Q_BLOCK), 1)

    def heads(x):
        return jnp.concatenate([x] * hps, axis=1)

    def scores(k, ok_t):
        bias = heads(jnp.where(ok_t, 0.0, NEG_INF))
        return [_dot(k, q_t[s]) + bias for s in streams]

    def values(vt_ref, tile0, ntile):
        vt = jnp.concatenate([vt_ref[0, 0, tile0 + i] for i in range(ntile)], axis=1)
        return jnp.concatenate([vt, jnp.ones(vt.shape, vt.dtype)], axis=0)

    kc = kc_ref[0, 0, 0]
    vc = vc_ref[0, 0, 0]
    ncmp = kc.shape[0]
    crow = lax.broadcasted_iota(jnp.int32, (ncmp, Q_BLOCK), 0)
    sb = scores(kc, (crow < ncmp - 1) & (crow * CMP_STRIDE + (CMP_LEN - 1) <= tq))
    p = [jnp.exp2((sb[s] - jnp.max(sb[s], axis=0, keepdims=True)) * c2) for s in streams]
    anyv = heads((tq >= CMP_LEN - 1).astype(F32))
    p = [p[s] * (anyv / jnp.sum(p[s], axis=0, keepdims=True)) for s in streams]
    o_c = [_dot_tn(vc, p[s].astype(BF16)) for s in streams]

    psum = None
    for s in streams:
        for h in range(hps):
            ph = p[s][:, h * Q_BLOCK:(h + 1) * Q_BLOCK]
            psum = ph if psum is None else psum + ph
    hi = psum.astype(BF16)
    lo = (psum - hi.astype(F32)).astype(BF16)
    imp = _dot(cmap_ref[...], hi) + _dot(cmap_ref[...], lo)
    jr = lax.broadcasted_iota(jnp.int32, (LANES, Q_BLOCK), 0)
    cur = tq // SEL_BLOCK
    imp = jnp.where(jr == 0, 1e6, imp)
    imp = jnp.where(jr == cur, 1e6, imp)
    imp = jnp.where(jr == cur - 1, 1e6, imp)
    imp = jnp.where(jr > cur, -1e6, imp)

    nrow = min(((nsel + 7) // 8) * 8, LANES)
    imp = jnp.where(jr >= nsel, -3e6, imp)[0:nrow]
    jrr = jr[0:nrow]
    rank = jnp.zeros((nrow, Q_BLOCK), F32)
    for i in range(min(nsel, LANES)):
        row = imp[i:i + 1, :]
        rank = rank + jnp.where(row > imp, 1.0, jnp.where(row == imp, jnp.where(jrr > i, 1.0, 0.0), 0.0))
    sel = jnp.where(rank < float(SEL_TOP), 1.0, 0.0)
    if nrow < LANES:
        sel = jnp.concatenate([sel, jnp.zeros((LANES - nrow, Q_BLOCK), F32)], axis=0)
    sel = sel.astype(BF16)

    n_ch = (t0 + Q_BLOCK + ck - 1) // ck
    krow = lax.broadcasted_iota(jnp.int32, (ck, Q_BLOCK), 0)

    def body(kb, carry):
        m_i, acc = carry[:ns], carry[ns:]
        k0 = pl.multiple_of(kb * ck, ck)
        kblk = ks_ref[0, 0, pl.ds(k0, ck), :]
        v1 = values(vs_ref, kb * (ck // LANES), ck // LANES)
        em = _dot(e_ref[kb], sel)
        sb = scores(kblk, (k0 + krow <= tq) & (em > 0.5))
        m_new = [jnp.maximum(m_i[s], jnp.max(sb[s], axis=0, keepdims=True)) for s in streams]
        alpha = [jnp.exp2((m_i[s] - m_new[s]) * c2) for s in streams]
        pp = [jnp.exp2(((sb[s] - m_new[s]) * c2).astype(BF16)) for s in streams]
        acc = [alpha[s] * acc[s] + _dot(v1, pp[s]) for s in streams]
        return tuple(m_new) + tuple(acc)

    init = tuple(jnp.full((1, sq), NEG_INF, F32) for _ in streams) + tuple(
        jnp.zeros((2 * HEAD_DIM, sq), F32) for _ in streams)
    acc_f = lax.fori_loop(0, n_ch, body, init)[ns:]
    o_s = [a[0:HEAD_DIM] / a[HEAD_DIM:HEAD_DIM + 1] for a in acc_f]

    wlen = min(WINDOW + Q_BLOCK, kw_ref.shape[2])
    w0 = pl.multiple_of(jnp.maximum(t0 + Q_BLOCK - wlen, 0), Q_BLOCK)
    kblk = kw_ref[0, 0, pl.ds(w0, wlen), :]
    v1 = values(vw_ref, w0 // LANES, wlen // LANES)
    kpos = w0 + lax.broadcasted_iota(jnp.int32, (wlen, Q_BLOCK), 0)
    sb = scores(kblk, (kpos <= tq) & (tq - kpos < WINDOW))
    pw = [jnp.exp2(((sb[s] - jnp.max(sb[s], axis=0, keepdims=True)) * c2).astype(BF16)) for s in streams]
    acc_w = [_dot(v1, pw[s]) for s in streams]
    o_w = [a[0:HEAD_DIM] / a[HEAD_DIM:HEAD_DIM + 1] for a in acc_w]

    gt = gt_ref[0, 0].T
    for h in range(n):
        s = h // hps
        r = slice((h % hps) * Q_BLOCK, (h % hps + 1) * Q_BLOCK)
        o = (gt[3 * h:3 * h + 1] * o_c[s][:, r] + gt[3 * h + 1:3 * h + 2] * o_s[s][:, r]
             + gt[3 * h + 2:3 * h + 3] * o_w[s][:, r])
        o_ref[0, :, h * HEAD_DIM:(h + 1) * HEAD_DIM] = o.T.astype(o_ref.dtype)


def nsa_attention(qkr, vt, cmp, gates, ck=1024):
    B, _, S, dh = qkr.shape
    G, n = NSA_GROUPS, NSA_HPG
    ncmp = cmp.shape[3]
    nsel = S // SEL_BLOCK
    ck = min(ck, S)
    r_sel, r_cmp = SEL_BLOCK // CMP_STRIDE, CMP_LEN // CMP_STRIDE
    tgt = (r_sel * jnp.arange(nsel)[:, None, None] + jnp.arange(r_sel)[None, :, None]
           - jnp.arange(r_cmp)[None, None, :])
    cmap = (jnp.arange(ncmp - 1)[:, None, None, None] == tgt[None]).sum((2, 3)).astype(F32)
    cmap = jnp.pad(cmap, ((0, 1), (0, LANES - nsel))).astype(BF16).T
    e = (jnp.arange(S)[:, None] // SEL_BLOCK == jnp.arange(LANES)[None, :]).astype(BF16)
    e3 = e.reshape(S // ck, ck, LANES)

    def head(h0):
        return pl.BlockSpec((1, 1, S, dh), lambda b, g, i: (b, h0 + g, 0, 0))

    def head_t(h0):
        return pl.BlockSpec((1, 1, S // LANES, dh, LANES), lambda b, g, i: (b, h0 + g, 0, 0, 0))

    return pl.pallas_call(
        functools.partial(_nsa_kernel, ck=ck),
        grid=(B, G, S // Q_BLOCK),
        in_specs=[
            pl.BlockSpec((1, n, Q_BLOCK, dh), lambda b, g, i: (b, g, i, 0)),
            head(16), head(20), head_t(0), head_t(G),
            pl.BlockSpec((1, 1, 1, ncmp, dh), lambda b, g, i: (0, b, g, 0, 0)),
            pl.BlockSpec((1, 1, 1, ncmp, dh), lambda b, g, i: (1, b, g, 0, 0)),
            pl.BlockSpec((1, 1, Q_BLOCK, LANES), lambda b, g, i: (b, g, i, 0)),
            pl.BlockSpec((LANES, ncmp), lambda b, g, i: (0, 0)),
            pl.BlockSpec((S // ck, ck, LANES), lambda b, g, i: (0, 0, 0)),
        ],
        out_specs=pl.BlockSpec((1, Q_BLOCK, n * dh), lambda b, g, i: (b, i, g)),
        out_shape=jax.ShapeDtypeStruct((B, S, G * n * dh), BF16),
        compiler_params=_cparams(("parallel", "parallel", "arbitrary")),
        name="nsa_attention",
    )(qkr, qkr, qkr, vt, vt, cmp, cmp, gates, cmap, e3)


def _rwkv_feat_kernel(z_ref, zp_ref, mu_ref, o_ref):
    i = pl.program_id(1)
    z = z_ref[0]
    prev = zp_ref[0, 7:8, :] * (i > 0).astype(F32)
    row = lax.broadcasted_iota(jnp.int32, z.shape, 0)
    sh = jnp.where(row == 0, prev, pltpu.roll(z, 1, 0))
    zs = z + (sh - z) * mu_ref[...]
    o_ref[0, :, 0:128] = jnp.tanh(zs[:, 0:128])
    o_ref[0, :, 128:256] = zs[:, 128:256]
    o_ref[0, :, 256:] = jax.nn.sigmoid(zs[:, 256:])


def rwkv_features(z_lora, mu_lora, tm=512):
    B, S, W = z_lora.shape
    tm = min(tm, S)
    return pl.pallas_call(
        _rwkv_feat_kernel,
        grid=(B, S // tm),
        in_specs=[pl.BlockSpec((1, tm, W), lambda b, i: (b, i, 0)),
                  pl.BlockSpec((1, 8, W), lambda b, i: (b, jnp.maximum(i * (tm // 8) - 1, 0), 0)),
                  pl.BlockSpec((1, W), lambda b, i: (0, 0))],
        out_specs=pl.BlockSpec((1, tm, W), lambda b, i: (b, i, 0)),
        out_shape=jax.ShapeDtypeStruct((B, S, W), F32),
        compiler_params=_cparams(("parallel", "arbitrary")),
        name="rwkv_features",
    )(z_lora, z_lora, mu_lora)


def _rwkv_kernel(r_ref, k_ref, v_ref, f_ref, par_ref, w2_ref, a2_ref, g2_ref, o_ref, st_ref, prev_ref):
    C = RWKV_CHUNK
    W = r_ref.shape[-1]
    npair = W // LANES
    c = pl.program_id(2)

    @pl.when(c == 0)
    def _():
        st_ref[...] = jnp.zeros_like(st_ref)
        prev_ref[...] = jnp.zeros_like(prev_ref)

    par = par_ref[...]
    mu_r, mu_k, mu_v = par[0:1], par[1:2], par[2:3]
    w0, a0, k_k, k_a, r_k, ln_w, ln_b = par[3:4], par[4:5], par[5:6], par[6:7], par[7:8], par[8:9], par[9:10]

    row = lax.broadcasted_iota(jnp.int32, (C, C), 0)
    col = lax.broadcasted_iota(jnp.int32, (C, C), 1)
    trow = lax.broadcasted_iota(jnp.int32, (C, W), 0)
    lane = lax.broadcasted_iota(jnp.int32, (C, LANES), 1)
    lr = lax.broadcasted_iota(jnp.int32, (LANES, LANES), 0)
    lc = lax.broadcasted_iota(jnp.int32, (LANES, LANES), 1)

    def shifted(z_ref, idx, mu):
        z = z_ref[0]
        sh = jnp.where(trow == 0, prev_ref[idx:idx + 1, :], pltpu.roll(z, 1, 0))
        prev_ref[idx:idx + 1, :] = z[C - 1:C, :]
        return z + (sh - z) * mu

    r = shifted(r_ref, 0, mu_r)
    k = shifted(k_ref, 1, mu_k)
    v = shifted(v_ref, 2, mu_v)
    f = f_ref[0]
    w_log = -_softplus(-(w0 + _dot3(_split(f[:, 0:128]), _split(w2_ref[...])))) - 0.5
    logd = -jnp.exp(w_log)
    a = jax.nn.sigmoid(a0 + _dot3(_split(f[:, 128:256]), _split(a2_ref[...])))
    g = _dot3(_split(f[:, 256:]), _split(g2_ref[...]))

    bd = _exact(jnp.where((lr // RWKV_HEAD) == (lc // RWKV_HEAD), 1.0, 0.0))

    def head_sum(x, passes=2):
        sp = _split if passes == 2 else (lambda z: (z.astype(BF16), None))
        return jnp.concatenate([_dot3(sp(x[:, p * LANES:(p + 1) * LANES]), bd) for p in range(npair)], axis=1)

    kk = k * k_k
    kk = kk / jnp.maximum(jnp.sqrt(head_sum(kk * kk)), 1e-12)
    k2 = k * (1.0 + (a - 1.0) * k_a)

    tri_incl = _exact(jnp.where(col <= row, 1.0, 0.0))
    cum = _dot3(tri_incl, _split(logd))
    cum_end = cum[C - 1:C, :]
    e_neg = jnp.exp(-cum)
    kka = kk * a
    a_t = -kk * jnp.exp(cum - logd)
    b_t = kka * e_neg
    k_t = k2 * e_neg
    r_t = r * jnp.exp(cum)
    e_end = jnp.exp(cum_end - cum)
    b_e = kka * e_end
    k_e = k2 * e_end
    p_end = jnp.exp(cum_end)

    strict = col < row
    incl = col <= row
    eye = jnp.where(col == row, 1.0, 0.0)
    n_sq = C.bit_length() - 2
    hpp = LANES // RWKV_HEAD
    heads = [(p, h) for p in range(npair) for h in range(hpp)]
    hd = []
    for p, h in heads:
        sl = slice(p * LANES, (p + 1) * LANES)
        mh = (lane // RWKV_HEAD) == h
        s_am = _split(jnp.where(mh, a_t[:, sl], 0.0))
        s_rm = _split(jnp.where(mh, r_t[:, sl], 0.0))
        hd.append(dict(
            sl=sl, s_am=s_am, s_rm=s_rm,
            s_vm=_split(jnp.where(mh, v[:, sl], 0.0)),
            s_bem=_split(jnp.where(mh, b_e[:, sl], 0.0)),
            s_kem=_split(jnp.where(mh, k_e[:, sl], 0.0)),
            rhs=_split(jnp.concatenate([b_t[:, sl], k_t[:, sl]], axis=0)) if h == 0 else hd[-1]["rhs"]))
    for d in hd:
        amat = _dot3(d["s_am"], d["rhs"], "nt")
        rmat = _dot_nt(d["s_rm"][0], d["rhs"][0])
        l_ab = jnp.where(strict, amat[:, 0:C], 0.0)
        d["l_ak"] = _split(jnp.where(strict, amat[:, C:2 * C], 0.0))
        d["l_rb"] = jnp.where(incl, rmat[:, 0:C], 0.0).astype(BF16)
        d["l_rk"] = jnp.where(incl, rmat[:, C:2 * C], 0.0).astype(BF16)
        d["x"] = eye + l_ab
        d["s_pw"] = _split(l_ab)
    for i in range(n_sq):
        for d in hd:
            d["s_pw"] = _split(_dot3(d["s_pw"], d["s_pw"])) if i == 0 else (_dot(d["s_pw"][0], d["s_pw"][0]).astype(BF16), None)
        for d in hd:
            d["x"] = d["x"] + (_dot3(_split(d["x"]), d["s_pw"]) if i == 0 else _dot(d["x"].astype(BF16), d["s_pw"][0]))
    cat = lambda parts, axis: tuple(None if parts[0][i] is None else jnp.concatenate([p[i] for p in parts], axis=axis)
                                    for i in range(2))
    for i, d in enumerate(hd):
        d["st"] = st_ref[i]
        d["s_st"] = _split(d["st"])
        d["t"] = _split(_dot3(cat([d["s_am"], d["l_ak"]], 1), cat([d["s_st"], d["s_vm"]], 0)))
    for d in hd:
        d["s_u"] = _split(_dot3(_split(d["x"]), d["t"]))
    for i, d in enumerate(hd):
        if i % hpp == 0:
            decay = jnp.broadcast_to(p_end[:, d["sl"]], (LANES, LANES)).T
        st_ref[i] = d["st"] * decay + _dot3(cat([d["s_bem"], d["s_kem"]], 0), cat([d["s_u"], d["s_vm"]], 0), "tn")
    ys = []
    for d in hd:
        ys.append(_dot(jnp.concatenate([d["s_rm"][0], d["l_rb"], d["l_rk"]], axis=1),
                       jnp.concatenate([d["s_st"][0], d["s_u"][0], d["s_vm"][0]], axis=0)))
    y = jnp.concatenate([sum(ys[hpp * p + 1:hpp * (p + 1)], ys[hpp * p]) for p in range(npair)], axis=1)

    inv_n = 1.0 / RWKV_HEAD
    mean = head_sum(y) * inv_n
    d = y - mean
    var = head_sum(d * d) * inv_n
    yn = d * lax.rsqrt(var + GN_EPS) * ln_w + ln_b
    bonus = head_sum(r * k2 * r_k, passes=1) * v
    o_ref[0] = ((yn + bonus) * g).astype(o_ref.dtype)


def rwkv_scan(z_rkv, feat, par, w2p, a2p, g2, pairs_per_step=8):
    B, S, R3 = z_rkv.shape
    R = R3 // 3
    W = pairs_per_step * LANES
    nblk = R // W
    C = RWKV_CHUNK
    assert S % C == 0 and R % W == 0

    def col(off):
        return pl.BlockSpec((1, C, W), lambda b, p, c: (b, c, off + p))

    return pl.pallas_call(
        _rwkv_kernel,
        grid=(B, nblk, S // C),
        in_specs=[
            col(0), col(nblk), col(2 * nblk),
            pl.BlockSpec((1, C, feat.shape[-1]), lambda b, p, c: (b, c, 0)),
            pl.BlockSpec((16, W), lambda b, p, c: (0, p)),
            pl.BlockSpec((LANES, W), lambda b, p, c: (0, p)),
            pl.BlockSpec((LANES, W), lambda b, p, c: (0, p)),
            pl.BlockSpec((GATE_LORA, W), lambda b, p, c: (0, p)),
        ],
        out_specs=pl.BlockSpec((1, C, W), lambda b, p, c: (b, c, p)),
        out_shape=jax.ShapeDtypeStruct((B, S, R), BF16),
        scratch_shapes=[pltpu.VMEM((W // RWKV_HEAD, LANES, LANES), F32), pltpu.VMEM((8, W), F32)],
        compiler_params=_cparams(("parallel", "parallel", "arbitrary")),
        name="rwkv_scan",
    )(z_rkv, z_rkv, z_rkv, feat, par, w2p, a2p, g2)


def _merge_kernel(on_ref, or_ref, wn_ref, wr_ref, ga_ref, gb_ref, o_ref):
    pa = _dot(on_ref[...], wn_ref[...].astype(on_ref.dtype))
    pb = _dot(or_ref[...], wr_ref[...].astype(or_ref.dtype))
    o_ref[...] = (ga_ref[...].astype(F32) * pa + gb_ref[...].astype(F32) * pb).astype(o_ref.dtype)


def merge_mixers(o_nsa, o_rwkv, wn, wr, gates, tm=1024, tn=512):
    M, Kn = o_nsa.shape
    Kr = o_rwkv.shape[1]
    D = wn.shape[1]
    tm, tn = min(tm, M), min(tn, D)
    nb = D // tn
    return pl.pallas_call(
        _merge_kernel,
        grid=(M // tm, nb),
        in_specs=[
            pl.BlockSpec((tm, Kn), lambda i, j: (i, 0)),
            pl.BlockSpec((tm, Kr), lambda i, j: (i, 0)),
            pl.BlockSpec((Kn, tn), lambda i, j: (0, j)),
            pl.BlockSpec((Kr, tn), lambda i, j: (0, j)),
            pl.BlockSpec((tm, tn), lambda i, j: (i, j)),
            pl.BlockSpec((tm, tn), lambda i, j: (i, nb + j)),
        ],
        out_specs=pl.BlockSpec((tm, tn), lambda i, j: (i, j)),
        out_shape=jax.ShapeDtypeStruct((M, D), BF16),
        compiler_params=_cparams(("parallel", "parallel")),
        name="merge_mixers",
    )(o_nsa, o_rwkv, wn, wr, gates, gates)


def _xattn_kernel(q_ref, kv_ref, wo_ref, x_ref, o_ref):
    nh = XATTN_HEADS
    scale = HEAD_DIM ** -0.5
    outs = []
    for h in range(nh):
        q = q_ref[:, h * HEAD_DIM:(h + 1) * HEAD_DIM]
        k = kv_ref[0, :, h * HEAD_DIM:(h + 1) * HEAD_DIM]
        v = kv_ref[0, :, (nh + h) * HEAD_DIM:(nh + h + 1) * HEAD_DIM]
        s = _dot_nt(q, k) * scale
        m = jnp.max(s, axis=-1, keepdims=True)
        p = jnp.exp(s - m)
        p = p / jnp.sum(p, axis=-1, keepdims=True)
        outs.append(_dot(p.astype(BF16), v).astype(BF16))
    o = jnp.concatenate(outs, axis=-1)
    o_ref[...] = x_ref[...] + _dot(o, wo_ref[...])


def cross_attention(q, kv, wo, x, seq, tm=256):
    T, D = x.shape
    tm = min(tm, seq)
    spb = seq // tm
    Mm = kv.shape[1]
    return pl.pallas_call(
        _xattn_kernel,
        grid=(T // tm,),
        in_specs=[
            pl.BlockSpec((tm, q.shape[1]), lambda i: (i, 0)),
            pl.BlockSpec((1, Mm, kv.shape[2]), lambda i: (i // spb, 0, 0)),
            pl.BlockSpec(wo.shape, lambda i: (0, 0)),
            pl.BlockSpec((tm, D), lambda i: (i, 0)),
        ],
        out_specs=pl.BlockSpec((tm, D), lambda i: (i, 0)),
        out_shape=jax.ShapeDtypeStruct((T, D), F32),
        compiler_params=_cparams(("parallel",)),
        name="cross_attention",
    )(q, kv, wo, x)


def _top16(x, iota=None, n=None):
    if iota is None:
        n = x.shape[0]
        iota = lax.broadcasted_iota(jnp.int32, x.shape, 0).astype(F32)
    cur = x
    rank = jnp.full(x.shape, float(PEER_TOPK), F32)
    vals = []
    for r in range(PEER_TOPK):
        m = jnp.max(cur, axis=0, keepdims=True)
        idx = jnp.min(jnp.where(cur == m, iota, float(n)), axis=0, keepdims=True)
        hit = iota == idx
        rank = jnp.where(hit, float(r), rank)
        cur = jnp.where(hit, -jnp.inf, cur)
        vals.append(m)
    return jnp.concatenate(vals, axis=0), rank


def _peer_route_kernel(q_ref, keys_ref, row_ref, pl_ref):
    K = PEER_TOPK
    q = q_ref[...]
    half = q.shape[1] // 2
    s1 = _dotf_nt(keys_ref[0, 0], q[:, :half])
    s2 = _dotf_nt(keys_ref[0, 1], q[:, half:])
    v1, rank1 = _top16(s1)
    v2, rank2 = _top16(s2)
    tmq = q.shape[0]
    r8 = lax.broadcasted_iota(jnp.int32, (8, tmq), 0)
    ninf = -jnp.inf
    lo8 = v2[0:8]
    tiles = [
        (v1[0:1] + lo8, r8),
        (v1[0:1] + v2[8:16], 8 + r8),
        (v1[1:2] + lo8, 16 + r8),
        (jnp.where(r8 < 5, v1[2:3] + lo8, ninf), 32 + r8),
        (jnp.where(r8 < 7, jnp.where(r8 < 4, v1[3:4], v1[4:5]) + jnp.where(r8 < 4, lo8, pltpu.roll(lo8, 4, 0)), ninf),
         jnp.where(r8 < 4, 48 + r8, 60 + r8)),
        (jnp.where(r8 < 6, jnp.where(r8 < 2, v1[5:6], jnp.where(r8 < 4, v1[6:7], v1[7:8]))
                   + jnp.where(r8 % 2 == 0, v2[0:1], v2[1:2]), ninf),
         80 + (r8 // 2) * 16 + r8 % 2),
        (v1[8:16] + v2[0:1], (8 + r8) * 16),
    ]
    cand = jnp.concatenate([t for t, _ in tiles], axis=0)
    cidx = jnp.concatenate([i for _, i in tiles], axis=0).astype(F32)
    top_s, crank = _top16(cand, cidx, K * K)
    chosen = jnp.where(crank < float(K), 1.0, 0.0)
    z = jnp.sum(jnp.exp(top_s - top_s[0:1]), axis=0, keepdims=True)
    ch = [chosen[8 * t:8 * (t + 1)] for t in range(len(tiles))]
    colsum = lambda x: jnp.sum(x, axis=0, keepdims=True)
    upper = (r8 < 4).astype(F32)
    j_rows = [colsum(ch[0]) + colsum(ch[1]), colsum(ch[2]), colsum(ch[3]),
              colsum(ch[4] * upper), colsum(ch[4] * (1.0 - upper))]
    j_rows += [ch[5][2 * m:2 * m + 1] + ch[5][2 * m + 1:2 * m + 2] for m in range(3)]
    j_rows += [ch[6][m:m + 1] for m in range(8)]
    jn = jnp.zeros_like(s1)
    for i in range(K):
        jn = jnp.where(rank1 == float(i), j_rows[i], jn)
    row_ref[0, 0] = jnp.exp(s1 - v1[0:1]) / z
    row_ref[0, 1] = jn
    pl_ref[0, 0] = jnp.exp(s2 - v2[0:1]).astype(pl_ref.dtype)
    pl_ref[0, 1] = rank2.astype(pl_ref.dtype)


def peer_route(q, keys, tm=256):
    T = q.shape[0]
    H, _, nkeys, hd = keys.shape
    tm = min(tm, T)
    spec = pl.BlockSpec((1, 2, nkeys, tm), lambda i, h: (h, 0, 0, i))
    return pl.pallas_call(
        _peer_route_kernel,
        grid=(T // tm, H),
        in_specs=[pl.BlockSpec((tm, 2 * hd), lambda i, h: (i, h)),
                  pl.BlockSpec((1, 2, nkeys, hd), lambda i, h: (h, 0, 0, 0))],
        out_specs=[spec, spec],
        out_shape=[jax.ShapeDtypeStruct((H, 2, nkeys, T), F32), jax.ShapeDtypeStruct((H, 2, nkeys, T), BF16)],
        compiler_params=_cparams(("parallel", "parallel")),
        name="peer_route",
    )(q, keys)


def _peer_act_kernel(h_ref, u_ref, row_ref, pl_ref, o_ref):
    nk = PEER_KEYS
    ec = u_ref.shape[0]
    j = pl.program_id(1)
    hu = _dot(u_ref[...], h_ref[...])
    zero = jnp.zeros((), o_ref.dtype)
    for al in range(ec // nk):
        a = j * (ec // nk) + al
        w = None
        for hd in range(row_ref.shape[0]):
            e1 = row_ref[hd, 0, pl.ds(a, 1), :].astype(o_ref.dtype)
            jn = row_ref[hd, 1, pl.ds(a, 1), :].astype(o_ref.dtype)
            g = jnp.where(pl_ref[hd, 1] < jn, e1 * pl_ref[hd, 0], zero)
            w = g if w is None else w + g
        act = _gelu(hu[al * nk:(al + 1) * nk, :].astype(o_ref.dtype)) * w
        o_ref[:, al * nk:(al + 1) * nk] = act.T


def peer_activations(h_t, u_tab, rows, planes, tm=512, ec=512):
    D, T = h_t.shape
    E = u_tab.shape[0]
    tm, ec = min(tm, T), min(ec, E)
    H, _, nkeys, _ = rows.shape
    rspec = pl.BlockSpec((H, 2, nkeys, tm), lambda i, j: (0, 0, 0, i))
    return pl.pallas_call(
        _peer_act_kernel,
        grid=(T // tm, E // ec),
        in_specs=[pl.BlockSpec((D, tm), lambda i, j: (0, i)),
                  pl.BlockSpec((ec, D), lambda i, j: (j, 0)),
                  rspec, rspec],
        out_specs=pl.BlockSpec((tm, ec), lambda i, j: (i, j)),
        out_shape=jax.ShapeDtypeStruct((T, E), BF16),
        compiler_params=_cparams(("parallel", "arbitrary")),
        name="peer_activations",
    )(h_t, u_tab, rows, planes)


def kernel(x, mem, positions, norm_mix, w_in, nsa_cmp_pe, nsa_cmp_w1, nsa_cmp_w2, rwkv_mu, rwkv_w0, rwkv_w2, rwkv_a0, rwkv_a2, rwkv_g2, rwkv_k_k, rwkv_k_a, rwkv_r_k, rwkv_ln_w, rwkv_ln_b, w_proj_nsa, w_proj_rwkv, w_out, norm_xattn, norm_mem, xattn_wq, xattn_wkv, xattn_wo, norm_ffn, peer_wq, peer_keys, peer_u, peer_v, norm_final):
    B, S, D = x.shape
    T = B * S
    depth = w_in.shape[0]
    G, n, dh = NSA_GROUPS, NSA_HPG, HEAD_DIM
    nq_cols = G * n * dh
    nkv_cols = 6 * G * dh
    ngate = 3 * G * n
    R = rwkv_w0.shape[1]
    c_q, c_kv, c_g = nq_cols, nq_cols + nkv_cols, nq_cols + nkv_cols + ngate
    c_r = c_g + 3 * R + DECAY_LORA + AAA_LORA + GATE_LORA
    gd = G * dh

    pos_flat = positions.reshape(T)
    rope_tabs = _rope_tables(pos_flat)
    cend = jnp.arange(S // CMP_STRIDE) * CMP_STRIDE + (CMP_LEN - 1)
    ctab = _rope_tables(positions[:, jnp.minimum(cend, S - 1)])
    ident = _rope_tables(jnp.zeros_like(positions[:, :S // CMP_STRIDE]))
    cmp_tabs = [jnp.stack([a, b]) for a, b in zip(ctab, ident)]

    xf = x.reshape(T, D)
    for l in range(depth):
        w_t = jnp.swapaxes(w_in, 1, 2)
        pieces = [(0, c_q), (c_q + 2 * gd, gd), (c_q + 4 * gd, gd),
                  (c_q, 2 * gd), (c_q + 3 * gd, gd), (c_q + 5 * gd, gd),
                  (c_g, 3 * R), (c_r, 2 * D)]
        wp = repack_rows(w_t, l, pieces)
        g_rope = (0, c_q + 2 * gd)
        g_plain = (g_rope[1], 4 * gd)
        g_rkv = (g_plain[0] + g_plain[1], 3 * R)
        g_merge = (g_rkv[0] + g_rkv[1], 2 * D)
        w_gate = jnp.pad(w_t[l, c_kv:c_g].reshape(G, 3 * n, D), ((0, 0), (0, LANES - 3 * n), (0, 0)))
        w_gate = w_gate.reshape(G * LANES, D)
        wl = w_t[l, c_g + 3 * R:c_r]
        padl = lambda a, n_: jnp.pad(a, ((0, n_ - a.shape[0]), (0, 0)))
        w_lora = jnp.concatenate([padl(wl[:DECAY_LORA], LANES), padl(wl[DECAY_LORA:DECAY_LORA + AAA_LORA], LANES),
                                  wl[DECAY_LORA + AAA_LORA:]], axis=0)

        h = rmsnorm(xf, norm_mix[l], BF16)
        proj = functools.partial(matmul, h, w_t=True)
        qkr = proj(wp, wcols=g_rope, mode="heads_rope", extras=rope_tabs, out_dtype=BF16, seq=S, name="proj_rope")
        kvc = proj(wp, wcols=(g_plain[0], 2 * gd), mode="heads", out_dtype=BF16, seq=S, name="proj_kvc")
        vt = proj(wp, wcols=(g_plain[0] + 2 * gd, 2 * gd), mode="heads_t", out_dtype=BF16, seq=S, name="proj_vt")
        gates_nsa = proj(w_gate, mode="heads_sigmoid", out_dtype=F32, seq=S, name="proj_gate")
        z_rkv = proj(wp, wcols=g_rkv, out_dtype=F32, name="proj_rkv")
        z_lora = proj(w_lora, out_dtype=F32, name="proj_lora")
        gates_merge = proj(wp, wcols=g_merge, mode="sigmoid", out_dtype=BF16, name="proj_merge")

        pe8 = jnp.broadcast_to(nsa_cmp_pe[l].reshape(2, 1, CMP_LEN * dh), (2, 8, CMP_LEN * dh)).astype(BF16)
        cmp = nsa_compress(kvc, nsa_cmp_w1[l].astype(BF16), nsa_cmp_w2[l].astype(BF16), pe8, cmp_tabs)
        o_nsa = nsa_attention(qkr, vt, cmp, gates_nsa).reshape(T, nq_cols)

        mu = rwkv_mu[l]
        mul = mu[3 * R:]
        pad1 = lambda a, n_: jnp.pad(a, (0, n_ - a.shape[0]))
        mu_lora = jnp.concatenate([pad1(mul[:DECAY_LORA], LANES), pad1(mul[DECAY_LORA:DECAY_LORA + AAA_LORA], LANES),
                                   mul[DECAY_LORA + AAA_LORA:]]).reshape(1, -1)
        feat = rwkv_features(z_lora.reshape(B, S, -1), mu_lora)
        par = jnp.stack([mu[:R], mu[R:2 * R], mu[2 * R:3 * R], rwkv_w0[l], rwkv_a0[l], rwkv_k_k[l], rwkv_k_a[l],
                         rwkv_r_k[l].reshape(R), rwkv_ln_w[l], rwkv_ln_b[l]])
        par = jnp.pad(par, ((0, 16 - par.shape[0]), (0, 0)))
        w2p = jnp.pad(rwkv_w2[l], ((0, LANES - DECAY_LORA), (0, 0)))
        a2p = jnp.pad(rwkv_a2[l], ((0, LANES - AAA_LORA), (0, 0)))
        o_rwkv = rwkv_scan(z_rkv.reshape(B, S, 3 * R), feat, par, w2p, a2p, rwkv_g2[l]).reshape(T, R)

        merged = merge_mixers(o_nsa, o_rwkv, w_proj_nsa[l], w_proj_rwkv[l], gates_merge)
        xf = matmul(merged, w_out[l], mode="residual", extras=(xf,), out_dtype=F32, name="out_proj")

        h = rmsnorm(xf, norm_xattn[l], BF16)
        m = rmsnorm(mem.reshape(-1, D), norm_mem[l], BF16)
        q = matmul(h, xattn_wq[l], out_dtype=BF16, name="xattn_q")
        kvm = matmul(m, xattn_wkv[l], out_dtype=BF16, name="xattn_kv").reshape(B, mem.shape[1], -1)
        xf = cross_attention(q, kvm, xattn_wo[l].astype(BF16), xf, S)

        h, h_t = rmsnorm(xf, norm_ffn[l], BF16, with_transpose=True)
        pq = matmul(h, peer_wq[l], out_dtype=F32, name="peer_q")
        rows, planes = peer_route(pq, peer_keys[l])
        act = peer_activations(h_t, peer_u[l].astype(BF16), rows, planes)
        xf = matmul(act, peer_v[l], mode="residual", extras=(xf,), out_dtype=F32,
                    tm=1024, tn=1024, tk=1024, name="peer_out")

    return rmsnorm(xf, norm_final, F32).reshape(B, S, D)
```

```python
import functools

import jax
import jax.numpy as jnp
from jax import lax
from jax.experimental import pallas as pl
from jax.experimental.pallas import tpu as pltpu

F32 = jnp.float32
BF16 = jnp.bfloat16
HI = lax.Precision.HIGHEST

LANES = 128
HEAD_DIM = 128
ROPE_DIM = HEAD_DIM // 4
ROPE_THETA = 500000.0
RMS_EPS = 1e-6
NEG_INF = -1e30

NSA_GROUPS = 4
NSA_HPG = 4
CMP_LEN = 32
CMP_STRIDE = 16
SEL_BLOCK = 64
SEL_TOP = 16
WINDOW = 512
Q_BLOCK = 128

RWKV_HEAD = 64
DECAY_LORA = 96
AAA_LORA = 96
GATE_LORA = 256
GN_EPS = 64e-5
RWKV_CHUNK = 128

XATTN_HEADS = 4
PEER_HEADS = 8
PEER_KEYS = 128
PEER_TOPK = 16

VMEM_LIMIT = 56 * 1024 * 1024


def _cparams(sem, flags=None):
    return pltpu.CompilerParams(dimension_semantics=sem, vmem_limit_bytes=VMEM_LIMIT, flags=flags)


def _gelu(x):
    return 0.5 * x * (1.0 + jnp.tanh(0.7978845608028654 * (x + 0.044715 * x * x * x)))


def _softplus(x):
    return jnp.maximum(x, 0.0) + jnp.log(1.0 + jnp.exp(-jnp.abs(x)))


def _dot(a, b):
    return jnp.dot(a, b, preferred_element_type=F32)


def _dot_nt(a, b):
    return lax.dot_general(a, b, (((1,), (1,)), ((), ())), preferred_element_type=F32)


def _dotf_nt(a, b):
    return lax.dot_general(a, b, (((1,), (1,)), ((), ())), preferred_element_type=F32, precision=HI)


def _split(x):
    hi = x.astype(BF16)
    return hi, (x - hi.astype(F32)).astype(BF16)


def _exact(x):
    return x.astype(BF16), None


_DN = {"nn": (((1,), (0,)), ((), ())), "nt": (((1,), (1,)), ((), ())), "tn": (((0,), (0,)), ((), ()))}


def _dot3(a, b, form="nn"):
    d = lambda x, y: lax.dot_general(x, y, _DN[form], preferred_element_type=F32)
    out = d(a[0], b[0])
    if a[1] is not None:
        out = out + d(a[1], b[0])
    if b[1] is not None:
        out = out + d(a[0], b[1])
    return out


def _rope_apply(a, c, sa, sb):
    return a * c + pltpu.roll(a, LANES - ROPE_DIM // 2, 1) * sa + pltpu.roll(a, ROPE_DIM // 2, 1) * sb


def _rope_tables(pos):
    half = ROPE_DIM // 2
    inv = ROPE_THETA ** (-jnp.arange(half, dtype=F32) / half)
    ang = pos.astype(F32)[..., None] * inv
    cos, sin = jnp.cos(ang), jnp.sin(ang)
    shp = pos.shape + (HEAD_DIM - ROPE_DIM,)
    c = jnp.concatenate([cos, cos, jnp.ones(shp, F32)], axis=-1)
    sa = jnp.concatenate([-sin, jnp.zeros(pos.shape + (HEAD_DIM - half,), F32)], axis=-1)
    sb = jnp.concatenate([jnp.zeros(pos.shape + (half,), F32), sin, jnp.zeros(shp, F32)], axis=-1)
    return c, sa, sb


def _rmsnorm_kernel(x_ref, g_ref, o_ref, *ot_ref):
    x = x_ref[...].astype(F32)
    ms = jnp.mean(x * x, axis=-1, keepdims=True)
    y = x * lax.rsqrt(ms + RMS_EPS) * g_ref[...]
    o_ref[...] = y.astype(o_ref.dtype)
    if ot_ref:
        ot_ref[0][...] = y.T.astype(ot_ref[0].dtype)


def rmsnorm(x2d, g, out_dtype, tm=256, with_transpose=False):
    M, D = x2d.shape
    tm = min(tm, M)
    out_specs = [pl.BlockSpec((tm, D), lambda i: (i, 0))]
    out_shape = [jax.ShapeDtypeStruct((M, D), out_dtype)]
    if with_transpose:
        out_specs.append(pl.BlockSpec((D, tm), lambda i: (0, i)))
        out_shape.append(jax.ShapeDtypeStruct((D, M), out_dtype))
    out = pl.pallas_call(
        _rmsnorm_kernel,
        grid=(M // tm,),
        in_specs=[pl.BlockSpec((tm, D), lambda i: (i, 0)), pl.BlockSpec((1, D), lambda i: (0, 0))],
        out_specs=out_specs,
        out_shape=out_shape,
        compiler_params=_cparams(("parallel",)),
        name="rmsnorm",
    )(x2d, g.reshape(1, D).astype(F32))
    return out if with_transpose else out[0]


def _mm_kernel(*refs, nk, mode, n_extra, w_t):
    x_ref, w_ref = refs[0], refs[1]
    extras = refs[2:2 + n_extra]
    o_ref = refs[2 + n_extra]

    def product():
        w = w_ref[...]
        if w.dtype != x_ref.dtype:
            w = w.astype(x_ref.dtype)
        return _dot_nt(x_ref[...], w) if w_t else _dot(x_ref[...], w)

    def epilogue(acc):
        if mode == "plain":
            o_ref[...] = acc.astype(o_ref.dtype)
        elif mode == "sigmoid":
            o_ref[...] = jax.nn.sigmoid(acc).astype(o_ref.dtype)
        elif mode == "residual":
            o_ref[...] = (extras[0][...] + acc).astype(o_ref.dtype)
        elif mode in ("heads", "heads_sigmoid", "heads_rope"):
            nh = acc.shape[1] // LANES
            for j in range(nh):
                a = acc[:, j * LANES:(j + 1) * LANES]
                if mode == "heads_sigmoid":
                    a = jax.nn.sigmoid(a)
                elif mode == "heads_rope":
                    a = _rope_apply(a, extras[0][...], extras[1][...], extras[2][...])
                o_ref[0, j] = a.astype(o_ref.dtype)
        elif mode == "heads_t":
            for j in range(acc.shape[1] // LANES):
                for c in range(acc.shape[0] // LANES):
                    o_ref[0, j, c] = acc[c * LANES:(c + 1) * LANES, j * LANES:(j + 1) * LANES].T.astype(o_ref.dtype)
        else:
            raise ValueError(mode)

    if nk == 1:
        epilogue(product())
    else:
        acc_ref = refs[-1]
        k = pl.program_id(2)

        @pl.when(k == 0)
        def _():
            acc_ref[...] = jnp.zeros_like(acc_ref)

        acc_ref[...] += product()

        @pl.when(k == nk - 1)
        def _():
            epilogue(acc_ref[...])


def matmul(x, w, *, mode="plain", extras=(), out_dtype=F32, tm=1024, tn=512, tk=None, seq=None, name="mm",
           w_t=False, wcols=None):
    M, K = x.shape
    wcol0, N = (0, w.shape[0 if w_t else 1]) if wcols is None else wcols
    tm, tn = min(tm, M), min(tn, N)
    assert wcol0 % tn == 0
    wblk0 = wcol0 // tn
    if mode.startswith("heads"):
        tm = min(tm, seq)
    tk = K if tk is None else min(tk, K)
    nk = K // tk
    assert M % tm == 0 and N % tn == 0 and K % tk == 0
    grid = (M // tm, N // tn, nk)
    if w_t:
        w_spec = pl.BlockSpec((tn, tk), lambda i, j, k: (wblk0 + j, k))
    else:
        w_spec = pl.BlockSpec((tk, tn), lambda i, j, k: (k, wblk0 + j))
    in_specs = [pl.BlockSpec((tm, tk), lambda i, j, k: (i, k)), w_spec]
    if mode == "residual":
        in_specs.append(pl.BlockSpec((tm, tn), lambda i, j, k: (i, j)))
    elif mode == "heads_rope":
        in_specs += [pl.BlockSpec((tm, LANES), lambda i, j, k: (i, 0))] * 3
    if mode == "heads_t":
        assert seq % tm == 0 and tn % LANES == 0 and tm % LANES == 0
        spb = seq // tm
        out_shape = jax.ShapeDtypeStruct((M // seq, N // LANES, seq // LANES, LANES, LANES), out_dtype)
        out_spec = pl.BlockSpec((1, tn // LANES, tm // LANES, LANES, LANES),
                                lambda i, j, k: (i // spb, j, i % spb, 0, 0))
    elif mode.startswith("heads"):
        assert seq % tm == 0 and tn % LANES == 0
        spb = seq // tm
        out_shape = jax.ShapeDtypeStruct((M // seq, N // LANES, seq, LANES), out_dtype)
        out_spec = pl.BlockSpec((1, tn // LANES, tm, LANES), lambda i, j, k: (i // spb, j, i % spb, 0))
    else:
        out_shape = jax.ShapeDtypeStruct((M, N), out_dtype)
        out_spec = pl.BlockSpec((tm, tn), lambda i, j, k: (i, j))
    scratch = [pltpu.VMEM((tm, tn), F32)] if nk > 1 else []
    return pl.pallas_call(
        functools.partial(_mm_kernel, nk=nk, mode=mode, n_extra=len(extras), w_t=w_t),
        grid=grid,
        in_specs=in_specs,
        out_specs=out_spec,
        out_shape=out_shape,
        scratch_shapes=scratch,
        compiler_params=_cparams(("parallel", "parallel", "arbitrary")),
        name=name,
    )(x, w, *extras)


SUBLANES = 8


def _repack_kernel(tab_ref, w_ref, o_ref):
    o_ref[...] = w_ref[0].astype(o_ref.dtype)


def repack_rows(wt3, l, pieces, tr=512):
    _, NC, D = wt3.shape
    assert all(r0 % SUBLANES == 0 and nrow % tr == 0 for r0, nrow in pieces)
    starts = [r0 + r for r0, nrow in pieces for r in range(0, nrow, tr)]
    tab = jnp.asarray([s // SUBLANES for s in starts], jnp.int32)
    grid_spec = pltpu.PrefetchScalarGridSpec(
        num_scalar_prefetch=1, grid=(len(starts),),
        in_specs=[pl.BlockSpec((pl.Element(1), pl.Element(tr), pl.Element(D)),
                               lambda j, t: (l, t[j] * SUBLANES, 0))],
        out_specs=pl.BlockSpec((tr, D), lambda j, t: (j, 0)))
    return pl.pallas_call(
        _repack_kernel, grid_spec=grid_spec, out_shape=jax.ShapeDtypeStruct((len(starts) * tr, D), BF16),
        compiler_params=_cparams(("arbitrary",)), name="repack_w_in",
    )(tab, wt3)


def _compress_kernel(x_ref, w1_ref, w2_ref, pe_ref, c_ref, sa_ref, sb_ref, o_ref):
    half = x_ref.shape[-1]
    x = x_ref[0, 0]
    a = _dot(x, w1_ref[0, :half, :])
    b = _dot(x, w1_ref[0, half:, :])
    pet = _dot(pe_ref[0], w1_ref[0])
    n = a.shape[0]
    hid = a + pltpu.roll(b, n - 1, 0) + pet[0:1]
    out = _dot(_gelu(hid).astype(BF16), w2_ref[0])
    out = _rope_apply(out, c_ref[0, 0], sa_ref[0, 0], sb_ref[0, 0])
    o_ref[0, 0, 0] = out.astype(o_ref.dtype)


def nsa_compress(kvp, w1, w2, pe8, tabs):
    B, _, S, dh = kvp.shape
    G = NSA_GROUPS
    nch = S // CMP_STRIDE
    xv = kvp.reshape(B, kvp.shape[1], nch, CMP_STRIDE * dh)
    hidden = w1.shape[-1]
    tab_spec = pl.BlockSpec((1, 1, nch, LANES), lambda w, b, g: (w, b, 0, 0))
    return pl.pallas_call(
        _compress_kernel,
        grid=(2, B, G),
        in_specs=[
            pl.BlockSpec((1, 1, nch, CMP_STRIDE * dh), lambda w, b, g: (b, w * G + g, 0, 0)),
            pl.BlockSpec((1, CMP_LEN * dh, hidden), lambda w, b, g: (w, 0, 0)),
            pl.BlockSpec((1, hidden, dh), lambda w, b, g: (w, 0, 0)),
            pl.BlockSpec((1, 8, CMP_LEN * dh), lambda w, b, g: (w, 0, 0)),
            tab_spec, tab_spec, tab_spec,
        ],
        out_specs=pl.BlockSpec((1, 1, 1, nch, dh), lambda w, b, g: (w, b, g, 0, 0)),
        out_shape=jax.ShapeDtypeStruct((2, B, G, nch, dh), BF16),
        compiler_params=_cparams(("parallel", "parallel", "parallel")),
        name="nsa_compress",
    )(xv, w1, w2, pe8, *tabs)


def _dot_tn(a, b):
    return lax.dot_general(a, b, (((0,), (0,)), ((), ())), preferred_element_type=F32)


NSA_SPLIT = 2


def _nsa_kernel(q_ref, ks_ref, kw_ref, vs_ref, vw_ref, kc_ref, vc_ref, gt_ref, cmap_ref, e_ref, o_ref, *, ck):
    n = NSA_HPG
    ns = NSA_SPLIT
    hps = n // ns
    sq = hps * Q_BLOCK
    streams = range(ns)
    qb = pl.program_id(2)
    t0 = qb * Q_BLOCK
    c2 = (HEAD_DIM ** -0.5) * 1.4426950408889634
    nsel = ks_ref.shape[2] // SEL_BLOCK
    q_t = [jnp.concatenate([q_ref[0, s * hps + h].T for h in range(hps)], axis=1) for s in streams]
    tq = t0 + lax.broadcasted_iota(jnp.int32, (1, Q_BLOCK), 1)

    def heads(x):
        return jnp.concatenate([x] * hps, axis=1)

    def scores(k, ok_t):
        bias = heads(jnp.where(ok_t, 0.0, NEG_INF))
        return [_dot(k, q_t[s]) + bias for s in streams]

    def values(vt_ref, tile0, ntile):
        vt = jnp.concatenate([vt_ref[0, 0, tile0 + i] for i in range(ntile)], axis=1)
        return jnp.concatenate([vt, jnp.ones(vt.shape, vt.dtype)], axis=0)

    kc = kc_ref[0, 0, 0]
    vc = vc_ref[0, 0, 0]
    ncmp = kc.shape[0]
    crow = lax.broadcasted_iota(jnp.int32, (ncmp, Q_BLOCK), 0)
    sb = scores(kc, (crow < ncmp - 1) & (crow * CMP_STRIDE + (CMP_LEN - 1) <= tq))
    p = [jnp.exp2((sb[s] - jnp.max(sb[s], axis=0, keepdims=True)) * c2) for s in streams]
    anyv = heads((tq >= CMP_LEN - 1).astype(F32))
    p = [p[s] * (anyv / jnp.sum(p[s], axis=0, keepdims=True)) for s in streams]
    o_c = [_dot_tn(vc, p[s].astype(BF16)) for s in streams]

    psum = None
    for s in streams:
        for h in range(hps):
            ph = p[s][:, h * Q_BLOCK:(h + 1) * Q_BLOCK]
            psum = ph if psum is None else psum + ph
    hi = psum.astype(BF16)
    lo = (psum - hi.astype(F32)).astype(BF16)
    imp = _dot(cmap_ref[...], hi) + _dot(cmap_ref[...], lo)
    jr = lax.broadcasted_iota(jnp.int32, (LANES, Q_BLOCK), 0)
    cur = tq // SEL_BLOCK
    imp = jnp.where(jr == 0, 1e6, imp)
    imp = jnp.where(jr == cur, 1e6, imp)
    imp = jnp.where(jr == cur - 1, 1e6, imp)
    imp = jnp.where(jr > cur, -1e6, imp)

    nrow = min(((nsel + 7) // 8) * 8, LANES)
    imp = jnp.where(jr >= nsel, -3e6, imp)[0:nrow]
    jrr = jr[0:nrow]
    rank = jnp.zeros((nrow, Q_BLOCK), F32)
    for i in range(min(nsel, LANES)):
        row = imp[i:i + 1, :]
        rank = rank + jnp.where(row > imp, 1.0, jnp.where(row == imp, jnp.where(jrr > i, 1.0, 0.0), 0.0))
    sel = jnp.where(rank < float(SEL_TOP), 1.0, 0.0)
    if nrow < LANES:
        sel = jnp.concatenate([sel, jnp.zeros((LANES - nrow, Q_BLOCK), F32)], axis=0)
    sel = sel.astype(BF16)

    n_ch = (t0 + Q_BLOCK + ck - 1) // ck
    krow = lax.broadcasted_iota(jnp.int32, (ck, Q_BLOCK), 0)

    def body(kb, carry):
        m_i, acc = carry[:ns], carry[ns:]
        k0 = pl.multiple_of(kb * ck, ck)
        kblk = ks_ref[0, 0, pl.ds(k0, ck), :]
        v1 = values(vs_ref, kb * (ck // LANES), ck // LANES)
        em = _dot(e_ref[kb], sel)
        sb = scores(kblk, (k0 + krow <= tq) & (em > 0.5))
        m_new = [jnp.maximum(m_i[s], jnp.max(sb[s], axis=0, keepdims=True)) for s in streams]
        alpha = [jnp.exp2((m_i[s] - m_new[s]) * c2) for s in streams]
        pp = [jnp.exp2(((sb[s] - m_new[s]) * c2).astype(BF16)) for s in streams]
        acc = [alpha[s] * acc[s] + _dot(v1, pp[s]) for s in streams]
        return tuple(m_new) + tuple(acc)

    init = tuple(jnp.full((1, sq), NEG_INF, F32) for _ in streams) + tuple(
        jnp.zeros((2 * HEAD_DIM, sq), F32) for _ in streams)
    acc_f = lax.fori_loop(0, n_ch, body, init)[ns:]
    o_s = [a[0:HEAD_DIM] / a[HEAD_DIM:HEAD_DIM + 1] for a in acc_f]

    wlen = min(WINDOW + Q_BLOCK, kw_ref.shape[2])
    w0 = pl.multiple_of(jnp.maximum(t0 + Q_BLOCK - wlen, 0), Q_BLOCK)
    kblk = kw_ref[0, 0, pl.ds(w0, wlen), :]
    v1 = values(vw_ref, w0 // LANES, wlen // LANES)
    kpos = w0 + lax.broadcasted_iota(jnp.int32, (wlen, Q_BLOCK), 0)
    sb = scores(kblk, (kpos <= tq) & (tq - kpos < WINDOW))
    pw = [jnp.exp2(((sb[s] - jnp.max(sb[s], axis=0, keepdims=True)) * c2).astype(BF16)) for s in streams]
    acc_w = [_dot(v1, pw[s]) for s in streams]
    o_w = [a[0:HEAD_DIM] / a[HEAD_DIM:HEAD_DIM + 1] for a in acc_w]

    gt = gt_ref[0, 0].T
    for h in range(n):
        s = h // hps
        r = slice((h % hps) * Q_BLOCK, (h % hps + 1) * Q_BLOCK)
        o = (gt[3 * h:3 * h + 1] * o_c[s][:, r] + gt[3 * h + 1:3 * h + 2] * o_s[s][:, r]
             + gt[3 * h + 2:3 * h + 3] * o_w[s][:, r])
        o_ref[0, :, h * HEAD_DIM:(h + 1) * HEAD_DIM] = o.T.astype(o_ref.dtype)


def nsa_attention(qkr, vt, cmp, gates, ck=1024):
    B, _, S, dh = qkr.shape
    G, n = NSA_GROUPS, NSA_HPG
    ncmp = cmp.shape[3]
    nsel = S // SEL_BLOCK
    ck = min(ck, S)
    r_sel, r_cmp = SEL_BLOCK // CMP_STRIDE, CMP_LEN // CMP_STRIDE
    tgt = (r_sel * jnp.arange(nsel)[:, None, None] + jnp.arange(r_sel)[None, :, None]
           - jnp.arange(r_cmp)[None, None, :])
    cmap = (jnp.arange(ncmp - 1)[:, None, None, None] == tgt[None]).sum((2, 3)).astype(F32)
    cmap = jnp.pad(cmap, ((0, 1), (0, LANES - nsel))).astype(BF16).T
    e = (jnp.arange(S)[:, None] // SEL_BLOCK == jnp.arange(LANES)[None, :]).astype(BF16)
    e3 = e.reshape(S // ck, ck, LANES)

    def head(h0):
        return pl.BlockSpec((1, 1, S, dh), lambda b, g, i: (b, h0 + g, 0, 0))

    def head_t(h0):
        return pl.BlockSpec((1, 1, S // LANES, dh, LANES), lambda b, g, i: (b, h0 + g, 0, 0, 0))

    return pl.pallas_call(
        functools.partial(_nsa_kernel, ck=ck),
        grid=(B, G, S // Q_BLOCK),
        in_specs=[
            pl.BlockSpec((1, n, Q_BLOCK, dh), lambda b, g, i: (b, g, i, 0)),
            head(16), head(20), head_t(0), head_t(G),
            pl.BlockSpec((1, 1, 1, ncmp, dh), lambda b, g, i: (0, b, g, 0, 0)),
            pl.BlockSpec((1, 1, 1, ncmp, dh), lambda b, g, i: (1, b, g, 0, 0)),
            pl.BlockSpec((1, 1, Q_BLOCK, LANES), lambda b, g, i: (b, g, i, 0)),
            pl.BlockSpec((LANES, ncmp), lambda b, g, i: (0, 0)),
            pl.BlockSpec((S // ck, ck, LANES), lambda b, g, i: (0, 0, 0)),
        ],
        out_specs=pl.BlockSpec((1, Q_BLOCK, n * dh), lambda b, g, i: (b, i, g)),
        out_shape=jax.ShapeDtypeStruct((B, S, G * n * dh), BF16),
        compiler_params=_cparams(("parallel", "parallel", "arbitrary")),
        name="nsa_attention",
    )(qkr, qkr, qkr, vt, vt, cmp, cmp, gates, cmap, e3)


def _rwkv_feat_kernel(z_ref, zp_ref, mu_ref, o_ref):
    i = pl.program_id(1)
    z = z_ref[0]
    prev = zp_ref[0, 7:8, :] * (i > 0).astype(F32)
    row = lax.broadcasted_iota(jnp.int32, z.shape, 0)
    sh = jnp.where(row == 0, prev, pltpu.roll(z, 1, 0))
    zs = z + (sh - z) * mu_ref[...]
    o_ref[0, :, 0:128] = jnp.tanh(zs[:, 0:128])
    o_ref[0, :, 128:256] = zs[:, 128:256]
    o_ref[0, :, 256:] = jax.nn.sigmoid(zs[:, 256:])


def rwkv_features(z_lora, mu_lora, tm=512):
    B, S, W = z_lora.shape
    tm = min(tm, S)
    return pl.pallas_call(
        _rwkv_feat_kernel,
        grid=(B, S // tm),
        in_specs=[pl.BlockSpec((1, tm, W), lambda b, i: (b, i, 0)),
                  pl.BlockSpec((1, 8, W), lambda b, i: (b, jnp.maximum(i * (tm // 8) - 1, 0), 0)),
                  pl.BlockSpec((1, W), lambda b, i: (0, 0))],
        out_specs=pl.BlockSpec((1, tm, W), lambda b, i: (b, i, 0)),
        out_shape=jax.ShapeDtypeStruct((B, S, W), F32),
        compiler_params=_cparams(("parallel", "arbitrary")),
        name="rwkv_features",
    )(z_lora, z_lora, mu_lora)


def _rwkv_kernel(r_ref, k_ref, v_ref, f_ref, par_ref, w2_ref, a2_ref, g2_ref, o_ref, st_ref, prev_ref):
    C = RWKV_CHUNK
    W = r_ref.shape[-1]
    npair = W // LANES
    c = pl.program_id(2)

    @pl.when(c == 0)
    def _():
        st_ref[...] = jnp.zeros_like(st_ref)
        prev_ref[...] = jnp.zeros_like(prev_ref)

    par = par_ref[...]
    mu_r, mu_k, mu_v = par[0:1], par[1:2], par[2:3]
    w0, a0, k_k, k_a, r_k, ln_w, ln_b = par[3:4], par[4:5], par[5:6], par[6:7], par[7:8], par[8:9], par[9:10]

    row = lax.broadcasted_iota(jnp.int32, (C, C), 0)
    col = lax.broadcasted_iota(jnp.int32, (C, C), 1)
    trow = lax.broadcasted_iota(jnp.int32, (C, W), 0)
    lane = lax.broadcasted_iota(jnp.int32, (C, LANES), 1)
    lr = lax.broadcasted_iota(jnp.int32, (LANES, LANES), 0)
    lc = lax.broadcasted_iota(jnp.int32, (LANES, LANES), 1)

    def shifted(z_ref, idx, mu):
        z = z_ref[0]
        sh = jnp.where(trow == 0, prev_ref[idx:idx + 1, :], pltpu.roll(z, 1, 0))
        prev_ref[idx:idx + 1, :] = z[C - 1:C, :]
        return z + (sh - z) * mu

    r = shifted(r_ref, 0, mu_r)
    k = shifted(k_ref, 1, mu_k)
    v = shifted(v_ref, 2, mu_v)
    f = f_ref[0]
    w_log = -_softplus(-(w0 + _dot3(_split(f[:, 0:128]), _split(w2_ref[...])))) - 0.5
    logd = -jnp.exp(w_log)
    a = jax.nn.sigmoid(a0 + _dot3(_split(f[:, 128:256]), _split(a2_ref[...])))
    g = _dot3(_split(f[:, 256:]), _split(g2_ref[...]))

    bd = _exact(jnp.where((lr // RWKV_HEAD) == (lc // RWKV_HEAD), 1.0, 0.0))

    def head_sum(x, passes=2):
        sp = _split if passes == 2 else (lambda z: (z.astype(BF16), None))
        return jnp.concatenate([_dot3(sp(x[:, p * LANES:(p + 1) * LANES]), bd) for p in range(npair)], axis=1)

    kk = k * k_k
    kk = kk / jnp.maximum(jnp.sqrt(head_sum(kk * kk)), 1e-12)
    k2 = k * (1.0 + (a - 1.0) * k_a)

    tri_incl = _exact(jnp.where(col <= row, 1.0, 0.0))
    cum = _dot3(tri_incl, _split(logd))
    cum_end = cum[C - 1:C, :]
    e_neg = jnp.exp(-cum)
    kka = kk * a
    a_t = -kk * jnp.exp(cum - logd)
    b_t = kka * e_neg
    k_t = k2 * e_neg
    r_t = r * jnp.exp(cum)
    e_end = jnp.exp(cum_end - cum)
    b_e = kka * e_end
    k_e = k2 * e_end
    p_end = jnp.exp(cum_end)

    strict = col < row
    incl = col <= row
    eye = jnp.where(col == row, 1.0, 0.0)
    n_sq = C.bit_length() - 2
    hpp = LANES // RWKV_HEAD
    heads = [(p, h) for p in range(npair) for h in range(hpp)]
    hd = []
    for p, h in heads:
        sl = slice(p * LANES, (p + 1) * LANES)
        mh = (lane // RWKV_HEAD) == h
        s_am = _split(jnp.where(mh, a_t[:, sl], 0.0))
        s_rm = _split(jnp.where(mh, r_t[:, sl], 0.0))
        hd.append(dict(
            sl=sl, s_am=s_am, s_rm=s_rm,
            s_vm=_split(jnp.where(mh, v[:, sl], 0.0)),
            s_bem=_split(jnp.where(mh, b_e[:, sl], 0.0)),
            s_kem=_split(jnp.where(mh, k_e[:, sl], 0.0)),
            rhs=_split(jnp.concatenate([b_t[:, sl], k_t[:, sl]], axis=0)) if h == 0 else hd[-1]["rhs"]))
    for d in hd:
        amat = _dot3(d["s_am"], d["rhs"], "nt")
        rmat = _dot_nt(d["s_rm"][0], d["rhs"][0])
        l_ab = jnp.where(strict, amat[:, 0:C], 0.0)
        d["l_ak"] = _split(jnp.where(strict, amat[:, C:2 * C], 0.0))
        d["l_rb"] = jnp.where(incl, rmat[:, 0:C], 0.0).astype(BF16)
        d["l_rk"] = jnp.where(incl, rmat[:, C:2 * C], 0.0).astype(BF16)
        d["x"] = eye + l_ab
        d["s_pw"] = _split(l_ab)
    for i in range(n_sq):
        for d in hd:
            d["s_pw"] = _split(_dot3(d["s_pw"], d["s_pw"])) if i == 0 else (_dot(d["s_pw"][0], d["s_pw"][0]).astype(BF16), None)
        for d in hd:
            d["x"] = d["x"] + (_dot3(_split(d["x"]), d["s_pw"]) if i == 0 else _dot(d["x"].astype(BF16), d["s_pw"][0]))
    cat = lambda parts, axis: tuple(None if parts[0][i] is None else jnp.concatenate([p[i] for p in parts], axis=axis)
                                    for i in range(2))
    for i, d in enumerate(hd):
        d["st"] = st_ref[i]
        d["s_st"] = _split(d["st"])
        d["t"] = _split(_dot3(cat([d["s_am"], d["l_ak"]], 1), cat([d["s_st"], d["s_vm"]], 0)))
    for d in hd:
        d["s_u"] = _split(_dot3(_split(d["x"]), d["t"]))
    for i, d in enumerate(hd):
        if i % hpp == 0:
            decay = jnp.broadcast_to(p_end[:, d["sl"]], (LANES, LANES)).T
        st_ref[i] = d["st"] * decay + _dot3(cat([d["s_bem"], d["s_kem"]], 0), cat([d["s_u"], d["s_vm"]], 0), "tn")
    ys = []
    for d in hd:
        ys.append(_dot(jnp.concatenate([d["s_rm"][0], d["l_rb"], d["l_rk"]], axis=1),
                       jnp.concatenate([d["s_st"][0], d["s_u"][0], d["s_vm"][0]], axis=0)))
    y = jnp.concatenate([sum(ys[hpp * p + 1:hpp * (p + 1)], ys[hpp * p]) for p in range(npair)], axis=1)

    inv_n = 1.0 / RWKV_HEAD
    mean = head_sum(y) * inv_n
    d = y - mean
    var = head_sum(d * d) * inv_n
    yn = d * lax.rsqrt(var + GN_EPS) * ln_w + ln_b
    bonus = head_sum(r * k2 * r_k, passes=1) * v
    o_ref[0] = ((yn + bonus) * g).astype(o_ref.dtype)


def rwkv_scan(z_rkv, feat, par, w2p, a2p, g2, pairs_per_step=8):
    B, S, R3 = z_rkv.shape
    R = R3 // 3
    W = pairs_per_step * LANES
    nblk = R // W
    C = RWKV_CHUNK
    assert S % C == 0 and R % W == 0

    def col(off):
        return pl.BlockSpec((1, C, W), lambda b, p, c: (b, c, off + p))

    return pl.pallas_call(
        _rwkv_kernel,
        grid=(B, nblk, S // C),
        in_specs=[
            col(0), col(nblk), col(2 * nblk),
            pl.BlockSpec((1, C, feat.shape[-1]), lambda b, p, c: (b, c, 0)),
            pl.BlockSpec((16, W), lambda b, p, c: (0, p)),
            pl.BlockSpec((LANES, W), lambda b, p, c: (0, p)),
            pl.BlockSpec((LANES, W), lambda b, p, c: (0, p)),
            pl.BlockSpec((GATE_LORA, W), lambda b, p, c: (0, p)),
        ],
        out_specs=pl.BlockSpec((1, C, W), lambda b, p, c: (b, c, p)),
        out_shape=jax.ShapeDtypeStruct((B, S, R), BF16),
        scratch_shapes=[pltpu.VMEM((W // RWKV_HEAD, LANES, LANES), F32), pltpu.VMEM((8, W), F32)],
        compiler_params=_cparams(("parallel", "parallel", "arbitrary")),
        name="rwkv_scan",
    )(z_rkv, z_rkv, z_rkv, feat, par, w2p, a2p, g2)


def _merge_kernel(on_ref, or_ref, wn_ref, wr_ref, ga_ref, gb_ref, o_ref):
    pa = _dot(on_ref[...], wn_ref[...].astype(on_ref.dtype))
    pb = _dot(or_ref[...], wr_ref[...].astype(or_ref.dtype))
    o_ref[...] = (ga_ref[...].astype(F32) * pa + gb_ref[...].astype(F32) * pb).astype(o_ref.dtype)


def merge_mixers(o_nsa, o_rwkv, wn, wr, gates, tm=1024, tn=512):
    M, Kn = o_nsa.shape
    Kr = o_rwkv.shape[1]
    D = wn.shape[1]
    tm, tn = min(tm, M), min(tn, D)
    nb = D // tn
    return pl.pallas_call(
        _merge_kernel,
        grid=(M // tm, nb),
        in_specs=[
            pl.BlockSpec((tm, Kn), lambda i, j: (i, 0)),
            pl.BlockSpec((tm, Kr), lambda i, j: (i, 0)),
            pl.BlockSpec((Kn, tn), lambda i, j: (0, j)),
            pl.BlockSpec((Kr, tn), lambda i, j: (0, j)),
            pl.BlockSpec((tm, tn), lambda i, j: (i, j)),
            pl.BlockSpec((tm, tn), lambda i, j: (i, nb + j)),
        ],
        out_specs=pl.BlockSpec((tm, tn), lambda i, j: (i, j)),
        out_shape=jax.ShapeDtypeStruct((M, D), BF16),
        compiler_params=_cparams(("parallel", "parallel")),
        name="merge_mixers",
    )(o_nsa, o_rwkv, wn, wr, gates, gates)


def _xattn_kernel(q_ref, kv_ref, wo_ref, x_ref, o_ref):
    nh = XATTN_HEADS
    scale = HEAD_DIM ** -0.5
    outs = []
    for h in range(nh):
        q = q_ref[:, h * HEAD_DIM:(h + 1) * HEAD_DIM]
        k = kv_ref[0, :, h * HEAD_DIM:(h + 1) * HEAD_DIM]
        v = kv_ref[0, :, (nh + h) * HEAD_DIM:(nh + h + 1) * HEAD_DIM]
        s = _dot_nt(q, k) * scale
        m = jnp.max(s, axis=-1, keepdims=True)
        p = jnp.exp(s - m)
        p = p / jnp.sum(p, axis=-1, keepdims=True)
        outs.append(_dot(p.astype(BF16), v).astype(BF16))
    o = jnp.concatenate(outs, axis=-1)
    o_ref[...] = x_ref[...] + _dot(o, wo_ref[...])


def cross_attention(q, kv, wo, x, seq, tm=256):
    T, D = x.shape
    tm = min(tm, seq)
    spb = seq // tm
    Mm = kv.shape[1]
    return pl.pallas_call(
        _xattn_kernel,
        grid=(T // tm,),
        in_specs=[
            pl.BlockSpec((tm, q.shape[1]), lambda i: (i, 0)),
            pl.BlockSpec((1, Mm, kv.shape[2]), lambda i: (i // spb, 0, 0)),
            pl.BlockSpec(wo.shape, lambda i: (0, 0)),
            pl.BlockSpec((tm, D), lambda i: (i, 0)),
        ],
        out_specs=pl.BlockSpec((tm, D), lambda i: (i, 0)),
        out_shape=jax.ShapeDtypeStruct((T, D), F32),
        compiler_params=_cparams(("parallel",)),
        name="cross_attention",
    )(q, kv, wo, x)


def _top16(x, iota=None, n=None):
    if iota is None:
        n = x.shape[0]
        iota = lax.broadcasted_iota(jnp.int32, x.shape, 0).astype(F32)
    cur = x
    rank = jnp.full(x.shape, float(PEER_TOPK), F32)
    vals = []
    for r in range(PEER_TOPK):
        m = jnp.max(cur, axis=0, keepdims=True)
        idx = jnp.min(jnp.where(cur == m, iota, float(n)), axis=0, keepdims=True)
        hit = iota == idx
        rank = jnp.where(hit, float(r), rank)
        cur = jnp.where(hit, -jnp.inf, cur)
        vals.append(m)
    return jnp.concatenate(vals, axis=0), rank


def _peer_route_kernel(q_ref, keys_ref, row_ref, pl_ref):
    K = PEER_TOPK
    q = q_ref[...]
    half = q.shape[1] // 2
    s1 = _dotf_nt(keys_ref[0, 0], q[:, :half])
    s2 = _dotf_nt(keys_ref[0, 1], q[:, half:])
    v1, rank1 = _top16(s1)
    v2, rank2 = _top16(s2)
    tmq = q.shape[0]
    r8 = lax.broadcasted_iota(jnp.int32, (8, tmq), 0)
    ninf = -jnp.inf
    lo8 = v2[0:8]
    tiles = [
        (v1[0:1] + lo8, r8),
        (v1[0:1] + v2[8:16], 8 + r8),
        (v1[1:2] + lo8, 16 + r8),
        (jnp.where(r8 < 5, v1[2:3] + lo8, ninf), 32 + r8),
        (jnp.where(r8 < 7, jnp.where(r8 < 4, v1[3:4], v1[4:5]) + jnp.where(r8 < 4, lo8, pltpu.roll(lo8, 4, 0)), ninf),
         jnp.where(r8 < 4, 48 + r8, 60 + r8)),
        (jnp.where(r8 < 6, jnp.where(r8 < 2, v1[5:6], jnp.where(r8 < 4, v1[6:7], v1[7:8]))
                   + jnp.where(r8 % 2 == 0, v2[0:1], v2[1:2]), ninf),
         80 + (r8 // 2) * 16 + r8 % 2),
        (v1[8:16] + v2[0:1], (8 + r8) * 16),
    ]
    cand = jnp.concatenate([t for t, _ in tiles], axis=0)
    cidx = jnp.concatenate([i for _, i in tiles], axis=0).astype(F32)
    top_s, crank = _top16(cand, cidx, K * K)
    chosen = jnp.where(crank < float(K), 1.0, 0.0)
    z = jnp.sum(jnp.exp(top_s - top_s[0:1]), axis=0, keepdims=True)
    ch = [chosen[8 * t:8 * (t + 1)] for t in range(len(tiles))]
    colsum = lambda x: jnp.sum(x, axis=0, keepdims=True)
    upper = (r8 < 4).astype(F32)
    j_rows = [colsum(ch[0]) + colsum(ch[1]), colsum(ch[2]), colsum(ch[3]),
              colsum(ch[4] * upper), colsum(ch[4] * (1.0 - upper))]
    j_rows += [ch[5][2 * m:2 * m + 1] + ch[5][2 * m + 1:2 * m + 2] for m in range(3)]
    j_rows += [ch[6][m:m + 1] for m in range(8)]
    jn = jnp.zeros_like(s1)
    for i in range(K):
        jn = jnp.where(rank1 == float(i), j_rows[i], jn)
    row_ref[0, 0] = jnp.exp(s1 - v1[0:1]) / z
    row_ref[0, 1] = jn
    pl_ref[0, 0] = jnp.exp(s2 - v2[0:1]).astype(pl_ref.dtype)
    pl_ref[0, 1] = rank2.astype(pl_ref.dtype)


def peer_route(q, keys, tm=512):
    T = q.shape[0]
    H, _, nkeys, hd = keys.shape
    tm = min(tm, T)
    spec = pl.BlockSpec((1, 2, nkeys, tm), lambda i, h: (h, 0, 0, i))
    return pl.pallas_call(
        _peer_route_kernel,
        grid=(T // tm, H),
        in_specs=[pl.BlockSpec((tm, 2 * hd), lambda i, h: (i, h)),
                  pl.BlockSpec((1, 2, nkeys, hd), lambda i, h: (h, 0, 0, 0))],
        out_specs=[spec, spec],
        out_shape=[jax.ShapeDtypeStruct((H, 2, nkeys, T), F32), jax.ShapeDtypeStruct((H, 2, nkeys, T), BF16)],
        compiler_params=_cparams(("parallel", "parallel")),
        name="peer_route",
    )(q, keys)


def _peer_act_kernel(h_ref, u_ref, row_ref, pl_ref, o_ref):
    nk = PEER_KEYS
    ec = u_ref.shape[0]
    j = pl.program_id(1)
    hu = _dot(u_ref[...], h_ref[...])
    zero = jnp.zeros((), o_ref.dtype)
    for al in range(ec // nk):
        a = j * (ec // nk) + al
        w = None
        for hd in range(row_ref.shape[0]):
            e1 = row_ref[hd, 0, pl.ds(a, 1), :].astype(o_ref.dtype)
            jn = row_ref[hd, 1, pl.ds(a, 1), :].astype(o_ref.dtype)
            g = jnp.where(pl_ref[hd, 1] < jn, e1 * pl_ref[hd, 0], zero)
            w = g if w is None else w + g
        act = _gelu(hu[al * nk:(al + 1) * nk, :].astype(o_ref.dtype)) * w
        o_ref[:, al * nk:(al + 1) * nk] = act.T


def peer_activations(h_t, u_tab, rows, planes, tm=512, ec=1024):
    D, T = h_t.shape
    E = u_tab.shape[0]
    tm, ec = min(tm, T), min(ec, E)
    H, _, nkeys, _ = rows.shape
    rspec = pl.BlockSpec((H, 2, nkeys, tm), lambda i, j: (0, 0, 0, i))
    return pl.pallas_call(
        _peer_act_kernel,
        grid=(T // tm, E // ec),
        in_specs=[pl.BlockSpec((D, tm), lambda i, j: (0, i)),
                  pl.BlockSpec((ec, D), lambda i, j: (j, 0)),
                  rspec, rspec],
        out_specs=pl.BlockSpec((tm, ec), lambda i, j: (i, j)),
        out_shape=jax.ShapeDtypeStruct((T, E), BF16),
        compiler_params=_cparams(("parallel", "arbitrary")),
        name="peer_activations",
    )(h_t, u_tab, rows, planes)


def kernel(x, mem, positions, norm_mix, w_in, nsa_cmp_pe, nsa_cmp_w1, nsa_cmp_w2, rwkv_mu, rwkv_w0, rwkv_w2, rwkv_a0, rwkv_a2, rwkv_g2, rwkv_k_k, rwkv_k_a, rwkv_r_k, rwkv_ln_w, rwkv_ln_b, w_proj_nsa, w_proj_rwkv, w_out, norm_xattn, norm_mem, xattn_wq, xattn_wkv, xattn_wo, norm_ffn, peer_wq, peer_keys, peer_u, peer_v, norm_final):
    B, S, D = x.shape
    T = B * S
    depth = w_in.shape[0]
    G, n, dh = NSA_GROUPS, NSA_HPG, HEAD_DIM
    nq_cols = G * n * dh
    nkv_cols = 6 * G * dh
    ngate = 3 * G * n
    R = rwkv_w0.shape[1]
    c_q, c_kv, c_g = nq_cols, nq_cols + nkv_cols, nq_cols + nkv_cols + ngate
    c_r = c_g + 3 * R + DECAY_LORA + AAA_LORA + GATE_LORA
    gd = G * dh

    pos_flat = positions.reshape(T)
    rope_tabs = _rope_tables(pos_flat)
    cend = jnp.arange(S // CMP_STRIDE) * CMP_STRIDE + (CMP_LEN - 1)
    ctab = _rope_tables(positions[:, jnp.minimum(cend, S - 1)])
    ident = _rope_tables(jnp.zeros_like(positions[:, :S // CMP_STRIDE]))
    cmp_tabs = [jnp.stack([a, b]) for a, b in zip(ctab, ident)]

    xf = x.reshape(T, D)
    for l in range(depth):
        w_t = jnp.swapaxes(w_in, 1, 2)
        pieces = [(0, c_q), (c_q + 2 * gd, gd), (c_q + 4 * gd, gd),
                  (c_q, 2 * gd), (c_q + 3 * gd, gd), (c_q + 5 * gd, gd),
                  (c_g, 3 * R), (c_r, 2 * D)]
        wp = repack_rows(w_t, l, pieces)
        g_rope = (0, c_q + 2 * gd)
        g_plain = (g_rope[1], 4 * gd)
        g_rkv = (g_plain[0] + g_plain[1], 3 * R)
        g_merge = (g_rkv[0] + g_rkv[1], 2 * D)
        w_gate = jnp.pad(w_t[l, c_kv:c_g].reshape(G, 3 * n, D), ((0, 0), (0, LANES - 3 * n), (0, 0)))
        w_gate = w_gate.reshape(G * LANES, D)
        wl = w_t[l, c_g + 3 * R:c_r]
        padl = lambda a, n_: jnp.pad(a, ((0, n_ - a.shape[0]), (0, 0)))
        w_lora = jnp.concatenate([padl(wl[:DECAY_LORA], LANES), padl(wl[DECAY_LORA:DECAY_LORA + AAA_LORA], LANES),
                                  wl[DECAY_LORA + AAA_LORA:]], axis=0)

        h = rmsnorm(xf, norm_mix[l], BF16)
        proj = functools.partial(matmul, h, w_t=True)
        qkr = proj(wp, wcols=g_rope, mode="heads_rope", extras=rope_tabs, out_dtype=BF16, seq=S, name="proj_rope")
        kvc = proj(wp, wcols=(g_plain[0], 2 * gd), mode="heads", out_dtype=BF16, seq=S, name="proj_kvc")
        vt = proj(wp, wcols=(g_plain[0] + 2 * gd, 2 * gd), mode="heads_t", out_dtype=BF16, seq=S, name="proj_vt")
        gates_nsa = proj(w_gate, mode="heads_sigmoid", out_dtype=F32, seq=S, name="proj_gate")
        z_rkv = proj(wp, wcols=g_rkv, out_dtype=F32, name="proj_rkv")
        z_lora = proj(w_lora, out_dtype=F32, name="proj_lora")
        gates_merge = proj(wp, wcols=g_merge, mode="sigmoid", out_dtype=BF16, name="proj_merge")

        pe8 = jnp.broadcast_to(nsa_cmp_pe[l].reshape(2, 1, CMP_LEN * dh), (2, 8, CMP_LEN * dh)).astype(BF16)
        cmp = nsa_compress(kvc, nsa_cmp_w1[l].astype(BF16), nsa_cmp_w2[l].astype(BF16), pe8, cmp_tabs)
        o_nsa = nsa_attention(qkr, vt, cmp, gates_nsa).reshape(T, nq_cols)

        mu = rwkv_mu[l]
        mul = mu[3 * R:]
        pad1 = lambda a, n_: jnp.pad(a, (0, n_ - a.shape[0]))
        mu_lora = jnp.concatenate([pad1(mul[:DECAY_LORA], LANES), pad1(mul[DECAY_LORA:DECAY_LORA + AAA_LORA], LANES),
                                   mul[DECAY_LORA + AAA_LORA:]]).reshape(1, -1)
        feat = rwkv_features(z_lora.reshape(B, S, -1), mu_lora)
        par = jnp.stack([mu[:R], mu[R:2 * R], mu[2 * R:3 * R], rwkv_w0[l], rwkv_a0[l], rwkv_k_k[l], rwkv_k_a[l],
                         rwkv_r_k[l].reshape(R), rwkv_ln_w[l], rwkv_ln_b[l]])
        par = jnp.pad(par, ((0, 16 - par.shape[0]), (0, 0)))
        w2p = jnp.pad(rwkv_w2[l], ((0, LANES - DECAY_LORA), (0, 0)))
        a2p = jnp.pad(rwkv_a2[l], ((0, LANES - AAA_LORA), (0, 0)))
        o_rwkv = rwkv_scan(z_rkv.reshape(B, S, 3 * R), feat, par, w2p, a2p, rwkv_g2[l]).reshape(T, R)

        merged = merge_mixers(o_nsa, o_rwkv, w_proj_nsa[l], w_proj_rwkv[l], gates_merge)
        xf = matmul(merged, w_out[l], mode="residual", extras=(xf,), out_dtype=F32, name="out_proj")

        h = rmsnorm(xf, norm_xattn[l], BF16)
        m = rmsnorm(mem.reshape(-1, D), norm_mem[l], BF16)
        q = matmul(h, xattn_wq[l], out_dtype=BF16, name="xattn_q")
        kvm = matmul(m, xattn_wkv[l], out_dtype=BF16, name="xattn_kv").reshape(B, mem.shape[1], -1)
        xf = cross_attention(q, kvm, xattn_wo[l].astype(BF16), xf, S)

        h, h_t = rmsnorm(xf, norm_ffn[l], BF16, with_transpose=True)
        pq = matmul(h, peer_wq[l], out_dtype=F32, name="peer_q")
        rows, planes = peer_route(pq, peer_keys[l])
        act = peer_activations(h_t, peer_u[l].astype(BF16), rows, planes)
        xf = matmul(act, peer_v[l], mode="residual", extras=(xf,), out_dtype=F32,
                    tm=1024, tn=1024, tk=2048, name="peer_out")

    return rmsnorm(xf, norm_final, F32).reshape(B, S, D)
```

```python
import functools

import jax
import jax.numpy as jnp
from jax import lax
from jax.experimental import pallas as pl
from jax.experimental.pallas import tpu as pltpu

F32 = jnp.float32
BF16 = jnp.bfloat16
HI = lax.Precision.HIGHEST

LANES = 128
HEAD_DIM = 128
ROPE_DIM = HEAD_DIM // 4
ROPE_THETA = 500000.0
RMS_EPS = 1e-6
NEG_INF = -1e30

NSA_GROUPS = 4
NSA_HPG = 4
CMP_LEN = 32
CMP_STRIDE = 16
SEL_BLOCK = 64
SEL_TOP = 16
WINDOW = 512
Q_BLOCK = 128

RWKV_HEAD = 64
DECAY_LORA = 96
AAA_LORA = 96
GATE_LORA = 256
GN_EPS = 64e-5
RWKV_CHUNK = 128

XATTN_HEADS = 4
PEER_HEADS = 8
PEER_KEYS = 128
PEER_TOPK = 16

VMEM_LIMIT = 56 * 1024 * 1024


def _cparams(sem, flags=None):
    return pltpu.CompilerParams(dimension_semantics=sem, vmem_limit_bytes=VMEM_LIMIT, flags=flags)


def _gelu(x):
    c = 0.7978845608028654
    return 0.5 * x * (1.0 + jnp.tanh(x * (c + (0.044715 * c) * (x * x))))


def _softplus(x):
    return jnp.maximum(x, 0.0) + jnp.log(1.0 + jnp.exp(-jnp.abs(x)))


def _dot(a, b):
    return jnp.dot(a, b, preferred_element_type=F32)


def _dot_nt(a, b):
    return lax.dot_general(a, b, (((1,), (1,)), ((), ())), preferred_element_type=F32)


def _dotf_nt(a, b):
    return lax.dot_general(a, b, (((1,), (1,)), ((), ())), preferred_element_type=F32, precision=HI)


def _split(x):
    hi = x.astype(BF16)
    return hi, (x - hi.astype(F32)).astype(BF16)


def _exact(x):
    return x.astype(BF16), None


_DN = {"nn": (((1,), (0,)), ((), ())), "nt": (((1,), (1,)), ((), ())), "tn": (((0,), (0,)), ((), ()))}


def _dot3(a, b, form="nn"):
    d = lambda x, y: lax.dot_general(x, y, _DN[form], preferred_element_type=F32)
    out = d(a[0], b[0])
    if a[1] is not None:
        out = out + d(a[1], b[0])
    if b[1] is not None:
        out = out + d(a[0], b[1])
    return out


def _rope_apply(a, c, sa, sb):
    return a * c + pltpu.roll(a, LANES - ROPE_DIM // 2, 1) * sa + pltpu.roll(a, ROPE_DIM // 2, 1) * sb


def _rope_tables(pos):
    half = ROPE_DIM // 2
    inv = ROPE_THETA ** (-jnp.arange(half, dtype=F32) / half)
    ang = pos.astype(F32)[..., None] * inv
    cos, sin = jnp.cos(ang), jnp.sin(ang)
    shp = pos.shape + (HEAD_DIM - ROPE_DIM,)
    c = jnp.concatenate([cos, cos, jnp.ones(shp, F32)], axis=-1)
    sa = jnp.concatenate([-sin, jnp.zeros(pos.shape + (HEAD_DIM - half,), F32)], axis=-1)
    sb = jnp.concatenate([jnp.zeros(pos.shape + (half,), F32), sin, jnp.zeros(shp, F32)], axis=-1)
    return c, sa, sb


def _rmsnorm_kernel(x_ref, g_ref, o_ref, *ot_ref):
    x = x_ref[...].astype(F32)
    ms = jnp.mean(x * x, axis=-1, keepdims=True)
    y = x * lax.rsqrt(ms + RMS_EPS) * g_ref[...]
    o_ref[...] = y.astype(o_ref.dtype)
    if ot_ref:
        ot_ref[0][...] = y.T.astype(ot_ref[0].dtype)


def rmsnorm(x2d, g, out_dtype, tm=256, with_transpose=False):
    M, D = x2d.shape
    tm = min(tm, M)
    out_specs = [pl.BlockSpec((tm, D), lambda i: (i, 0))]
    out_shape = [jax.ShapeDtypeStruct((M, D), out_dtype)]
    if with_transpose:
        out_specs.append(pl.BlockSpec((D, tm), lambda i: (0, i)))
        out_shape.append(jax.ShapeDtypeStruct((D, M), out_dtype))
    out = pl.pallas_call(
        _rmsnorm_kernel,
        grid=(M // tm,),
        in_specs=[pl.BlockSpec((tm, D), lambda i: (i, 0)), pl.BlockSpec((1, D), lambda i: (0, 0))],
        out_specs=out_specs,
        out_shape=out_shape,
        compiler_params=_cparams(("parallel",)),
        name="rmsnorm",
    )(x2d, g.reshape(1, D).astype(F32))
    return out if with_transpose else out[0]


def _mm_kernel(*refs, nk, mode, n_extra, w_t):
    x_ref, w_ref = refs[0], refs[1]
    extras = refs[2:2 + n_extra]
    o_ref = refs[2 + n_extra]

    def product():
        w = w_ref[...]
        if w.dtype != x_ref.dtype:
            w = w.astype(x_ref.dtype)
        return _dot_nt(x_ref[...], w) if w_t else _dot(x_ref[...], w)

    def epilogue(acc):
        if mode == "plain":
            o_ref[...] = acc.astype(o_ref.dtype)
        elif mode == "sigmoid":
            o_ref[...] = jax.nn.sigmoid(acc).astype(o_ref.dtype)
        elif mode == "residual":
            o_ref[...] = (extras[0][...] + acc).astype(o_ref.dtype)
        elif mode in ("heads", "heads_sigmoid", "heads_rope"):
            nh = acc.shape[1] // LANES
            for j in range(nh):
                a = acc[:, j * LANES:(j + 1) * LANES]
                if mode == "heads_sigmoid":
                    a = jax.nn.sigmoid(a)
                elif mode == "heads_rope":
                    a = _rope_apply(a, extras[0][...], extras[1][...], extras[2][...])
                o_ref[0, j] = a.astype(o_ref.dtype)
        elif mode == "heads_t":
            for j in range(acc.shape[1] // LANES):
                for c in range(acc.shape[0] // LANES):
                    o_ref[0, j, c] = acc[c * LANES:(c + 1) * LANES, j * LANES:(j + 1) * LANES].T.astype(o_ref.dtype)
        else:
            raise ValueError(mode)

    if nk == 1:
        epilogue(product())
    else:
        acc_ref = refs[-1]
        k = pl.program_id(2)

        @pl.when(k == 0)
        def _():
            acc_ref[...] = jnp.zeros_like(acc_ref)

        acc_ref[...] += product()

        @pl.when(k == nk - 1)
        def _():
            epilogue(acc_ref[...])


def matmul(x, w, *, mode="plain", extras=(), out_dtype=F32, tm=1024, tn=512, tk=None, seq=None, name="mm",
           w_t=False, wcols=None):
    M, K = x.shape
    wcol0, N = (0, w.shape[0 if w_t else 1]) if wcols is None else wcols
    tm, tn = min(tm, M), min(tn, N)
    assert wcol0 % tn == 0
    wblk0 = wcol0 // tn
    if mode.startswith("heads"):
        tm = min(tm, seq)
    tk = K if tk is None else min(tk, K)
    nk = K // tk
    assert M % tm == 0 and N % tn == 0 and K % tk == 0
    grid = (M // tm, N // tn, nk)
    if w_t:
        w_spec = pl.BlockSpec((tn, tk), lambda i, j, k: (wblk0 + j, k))
    else:
        w_spec = pl.BlockSpec((tk, tn), lambda i, j, k: (k, wblk0 + j))
    in_specs = [pl.BlockSpec((tm, tk), lambda i, j, k: (i, k)), w_spec]
    if mode == "residual":
        in_specs.append(pl.BlockSpec((tm, tn), lambda i, j, k: (i, j)))
    elif mode == "heads_rope":
        in_specs += [pl.BlockSpec((tm, LANES), lambda i, j, k: (i, 0))] * 3
    if mode == "heads_t":
        assert seq % tm == 0 and tn % LANES == 0 and tm % LANES == 0
        spb = seq // tm
        out_shape = jax.ShapeDtypeStruct((M // seq, N // LANES, seq // LANES, LANES, LANES), out_dtype)
        out_spec = pl.BlockSpec((1, tn // LANES, tm // LANES, LANES, LANES),
                                lambda i, j, k: (i // spb, j, i % spb, 0, 0))
    elif mode.startswith("heads"):
        assert seq % tm == 0 and tn % LANES == 0
        spb = seq // tm
        out_shape = jax.ShapeDtypeStruct((M // seq, N // LANES, seq, LANES), out_dtype)
        out_spec = pl.BlockSpec((1, tn // LANES, tm, LANES), lambda i, j, k: (i // spb, j, i % spb, 0))
    else:
        out_shape = jax.ShapeDtypeStruct((M, N), out_dtype)
        out_spec = pl.BlockSpec((tm, tn), lambda i, j, k: (i, j))
    scratch = [pltpu.VMEM((tm, tn), F32)] if nk > 1 else []
    return pl.pallas_call(
        functools.partial(_mm_kernel, nk=nk, mode=mode, n_extra=len(extras), w_t=w_t),
        grid=grid,
        in_specs=in_specs,
        out_specs=out_spec,
        out_shape=out_shape,
        scratch_shapes=scratch,
        compiler_params=_cparams(("parallel", "parallel", "arbitrary")),
        name=name,
    )(x, w, *extras)


SUBLANES = 8


def _repack_kernel(tab_ref, w_ref, o_ref):
    o_ref[...] = w_ref[0].astype(o_ref.dtype)


def repack_rows(wt3, l, pieces, tr=512):
    _, NC, D = wt3.shape
    assert all(r0 % SUBLANES == 0 and nrow % tr == 0 for r0, nrow in pieces)
    starts = [r0 + r for r0, nrow in pieces for r in range(0, nrow, tr)]
    tab = jnp.asarray([s // SUBLANES for s in starts], jnp.int32)
    grid_spec = pltpu.PrefetchScalarGridSpec(
        num_scalar_prefetch=1, grid=(len(starts),),
        in_specs=[pl.BlockSpec((pl.Element(1), pl.Element(tr), pl.Element(D)),
                               lambda j, t: (l, t[j] * SUBLANES, 0))],
        out_specs=pl.BlockSpec((tr, D), lambda j, t: (j, 0)))
    return pl.pallas_call(
        _repack_kernel, grid_spec=grid_spec, out_shape=jax.ShapeDtypeStruct((len(starts) * tr, D), BF16),
        compiler_params=_cparams(("arbitrary",)), name="repack_w_in",
    )(tab, wt3)


def _compress_kernel(x_ref, w1_ref, w2_ref, pe_ref, c_ref, sa_ref, sb_ref, o_ref):
    half = x_ref.shape[-1]
    x = x_ref[0, 0]
    a = _dot(x, w1_ref[0, :half, :])
    b = _dot(x, w1_ref[0, half:, :])
    pet = _dot(pe_ref[0], w1_ref[0])
    n = a.shape[0]
    hid = a + pltpu.roll(b, n - 1, 0) + pet[0:1]
    out = _dot(_gelu(hid).astype(BF16), w2_ref[0])
    out = _rope_apply(out, c_ref[0, 0], sa_ref[0, 0], sb_ref[0, 0])
    o_ref[0, 0, 0] = out.astype(o_ref.dtype)


def nsa_compress(kvp, w1, w2, pe8, tabs):
    B, _, S, dh = kvp.shape
    G = NSA_GROUPS
    nch = S // CMP_STRIDE
    xv = kvp.reshape(B, kvp.shape[1], nch, CMP_STRIDE * dh)
    hidden = w1.shape[-1]
    tab_spec = pl.BlockSpec((1, 1, nch, LANES), lambda w, b, g: (w, b, 0, 0))
    return pl.pallas_call(
        _compress_kernel,
        grid=(2, B, G),
        in_specs=[
            pl.BlockSpec((1, 1, nch, CMP_STRIDE * dh), lambda w, b, g: (b, w * G + g, 0, 0)),
            pl.BlockSpec((1, CMP_LEN * dh, hidden), lambda w, b, g: (w, 0, 0)),
            pl.BlockSpec((1, hidden, dh), lambda w, b, g: (w, 0, 0)),
            pl.BlockSpec((1, 8, CMP_LEN * dh), lambda w, b, g: (w, 0, 0)),
            tab_spec, tab_spec, tab_spec,
        ],
        out_specs=pl.BlockSpec((1, 1, 1, nch, dh), lambda w, b, g: (w, b, g, 0, 0)),
        out_shape=jax.ShapeDtypeStruct((2, B, G, nch, dh), BF16),
        compiler_params=_cparams(("parallel", "parallel", "parallel")),
        name="nsa_compress",
    )(xv, w1, w2, pe8, *tabs)


def _dot_tn(a, b):
    return lax.dot_general(a, b, (((0,), (0,)), ((), ())), preferred_element_type=F32)


NSA_SPLIT = 2


def _nsa_kernel(q_ref, ks_ref, kw_ref, vs_ref, vw_ref, kc_ref, vc_ref, gt_ref, cmap_ref, e_ref, o_ref, *, ck):
    n = NSA_HPG
    ns = NSA_SPLIT
    hps = n // ns
    sq = hps * Q_BLOCK
    streams = range(ns)
    qb = pl.program_id(2)
    t0 = qb * Q_BLOCK
    c2 = (HEAD_DIM ** -0.5) * 1.4426950408889634
    nsel = ks_ref.shape[2] // SEL_BLOCK
    q_t = [jnp.concatenate([q_ref[0, s * hps + h].T for h in range(hps)], axis=1) for s in streams]
    tq = t0 + lax.broadcasted_iota(jnp.int32, (1, Q_BLOCK), 1)

    def heads(x):
        return jnp.concatenate([x] * hps, axis=1)

    def scores(k, ok_t):
        bias = heads(jnp.where(ok_t, 0.0, NEG_INF))
        return [_dot(k, q_t[s]) + bias for s in streams]

    def values(vt_ref, tile0, ntile):
        vt = jnp.concatenate([vt_ref[0, 0, tile0 + i] for i in range(ntile)], axis=1)
        return jnp.concatenate([vt, jnp.ones(vt.shape, vt.dtype)], axis=0)

    kc = kc_ref[0, 0, 0]
    vc = vc_ref[0, 0, 0]
    ncmp = kc.shape[0]
    crow = lax.broadcasted_iota(jnp.int32, (ncmp, Q_BLOCK), 0)
    sb = scores(kc, (crow < ncmp - 1) & (crow * CMP_STRIDE + (CMP_LEN - 1) <= tq))
    p = [jnp.exp2((sb[s] - jnp.max(sb[s], axis=0, keepdims=True)) * c2) for s in streams]
    anyv = heads((tq >= CMP_LEN - 1).astype(F32))
    p = [p[s] * (anyv / jnp.sum(p[s], axis=0, keepdims=True)) for s in streams]
    o_c = [_dot_tn(vc, p[s].astype(BF16)) for s in streams]

    psum = None
    for s in streams:
        for h in range(hps):
            ph = p[s][:, h * Q_BLOCK:(h + 1) * Q_BLOCK]
            psum = ph if psum is None else psum + ph
    hi = psum.astype(BF16)
    lo = (psum - hi.astype(F32)).astype(BF16)
    imp = _dot(cmap_ref[...], hi) + _dot(cmap_ref[...], lo)
    jr = lax.broadcasted_iota(jnp.int32, (LANES, Q_BLOCK), 0)
    cur = tq // SEL_BLOCK
    imp = jnp.where(jr == 0, 1e6, imp)
    imp = jnp.where(jr == cur, 1e6, imp)
    imp = jnp.where(jr == cur - 1, 1e6, imp)
    imp = jnp.where(jr > cur, -1e6, imp)

    nrow = min(((nsel + 7) // 8) * 8, LANES)
    imp = jnp.where(jr >= nsel, -3e6, imp)[0:nrow]
    jrr = jr[0:nrow]
    rank = jnp.zeros((nrow, Q_BLOCK), F32)
    for i in range(min(nsel, LANES)):
        row = imp[i:i + 1, :]
        rank = rank + jnp.where(row > imp, 1.0, jnp.where(row == imp, jnp.where(jrr > i, 1.0, 0.0), 0.0))
    sel = jnp.where(rank < float(SEL_TOP), 1.0, 0.0)
    if nrow < LANES:
        sel = jnp.concatenate([sel, jnp.zeros((LANES - nrow, Q_BLOCK), F32)], axis=0)
    sel = sel.astype(BF16)

    n_ch = (t0 + Q_BLOCK + ck - 1) // ck
    krow = lax.broadcasted_iota(jnp.int32, (ck, Q_BLOCK), 0)

    def body(kb, carry):
        m_i, acc = carry[:ns], carry[ns:]
        k0 = pl.multiple_of(kb * ck, ck)
        kblk = ks_ref[0, 0, pl.ds(k0, ck), :]
        v1 = values(vs_ref, kb * (ck // LANES), ck // LANES)
        em = _dot(e_ref[kb], sel)
        sb = scores(kblk, (k0 + krow <= tq) & (em > 0.5))
        m_new = [jnp.maximum(m_i[s], jnp.max(sb[s], axis=0, keepdims=True)) for s in streams]
        alpha = [jnp.exp2((m_i[s] - m_new[s]) * c2) for s in streams]
        pp = [jnp.exp2(((sb[s] - m_new[s]) * c2).astype(BF16)) for s in streams]
        acc = [alpha[s] * acc[s] + _dot(v1, pp[s]) for s in streams]
        return tuple(m_new) + tuple(acc)

    init = tuple(jnp.full((1, sq), NEG_INF, F32) for _ in streams) + tuple(
        jnp.zeros((2 * HEAD_DIM, sq), F32) for _ in streams)
    acc_f = lax.fori_loop(0, n_ch, body, init)[ns:]
    o_s = [a[0:HEAD_DIM] / a[HEAD_DIM:HEAD_DIM + 1] for a in acc_f]

    wlen = min(WINDOW + Q_BLOCK, kw_ref.shape[2])
    w0 = pl.multiple_of(jnp.maximum(t0 + Q_BLOCK - wlen, 0), Q_BLOCK)
    kblk = kw_ref[0, 0, pl.ds(w0, wlen), :]
    v1 = values(vw_ref, w0 // LANES, wlen // LANES)
    kpos = w0 + lax.broadcasted_iota(jnp.int32, (wlen, Q_BLOCK), 0)
    sb = scores(kblk, (kpos <= tq) & (tq - kpos < WINDOW))
    pw = [jnp.exp2(((sb[s] - jnp.max(sb[s], axis=0, keepdims=True)) * c2).astype(BF16)) for s in streams]
    acc_w = [_dot(v1, pw[s]) for s in streams]
    o_w = [a[0:HEAD_DIM] / a[HEAD_DIM:HEAD_DIM + 1] for a in acc_w]

    gt = gt_ref[0, 0].T
    for h in range(n):
        s = h // hps
        r = slice((h % hps) * Q_BLOCK, (h % hps + 1) * Q_BLOCK)
        o = (gt[3 * h:3 * h + 1] * o_c[s][:, r] + gt[3 * h + 1:3 * h + 2] * o_s[s][:, r]
             + gt[3 * h + 2:3 * h + 3] * o_w[s][:, r])
        o_ref[0, :, h * HEAD_DIM:(h + 1) * HEAD_DIM] = o.T.astype(o_ref.dtype)


def nsa_attention(qkr, vt, cmp, gates, ck=1024):
    B, _, S, dh = qkr.shape
    G, n = NSA_GROUPS, NSA_HPG
    ncmp = cmp.shape[3]
    nsel = S // SEL_BLOCK
    ck = min(ck, S)
    r_sel, r_cmp = SEL_BLOCK // CMP_STRIDE, CMP_LEN // CMP_STRIDE
    tgt = (r_sel * jnp.arange(nsel)[:, None, None] + jnp.arange(r_sel)[None, :, None]
           - jnp.arange(r_cmp)[None, None, :])
    cmap = (jnp.arange(ncmp - 1)[:, None, None, None] == tgt[None]).sum((2, 3)).astype(F32)
    cmap = jnp.pad(cmap, ((0, 1), (0, LANES - nsel))).astype(BF16).T
    e = (jnp.arange(S)[:, None] // SEL_BLOCK == jnp.arange(LANES)[None, :]).astype(BF16)
    e3 = e.reshape(S // ck, ck, LANES)

    def head(h0):
        return pl.BlockSpec((1, 1, S, dh), lambda b, g, i: (b, h0 + g, 0, 0))

    def head_t(h0):
        return pl.BlockSpec((1, 1, S // LANES, dh, LANES), lambda b, g, i: (b, h0 + g, 0, 0, 0))

    return pl.pallas_call(
        functools.partial(_nsa_kernel, ck=ck),
        grid=(B, G, S // Q_BLOCK),
        in_specs=[
            pl.BlockSpec((1, n, Q_BLOCK, dh), lambda b, g, i: (b, g, i, 0)),
            head(16), head(20), head_t(0), head_t(G),
            pl.BlockSpec((1, 1, 1, ncmp, dh), lambda b, g, i: (0, b, g, 0, 0)),
            pl.BlockSpec((1, 1, 1, ncmp, dh), lambda b, g, i: (1, b, g, 0, 0)),
            pl.BlockSpec((1, 1, Q_BLOCK, LANES), lambda b, g, i: (b, g, i, 0)),
            pl.BlockSpec((LANES, ncmp), lambda b, g, i: (0, 0)),
            pl.BlockSpec((S // ck, ck, LANES), lambda b, g, i: (0, 0, 0)),
        ],
        out_specs=pl.BlockSpec((1, Q_BLOCK, n * dh), lambda b, g, i: (b, i, g)),
        out_shape=jax.ShapeDtypeStruct((B, S, G * n * dh), BF16),
        compiler_params=_cparams(("parallel", "parallel", "arbitrary")),
        name="nsa_attention",
    )(qkr, qkr, qkr, vt, vt, cmp, cmp, gates, cmap, e3)


def _rwkv_feat_kernel(z_ref, zp_ref, mu_ref, o_ref):
    i = pl.program_id(1)
    z = z_ref[0]
    prev = zp_ref[0, 7:8, :] * (i > 0).astype(F32)
    row = lax.broadcasted_iota(jnp.int32, z.shape, 0)
    sh = jnp.where(row == 0, prev, pltpu.roll(z, 1, 0))
    zs = z + (sh - z) * mu_ref[...]
    o_ref[0, :, 0:128] = jnp.tanh(zs[:, 0:128])
    o_ref[0, :, 128:256] = zs[:, 128:256]
    o_ref[0, :, 256:] = jax.nn.sigmoid(zs[:, 256:])


def rwkv_features(z_lora, mu_lora, tm=512):
    B, S, W = z_lora.shape
    tm = min(tm, S)
    return pl.pallas_call(
        _rwkv_feat_kernel,
        grid=(B, S // tm),
        in_specs=[pl.BlockSpec((1, tm, W), lambda b, i: (b, i, 0)),
                  pl.BlockSpec((1, 8, W), lambda b, i: (b, jnp.maximum(i * (tm // 8) - 1, 0), 0)),
                  pl.BlockSpec((1, W), lambda b, i: (0, 0))],
        out_specs=pl.BlockSpec((1, tm, W), lambda b, i: (b, i, 0)),
        out_shape=jax.ShapeDtypeStruct((B, S, W), F32),
        compiler_params=_cparams(("parallel", "arbitrary")),
        name="rwkv_features",
    )(z_lora, z_lora, mu_lora)


def _rwkv_kernel(r_ref, k_ref, v_ref, f_ref, par_ref, w2_ref, a2_ref, g2_ref, o_ref, st_ref, prev_ref):
    C = RWKV_CHUNK
    W = r_ref.shape[-1]
    npair = W // LANES
    c = pl.program_id(2)

    @pl.when(c == 0)
    def _():
        st_ref[...] = jnp.zeros_like(st_ref)
        prev_ref[...] = jnp.zeros_like(prev_ref)

    par = par_ref[...]
    mu_r, mu_k, mu_v = par[0:1], par[1:2], par[2:3]
    w0, a0, k_k, k_a, r_k, ln_w, ln_b = par[3:4], par[4:5], par[5:6], par[6:7], par[7:8], par[8:9], par[9:10]

    row = lax.broadcasted_iota(jnp.int32, (C, C), 0)
    col = lax.broadcasted_iota(jnp.int32, (C, C), 1)
    trow = lax.broadcasted_iota(jnp.int32, (C, W), 0)
    lane = lax.broadcasted_iota(jnp.int32, (C, LANES), 1)
    lr = lax.broadcasted_iota(jnp.int32, (LANES, LANES), 0)
    lc = lax.broadcasted_iota(jnp.int32, (LANES, LANES), 1)

    def shifted(z_ref, idx, mu):
        z = z_ref[0]
        sh = jnp.where(trow == 0, prev_ref[idx:idx + 1, :], pltpu.roll(z, 1, 0))
        prev_ref[idx:idx + 1, :] = z[C - 1:C, :]
        return z + (sh - z) * mu

    r = shifted(r_ref, 0, mu_r)
    k = shifted(k_ref, 1, mu_k)
    v = shifted(v_ref, 2, mu_v)
    f = f_ref[0]
    w_log = -_softplus(-(w0 + _dot3(_split(f[:, 0:128]), _split(w2_ref[...])))) - 0.5
    logd = -jnp.exp(w_log)
    a = jax.nn.sigmoid(a0 + _dot3(_split(f[:, 128:256]), _split(a2_ref[...])))
    g = _dot3(_split(f[:, 256:]), _split(g2_ref[...]))

    bd = _exact(jnp.where((lr // RWKV_HEAD) == (lc // RWKV_HEAD), 1.0, 0.0))

    def head_sum(x, passes=2):
        sp = _split if passes == 2 else (lambda z: (z.astype(BF16), None))
        return jnp.concatenate([_dot3(sp(x[:, p * LANES:(p + 1) * LANES]), bd) for p in range(npair)], axis=1)

    kk = k * k_k
    kk = kk / jnp.maximum(jnp.sqrt(head_sum(kk * kk)), 1e-12)
    k2 = k * (1.0 + (a - 1.0) * k_a)

    tri_incl = _exact(jnp.where(col <= row, 1.0, 0.0))
    cum = _dot3(tri_incl, _split(logd))
    cum_end = cum[C - 1:C, :]
    e_neg = jnp.exp(-cum)
    kka = kk * a
    a_t = -kk * jnp.exp(cum - logd)
    b_t = kka * e_neg
    k_t = k2 * e_neg
    r_t = r * jnp.exp(cum)
    e_end = jnp.exp(cum_end - cum)
    b_e = kka * e_end
    k_e = k2 * e_end
    p_end = jnp.exp(cum_end)

    strict = col < row
    incl = col <= row
    eye = jnp.where(col == row, 1.0, 0.0)
    n_sq = C.bit_length() - 2
    hpp = LANES // RWKV_HEAD
    heads = [(p, h) for p in range(npair) for h in range(hpp)]
    hd = []
    for p, h in heads:
        sl = slice(p * LANES, (p + 1) * LANES)
        mh = (lane // RWKV_HEAD) == h
        s_am = _split(jnp.where(mh, a_t[:, sl], 0.0))
        s_rm = _split(jnp.where(mh, r_t[:, sl], 0.0))
        hd.append(dict(
            sl=sl, s_am=s_am, s_rm=s_rm,
            s_vm=_split(jnp.where(mh, v[:, sl], 0.0)),
            s_bem=_split(jnp.where(mh, b_e[:, sl], 0.0)),
            s_kem=_split(jnp.where(mh, k_e[:, sl], 0.0)),
            rhs=_split(jnp.concatenate([b_t[:, sl], k_t[:, sl]], axis=0)) if h == 0 else hd[-1]["rhs"]))
    for d in hd:
        amat = _dot3(d["s_am"], d["rhs"], "nt")
        rmat = _dot_nt(d["s_rm"][0], d["rhs"][0])
        l_ab = jnp.where(strict, amat[:, 0:C], 0.0)
        d["l_ak"] = _split(jnp.where(strict, amat[:, C:2 * C], 0.0))
        d["l_rb"] = jnp.where(incl, rmat[:, 0:C], 0.0).astype(BF16)
        d["l_rk"] = jnp.where(incl, rmat[:, C:2 * C], 0.0).astype(BF16)
        d["x"] = eye + l_ab
        d["s_pw"] = _split(l_ab)
    for i in range(n_sq):
        for d in hd:
            d["s_pw"] = _split(_dot3(d["s_pw"], d["s_pw"])) if i == 0 else (_dot(d["s_pw"][0], d["s_pw"][0]).astype(BF16), None)
        for d in hd:
            d["x"] = d["x"] + (_dot3(_split(d["x"]), d["s_pw"]) if i == 0 else _dot(d["x"].astype(BF16), d["s_pw"][0]))
    cat = lambda parts, axis: tuple(None if parts[0][i] is None else jnp.concatenate([p[i] for p in parts], axis=axis)
                                    for i in range(2))
    for i, d in enumerate(hd):
        d["st"] = st_ref[i]
        d["s_st"] = _split(d["st"])
        d["t"] = _split(_dot3(cat([d["s_am"], d["l_ak"]], 1), cat([d["s_st"], d["s_vm"]], 0)))
    for d in hd:
        d["s_u"] = _split(_dot3(_split(d["x"]), d["t"]))
    for i, d in enumerate(hd):
        if i % hpp == 0:
            decay = jnp.broadcast_to(p_end[:, d["sl"]], (LANES, LANES)).T
        st_ref[i] = d["st"] * decay + _dot3(cat([d["s_bem"], d["s_kem"]], 0), cat([d["s_u"], d["s_vm"]], 0), "tn")
    ys = []
    for d in hd:
        ys.append(_dot(jnp.concatenate([d["s_rm"][0], d["l_rb"], d["l_rk"]], axis=1),
                       jnp.concatenate([d["s_st"][0], d["s_u"][0], d["s_vm"][0]], axis=0)))
    y = jnp.concatenate([sum(ys[hpp * p + 1:hpp * (p + 1)], ys[hpp * p]) for p in range(npair)], axis=1)

    inv_n = 1.0 / RWKV_HEAD
    mean = head_sum(y) * inv_n
    d = y - mean
    var = head_sum(d * d) * inv_n
    yn = d * lax.rsqrt(var + GN_EPS) * ln_w + ln_b
    bonus = head_sum(r * k2 * r_k, passes=1) * v
    o_ref[0] = ((yn + bonus) * g).astype(o_ref.dtype)


def rwkv_scan(z_rkv, feat, par, w2p, a2p, g2, pairs_per_step=8):
    B, S, R3 = z_rkv.shape
    R = R3 // 3
    W = pairs_per_step * LANES
    nblk = R // W
    C = RWKV_CHUNK
    assert S % C == 0 and R % W == 0

    def col(off):
        return pl.BlockSpec((1, C, W), lambda b, p, c: (b, c, off + p))

    return pl.pallas_call(
        _rwkv_kernel,
        grid=(B, nblk, S // C),
        in_specs=[
            col(0), col(nblk), col(2 * nblk),
            pl.BlockSpec((1, C, feat.shape[-1]), lambda b, p, c: (b, c, 0)),
            pl.BlockSpec((16, W), lambda b, p, c: (0, p)),
            pl.BlockSpec((LANES, W), lambda b, p, c: (0, p)),
            pl.BlockSpec((LANES, W), lambda b, p, c: (0, p)),
            pl.BlockSpec((GATE_LORA, W), lambda b, p, c: (0, p)),
        ],
        out_specs=pl.BlockSpec((1, C, W), lambda b, p, c: (b, c, p)),
        out_shape=jax.ShapeDtypeStruct((B, S, R), BF16),
        scratch_shapes=[pltpu.VMEM((W // RWKV_HEAD, LANES, LANES), F32), pltpu.VMEM((8, W), F32)],
        compiler_params=_cparams(("parallel", "parallel", "arbitrary")),
        name="rwkv_scan",
    )(z_rkv, z_rkv, z_rkv, feat, par, w2p, a2p, g2)


def _merge_kernel(on_ref, or_ref, wn_ref, wr_ref, ga_ref, gb_ref, o_ref):
    pa = _dot(on_ref[...], wn_ref[...].astype(on_ref.dtype))
    pb = _dot(or_ref[...], wr_ref[...].astype(or_ref.dtype))
    o_ref[...] = (ga_ref[...].astype(F32) * pa + gb_ref[...].astype(F32) * pb).astype(o_ref.dtype)


def merge_mixers(o_nsa, o_rwkv, wn, wr, gates, tm=1024, tn=512):
    M, Kn = o_nsa.shape
    Kr = o_rwkv.shape[1]
    D = wn.shape[1]
    tm, tn = min(tm, M), min(tn, D)
    nb = D // tn
    return pl.pallas_call(
        _merge_kernel,
        grid=(M // tm, nb),
        in_specs=[
            pl.BlockSpec((tm, Kn), lambda i, j: (i, 0)),
            pl.BlockSpec((tm, Kr), lambda i, j: (i, 0)),
            pl.BlockSpec((Kn, tn), lambda i, j: (0, j)),
            pl.BlockSpec((Kr, tn), lambda i, j: (0, j)),
            pl.BlockSpec((tm, tn), lambda i, j: (i, j)),
            pl.BlockSpec((tm, tn), lambda i, j: (i, nb + j)),
        ],
        out_specs=pl.BlockSpec((tm, tn), lambda i, j: (i, j)),
        out_shape=jax.ShapeDtypeStruct((M, D), BF16),
        compiler_params=_cparams(("parallel", "parallel")),
        name="merge_mixers",
    )(o_nsa, o_rwkv, wn, wr, gates, gates)


def _xattn_kernel(q_ref, kv_ref, wo_ref, x_ref, o_ref):
    nh = XATTN_HEADS
    scale = HEAD_DIM ** -0.5
    outs = []
    for h in range(nh):
        q = q_ref[:, h * HEAD_DIM:(h + 1) * HEAD_DIM]
        k = kv_ref[0, :, h * HEAD_DIM:(h + 1) * HEAD_DIM]
        v = kv_ref[0, :, (nh + h) * HEAD_DIM:(nh + h + 1) * HEAD_DIM]
        s = _dot_nt(q, k) * scale
        m = jnp.max(s, axis=-1, keepdims=True)
        p = jnp.exp(s - m)
        p = p / jnp.sum(p, axis=-1, keepdims=True)
        outs.append(_dot(p.astype(BF16), v).astype(BF16))
    o = jnp.concatenate(outs, axis=-1)
    o_ref[...] = x_ref[...] + _dot(o, wo_ref[...])


def cross_attention(q, kv, wo, x, seq, tm=256):
    T, D = x.shape
    tm = min(tm, seq)
    spb = seq // tm
    Mm = kv.shape[1]
    return pl.pallas_call(
        _xattn_kernel,
        grid=(T // tm,),
        in_specs=[
            pl.BlockSpec((tm, q.shape[1]), lambda i: (i, 0)),
            pl.BlockSpec((1, Mm, kv.shape[2]), lambda i: (i // spb, 0, 0)),
            pl.BlockSpec(wo.shape, lambda i: (0, 0)),
            pl.BlockSpec((tm, D), lambda i: (i, 0)),
        ],
        out_specs=pl.BlockSpec((tm, D), lambda i: (i, 0)),
        out_shape=jax.ShapeDtypeStruct((T, D), F32),
        compiler_params=_cparams(("parallel",)),
        name="cross_attention",
    )(q, kv, wo, x)


def _top16(x, iota=None, n=None):
    return _top16_many([x], iota, n)[0]


def _top16_many(xs, iota=None, n=None):
    shape = xs[0].shape
    if iota is None:
        n = shape[0]
        iota = lax.broadcasted_iota(jnp.int32, shape, 0).astype(F32)
    cur = list(xs)
    rank = [jnp.full(shape, float(PEER_TOPK), F32) for _ in xs]
    vals = [[] for _ in xs]
    for r in range(PEER_TOPK):
        m = [jnp.max(c, axis=0, keepdims=True) for c in cur]
        idx = [jnp.min(jnp.where(c == mi, iota, float(n)), axis=0, keepdims=True) for c, mi in zip(cur, m)]
        hit = [iota == i for i in idx]
        rank = [jnp.where(h, float(r), rk) for h, rk in zip(hit, rank)]
        cur = [jnp.where(h, -jnp.inf, c) for h, c in zip(hit, cur)]
        for v, mi in zip(vals, m):
            v.append(mi)
    return [(jnp.concatenate(v, axis=0), rk) for v, rk in zip(vals, rank)]


def _peer_route_kernel(q_ref, keys_ref, row_ref, pl_ref):
    K = PEER_TOPK
    q = q_ref[...]
    half = q.shape[1] // 2
    s1 = _dotf_nt(keys_ref[0, 0], q[:, :half])
    s2 = _dotf_nt(keys_ref[0, 1], q[:, half:])
    ht = q.shape[0] // 2
    halves = lambda parts: tuple(jnp.concatenate([a, b], axis=1) for a, b in zip(*parts))
    t16 = _top16_many([s1[:, :ht], s1[:, ht:], s2[:, :ht], s2[:, ht:]])
    (v1, rank1), (v2, rank2) = halves(t16[0:2]), halves(t16[2:4])
    tmq = q.shape[0]
    r8 = lax.broadcasted_iota(jnp.int32, (8, tmq), 0)
    ninf = -jnp.inf
    lo8 = v2[0:8]
    tiles = [
        (v1[0:1] + lo8, r8),
        (v1[0:1] + v2[8:16], 8 + r8),
        (v1[1:2] + lo8, 16 + r8),
        (jnp.where(r8 < 5, v1[2:3] + lo8, ninf), 32 + r8),
        (jnp.where(r8 < 7, jnp.where(r8 < 4, v1[3:4], v1[4:5]) + jnp.where(r8 < 4, lo8, pltpu.roll(lo8, 4, 0)), ninf),
         jnp.where(r8 < 4, 48 + r8, 60 + r8)),
        (jnp.where(r8 < 6, jnp.where(r8 < 2, v1[5:6], jnp.where(r8 < 4, v1[6:7], v1[7:8]))
                   + jnp.where(r8 % 2 == 0, v2[0:1], v2[1:2]), ninf),
         80 + (r8 // 2) * 16 + r8 % 2),
        (v1[8:16] + v2[0:1], (8 + r8) * 16),
    ]
    cand = jnp.concatenate([t for t, _ in tiles], axis=0)
    cidx = jnp.concatenate([i for _, i in tiles], axis=0).astype(F32)
    top_s, crank = halves(_top16_many([cand[:, :ht], cand[:, ht:]], cidx[:, :ht], K * K))
    chosen = jnp.where(crank < float(K), 1.0, 0.0)
    z = jnp.sum(jnp.exp(top_s - top_s[0:1]), axis=0, keepdims=True)
    ch = [chosen[8 * t:8 * (t + 1)] for t in range(len(tiles))]
    colsum = lambda x: jnp.sum(x, axis=0, keepdims=True)
    upper = (r8 < 4).astype(F32)
    j_rows = [colsum(ch[0]) + colsum(ch[1]), colsum(ch[2]), colsum(ch[3]),
              colsum(ch[4] * upper), colsum(ch[4] * (1.0 - upper))]
    j_rows += [ch[5][2 * m:2 * m + 1] + ch[5][2 * m + 1:2 * m + 2] for m in range(3)]
    j_rows += [ch[6][m:m + 1] for m in range(8)]
    jn = jnp.zeros_like(s1)
    for i in range(K):
        jn = jnp.where(rank1 == float(i), j_rows[i], jn)
    row_ref[0, 0] = jnp.exp(s1 - v1[0:1]) / z
    row_ref[0, 1] = jn
    pl_ref[0, 0] = jnp.exp(s2 - v2[0:1]).astype(pl_ref.dtype)
    pl_ref[0, 1] = rank2.astype(pl_ref.dtype)


def peer_route(q, keys, tm=512):
    T = q.shape[0]
    H, _, nkeys, hd = keys.shape
    tm = min(tm, T)
    spec = pl.BlockSpec((1, 2, nkeys, tm), lambda i, h: (h, 0, 0, i))
    return pl.pallas_call(
        _peer_route_kernel,
        grid=(T // tm, H),
        in_specs=[pl.BlockSpec((tm, 2 * hd), lambda i, h: (i, h)),
                  pl.BlockSpec((1, 2, nkeys, hd), lambda i, h: (h, 0, 0, 0))],
        out_specs=[spec, spec],
        out_shape=[jax.ShapeDtypeStruct((H, 2, nkeys, T), F32), jax.ShapeDtypeStruct((H, 2, nkeys, T), BF16)],
        compiler_params=_cparams(("parallel", "parallel")),
        name="peer_route",
    )(q, keys)


def _peer_act_kernel(h_ref, u_ref, row_ref, pl_ref, o_ref):
    nk = PEER_KEYS
    ec = u_ref.shape[0]
    j = pl.program_id(1)
    hu = _dot(u_ref[...], h_ref[...])
    zero = jnp.zeros((), o_ref.dtype)
    for al in range(ec // nk):
        a = j * (ec // nk) + al
        w = None
        for hd in range(row_ref.shape[0]):
            e1 = row_ref[hd, 0, pl.ds(a, 1), :].astype(o_ref.dtype)
            jn = row_ref[hd, 1, pl.ds(a, 1), :].astype(o_ref.dtype)
            g = jnp.where(pl_ref[hd, 1] < jn, e1 * pl_ref[hd, 0], zero)
            w = g if w is None else w + g
        act = _gelu(hu[al * nk:(al + 1) * nk, :].astype(o_ref.dtype)) * w
        o_ref[:, al * nk:(al + 1) * nk] = act.T


def peer_activations(h_t, u_tab, rows, planes, tm=512, ec=1024):
    D, T = h_t.shape
    E = u_tab.shape[0]
    tm, ec = min(tm, T), min(ec, E)
    H, _, nkeys, _ = rows.shape
    rspec = pl.BlockSpec((H, 2, nkeys, tm), lambda i, j: (0, 0, 0, i))
    return pl.pallas_call(
        _peer_act_kernel,
        grid=(T // tm, E // ec),
        in_specs=[pl.BlockSpec((D, tm), lambda i, j: (0, i)),
                  pl.BlockSpec((ec, D), lambda i, j: (j, 0)),
                  rspec, rspec],
        out_specs=pl.BlockSpec((tm, ec), lambda i, j: (i, j)),
        out_shape=jax.ShapeDtypeStruct((T, E), BF16),
        compiler_params=_cparams(("parallel", "arbitrary")),
        name="peer_activations",
    )(h_t, u_tab, rows, planes)


def kernel(x, mem, positions, norm_mix, w_in, nsa_cmp_pe, nsa_cmp_w1, nsa_cmp_w2, rwkv_mu, rwkv_w0, rwkv_w2, rwkv_a0, rwkv_a2, rwkv_g2, rwkv_k_k, rwkv_k_a, rwkv_r_k, rwkv_ln_w, rwkv_ln_b, w_proj_nsa, w_proj_rwkv, w_out, norm_xattn, norm_mem, xattn_wq, xattn_wkv, xattn_wo, norm_ffn, peer_wq, peer_keys, peer_u, peer_v, norm_final):
    B, S, D = x.shape
    T = B * S
    depth = w_in.shape[0]
    G, n, dh = NSA_GROUPS, NSA_HPG, HEAD_DIM
    nq_cols = G * n * dh
    nkv_cols = 6 * G * dh
    ngate = 3 * G * n
    R = rwkv_w0.shape[1]
    c_q, c_kv, c_g = nq_cols, nq_cols + nkv_cols, nq_cols + nkv_cols + ngate
    c_r = c_g + 3 * R + DECAY_LORA + AAA_LORA + GATE_LORA
    gd = G * dh

    pos_flat = positions.reshape(T)
    rope_tabs = _rope_tables(pos_flat)
    cend = jnp.arange(S // CMP_STRIDE) * CMP_STRIDE + (CMP_LEN - 1)
    ctab = _rope_tables(positions[:, jnp.minimum(cend, S - 1)])
    ident = _rope_tables(jnp.zeros_like(positions[:, :S // CMP_STRIDE]))
    cmp_tabs = [jnp.stack([a, b]) for a, b in zip(ctab, ident)]

    xf = x.reshape(T, D)
    for l in range(depth):
        w_t = jnp.swapaxes(w_in, 1, 2)
        pieces = [(0, c_q), (c_q + 2 * gd, gd), (c_q + 4 * gd, gd),
                  (c_q, 2 * gd), (c_q + 3 * gd, gd), (c_q + 5 * gd, gd),
                  (c_g, 3 * R), (c_r, 2 * D)]
        wp = repack_rows(w_t, l, pieces)
        g_rope = (0, c_q + 2 * gd)
        g_plain = (g_rope[1], 4 * gd)
        g_rkv = (g_plain[0] + g_plain[1], 3 * R)
        g_merge = (g_rkv[0] + g_rkv[1], 2 * D)
        w_gate = jnp.pad(w_t[l, c_kv:c_g].reshape(G, 3 * n, D), ((0, 0), (0, LANES - 3 * n), (0, 0)))
        w_gate = w_gate.reshape(G * LANES, D)
        wl = w_t[l, c_g + 3 * R:c_r]
        padl = lambda a, n_: jnp.pad(a, ((0, n_ - a.shape[0]), (0, 0)))
        w_lora = jnp.concatenate([padl(wl[:DECAY_LORA], LANES), padl(wl[DECAY_LORA:DECAY_LORA + AAA_LORA], LANES),
                                  wl[DECAY_LORA + AAA_LORA:]], axis=0)

        h = rmsnorm(xf, norm_mix[l], BF16)
        proj = functools.partial(matmul, h, w_t=True)
        qkr = proj(wp, wcols=g_rope, mode="heads_rope", extras=rope_tabs, out_dtype=BF16, seq=S, name="proj_rope")
        kvc = proj(wp, wcols=(g_plain[0], 2 * gd), mode="heads", out_dtype=BF16, seq=S, name="proj_kvc")
        vt = proj(wp, wcols=(g_plain[0] + 2 * gd, 2 * gd), mode="heads_t", out_dtype=BF16, seq=S, name="proj_vt")
        gates_nsa = proj(w_gate, mode="heads_sigmoid", out_dtype=F32, seq=S, name="proj_gate")
        z_rkv = proj(wp, wcols=g_rkv, out_dtype=F32, name="proj_rkv")
        z_lora = proj(w_lora, out_dtype=F32, name="proj_lora")
        gates_merge = proj(wp, wcols=g_merge, mode="sigmoid", out_dtype=BF16, name="proj_merge")

        pe8 = jnp.broadcast_to(nsa_cmp_pe[l].reshape(2, 1, CMP_LEN * dh), (2, 8, CMP_LEN * dh)).astype(BF16)
        cmp = nsa_compress(kvc, nsa_cmp_w1[l].astype(BF16), nsa_cmp_w2[l].astype(BF16), pe8, cmp_tabs)
        o_nsa = nsa_attention(qkr, vt, cmp, gates_nsa).reshape(T, nq_cols)

        mu = rwkv_mu[l]
        mul = mu[3 * R:]
        pad1 = lambda a, n_: jnp.pad(a, (0, n_ - a.shape[0]))
        mu_lora = jnp.concatenate([pad1(mul[:DECAY_LORA], LANES), pad1(mul[DECAY_LORA:DECAY_LORA + AAA_LORA], LANES),
                                   mul[DECAY_LORA + AAA_LORA:]]).reshape(1, -1)
        feat = rwkv_features(z_lora.reshape(B, S, -1), mu_lora)
        par = jnp.stack([mu[:R], mu[R:2 * R], mu[2 * R:3 * R], rwkv_w0[l], rwkv_a0[l], rwkv_k_k[l], rwkv_k_a[l],
                         rwkv_r_k[l].reshape(R), rwkv_ln_w[l], rwkv_ln_b[l]])
        par = jnp.pad(par, ((0, 16 - par.shape[0]), (0, 0)))
        w2p = jnp.pad(rwkv_w2[l], ((0, LANES - DECAY_LORA), (0, 0)))
        a2p = jnp.pad(rwkv_a2[l], ((0, LANES - AAA_LORA), (0, 0)))
        o_rwkv = rwkv_scan(z_rkv.reshape(B, S, 3 * R), feat, par, w2p, a2p, rwkv_g2[l]).reshape(T, R)

        merged = merge_mixers(o_nsa, o_rwkv, w_proj_nsa[l], w_proj_rwkv[l], gates_merge)
        xf = matmul(merged, w_out[l], mode="residual", extras=(xf,), out_dtype=F32, name="out_proj")

        h = rmsnorm(xf, norm_xattn[l], BF16)
        m = rmsnorm(mem.reshape(-1, D), norm_mem[l], BF16)
        q = matmul(h, xattn_wq[l], out_dtype=BF16, name="xattn_q")
        kvm = matmul(m, xattn_wkv[l], out_dtype=BF16, name="xattn_kv").reshape(B, mem.shape[1], -1)
        xf = cross_attention(q, kvm, xattn_wo[l].astype(BF16), xf, S)

        h, h_t = rmsnorm(xf, norm_ffn[l], BF16, with_transpose=True)
        pq = matmul(h, peer_wq[l], out_dtype=F32, name="peer_q")
        rows, planes = peer_route(pq, peer_keys[l])
        act = peer_activations(h_t, peer_u[l].astype(BF16), rows, planes)
        xf = matmul(act, peer_v[l], mode="residual", extras=(xf,), out_dtype=F32,
                    tm=1024, tn=1024, tk=2048, name="peer_out")

    return rmsnorm(xf, norm_final, F32).reshape(B, S, D)
```

```python
import functools

import jax
import jax.numpy as jnp
from jax import lax
from jax.experimental import pallas as pl
from jax.experimental.pallas import tpu as pltpu

F32 = jnp.float32
BF16 = jnp.bfloat16
HI = lax.Precision.HIGHEST

LANES = 128
HEAD_DIM = 128
ROPE_DIM = HEAD_DIM // 4
ROPE_THETA = 500000.0
RMS_EPS = 1e-6
NEG_INF = -1e30

NSA_GROUPS = 4
NSA_HPG = 4
CMP_LEN = 32
CMP_STRIDE = 16
SEL_BLOCK = 64
SEL_TOP = 16
WINDOW = 512
Q_BLOCK = 128

RWKV_HEAD = 64
DECAY_LORA = 96
AAA_LORA = 96
GATE_LORA = 256
GN_EPS = 64e-5
RWKV_CHUNK = 128

XATTN_HEADS = 4
PEER_HEADS = 8
PEER_KEYS = 128
PEER_TOPK = 16

VMEM_LIMIT = 56 * 1024 * 1024


def _cparams(sem, flags=None):
    return pltpu.CompilerParams(dimension_semantics=sem, vmem_limit_bytes=VMEM_LIMIT, flags=flags)


def _gelu(x):
    c = 0.7978845608028654
    return 0.5 * x * (1.0 + jnp.tanh(x * (c + (0.044715 * c) * (x * x))))


def _softplus(x):
    return jnp.maximum(x, 0.0) + jnp.log(1.0 + jnp.exp(-jnp.abs(x)))


def _dot(a, b):
    return jnp.dot(a, b, preferred_element_type=F32)


def _dot_nt(a, b):
    return lax.dot_general(a, b, (((1,), (1,)), ((), ())), preferred_element_type=F32)


def _dotf_nt(a, b):
    return lax.dot_general(a, b, (((1,), (1,)), ((), ())), preferred_element_type=F32, precision=HI)


def _split(x):
    hi = x.astype(BF16)
    return hi, (x - hi.astype(F32)).astype(BF16)


def _exact(x):
    return x.astype(BF16), None


_DN = {"nn": (((1,), (0,)), ((), ())), "nt": (((1,), (1,)), ((), ())), "tn": (((0,), (0,)), ((), ()))}


def _dot3(a, b, form="nn"):
    d = lambda x, y: lax.dot_general(x, y, _DN[form], preferred_element_type=F32)
    out = d(a[0], b[0])
    if a[1] is not None:
        out = out + d(a[1], b[0])
    if b[1] is not None:
        out = out + d(a[0], b[1])
    return out


def _rope_apply(a, c, sa, sb):
    return a * c + pltpu.roll(a, LANES - ROPE_DIM // 2, 1) * sa + pltpu.roll(a, ROPE_DIM // 2, 1) * sb


def _rope_tables(pos):
    half = ROPE_DIM // 2
    inv = ROPE_THETA ** (-jnp.arange(half, dtype=F32) / half)
    ang = pos.astype(F32)[..., None] * inv
    cos, sin = jnp.cos(ang), jnp.sin(ang)
    shp = pos.shape + (HEAD_DIM - ROPE_DIM,)
    c = jnp.concatenate([cos, cos, jnp.ones(shp, F32)], axis=-1)
    sa = jnp.concatenate([-sin, jnp.zeros(pos.shape + (HEAD_DIM - half,), F32)], axis=-1)
    sb = jnp.concatenate([jnp.zeros(pos.shape + (half,), F32), sin, jnp.zeros(shp, F32)], axis=-1)
    return c, sa, sb


def _rmsnorm_kernel(x_ref, g_ref, o_ref):
    x = x_ref[...].astype(F32)
    ms = jnp.mean(x * x, axis=-1, keepdims=True)
    o_ref[...] = (x * lax.rsqrt(ms + RMS_EPS) * g_ref[...]).astype(o_ref.dtype)


def rmsnorm(x2d, g, out_dtype, tm=256):
    M, D = x2d.shape
    tm = min(tm, M)
    return pl.pallas_call(
        _rmsnorm_kernel,
        grid=(M // tm,),
        in_specs=[pl.BlockSpec((tm, D), lambda i: (i, 0)), pl.BlockSpec((1, D), lambda i: (0, 0))],
        out_specs=pl.BlockSpec((tm, D), lambda i: (i, 0)),
        out_shape=jax.ShapeDtypeStruct((M, D), out_dtype),
        compiler_params=_cparams(("parallel",)),
        name="rmsnorm",
    )(x2d, g.reshape(1, D).astype(F32))


def _mm_kernel(*refs, nk, mode, n_extra, w_t):
    x_ref, w_ref = refs[0], refs[1]
    extras = refs[2:2 + n_extra]
    o_ref = refs[2 + n_extra]

    def product():
        w = w_ref[...]
        if w.dtype != x_ref.dtype:
            w = w.astype(x_ref.dtype)
        return _dot_nt(x_ref[...], w) if w_t else _dot(x_ref[...], w)

    def epilogue(acc):
        if mode == "plain":
            o_ref[...] = acc.astype(o_ref.dtype)
        elif mode == "sigmoid":
            o_ref[...] = jax.nn.sigmoid(acc).astype(o_ref.dtype)
        elif mode == "residual":
            o_ref[...] = (extras[0][...] + acc).astype(o_ref.dtype)
        elif mode in ("heads", "heads_sigmoid", "heads_rope"):
            nh = acc.shape[1] // LANES
            for j in range(nh):
                a = acc[:, j * LANES:(j + 1) * LANES]
                if mode == "heads_sigmoid":
                    a = jax.nn.sigmoid(a)
                elif mode == "heads_rope":
                    a = _rope_apply(a, extras[0][...], extras[1][...], extras[2][...])
                o_ref[0, j] = a.astype(o_ref.dtype)
        elif mode == "heads_t":
            for j in range(acc.shape[1] // LANES):
                for c in range(acc.shape[0] // LANES):
                    o_ref[0, j, c] = acc[c * LANES:(c + 1) * LANES, j * LANES:(j + 1) * LANES].T.astype(o_ref.dtype)
        else:
            raise ValueError(mode)

    if nk == 1:
        epilogue(product())
    else:
        acc_ref = refs[-1]
        k = pl.program_id(2)

        @pl.when(k == 0)
        def _():
            acc_ref[...] = jnp.zeros_like(acc_ref)

        acc_ref[...] += product()

        @pl.when(k == nk - 1)
        def _():
            epilogue(acc_ref[...])


def matmul(x, w, *, mode="plain", extras=(), out_dtype=F32, tm=1024, tn=512, tk=None, seq=None, name="mm",
           w_t=False, wcols=None):
    M, K = x.shape
    wcol0, N = (0, w.shape[0 if w_t else 1]) if wcols is None else wcols
    tm, tn = min(tm, M), min(tn, N)
    assert wcol0 % tn == 0
    wblk0 = wcol0 // tn
    if mode.startswith("heads"):
        tm = min(tm, seq)
    tk = K if tk is None else min(tk, K)
    nk = K // tk
    assert M % tm == 0 and N % tn == 0 and K % tk == 0
    grid = (M // tm, N // tn, nk)
    if w_t:
        w_spec = pl.BlockSpec((tn, tk), lambda i, j, k: (wblk0 + j, k))
    else:
        w_spec = pl.BlockSpec((tk, tn), lambda i, j, k: (k, wblk0 + j))
    in_specs = [pl.BlockSpec((tm, tk), lambda i, j, k: (i, k)), w_spec]
    if mode == "residual":
        in_specs.append(pl.BlockSpec((tm, tn), lambda i, j, k: (i, j)))
    elif mode == "heads_rope":
        in_specs += [pl.BlockSpec((tm, LANES), lambda i, j, k: (i, 0))] * 3
    if mode == "heads_t":
        assert seq % tm == 0 and tn % LANES == 0 and tm % LANES == 0
        spb = seq // tm
        out_shape = jax.ShapeDtypeStruct((M // seq, N // LANES, seq // LANES, LANES, LANES), out_dtype)
        out_spec = pl.BlockSpec((1, tn // LANES, tm // LANES, LANES, LANES),
                                lambda i, j, k: (i // spb, j, i % spb, 0, 0))
    elif mode.startswith("heads"):
        assert seq % tm == 0 and tn % LANES == 0
        spb = seq // tm
        out_shape = jax.ShapeDtypeStruct((M // seq, N // LANES, seq, LANES), out_dtype)
        out_spec = pl.BlockSpec((1, tn // LANES, tm, LANES), lambda i, j, k: (i // spb, j, i % spb, 0))
    else:
        out_shape = jax.ShapeDtypeStruct((M, N), out_dtype)
        out_spec = pl.BlockSpec((tm, tn), lambda i, j, k: (i, j))
    scratch = [pltpu.VMEM((tm, tn), F32)] if nk > 1 else []
    return pl.pallas_call(
        functools.partial(_mm_kernel, nk=nk, mode=mode, n_extra=len(extras), w_t=w_t),
        grid=grid,
        in_specs=in_specs,
        out_specs=out_spec,
        out_shape=out_shape,
        scratch_shapes=scratch,
        compiler_params=_cparams(("parallel", "parallel", "arbitrary")),
        name=name,
    )(x, w, *extras)


SUBLANES = 8


def _repack_kernel(tab_ref, w_ref, o_ref):
    o_ref[...] = w_ref[0].astype(o_ref.dtype)


def repack_rows(wt3, l, pieces, tr=512):
    _, NC, D = wt3.shape
    assert all(r0 % SUBLANES == 0 and nrow % tr == 0 for r0, nrow in pieces)
    starts = [r0 + r for r0, nrow in pieces for r in range(0, nrow, tr)]
    tab = jnp.asarray([s // SUBLANES for s in starts], jnp.int32)
    grid_spec = pltpu.PrefetchScalarGridSpec(
        num_scalar_prefetch=1, grid=(len(starts),),
        in_specs=[pl.BlockSpec((pl.Element(1), pl.Element(tr), pl.Element(D)),
                               lambda j, t: (l, t[j] * SUBLANES, 0))],
        out_specs=pl.BlockSpec((tr, D), lambda j, t: (j, 0)))
    return pl.pallas_call(
        _repack_kernel, grid_spec=grid_spec, out_shape=jax.ShapeDtypeStruct((len(starts) * tr, D), BF16),
        compiler_params=_cparams(("arbitrary",)), name="repack_w_in",
    )(tab, wt3)


def _compress_kernel(x_ref, w1_ref, w2_ref, pe_ref, c_ref, sa_ref, sb_ref, o_ref):
    half = x_ref.shape[-1]
    x = x_ref[0, 0]
    a = _dot(x, w1_ref[0, :half, :])
    b = _dot(x, w1_ref[0, half:, :])
    pet = _dot(pe_ref[0], w1_ref[0])
    n = a.shape[0]
    hid = a + pltpu.roll(b, n - 1, 0) + pet[0:1]
    out = _dot(_gelu(hid).astype(BF16), w2_ref[0])
    out = _rope_apply(out, c_ref[0, 0], sa_ref[0, 0], sb_ref[0, 0])
    o_ref[0, 0, 0] = out.astype(o_ref.dtype)


def nsa_compress(kvp, w1, w2, pe8, tabs):
    B, _, S, dh = kvp.shape
    G = NSA_GROUPS
    nch = S // CMP_STRIDE
    xv = kvp.reshape(B, kvp.shape[1], nch, CMP_STRIDE * dh)
    hidden = w1.shape[-1]
    tab_spec = pl.BlockSpec((1, 1, nch, LANES), lambda w, b, g: (w, b, 0, 0))
    return pl.pallas_call(
        _compress_kernel,
        grid=(2, B, G),
        in_specs=[
            pl.BlockSpec((1, 1, nch, CMP_STRIDE * dh), lambda w, b, g: (b, w * G + g, 0, 0)),
            pl.BlockSpec((1, CMP_LEN * dh, hidden), lambda w, b, g: (w, 0, 0)),
            pl.BlockSpec((1, hidden, dh), lambda w, b, g: (w, 0, 0)),
            pl.BlockSpec((1, 8, CMP_LEN * dh), lambda w, b, g: (w, 0, 0)),
            tab_spec, tab_spec, tab_spec,
        ],
        out_specs=pl.BlockSpec((1, 1, 1, nch, dh), lambda w, b, g: (w, b, g, 0, 0)),
        out_shape=jax.ShapeDtypeStruct((2, B, G, nch, dh), BF16),
        compiler_params=_cparams(("parallel", "parallel", "parallel")),
        name="nsa_compress",
    )(xv, w1, w2, pe8, *tabs)


def _dot_tn(a, b):
    return lax.dot_general(a, b, (((0,), (0,)), ((), ())), preferred_element_type=F32)


NSA_SPLIT = 2


def _nsa_kernel(q_ref, ks_ref, kw_ref, vs_ref, vw_ref, kc_ref, vc_ref, gt_ref, cmap_ref, e_ref, o_ref, *, ck):
    n = NSA_HPG
    ns = NSA_SPLIT
    hps = n // ns
    sq = hps * Q_BLOCK
    streams = range(ns)
    qb = pl.program_id(2)
    t0 = qb * Q_BLOCK
    c2 = (HEAD_DIM ** -0.5) * 1.4426950408889634
    nsel = ks_ref.shape[2] // SEL_BLOCK
    q_t = [jnp.concatenate([q_ref[0, s * hps + h].T for h in range(hps)], axis=1) for s in streams]
    tq = t0 + lax.broadcasted_iota(jnp.int32, (1, Q_BLOCK), 1)

    def heads(x):
        return jnp.concatenate([x] * hps, axis=1)

    def scores(k, ok_t):
        bias = heads(jnp.where(ok_t, 0.0, NEG_INF))
        return [_dot(k, q_t[s]) + bias for s in streams]

    def values(vt_ref, tile0, ntile):
        vt = jnp.concatenate([vt_ref[0, 0, tile0 + i] for i in range(ntile)], axis=1)
        return jnp.concatenate([vt, jnp.ones(vt.shape, vt.dtype)], axis=0)

    kc = kc_ref[0, 0, 0]
    vc = vc_ref[0, 0, 0]
    ncmp = kc.shape[0]
    crow = lax.broadcasted_iota(jnp.int32, (ncmp, Q_BLOCK), 0)
    sb = scores(kc, (crow < ncmp - 1) & (crow * CMP_STRIDE + (CMP_LEN - 1) <= tq))
    p = [jnp.exp2((sb[s] - jnp.max(sb[s], axis=0, keepdims=True)) * c2) for s in streams]
    anyv = heads((tq >= CMP_LEN - 1).astype(F32))
    p = [p[s] * (anyv / jnp.sum(p[s], axis=0, keepdims=True)) for s in streams]
    o_c = [_dot_tn(vc, p[s].astype(BF16)) for s in streams]

    psum = None
    for s in streams:
        for h in range(hps):
            ph = p[s][:, h * Q_BLOCK:(h + 1) * Q_BLOCK]
            psum = ph if psum is None else psum + ph
    hi = psum.astype(BF16)
    lo = (psum - hi.astype(F32)).astype(BF16)
    imp = _dot(cmap_ref[...], hi) + _dot(cmap_ref[...], lo)
    jr = lax.broadcasted_iota(jnp.int32, (LANES, Q_BLOCK), 0)
    cur = tq // SEL_BLOCK
    imp = jnp.where(jr == 0, 1e6, imp)
    imp = jnp.where(jr == cur, 1e6, imp)
    imp = jnp.where(jr == cur - 1, 1e6, imp)
    imp = jnp.where(jr > cur, -1e6, imp)

    nrow = min(((nsel + 7) // 8) * 8, LANES)
    imp = jnp.where(jr >= nsel, -3e6, imp)[0:nrow]
    jrr = jr[0:nrow]
    rank = jnp.zeros((nrow, Q_BLOCK), F32)
    for i in range(min(nsel, LANES)):
        row = imp[i:i + 1, :]
        rank = rank + jnp.where(row > imp, 1.0, jnp.where(row == imp, jnp.where(jrr > i, 1.0, 0.0), 0.0))
    sel = jnp.where(rank < float(SEL_TOP), 1.0, 0.0)
    if nrow < LANES:
        sel = jnp.concatenate([sel, jnp.zeros((LANES - nrow, Q_BLOCK), F32)], axis=0)
    sel = sel.astype(BF16)

    n_ch = (t0 + Q_BLOCK + ck - 1) // ck
    krow = lax.broadcasted_iota(jnp.int32, (ck, Q_BLOCK), 0)

    def body(kb, carry):
        m_i, acc = carry[:ns], carry[ns:]
        k0 = pl.multiple_of(kb * ck, ck)
        kblk = ks_ref[0, 0, pl.ds(k0, ck), :]
        v1 = values(vs_ref, kb * (ck // LANES), ck // LANES)
        em = _dot(e_ref[kb], sel)
        sb = scores(kblk, (k0 + krow <= tq) & (em > 0.5))
        m_new = [jnp.maximum(m_i[s], jnp.max(sb[s], axis=0, keepdims=True)) for s in streams]
        alpha = [jnp.exp2((m_i[s] - m_new[s]) * c2) for s in streams]
        pp = [jnp.exp2(((sb[s] - m_new[s]) * c2).astype(BF16)) for s in streams]
        acc = [alpha[s] * acc[s] + _dot(v1, pp[s]) for s in streams]
        return tuple(m_new) + tuple(acc)

    init = tuple(jnp.full((1, sq), NEG_INF, F32) for _ in streams) + tuple(
        jnp.zeros((2 * HEAD_DIM, sq), F32) for _ in streams)
    acc_f = lax.fori_loop(0, n_ch, body, init)[ns:]
    o_s = [a[0:HEAD_DIM] / a[HEAD_DIM:HEAD_DIM + 1] for a in acc_f]

    wlen = min(WINDOW + Q_BLOCK, kw_ref.shape[2])
    w0 = pl.multiple_of(jnp.maximum(t0 + Q_BLOCK - wlen, 0), Q_BLOCK)
    kblk = kw_ref[0, 0, pl.ds(w0, wlen), :]
    v1 = values(vw_ref, w0 // LANES, wlen // LANES)
    kpos = w0 + lax.broadcasted_iota(jnp.int32, (wlen, Q_BLOCK), 0)
    sb = scores(kblk, (kpos <= tq) & (tq - kpos < WINDOW))
    pw = [jnp.exp2(((sb[s] - jnp.max(sb[s], axis=0, keepdims=True)) * c2).astype(BF16)) for s in streams]
    acc_w = [_dot(v1, pw[s]) for s in streams]
    o_w = [a[0:HEAD_DIM] / a[HEAD_DIM:HEAD_DIM + 1] for a in acc_w]

    gt = gt_ref[0, 0].T
    for h in range(n):
        s = h // hps
        r = slice((h % hps) * Q_BLOCK, (h % hps + 1) * Q_BLOCK)
        o = (gt[3 * h:3 * h + 1] * o_c[s][:, r] + gt[3 * h + 1:3 * h + 2] * o_s[s][:, r]
             + gt[3 * h + 2:3 * h + 3] * o_w[s][:, r])
        o_ref[0, :, h * HEAD_DIM:(h + 1) * HEAD_DIM] = o.T.astype(o_ref.dtype)


def nsa_attention(qkr, vt, cmp, gates, ck=1024):
    B, _, S, dh = qkr.shape
    G, n = NSA_GROUPS, NSA_HPG
    ncmp = cmp.shape[3]
    nsel = S // SEL_BLOCK
    ck = min(ck, S)
    r_sel, r_cmp = SEL_BLOCK // CMP_STRIDE, CMP_LEN // CMP_STRIDE
    tgt = (r_sel * jnp.arange(nsel)[:, None, None] + jnp.arange(r_sel)[None, :, None]
           - jnp.arange(r_cmp)[None, None, :])
    cmap = (jnp.arange(ncmp - 1)[:, None, None, None] == tgt[None]).sum((2, 3)).astype(F32)
    cmap = jnp.pad(cmap, ((0, 1), (0, LANES - nsel))).astype(BF16).T
    e = (jnp.arange(S)[:, None] // SEL_BLOCK == jnp.arange(LANES)[None, :]).astype(BF16)
    e3 = e.reshape(S // ck, ck, LANES)

    def head(h0):
        return pl.BlockSpec((1, 1, S, dh), lambda b, g, i: (b, h0 + g, 0, 0))

    def head_t(h0):
        return pl.BlockSpec((1, 1, S // LANES, dh, LANES), lambda b, g, i: (b, h0 + g, 0, 0, 0))

    return pl.pallas_call(
        functools.partial(_nsa_kernel, ck=ck),
        grid=(B, G, S // Q_BLOCK),
        in_specs=[
            pl.BlockSpec((1, n, Q_BLOCK, dh), lambda b, g, i: (b, g, i, 0)),
            head(16), head(20), head_t(0), head_t(G),
            pl.BlockSpec((1, 1, 1, ncmp, dh), lambda b, g, i: (0, b, g, 0, 0)),
            pl.BlockSpec((1, 1, 1, ncmp, dh), lambda b, g, i: (1, b, g, 0, 0)),
            pl.BlockSpec((1, 1, Q_BLOCK, LANES), lambda b, g, i: (b, g, i, 0)),
            pl.BlockSpec((LANES, ncmp), lambda b, g, i: (0, 0)),
            pl.BlockSpec((S // ck, ck, LANES), lambda b, g, i: (0, 0, 0)),
        ],
        out_specs=pl.BlockSpec((1, Q_BLOCK, n * dh), lambda b, g, i: (b, i, g)),
        out_shape=jax.ShapeDtypeStruct((B, S, G * n * dh), BF16),
        compiler_params=_cparams(("parallel", "parallel", "arbitrary")),
        name="nsa_attention",
    )(qkr, qkr, qkr, vt, vt, cmp, cmp, gates, cmap, e3)


def _rwkv_feat_kernel(z_ref, zp_ref, mu_ref, o_ref):
    i = pl.program_id(1)
    z = z_ref[0]
    prev = zp_ref[0, 7:8, :] * (i > 0).astype(F32)
    row = lax.broadcasted_iota(jnp.int32, z.shape, 0)
    sh = jnp.where(row == 0, prev, pltpu.roll(z, 1, 0))
    zs = z + (sh - z) * mu_ref[...]
    o_ref[0, :, 0:128] = jnp.tanh(zs[:, 0:128])
    o_ref[0, :, 128:256] = zs[:, 128:256]
    o_ref[0, :, 256:] = jax.nn.sigmoid(zs[:, 256:])


def rwkv_features(z_lora, mu_lora, tm=512):
    B, S, W = z_lora.shape
    tm = min(tm, S)
    return pl.pallas_call(
        _rwkv_feat_kernel,
        grid=(B, S // tm),
        in_specs=[pl.BlockSpec((1, tm, W), lambda b, i: (b, i, 0)),
                  pl.BlockSpec((1, 8, W), lambda b, i: (b, jnp.maximum(i * (tm // 8) - 1, 0), 0)),
                  pl.BlockSpec((1, W), lambda b, i: (0, 0))],
        out_specs=pl.BlockSpec((1, tm, W), lambda b, i: (b, i, 0)),
        out_shape=jax.ShapeDtypeStruct((B, S, W), F32),
        compiler_params=_cparams(("parallel", "arbitrary")),
        name="rwkv_features",
    )(z_lora, z_lora, mu_lora)


def _rwkv_kernel(r_ref, k_ref, v_ref, f_ref, par_ref, w2_ref, a2_ref, g2_ref, o_ref, st_ref, prev_ref):
    C = RWKV_CHUNK
    W = r_ref.shape[-1]
    npair = W // LANES
    c = pl.program_id(2)

    @pl.when(c == 0)
    def _():
        st_ref[...] = jnp.zeros_like(st_ref)
        prev_ref[...] = jnp.zeros_like(prev_ref)

    par = par_ref[...]
    mu_r, mu_k, mu_v = par[0:1], par[1:2], par[2:3]
    w0, a0, k_k, k_a, r_k, ln_w, ln_b = par[3:4], par[4:5], par[5:6], par[6:7], par[7:8], par[8:9], par[9:10]

    row = lax.broadcasted_iota(jnp.int32, (C, C), 0)
    col = lax.broadcasted_iota(jnp.int32, (C, C), 1)
    trow = lax.broadcasted_iota(jnp.int32, (C, W), 0)
    lane = lax.broadcasted_iota(jnp.int32, (C, LANES), 1)
    lr = lax.broadcasted_iota(jnp.int32, (LANES, LANES), 0)
    lc = lax.broadcasted_iota(jnp.int32, (LANES, LANES), 1)

    def shifted(z_ref, idx, mu):
        z = z_ref[0]
        sh = jnp.where(trow == 0, prev_ref[idx:idx + 1, :], pltpu.roll(z, 1, 0))
        prev_ref[idx:idx + 1, :] = z[C - 1:C, :]
        return z + (sh - z) * mu

    r = shifted(r_ref, 0, mu_r)
    k = shifted(k_ref, 1, mu_k)
    v = shifted(v_ref, 2, mu_v)
    f = f_ref[0]
    w_log = -_softplus(-(w0 + _dot3(_split(f[:, 0:128]), _split(w2_ref[...])))) - 0.5
    logd = -jnp.exp(w_log)
    a = jax.nn.sigmoid(a0 + _dot3(_split(f[:, 128:256]), _split(a2_ref[...])))
    g = _dot3(_split(f[:, 256:]), _split(g2_ref[...]))

    bd = _exact(jnp.where((lr // RWKV_HEAD) == (lc // RWKV_HEAD), 1.0, 0.0))

    def head_sum(x, passes=2):
        sp = _split if passes == 2 else (lambda z: (z.astype(BF16), None))
        return jnp.concatenate([_dot3(sp(x[:, p * LANES:(p + 1) * LANES]), bd) for p in range(npair)], axis=1)

    kk = k * k_k
    kk = kk / jnp.maximum(jnp.sqrt(head_sum(kk * kk)), 1e-12)
    k2 = k * (1.0 + (a - 1.0) * k_a)

    tri_incl = _exact(jnp.where(col <= row, 1.0, 0.0))
    cum = _dot3(tri_incl, _split(logd))
    cum_end = cum[C - 1:C, :]
    e_neg = jnp.exp(-cum)
    kka = kk * a
    a_t = -kk * jnp.exp(cum - logd)
    b_t = kka * e_neg
    k_t = k2 * e_neg
    r_t = r * jnp.exp(cum)
    e_end = jnp.exp(cum_end - cum)
    b_e = kka * e_end
    k_e = k2 * e_end
    p_end = jnp.exp(cum_end)

    strict = col < row
    incl = col <= row
    eye = jnp.where(col == row, 1.0, 0.0)
    n_sq = C.bit_length() - 2
    hpp = LANES // RWKV_HEAD
    heads = [(p, h) for p in range(npair) for h in range(hpp)]
    hd = []
    for p, h in heads:
        sl = slice(p * LANES, (p + 1) * LANES)
        mh = (lane // RWKV_HEAD) == h
        s_am = _split(jnp.where(mh, a_t[:, sl], 0.0))
        s_rm = _split(jnp.where(mh, r_t[:, sl], 0.0))
        hd.append(dict(
            sl=sl, s_am=s_am, s_rm=s_rm,
            s_vm=_split(jnp.where(mh, v[:, sl], 0.0)),
            s_bem=_split(jnp.where(mh, b_e[:, sl], 0.0)),
            s_kem=_split(jnp.where(mh, k_e[:, sl], 0.0)),
            rhs=_split(jnp.concatenate([b_t[:, sl], k_t[:, sl]], axis=0)) if h == 0 else hd[-1]["rhs"]))
    for d in hd:
        amat = _dot3(d["s_am"], d["rhs"], "nt")
        rmat = _dot_nt(d["s_rm"][0], d["rhs"][0])
        l_ab = jnp.where(strict, amat[:, 0:C], 0.0)
        d["l_ak"] = _split(jnp.where(strict, amat[:, C:2 * C], 0.0))
        d["l_rb"] = jnp.where(incl, rmat[:, 0:C], 0.0).astype(BF16)
        d["l_rk"] = jnp.where(incl, rmat[:, C:2 * C], 0.0).astype(BF16)
        d["x"] = eye + l_ab
        d["s_pw"] = _split(l_ab)
    for i in range(n_sq):
        for d in hd:
            d["s_pw"] = _split(_dot3(d["s_pw"], d["s_pw"])) if i == 0 else (_dot(d["s_pw"][0], d["s_pw"][0]).astype(BF16), None)
        for d in hd:
            d["x"] = d["x"] + (_dot3(_split(d["x"]), d["s_pw"]) if i == 0 else _dot(d["x"].astype(BF16), d["s_pw"][0]))
    cat = lambda parts, axis: tuple(None if parts[0][i] is None else jnp.concatenate([p[i] for p in parts], axis=axis)
                                    for i in range(2))
    for i, d in enumerate(hd):
        d["st"] = st_ref[i]
        d["s_st"] = _split(d["st"])
        d["t"] = _split(_dot3(cat([d["s_am"], d["l_ak"]], 1), cat([d["s_st"], d["s_vm"]], 0)))
    for d in hd:
        d["s_u"] = _split(_dot3(_split(d["x"]), d["t"]))
    for i, d in enumerate(hd):
        if i % hpp == 0:
            decay = jnp.broadcast_to(p_end[:, d["sl"]], (LANES, LANES)).T
        st_ref[i] = d["st"] * decay + _dot3(cat([d["s_bem"], d["s_kem"]], 0), cat([d["s_u"], d["s_vm"]], 0), "tn")
    ys = []
    for d in hd:
        ys.append(_dot(jnp.concatenate([d["s_rm"][0], d["l_rb"], d["l_rk"]], axis=1),
                       jnp.concatenate([d["s_st"][0], d["s_u"][0], d["s_vm"][0]], axis=0)))
    y = jnp.concatenate([sum(ys[hpp * p + 1:hpp * (p + 1)], ys[hpp * p]) for p in range(npair)], axis=1)

    inv_n = 1.0 / RWKV_HEAD
    mean = head_sum(y) * inv_n
    d = y - mean
    var = head_sum(d * d) * inv_n
    yn = d * lax.rsqrt(var + GN_EPS) * ln_w + ln_b
    bonus = head_sum(r * k2 * r_k, passes=1) * v
    o_ref[0] = ((yn + bonus) * g).astype(o_ref.dtype)


def rwkv_scan(z_rkv, feat, par, w2p, a2p, g2, pairs_per_step=8):
    B, S, R3 = z_rkv.shape
    R = R3 // 3
    W = pairs_per_step * LANES
    nblk = R // W
    C = RWKV_CHUNK
    assert S % C == 0 and R % W == 0

    def col(off):
        return pl.BlockSpec((1, C, W), lambda b, p, c: (b, c, off + p))

    return pl.pallas_call(
        _rwkv_kernel,
        grid=(B, nblk, S // C),
        in_specs=[
            col(0), col(nblk), col(2 * nblk),
            pl.BlockSpec((1, C, feat.shape[-1]), lambda b, p, c: (b, c, 0)),
            pl.BlockSpec((16, W), lambda b, p, c: (0, p)),
            pl.BlockSpec((LANES, W), lambda b, p, c: (0, p)),
            pl.BlockSpec((LANES, W), lambda b, p, c: (0, p)),
            pl.BlockSpec((GATE_LORA, W), lambda b, p, c: (0, p)),
        ],
        out_specs=pl.BlockSpec((1, C, W), lambda b, p, c: (b, c, p)),
        out_shape=jax.ShapeDtypeStruct((B, S, R), BF16),
        scratch_shapes=[pltpu.VMEM((W // RWKV_HEAD, LANES, LANES), F32), pltpu.VMEM((8, W), F32)],
        compiler_params=_cparams(("parallel", "parallel", "arbitrary")),
        name="rwkv_scan",
    )(z_rkv, z_rkv, z_rkv, feat, par, w2p, a2p, g2)


def _merge_kernel(on_ref, or_ref, wn_ref, wr_ref, ga_ref, gb_ref, o_ref):
    pa = _dot(on_ref[...], wn_ref[...].astype(on_ref.dtype))
    pb = _dot(or_ref[...], wr_ref[...].astype(or_ref.dtype))
    o_ref[...] = (ga_ref[...].astype(F32) * pa + gb_ref[...].astype(F32) * pb).astype(o_ref.dtype)


def merge_mixers(o_nsa, o_rwkv, wn, wr, gates, tm=1024, tn=512):
    M, Kn = o_nsa.shape
    Kr = o_rwkv.shape[1]
    D = wn.shape[1]
    tm, tn = min(tm, M), min(tn, D)
    nb = D // tn
    return pl.pallas_call(
        _merge_kernel,
        grid=(M // tm, nb),
        in_specs=[
            pl.BlockSpec((tm, Kn), lambda i, j: (i, 0)),
            pl.BlockSpec((tm, Kr), lambda i, j: (i, 0)),
            pl.BlockSpec((Kn, tn), lambda i, j: (0, j)),
            pl.BlockSpec((Kr, tn), lambda i, j: (0, j)),
            pl.BlockSpec((tm, tn), lambda i, j: (i, j)),
            pl.BlockSpec((tm, tn), lambda i, j: (i, nb + j)),
        ],
        out_specs=pl.BlockSpec((tm, tn), lambda i, j: (i, j)),
        out_shape=jax.ShapeDtypeStruct((M, D), BF16),
        compiler_params=_cparams(("parallel", "parallel")),
        name="merge_mixers",
    )(o_nsa, o_rwkv, wn, wr, gates, gates)


def _xattn_kernel(q_ref, kv_ref, wo_ref, x_ref, g_ref, o_ref, h_ref, ht_ref):
    nh = XATTN_HEADS
    scale = HEAD_DIM ** -0.5
    outs = []
    for h in range(nh):
        q = q_ref[:, h * HEAD_DIM:(h + 1) * HEAD_DIM]
        k = kv_ref[0, :, h * HEAD_DIM:(h + 1) * HEAD_DIM]
        v = kv_ref[0, :, (nh + h) * HEAD_DIM:(nh + h + 1) * HEAD_DIM]
        s = _dot_nt(q, k) * scale
        m = jnp.max(s, axis=-1, keepdims=True)
        p = jnp.exp(s - m)
        p = p / jnp.sum(p, axis=-1, keepdims=True)
        outs.append(_dot(p.astype(BF16), v).astype(BF16))
    o = jnp.concatenate(outs, axis=-1)
    x = x_ref[...] + _dot(o, wo_ref[...])
    o_ref[...] = x
    y = x * lax.rsqrt(jnp.mean(x * x, axis=-1, keepdims=True) + RMS_EPS) * g_ref[...]
    h_ref[...] = y.astype(h_ref.dtype)
    ht_ref[...] = y.T.astype(ht_ref.dtype)


def cross_attention(q, kv, wo, x, g_next, seq, tm=256):
    T, D = x.shape
    tm = min(tm, seq)
    spb = seq // tm
    Mm = kv.shape[1]
    row = pl.BlockSpec((tm, D), lambda i: (i, 0))
    return pl.pallas_call(
        _xattn_kernel,
        grid=(T // tm,),
        in_specs=[
            pl.BlockSpec((tm, q.shape[1]), lambda i: (i, 0)),
            pl.BlockSpec((1, Mm, kv.shape[2]), lambda i: (i // spb, 0, 0)),
            pl.BlockSpec(wo.shape, lambda i: (0, 0)),
            row,
            pl.BlockSpec((1, D), lambda i: (0, 0)),
        ],
        out_specs=[row, row, pl.BlockSpec((D, tm), lambda i: (0, i))],
        out_shape=[jax.ShapeDtypeStruct((T, D), F32), jax.ShapeDtypeStruct((T, D), BF16),
                   jax.ShapeDtypeStruct((D, T), BF16)],
        compiler_params=_cparams(("parallel",)),
        name="cross_attention",
    )(q, kv, wo, x, g_next.reshape(1, D).astype(F32))


def _top16(x, iota=None, n=None):
    return _top16_many([x], iota, n)[0]


def _top16_many(xs, iota=None, n=None):
    shape = xs[0].shape
    if iota is None:
        n = shape[0]
        iota = lax.broadcasted_iota(jnp.int32, shape, 0).astype(F32)
    cur = list(xs)
    rank = [jnp.full(shape, float(PEER_TOPK), F32) for _ in xs]
    vals = [[] for _ in xs]
    for r in range(PEER_TOPK):
        m = [jnp.max(c, axis=0, keepdims=True) for c in cur]
        idx = [jnp.min(jnp.where(c == mi, iota, float(n)), axis=0, keepdims=True) for c, mi in zip(cur, m)]
        hit = [iota == i for i in idx]
        rank = [jnp.where(h, float(r), rk) for h, rk in zip(hit, rank)]
        cur = [jnp.where(h, -jnp.inf, c) for h, c in zip(hit, cur)]
        for v, mi in zip(vals, m):
            v.append(mi)
    return [(jnp.concatenate(v, axis=0), rk) for v, rk in zip(vals, rank)]


def _peer_route_kernel(q_ref, keys_ref, row_ref, pl_ref):
    K = PEER_TOPK
    q = q_ref[...]
    half = q.shape[1] // 2
    s1 = _dotf_nt(keys_ref[0, 0], q[:, :half])
    s2 = _dotf_nt(keys_ref[0, 1], q[:, half:])
    ht = q.shape[0] // 2
    halves = lambda parts: tuple(jnp.concatenate([a, b], axis=1) for a, b in zip(*parts))
    t16 = _top16_many([s1[:, :ht], s1[:, ht:], s2[:, :ht], s2[:, ht:]])
    (v1, rank1), (v2, rank2) = halves(t16[0:2]), halves(t16[2:4])
    tmq = q.shape[0]
    r8 = lax.broadcasted_iota(jnp.int32, (8, tmq), 0)
    ninf = -jnp.inf
    lo8 = v2[0:8]
    tiles = [
        (v1[0:1] + lo8, r8),
        (v1[0:1] + v2[8:16], 8 + r8),
        (v1[1:2] + lo8, 16 + r8),
        (jnp.where(r8 < 5, v1[2:3] + lo8, ninf), 32 + r8),
        (jnp.where(r8 < 7, jnp.where(r8 < 4, v1[3:4], v1[4:5]) + jnp.where(r8 < 4, lo8, pltpu.roll(lo8, 4, 0)), ninf),
         jnp.where(r8 < 4, 48 + r8, 60 + r8)),
        (jnp.where(r8 < 6, jnp.where(r8 < 2, v1[5:6], jnp.where(r8 < 4, v1[6:7], v1[7:8]))
                   + jnp.where(r8 % 2 == 0, v2[0:1], v2[1:2]), ninf),
         80 + (r8 // 2) * 16 + r8 % 2),
        (v1[8:16] + v2[0:1], (8 + r8) * 16),
    ]
    cand = jnp.concatenate([t for t, _ in tiles], axis=0)
    cidx = jnp.concatenate([i for _, i in tiles], axis=0).astype(F32)
    top_s, crank = halves(_top16_many([cand[:, :ht], cand[:, ht:]], cidx[:, :ht], K * K))
    chosen = jnp.where(crank < float(K), 1.0, 0.0)
    z = jnp.sum(jnp.exp(top_s - top_s[0:1]), axis=0, keepdims=True)
    ch = [chosen[8 * t:8 * (t + 1)] for t in range(len(tiles))]
    colsum = lambda x: jnp.sum(x, axis=0, keepdims=True)
    upper = (r8 < 4).astype(F32)
    j_rows = [colsum(ch[0]) + colsum(ch[1]), colsum(ch[2]), colsum(ch[3]),
              colsum(ch[4] * upper), colsum(ch[4] * (1.0 - upper))]
    j_rows += [ch[5][2 * m:2 * m + 1] + ch[5][2 * m + 1:2 * m + 2] for m in range(3)]
    j_rows += [ch[6][m:m + 1] for m in range(8)]
    jn = jnp.zeros_like(s1)
    for i in range(K):
        jn = jnp.where(rank1 == float(i), j_rows[i], jn)
    row_ref[0, 0] = jnp.exp(s1 - v1[0:1]) / z
    row_ref[0, 1] = jn
    pl_ref[0, 0] = jnp.exp(s2 - v2[0:1]).astype(pl_ref.dtype)
    pl_ref[0, 1] = rank2.astype(pl_ref.dtype)


def peer_route(q, keys, tm=512):
    T = q.shape[0]
    H, _, nkeys, hd = keys.shape
    tm = min(tm, T)
    spec = pl.BlockSpec((1, 2, nkeys, tm), lambda i, h: (h, 0, 0, i))
    return pl.pallas_call(
        _peer_route_kernel,
        grid=(T // tm, H),
        in_specs=[pl.BlockSpec((tm, 2 * hd), lambda i, h: (i, h)),
                  pl.BlockSpec((1, 2, nkeys, hd), lambda i, h: (h, 0, 0, 0))],
        out_specs=[spec, spec],
        out_shape=[jax.ShapeDtypeStruct((H, 2, nkeys, T), F32), jax.ShapeDtypeStruct((H, 2, nkeys, T), BF16)],
        compiler_params=_cparams(("parallel", "parallel")),
        name="peer_route",
    )(q, keys)


def _peer_act_kernel(h_ref, u_ref, row_ref, pl_ref, o_ref):
    nk = PEER_KEYS
    ec = u_ref.shape[0]
    j = pl.program_id(1)
    hu = _dot(u_ref[...], h_ref[...])
    zero = jnp.zeros((), o_ref.dtype)
    for al in range(ec // nk):
        a = j * (ec // nk) + al
        w = None
        for hd in range(row_ref.shape[0]):
            e1 = row_ref[hd, 0, pl.ds(a, 1), :].astype(o_ref.dtype)
            jn = row_ref[hd, 1, pl.ds(a, 1), :].astype(o_ref.dtype)
            g = jnp.where(pl_ref[hd, 1] < jn, e1 * pl_ref[hd, 0], zero)
            w = g if w is None else w + g
        act = _gelu(hu[al * nk:(al + 1) * nk, :].astype(o_ref.dtype)) * w
        o_ref[:, al * nk:(al + 1) * nk] = act.T


def peer_activations(h_t, u_tab, rows, planes, tm=512, ec=1024):
    D, T = h_t.shape
    E = u_tab.shape[0]
    tm, ec = min(tm, T), min(ec, E)
    H, _, nkeys, _ = rows.shape
    rspec = pl.BlockSpec((H, 2, nkeys, tm), lambda i, j: (0, 0, 0, i))
    return pl.pallas_call(
        _peer_act_kernel,
        grid=(T // tm, E // ec),
        in_specs=[pl.BlockSpec((D, tm), lambda i, j: (0, i)),
                  pl.BlockSpec((ec, D), lambda i, j: (j, 0)),
                  rspec, rspec],
        out_specs=pl.BlockSpec((tm, ec), lambda i, j: (i, j)),
        out_shape=jax.ShapeDtypeStruct((T, E), BF16),
        compiler_params=_cparams(("parallel", "arbitrary")),
        name="peer_activations",
    )(h_t, u_tab, rows, planes)


def kernel(x, mem, positions, norm_mix, w_in, nsa_cmp_pe, nsa_cmp_w1, nsa_cmp_w2, rwkv_mu, rwkv_w0, rwkv_w2, rwkv_a0, rwkv_a2, rwkv_g2, rwkv_k_k, rwkv_k_a, rwkv_r_k, rwkv_ln_w, rwkv_ln_b, w_proj_nsa, w_proj_rwkv, w_out, norm_xattn, norm_mem, xattn_wq, xattn_wkv, xattn_wo, norm_ffn, peer_wq, peer_keys, peer_u, peer_v, norm_final):
    B, S, D = x.shape
    T = B * S
    depth = w_in.shape[0]
    G, n, dh = NSA_GROUPS, NSA_HPG, HEAD_DIM
    nq_cols = G * n * dh
    nkv_cols = 6 * G * dh
    ngate = 3 * G * n
    R = rwkv_w0.shape[1]
    c_q, c_kv, c_g = nq_cols, nq_cols + nkv_cols, nq_cols + nkv_cols + ngate
    c_r = c_g + 3 * R + DECAY_LORA + AAA_LORA + GATE_LORA
    gd = G * dh

    pos_flat = positions.reshape(T)
    rope_tabs = _rope_tables(pos_flat)
    cend = jnp.arange(S // CMP_STRIDE) * CMP_STRIDE + (CMP_LEN - 1)
    ctab = _rope_tables(positions[:, jnp.minimum(cend, S - 1)])
    ident = _rope_tables(jnp.zeros_like(positions[:, :S // CMP_STRIDE]))
    cmp_tabs = [jnp.stack([a, b]) for a, b in zip(ctab, ident)]

    xf = x.reshape(T, D)
    for l in range(depth):
        w_t = jnp.swapaxes(w_in, 1, 2)
        pieces = [(0, c_q), (c_q + 2 * gd, gd), (c_q + 4 * gd, gd),
                  (c_q, 2 * gd), (c_q + 3 * gd, gd), (c_q + 5 * gd, gd),
                  (c_g, 3 * R), (c_r, 2 * D)]
        wp = repack_rows(w_t, l, pieces)
        g_rope = (0, c_q + 2 * gd)
        g_plain = (g_rope[1], 4 * gd)
        g_rkv = (g_plain[0] + g_plain[1], 3 * R)
        g_merge = (g_rkv[0] + g_rkv[1], 2 * D)
        w_gate = jnp.pad(w_t[l, c_kv:c_g].reshape(G, 3 * n, D), ((0, 0), (0, LANES - 3 * n), (0, 0)))
        w_gate = w_gate.reshape(G * LANES, D)
        wl = w_t[l, c_g + 3 * R:c_r]
        padl = lambda a, n_: jnp.pad(a, ((0, n_ - a.shape[0]), (0, 0)))
        w_lora = jnp.concatenate([padl(wl[:DECAY_LORA], LANES), padl(wl[DECAY_LORA:DECAY_LORA + AAA_LORA], LANES),
                                  wl[DECAY_LORA + AAA_LORA:]], axis=0)

        h = rmsnorm(xf, norm_mix[l], BF16)
        proj = functools.partial(matmul, h, w_t=True)
        qkr = proj(wp, wcols=g_rope, mode="heads_rope", extras=rope_tabs, out_dtype=BF16, seq=S, name="proj_rope")
        kvc = proj(wp, wcols=(g_plain[0], 2 * gd), mode="heads", out_dtype=BF16, seq=S, name="proj_kvc")
        vt = proj(wp, wcols=(g_plain[0] + 2 * gd, 2 * gd), mode="heads_t", out_dtype=BF16, seq=S, name="proj_vt")
        gates_nsa = proj(w_gate, mode="heads_sigmoid", out_dtype=F32, seq=S, name="proj_gate")
        z_rkv = proj(wp, wcols=g_rkv, out_dtype=F32, name="proj_rkv")
        z_lora = proj(w_lora, out_dtype=F32, name="proj_lora")
        gates_merge = proj(wp, wcols=g_merge, mode="sigmoid", out_dtype=BF16, name="proj_merge")

        pe8 = jnp.broadcast_to(nsa_cmp_pe[l].reshape(2, 1, CMP_LEN * dh), (2, 8, CMP_LEN * dh)).astype(BF16)
        cmp = nsa_compress(kvc, nsa_cmp_w1[l].astype(BF16), nsa_cmp_w2[l].astype(BF16), pe8, cmp_tabs)
        o_nsa = nsa_attention(qkr, vt, cmp, gates_nsa).reshape(T, nq_cols)

        mu = rwkv_mu[l]
        mul = mu[3 * R:]
        pad1 = lambda a, n_: jnp.pad(a, (0, n_ - a.shape[0]))
        mu_lora = jnp.concatenate([pad1(mul[:DECAY_LORA], LANES), pad1(mul[DECAY_LORA:DECAY_LORA + AAA_LORA], LANES),
                                   mul[DECAY_LORA + AAA_LORA:]]).reshape(1, -1)
        feat = rwkv_features(z_lora.reshape(B, S, -1), mu_lora)
        par = jnp.stack([mu[:R], mu[R:2 * R], mu[2 * R:3 * R], rwkv_w0[l], rwkv_a0[l], rwkv_k_k[l], rwkv_k_a[l],
                         rwkv_r_k[l].reshape(R), rwkv_ln_w[l], rwkv_ln_b[l]])
        par = jnp.pad(par, ((0, 16 - par.shape[0]), (0, 0)))
        w2p = jnp.pad(rwkv_w2[l], ((0, LANES - DECAY_LORA), (0, 0)))
        a2p = jnp.pad(rwkv_a2[l], ((0, LANES - AAA_LORA), (0, 0)))
        o_rwkv = rwkv_scan(z_rkv.reshape(B, S, 3 * R), feat, par, w2p, a2p, rwkv_g2[l]).reshape(T, R)

        merged = merge_mixers(o_nsa, o_rwkv, w_proj_nsa[l], w_proj_rwkv[l], gates_merge)
        xf = matmul(merged, w_out[l], mode="residual", extras=(xf,), out_dtype=F32, name="out_proj")

        h = rmsnorm(xf, norm_xattn[l], BF16)
        m = rmsnorm(mem.reshape(-1, D), norm_mem[l], BF16)
        q = matmul(h, xattn_wq[l], out_dtype=BF16, name="xattn_q")
        kvm = matmul(m, xattn_wkv[l], out_dtype=BF16, name="xattn_kv").reshape(B, mem.shape[1], -1)
        xf, h, h_t = cross_attention(q, kvm, xattn_wo[l].astype(BF16), xf, norm_ffn[l], S)

        pq = matmul(h, peer_wq[l], out_dtype=F32, name="peer_q")
        rows, planes = peer_route(pq, peer_keys[l])
        act = peer_activations(h_t, peer_u[l].astype(BF16), rows, planes)
        xf = matmul(act, peer_v[l], mode="residual", extras=(xf,), out_dtype=F32,
                    tm=1024, tn=1024, tk=2048, name="peer_out")

    return rmsnorm(xf, norm_final, F32).reshape(B, S, D)
```

```python
import functools

import jax
import jax.numpy as jnp
from jax import lax
from jax.experimental import pallas as pl
from jax.experimental.pallas import tpu as pltpu

F32 = jnp.float32
BF16 = jnp.bfloat16
HI = lax.Precision.HIGHEST

LANES = 128
HEAD_DIM = 128
ROPE_DIM = HEAD_DIM // 4
ROPE_THETA = 500000.0
RMS_EPS = 1e-6
NEG_INF = -1e30

NSA_GROUPS = 4
NSA_HPG = 4
CMP_LEN = 32
CMP_STRIDE = 16
SEL_BLOCK = 64
SEL_TOP = 16
WINDOW = 512
Q_BLOCK = 128

RWKV_HEAD = 64
DECAY_LORA = 96
AAA_LORA = 96
GATE_LORA = 256
GN_EPS = 64e-5
RWKV_CHUNK = 128

XATTN_HEADS = 4
PEER_HEADS = 8
PEER_KEYS = 128
PEER_TOPK = 16

VMEM_LIMIT = 56 * 1024 * 1024


def _cparams(sem, flags=None):
    return pltpu.CompilerParams(dimension_semantics=sem, vmem_limit_bytes=VMEM_LIMIT, flags=flags)


def _gelu(x):
    c = 0.7978845608028654
    return 0.5 * x * (1.0 + jnp.tanh(x * (c + (0.044715 * c) * (x * x))))


def _softplus(x):
    return jnp.maximum(x, 0.0) + jnp.log(1.0 + jnp.exp(-jnp.abs(x)))


def _dot(a, b):
    return jnp.dot(a, b, preferred_element_type=F32)


def _dot_nt(a, b):
    return lax.dot_general(a, b, (((1,), (1,)), ((), ())), preferred_element_type=F32)


def _dotf_nt(a, b):
    return lax.dot_general(a, b, (((1,), (1,)), ((), ())), preferred_element_type=F32, precision=HI)


def _split(x):
    hi = x.astype(BF16)
    return hi, (x - hi.astype(F32)).astype(BF16)


def _exact(x):
    return x.astype(BF16), None


_DN = {"nn": (((1,), (0,)), ((), ())), "nt": (((1,), (1,)), ((), ())), "tn": (((0,), (0,)), ((), ()))}


def _dot3(a, b, form="nn"):
    d = lambda x, y: lax.dot_general(x, y, _DN[form], preferred_element_type=F32)
    out = d(a[0], b[0])
    if a[1] is not None:
        out = out + d(a[1], b[0])
    if b[1] is not None:
        out = out + d(a[0], b[1])
    return out


def _rope_apply(a, c, sa, sb):
    return a * c + pltpu.roll(a, LANES - ROPE_DIM // 2, 1) * sa + pltpu.roll(a, ROPE_DIM // 2, 1) * sb


def _rope_tables(pos):
    half = ROPE_DIM // 2
    inv = ROPE_THETA ** (-jnp.arange(half, dtype=F32) / half)
    ang = pos.astype(F32)[..., None] * inv
    cos, sin = jnp.cos(ang), jnp.sin(ang)
    shp = pos.shape + (HEAD_DIM - ROPE_DIM,)
    c = jnp.concatenate([cos, cos, jnp.ones(shp, F32)], axis=-1)
    sa = jnp.concatenate([-sin, jnp.zeros(pos.shape + (HEAD_DIM - half,), F32)], axis=-1)
    sb = jnp.concatenate([jnp.zeros(pos.shape + (half,), F32), sin, jnp.zeros(shp, F32)], axis=-1)
    return c, sa, sb


def _rmsnorm_kernel(x_ref, g_ref, o_ref):
    x = x_ref[...].astype(F32)
    ms = jnp.mean(x * x, axis=-1, keepdims=True)
    o_ref[...] = (x * lax.rsqrt(ms + RMS_EPS) * g_ref[...]).astype(o_ref.dtype)


def rmsnorm(x2d, g, out_dtype, tm=256):
    M, D = x2d.shape
    tm = min(tm, M)
    return pl.pallas_call(
        _rmsnorm_kernel,
        grid=(M // tm,),
        in_specs=[pl.BlockSpec((tm, D), lambda i: (i, 0)), pl.BlockSpec((1, D), lambda i: (0, 0))],
        out_specs=pl.BlockSpec((tm, D), lambda i: (i, 0)),
        out_shape=jax.ShapeDtypeStruct((M, D), out_dtype),
        compiler_params=_cparams(("parallel",)),
        name="rmsnorm",
    )(x2d, g.reshape(1, D).astype(F32))


def _rmsnorm_proj_kernel(x_ref, g_ref, w_ref, o_ref):
    x = x_ref[...]
    y = x * lax.rsqrt(jnp.mean(x * x, axis=-1, keepdims=True) + RMS_EPS) * g_ref[...]
    o_ref[...] = _dot(y.astype(BF16), w_ref[...]).astype(o_ref.dtype)


def rmsnorm_proj(x2d, g, w, out_dtype, tm=256):
    M, D = x2d.shape
    N = w.shape[1]
    tm = min(tm, M)
    return pl.pallas_call(
        _rmsnorm_proj_kernel,
        grid=(M // tm,),
        in_specs=[pl.BlockSpec((tm, D), lambda i: (i, 0)), pl.BlockSpec((1, D), lambda i: (0, 0)),
                  pl.BlockSpec((D, N), lambda i: (0, 0))],
        out_specs=pl.BlockSpec((tm, N), lambda i: (i, 0)),
        out_shape=jax.ShapeDtypeStruct((M, N), out_dtype),
        compiler_params=_cparams(("parallel",)),
        name="rmsnorm_proj",
    )(x2d, g.reshape(1, D).astype(F32), w)


def _mm_kernel(*refs, nk, mode, n_extra, w_t):
    x_ref, w_ref = refs[0], refs[1]
    extras = refs[2:2 + n_extra]
    o_ref = refs[2 + n_extra]

    def product():
        w = w_ref[...]
        if w.dtype != x_ref.dtype:
            w = w.astype(x_ref.dtype)
        return _dot_nt(x_ref[...], w) if w_t else _dot(x_ref[...], w)

    def epilogue(acc):
        if mode == "plain":
            o_ref[...] = acc.astype(o_ref.dtype)
        elif mode == "sigmoid":
            o_ref[...] = jax.nn.sigmoid(acc).astype(o_ref.dtype)
        elif mode == "residual":
            o_ref[...] = (extras[0][...] + acc).astype(o_ref.dtype)
        elif mode in ("heads", "heads_sigmoid", "heads_rope"):
            nh = acc.shape[1] // LANES
            for j in range(nh):
                a = acc[:, j * LANES:(j + 1) * LANES]
                if mode == "heads_sigmoid":
                    a = jax.nn.sigmoid(a)
                elif mode == "heads_rope":
                    a = _rope_apply(a, extras[0][...], extras[1][...], extras[2][...])
                o_ref[0, j] = a.astype(o_ref.dtype)
        elif mode == "heads_t":
            for j in range(acc.shape[1] // LANES):
                for c in range(acc.shape[0] // LANES):
                    o_ref[0, j, c] = acc[c * LANES:(c + 1) * LANES, j * LANES:(j + 1) * LANES].T.astype(o_ref.dtype)
        else:
            raise ValueError(mode)

    if nk == 1:
        epilogue(product())
    else:
        acc_ref = refs[-1]
        k = pl.program_id(2)

        @pl.when(k == 0)
        def _():
            acc_ref[...] = jnp.zeros_like(acc_ref)

        acc_ref[...] += product()

        @pl.when(k == nk - 1)
        def _():
            epilogue(acc_ref[...])


def matmul(x, w, *, mode="plain", extras=(), out_dtype=F32, tm=1024, tn=512, tk=None, seq=None, name="mm",
           w_t=False, wcols=None):
    M, K = x.shape
    wcol0, N = (0, w.shape[0 if w_t else 1]) if wcols is None else wcols
    tm, tn = min(tm, M), min(tn, N)
    assert wcol0 % tn == 0
    wblk0 = wcol0 // tn
    if mode.startswith("heads"):
        tm = min(tm, seq)
    tk = K if tk is None else min(tk, K)
    nk = K // tk
    assert M % tm == 0 and N % tn == 0 and K % tk == 0
    grid = (M // tm, N // tn, nk)
    if w_t:
        w_spec = pl.BlockSpec((tn, tk), lambda i, j, k: (wblk0 + j, k))
    else:
        w_spec = pl.BlockSpec((tk, tn), lambda i, j, k: (k, wblk0 + j))
    in_specs = [pl.BlockSpec((tm, tk), lambda i, j, k: (i, k)), w_spec]
    if mode == "residual":
        in_specs.append(pl.BlockSpec((tm, tn), lambda i, j, k: (i, j)))
    elif mode == "heads_rope":
        in_specs += [pl.BlockSpec((tm, LANES), lambda i, j, k: (i, 0))] * 3
    if mode == "heads_t":
        assert seq % tm == 0 and tn % LANES == 0 and tm % LANES == 0
        spb = seq // tm
        out_shape = jax.ShapeDtypeStruct((M // seq, N // LANES, seq // LANES, LANES, LANES), out_dtype)
        out_spec = pl.BlockSpec((1, tn // LANES, tm // LANES, LANES, LANES),
                                lambda i, j, k: (i // spb, j, i % spb, 0, 0))
    elif mode.startswith("heads"):
        assert seq % tm == 0 and tn % LANES == 0
        spb = seq // tm
        out_shape = jax.ShapeDtypeStruct((M // seq, N // LANES, seq, LANES), out_dtype)
        out_spec = pl.BlockSpec((1, tn // LANES, tm, LANES), lambda i, j, k: (i // spb, j, i % spb, 0))
    else:
        out_shape = jax.ShapeDtypeStruct((M, N), out_dtype)
        out_spec = pl.BlockSpec((tm, tn), lambda i, j, k: (i, j))
    scratch = [pltpu.VMEM((tm, tn), F32)] if nk > 1 else []
    return pl.pallas_call(
        functools.partial(_mm_kernel, nk=nk, mode=mode, n_extra=len(extras), w_t=w_t),
        grid=grid,
        in_specs=in_specs,
        out_specs=out_spec,
        out_shape=out_shape,
        scratch_shapes=scratch,
        compiler_params=_cparams(("parallel", "parallel", "arbitrary")),
        name=name,
    )(x, w, *extras)


SUBLANES = 8


def _repack_kernel(tab_ref, w_ref, o_ref):
    o_ref[...] = w_ref[0].astype(o_ref.dtype)


def repack_rows(wt3, l, pieces, tr=512):
    _, NC, D = wt3.shape
    assert all(r0 % SUBLANES == 0 and nrow % tr == 0 for r0, nrow in pieces)
    starts = [r0 + r for r0, nrow in pieces for r in range(0, nrow, tr)]
    tab = jnp.asarray([s // SUBLANES for s in starts], jnp.int32)
    grid_spec = pltpu.PrefetchScalarGridSpec(
        num_scalar_prefetch=1, grid=(len(starts),),
        in_specs=[pl.BlockSpec((pl.Element(1), pl.Element(tr), pl.Element(D)),
                               lambda j, t: (l, t[j] * SUBLANES, 0))],
        out_specs=pl.BlockSpec((tr, D), lambda j, t: (j, 0)))
    return pl.pallas_call(
        _repack_kernel, grid_spec=grid_spec, out_shape=jax.ShapeDtypeStruct((len(starts) * tr, D), BF16),
        compiler_params=_cparams(("arbitrary",)), name="repack_w_in",
    )(tab, wt3)


def _compress_kernel(x_ref, w1_ref, w2_ref, pe_ref, c_ref, sa_ref, sb_ref, o_ref):
    half = x_ref.shape[-1]
    x = x_ref[0, 0]
    a = _dot(x, w1_ref[0, :half, :])
    b = _dot(x, w1_ref[0, half:, :])
    pet = _dot(pe_ref[0], w1_ref[0])
    n = a.shape[0]
    hid = a + pltpu.roll(b, n - 1, 0) + pet[0:1]
    out = _dot(_gelu(hid).astype(BF16), w2_ref[0])
    out = _rope_apply(out, c_ref[0, 0], sa_ref[0, 0], sb_ref[0, 0])
    o_ref[0, 0, 0] = out.astype(o_ref.dtype)


def nsa_compress(kvp, w1, w2, pe8, tabs):
    B, _, S, dh = kvp.shape
    G = NSA_GROUPS
    nch = S // CMP_STRIDE
    xv = kvp.reshape(B, kvp.shape[1], nch, CMP_STRIDE * dh)
    hidden = w1.shape[-1]
    tab_spec = pl.BlockSpec((1, 1, nch, LANES), lambda w, b, g: (w, b, 0, 0))
    return pl.pallas_call(
        _compress_kernel,
        grid=(2, B, G),
        in_specs=[
            pl.BlockSpec((1, 1, nch, CMP_STRIDE * dh), lambda w, b, g: (b, w * G + g, 0, 0)),
            pl.BlockSpec((1, CMP_LEN * dh, hidden), lambda w, b, g: (w, 0, 0)),
            pl.BlockSpec((1, hidden, dh), lambda w, b, g: (w, 0, 0)),
            pl.BlockSpec((1, 8, CMP_LEN * dh), lambda w, b, g: (w, 0, 0)),
            tab_spec, tab_spec, tab_spec,
        ],
        out_specs=pl.BlockSpec((1, 1, 1, nch, dh), lambda w, b, g: (w, b, g, 0, 0)),
        out_shape=jax.ShapeDtypeStruct((2, B, G, nch, dh), BF16),
        compiler_params=_cparams(("parallel", "parallel", "parallel")),
        name="nsa_compress",
    )(xv, w1, w2, pe8, *tabs)


def _dot_tn(a, b):
    return lax.dot_general(a, b, (((0,), (0,)), ((), ())), preferred_element_type=F32)


NSA_SPLIT = 2


def _nsa_kernel(q_ref, ks_ref, kw_ref, vs_ref, vw_ref, kc_ref, vc_ref, gt_ref, cmap_ref, e_ref, o_ref, *, ck):
    n = NSA_HPG
    ns = NSA_SPLIT
    hps = n // ns
    sq = hps * Q_BLOCK
    streams = range(ns)
    qb = pl.program_id(2)
    t0 = qb * Q_BLOCK
    c2 = (HEAD_DIM ** -0.5) * 1.4426950408889634
    nsel = ks_ref.shape[2] // SEL_BLOCK
    q_t = [jnp.concatenate([q_ref[0, s * hps + h].T for h in range(hps)], axis=1) for s in streams]
    tq = t0 + lax.broadcasted_iota(jnp.int32, (1, Q_BLOCK), 1)

    def heads(x):
        return jnp.concatenate([x] * hps, axis=1)

    def scores(k, ok_t):
        bias = heads(jnp.where(ok_t, 0.0, NEG_INF))
        return [_dot(k, q_t[s]) + bias for s in streams]

    def values(vt_ref, tile0, ntile):
        vt = jnp.concatenate([vt_ref[0, 0, tile0 + i] for i in range(ntile)], axis=1)
        return jnp.concatenate([vt, jnp.ones(vt.shape, vt.dtype)], axis=0)

    kc = kc_ref[0, 0, 0]
    vc = vc_ref[0, 0, 0]
    ncmp = kc.shape[0]
    crow = lax.broadcasted_iota(jnp.int32, (ncmp, Q_BLOCK), 0)
    sb = scores(kc, (crow < ncmp - 1) & (crow * CMP_STRIDE + (CMP_LEN - 1) <= tq))
    p = [jnp.exp2((sb[s] - jnp.max(sb[s], axis=0, keepdims=True)) * c2) for s in streams]
    anyv = heads((tq >= CMP_LEN - 1).astype(F32))
    p = [p[s] * (anyv / jnp.sum(p[s], axis=0, keepdims=True)) for s in streams]
    o_c = [_dot_tn(vc, p[s].astype(BF16)) for s in streams]

    psum = None
    for s in streams:
        for h in range(hps):
            ph = p[s][:, h * Q_BLOCK:(h + 1) * Q_BLOCK]
            psum = ph if psum is None else psum + ph
    hi = psum.astype(BF16)
    lo = (psum - hi.astype(F32)).astype(BF16)
    imp = _dot(cmap_ref[...], hi) + _dot(cmap_ref[...], lo)
    jr = lax.broadcasted_iota(jnp.int32, (LANES, Q_BLOCK), 0)
    cur = tq // SEL_BLOCK
    imp = jnp.where(jr == 0, 1e6, imp)
    imp = jnp.where(jr == cur, 1e6, imp)
    imp = jnp.where(jr == cur - 1, 1e6, imp)
    imp = jnp.where(jr > cur, -1e6, imp)

    nrow = min(((nsel + 7) // 8) * 8, LANES)
    imp = jnp.where(jr >= nsel, -3e6, imp)[0:nrow]
    jrr = jr[0:nrow]
    rank = jnp.zeros((nrow, Q_BLOCK), F32)
    for i in range(min(nsel, LANES)):
        row = imp[i:i + 1, :]
        rank = rank + jnp.where(row > imp, 1.0, jnp.where(row == imp, jnp.where(jrr > i, 1.0, 0.0), 0.0))
    sel = jnp.where(rank < float(SEL_TOP), 1.0, 0.0)
    if nrow < LANES:
        sel = jnp.concatenate([sel, jnp.zeros((LANES - nrow, Q_BLOCK), F32)], axis=0)
    sel = sel.astype(BF16)

    n_ch = (t0 + Q_BLOCK + ck - 1) // ck
    krow = lax.broadcasted_iota(jnp.int32, (ck, Q_BLOCK), 0)

    def body(kb, carry):
        m_i, acc = carry[:ns], carry[ns:]
        k0 = pl.multiple_of(kb * ck, ck)
        kblk = ks_ref[0, 0, pl.ds(k0, ck), :]
        v1 = values(vs_ref, kb * (ck // LANES), ck // LANES)
        em = _dot(e_ref[kb], sel)
        sb = scores(kblk, (k0 + krow <= tq) & (em > 0.5))
        m_new = [jnp.maximum(m_i[s], jnp.max(sb[s], axis=0, keepdims=True)) for s in streams]
        alpha = [jnp.exp2((m_i[s] - m_new[s]) * c2) for s in streams]
        pp = [jnp.exp2(((sb[s] - m_new[s]) * c2).astype(BF16)) for s in streams]
        acc = [alpha[s] * acc[s] + _dot(v1, pp[s]) for s in streams]
        return tuple(m_new) + tuple(acc)

    init = tuple(jnp.full((1, sq), NEG_INF, F32) for _ in streams) + tuple(
        jnp.zeros((2 * HEAD_DIM, sq), F32) for _ in streams)
    acc_f = lax.fori_loop(0, n_ch, body, init)[ns:]
    o_s = [a[0:HEAD_DIM] / a[HEAD_DIM:HEAD_DIM + 1] for a in acc_f]

    wlen = min(WINDOW + Q_BLOCK, kw_ref.shape[2])
    w0 = pl.multiple_of(jnp.maximum(t0 + Q_BLOCK - wlen, 0), Q_BLOCK)
    kblk = kw_ref[0, 0, pl.ds(w0, wlen), :]
    v1 = values(vw_ref, w0 // LANES, wlen // LANES)
    kpos = w0 + lax.broadcasted_iota(jnp.int32, (wlen, Q_BLOCK), 0)
    sb = scores(kblk, (kpos <= tq) & (tq - kpos < WINDOW))
    pw = [jnp.exp2(((sb[s] - jnp.max(sb[s], axis=0, keepdims=True)) * c2).astype(BF16)) for s in streams]
    acc_w = [_dot(v1, pw[s]) for s in streams]
    o_w = [a[0:HEAD_DIM] / a[HEAD_DIM:HEAD_DIM + 1] for a in acc_w]

    gt = gt_ref[0, 0].T
    for h in range(n):
        s = h // hps
        r = slice((h % hps) * Q_BLOCK, (h % hps + 1) * Q_BLOCK)
        o = (gt[3 * h:3 * h + 1] * o_c[s][:, r] + gt[3 * h + 1:3 * h + 2] * o_s[s][:, r]
             + gt[3 * h + 2:3 * h + 3] * o_w[s][:, r])
        o_ref[0, :, h * HEAD_DIM:(h + 1) * HEAD_DIM] = o.T.astype(o_ref.dtype)


def nsa_attention(qkr, vt, cmp, gates, ck=1024):
    B, _, S, dh = qkr.shape
    G, n = NSA_GROUPS, NSA_HPG
    ncmp = cmp.shape[3]
    nsel = S // SEL_BLOCK
    ck = min(ck, S)
    r_sel, r_cmp = SEL_BLOCK // CMP_STRIDE, CMP_LEN // CMP_STRIDE
    tgt = (r_sel * jnp.arange(nsel)[:, None, None] + jnp.arange(r_sel)[None, :, None]
           - jnp.arange(r_cmp)[None, None, :])
    cmap = (jnp.arange(ncmp - 1)[:, None, None, None] == tgt[None]).sum((2, 3)).astype(F32)
    cmap = jnp.pad(cmap, ((0, 1), (0, LANES - nsel))).astype(BF16).T
    e = (jnp.arange(S)[:, None] // SEL_BLOCK == jnp.arange(LANES)[None, :]).astype(BF16)
    e3 = e.reshape(S // ck, ck, LANES)

    def head(h0):
        return pl.BlockSpec((1, 1, S, dh), lambda b, g, i: (b, h0 + g, 0, 0))

    def head_t(h0):
        return pl.BlockSpec((1, 1, S // LANES, dh, LANES), lambda b, g, i: (b, h0 + g, 0, 0, 0))

    return pl.pallas_call(
        functools.partial(_nsa_kernel, ck=ck),
        grid=(B, G, S // Q_BLOCK),
        in_specs=[
            pl.BlockSpec((1, n, Q_BLOCK, dh), lambda b, g, i: (b, g, i, 0)),
            head(16), head(20), head_t(0), head_t(G),
            pl.BlockSpec((1, 1, 1, ncmp, dh), lambda b, g, i: (0, b, g, 0, 0)),
            pl.BlockSpec((1, 1, 1, ncmp, dh), lambda b, g, i: (1, b, g, 0, 0)),
            pl.BlockSpec((1, 1, Q_BLOCK, LANES), lambda b, g, i: (b, g, i, 0)),
            pl.BlockSpec((LANES, ncmp), lambda b, g, i: (0, 0)),
            pl.BlockSpec((S // ck, ck, LANES), lambda b, g, i: (0, 0, 0)),
        ],
        out_specs=pl.BlockSpec((1, Q_BLOCK, n * dh), lambda b, g, i: (b, i, g)),
        out_shape=jax.ShapeDtypeStruct((B, S, G * n * dh), BF16),
        compiler_params=_cparams(("parallel", "parallel", "arbitrary")),
        name="nsa_attention",
    )(qkr, qkr, qkr, vt, vt, cmp, cmp, gates, cmap, e3)


def _rwkv_feat_kernel(z_ref, zp_ref, mu_ref, o_ref):
    i = pl.program_id(1)
    z = z_ref[0]
    prev = zp_ref[0, 7:8, :] * (i > 0).astype(F32)
    row = lax.broadcasted_iota(jnp.int32, z.shape, 0)
    sh = jnp.where(row == 0, prev, pltpu.roll(z, 1, 0))
    zs = z + (sh - z) * mu_ref[...]
    o_ref[0, :, 0:128] = jnp.tanh(zs[:, 0:128])
    o_ref[0, :, 128:256] = zs[:, 128:256]
    o_ref[0, :, 256:] = jax.nn.sigmoid(zs[:, 256:])


def rwkv_features(z_lora, mu_lora, tm=512):
    B, S, W = z_lora.shape
    tm = min(tm, S)
    return pl.pallas_call(
        _rwkv_feat_kernel,
        grid=(B, S // tm),
        in_specs=[pl.BlockSpec((1, tm, W), lambda b, i: (b, i, 0)),
                  pl.BlockSpec((1, 8, W), lambda b, i: (b, jnp.maximum(i * (tm // 8) - 1, 0), 0)),
                  pl.BlockSpec((1, W), lambda b, i: (0, 0))],
        out_specs=pl.BlockSpec((1, tm, W), lambda b, i: (b, i, 0)),
        out_shape=jax.ShapeDtypeStruct((B, S, W), F32),
        compiler_params=_cparams(("parallel", "arbitrary")),
        name="rwkv_features",
    )(z_lora, z_lora, mu_lora)


def _rwkv_kernel(r_ref, k_ref, v_ref, f_ref, par_ref, w2_ref, a2_ref, g2_ref, o_ref, st_ref, prev_ref):
    C = RWKV_CHUNK
    W = r_ref.shape[-1]
    npair = W // LANES
    c = pl.program_id(2)

    @pl.when(c == 0)
    def _():
        st_ref[...] = jnp.zeros_like(st_ref)
        prev_ref[...] = jnp.zeros_like(prev_ref)

    par = par_ref[...]
    mu_r, mu_k, mu_v = par[0:1], par[1:2], par[2:3]
    w0, a0, k_k, k_a, r_k, ln_w, ln_b = par[3:4], par[4:5], par[5:6], par[6:7], par[7:8], par[8:9], par[9:10]

    row = lax.broadcasted_iota(jnp.int32, (C, C), 0)
    col = lax.broadcasted_iota(jnp.int32, (C, C), 1)
    trow = lax.broadcasted_iota(jnp.int32, (C, W), 0)
    lane = lax.broadcasted_iota(jnp.int32, (C, LANES), 1)
    lr = lax.broadcasted_iota(jnp.int32, (LANES, LANES), 0)
    lc = lax.broadcasted_iota(jnp.int32, (LANES, LANES), 1)

    def shifted(z_ref, idx, mu):
        z = z_ref[0]
        sh = jnp.where(trow == 0, prev_ref[idx:idx + 1, :], pltpu.roll(z, 1, 0))
        prev_ref[idx:idx + 1, :] = z[C - 1:C, :]
        return z + (sh - z) * mu

    r = shifted(r_ref, 0, mu_r)
    k = shifted(k_ref, 1, mu_k)
    v = shifted(v_ref, 2, mu_v)
    f = f_ref[0]
    w_log = -_softplus(-(w0 + _dot3(_split(f[:, 0:128]), _split(w2_ref[...])))) - 0.5
    logd = -jnp.exp(w_log)
    a = jax.nn.sigmoid(a0 + _dot3(_split(f[:, 128:256]), _split(a2_ref[...])))
    g = _dot3(_split(f[:, 256:]), _split(g2_ref[...]))

    bd = _exact(jnp.where((lr // RWKV_HEAD) == (lc // RWKV_HEAD), 1.0, 0.0))

    def head_sum(x, passes=2):
        sp = _split if passes == 2 else (lambda z: (z.astype(BF16), None))
        return jnp.concatenate([_dot3(sp(x[:, p * LANES:(p + 1) * LANES]), bd) for p in range(npair)], axis=1)

    kk = k * k_k
    kk = kk / jnp.maximum(jnp.sqrt(head_sum(kk * kk)), 1e-12)
    k2 = k * (1.0 + (a - 1.0) * k_a)

    tri_incl = _exact(jnp.where(col <= row, 1.0, 0.0))
    cum = _dot3(tri_incl, _split(logd))
    cum_end = cum[C - 1:C, :]
    e_neg = jnp.exp(-cum)
    kka = kk * a
    a_t = -kk * jnp.exp(cum - logd)
    b_t = kka * e_neg
    k_t = k2 * e_neg
    r_t = r * jnp.exp(cum)
    e_end = jnp.exp(cum_end - cum)
    b_e = kka * e_end
    k_e = k2 * e_end
    p_end = jnp.exp(cum_end)

    strict = col < row
    incl = col <= row
    eye = jnp.where(col == row, 1.0, 0.0)
    n_sq = C.bit_length() - 2
    hpp = LANES // RWKV_HEAD
    heads = [(p, h) for p in range(npair) for h in range(hpp)]
    hd = []
    for p, h in heads:
        sl = slice(p * LANES, (p + 1) * LANES)
        mh = (lane // RWKV_HEAD) == h
        s_am = _split(jnp.where(mh, a_t[:, sl], 0.0))
        s_rm = _split(jnp.where(mh, r_t[:, sl], 0.0))
        hd.append(dict(
            sl=sl, s_am=s_am, s_rm=s_rm,
            s_vm=_split(jnp.where(mh, v[:, sl], 0.0)),
            s_bem=_split(jnp.where(mh, b_e[:, sl], 0.0)),
            s_kem=_split(jnp.where(mh, k_e[:, sl], 0.0)),
            rhs=_split(jnp.concatenate([b_t[:, sl], k_t[:, sl]], axis=0)) if h == 0 else hd[-1]["rhs"]))
    for d in hd:
        amat = _dot3(d["s_am"], d["rhs"], "nt")
        rmat = _dot_nt(d["s_rm"][0], d["rhs"][0])
        l_ab = jnp.where(strict, amat[:, 0:C], 0.0)
        d["l_ak"] = _split(jnp.where(strict, amat[:, C:2 * C], 0.0))
        d["l_rb"] = jnp.where(incl, rmat[:, 0:C], 0.0).astype(BF16)
        d["l_rk"] = jnp.where(incl, rmat[:, C:2 * C], 0.0).astype(BF16)
        d["x"] = eye + l_ab
        d["s_pw"] = _split(l_ab)
    for i in range(n_sq):
        for d in hd:
            d["s_pw"] = _split(_dot3(d["s_pw"], d["s_pw"])) if i == 0 else (_dot(d["s_pw"][0], d["s_pw"][0]).astype(BF16), None)
        for d in hd:
            d["x"] = d["x"] + (_dot3(_split(d["x"]), d["s_pw"]) if i == 0 else _dot(d["x"].astype(BF16), d["s_pw"][0]))
    cat = lambda parts, axis: tuple(None if parts[0][i] is None else jnp.concatenate([p[i] for p in parts], axis=axis)
                                    for i in range(2))
    for i, d in enumerate(hd):
        d["st"] = st_ref[i]
        d["s_st"] = _split(d["st"])
        d["t"] = _split(_dot3(cat([d["s_am"], d["l_ak"]], 1), cat([d["s_st"], d["s_vm"]], 0)))
    for d in hd:
        d["s_u"] = _split(_dot3(_split(d["x"]), d["t"]))
    for i, d in enumerate(hd):
        if i % hpp == 0:
            decay = jnp.broadcast_to(p_end[:, d["sl"]], (LANES, LANES)).T
        st_ref[i] = d["st"] * decay + _dot3(cat([d["s_bem"], d["s_kem"]], 0), cat([d["s_u"], d["s_vm"]], 0), "tn")
    ys = []
    for d in hd:
        ys.append(_dot(jnp.concatenate([d["s_rm"][0], d["l_rb"], d["l_rk"]], axis=1),
                       jnp.concatenate([d["s_st"][0], d["s_u"][0], d["s_vm"][0]], axis=0)))
    y = jnp.concatenate([sum(ys[hpp * p + 1:hpp * (p + 1)], ys[hpp * p]) for p in range(npair)], axis=1)

    inv_n = 1.0 / RWKV_HEAD
    mean = head_sum(y) * inv_n
    d = y - mean
    var = head_sum(d * d) * inv_n
    yn = d * lax.rsqrt(var + GN_EPS) * ln_w + ln_b
    bonus = head_sum(r * k2 * r_k, passes=1) * v
    o_ref[0] = ((yn + bonus) * g).astype(o_ref.dtype)


def rwkv_scan(z_rkv, feat, par, w2p, a2p, g2, pairs_per_step=8):
    B, S, R3 = z_rkv.shape
    R = R3 // 3
    W = pairs_per_step * LANES
    nblk = R // W
    C = RWKV_CHUNK
    assert S % C == 0 and R % W == 0

    def col(off):
        return pl.BlockSpec((1, C, W), lambda b, p, c: (b, c, off + p))

    return pl.pallas_call(
        _rwkv_kernel,
        grid=(B, nblk, S // C),
        in_specs=[
            col(0), col(nblk), col(2 * nblk),
            pl.BlockSpec((1, C, feat.shape[-1]), lambda b, p, c: (b, c, 0)),
            pl.BlockSpec((16, W), lambda b, p, c: (0, p)),
            pl.BlockSpec((LANES, W), lambda b, p, c: (0, p)),
            pl.BlockSpec((LANES, W), lambda b, p, c: (0, p)),
            pl.BlockSpec((GATE_LORA, W), lambda b, p, c: (0, p)),
        ],
        out_specs=pl.BlockSpec((1, C, W), lambda b, p, c: (b, c, p)),
        out_shape=jax.ShapeDtypeStruct((B, S, R), BF16),
        scratch_shapes=[pltpu.VMEM((W // RWKV_HEAD, LANES, LANES), F32), pltpu.VMEM((8, W), F32)],
        compiler_params=_cparams(("parallel", "parallel", "arbitrary")),
        name="rwkv_scan",
    )(z_rkv, z_rkv, z_rkv, feat, par, w2p, a2p, g2)


def _merge_kernel(on_ref, or_ref, wn_ref, wr_ref, ga_ref, gb_ref, o_ref):
    pa = _dot(on_ref[...], wn_ref[...].astype(on_ref.dtype))
    pb = _dot(or_ref[...], wr_ref[...].astype(or_ref.dtype))
    o_ref[...] = (ga_ref[...].astype(F32) * pa + gb_ref[...].astype(F32) * pb).astype(o_ref.dtype)


def merge_mixers(o_nsa, o_rwkv, wn, wr, gates, tm=1024, tn=512):
    M, Kn = o_nsa.shape
    Kr = o_rwkv.shape[1]
    D = wn.shape[1]
    tm, tn = min(tm, M), min(tn, D)
    nb = D // tn
    return pl.pallas_call(
        _merge_kernel,
        grid=(M // tm, nb),
        in_specs=[
            pl.BlockSpec((tm, Kn), lambda i, j: (i, 0)),
            pl.BlockSpec((tm, Kr), lambda i, j: (i, 0)),
            pl.BlockSpec((Kn, tn), lambda i, j: (0, j)),
            pl.BlockSpec((Kr, tn), lambda i, j: (0, j)),
            pl.BlockSpec((tm, tn), lambda i, j: (i, j)),
            pl.BlockSpec((tm, tn), lambda i, j: (i, nb + j)),
        ],
        out_specs=pl.BlockSpec((tm, tn), lambda i, j: (i, j)),
        out_shape=jax.ShapeDtypeStruct((M, D), BF16),
        compiler_params=_cparams(("parallel", "parallel")),
        name="merge_mixers",
    )(o_nsa, o_rwkv, wn, wr, gates, gates)


def _xattn_kernel(q_ref, kv_ref, wo_ref, x_ref, g_ref, o_ref, h_ref, ht_ref):
    nh = XATTN_HEADS
    scale = HEAD_DIM ** -0.5
    outs = []
    for h in range(nh):
        q = q_ref[:, h * HEAD_DIM:(h + 1) * HEAD_DIM]
        k = kv_ref[0, :, h * HEAD_DIM:(h + 1) * HEAD_DIM]
        v = kv_ref[0, :, (nh + h) * HEAD_DIM:(nh + h + 1) * HEAD_DIM]
        s = _dot_nt(q, k) * scale
        m = jnp.max(s, axis=-1, keepdims=True)
        p = jnp.exp(s - m)
        p = p / jnp.sum(p, axis=-1, keepdims=True)
        outs.append(_dot(p.astype(BF16), v).astype(BF16))
    o = jnp.concatenate(outs, axis=-1)
    x = x_ref[...] + _dot(o, wo_ref[...])
    o_ref[...] = x
    y = x * lax.rsqrt(jnp.mean(x * x, axis=-1, keepdims=True) + RMS_EPS) * g_ref[...]
    h_ref[...] = y.astype(h_ref.dtype)
    ht_ref[...] = y.T.astype(ht_ref.dtype)


def cross_attention(q, kv, wo, x, g_next, seq, tm=256):
    T, D = x.shape
    tm = min(tm, seq)
    spb = seq // tm
    Mm = kv.shape[1]
    row = pl.BlockSpec((tm, D), lambda i: (i, 0))
    return pl.pallas_call(
        _xattn_kernel,
        grid=(T // tm,),
        in_specs=[
            pl.BlockSpec((tm, q.shape[1]), lambda i: (i, 0)),
            pl.BlockSpec((1, Mm, kv.shape[2]), lambda i: (i // spb, 0, 0)),
            pl.BlockSpec(wo.shape, lambda i: (0, 0)),
            row,
            pl.BlockSpec((1, D), lambda i: (0, 0)),
        ],
        out_specs=[row, row, pl.BlockSpec((D, tm), lambda i: (0, i))],
        out_shape=[jax.ShapeDtypeStruct((T, D), F32), jax.ShapeDtypeStruct((T, D), BF16),
                   jax.ShapeDtypeStruct((D, T), BF16)],
        compiler_params=_cparams(("parallel",)),
        name="cross_attention",
    )(q, kv, wo, x, g_next.reshape(1, D).astype(F32))


def _top16(x, iota=None, n=None):
    return _top16_many([x], iota, n)[0]


def _top16_many(xs, iota=None, n=None):
    shape = xs[0].shape
    if iota is None:
        n = shape[0]
        iota = lax.broadcasted_iota(jnp.int32, shape, 0).astype(F32)
    cur = list(xs)
    rank = [jnp.full(shape, float(PEER_TOPK), F32) for _ in xs]
    vals = [[] for _ in xs]
    for r in range(PEER_TOPK):
        m = [jnp.max(c, axis=0, keepdims=True) for c in cur]
        idx = [jnp.min(jnp.where(c == mi, iota, float(n)), axis=0, keepdims=True) for c, mi in zip(cur, m)]
        hit = [iota == i for i in idx]
        rank = [jnp.where(h, float(r), rk) for h, rk in zip(hit, rank)]
        cur = [jnp.where(h, -jnp.inf, c) for h, c in zip(hit, cur)]
        for v, mi in zip(vals, m):
            v.append(mi)
    return [(jnp.concatenate(v, axis=0), rk) for v, rk in zip(vals, rank)]


def _peer_route_kernel(q_ref, keys_ref, row_ref, pl_ref):
    K = PEER_TOPK
    q = q_ref[...]
    half = q.shape[1] // 2
    s1 = _dotf_nt(keys_ref[0, 0], q[:, :half])
    s2 = _dotf_nt(keys_ref[0, 1], q[:, half:])
    ht = q.shape[0] // 2
    halves = lambda parts: tuple(jnp.concatenate([a, b], axis=1) for a, b in zip(*parts))
    t16 = _top16_many([s1[:, :ht], s1[:, ht:], s2[:, :ht], s2[:, ht:]])
    (v1, rank1), (v2, rank2) = halves(t16[0:2]), halves(t16[2:4])
    tmq = q.shape[0]
    r8 = lax.broadcasted_iota(jnp.int32, (8, tmq), 0)
    ninf = -jnp.inf
    lo8 = v2[0:8]
    tiles = [
        (v1[0:1] + lo8, r8),
        (v1[0:1] + v2[8:16], 8 + r8),
        (v1[1:2] + lo8, 16 + r8),
        (jnp.where(r8 < 5, v1[2:3] + lo8, ninf), 32 + r8),
        (jnp.where(r8 < 7, jnp.where(r8 < 4, v1[3:4], v1[4:5]) + jnp.where(r8 < 4, lo8, pltpu.roll(lo8, 4, 0)), ninf),
         jnp.where(r8 < 4, 48 + r8, 60 + r8)),
        (jnp.where(r8 < 6, jnp.where(r8 < 2, v1[5:6], jnp.where(r8 < 4, v1[6:7], v1[7:8]))
                   + jnp.where(r8 % 2 == 0, v2[0:1], v2[1:2]), ninf),
         80 + (r8 // 2) * 16 + r8 % 2),
        (v1[8:16] + v2[0:1], (8 + r8) * 16),
    ]
    cand = jnp.concatenate([t for t, _ in tiles], axis=0)
    cidx = jnp.concatenate([i for _, i in tiles], axis=0).astype(F32)
    top_s, crank = halves(_top16_many([cand[:, :ht], cand[:, ht:]], cidx[:, :ht], K * K))
    chosen = jnp.where(crank < float(K), 1.0, 0.0)
    z = jnp.sum(jnp.exp(top_s - top_s[0:1]), axis=0, keepdims=True)
    ch = [chosen[8 * t:8 * (t + 1)] for t in range(len(tiles))]
    colsum = lambda x: jnp.sum(x, axis=0, keepdims=True)
    upper = (r8 < 4).astype(F32)
    j_rows = [colsum(ch[0]) + colsum(ch[1]), colsum(ch[2]), colsum(ch[3]),
              colsum(ch[4] * upper), colsum(ch[4] * (1.0 - upper))]
    j_rows += [ch[5][2 * m:2 * m + 1] + ch[5][2 * m + 1:2 * m + 2] for m in range(3)]
    j_rows += [ch[6][m:m + 1] for m in range(8)]
    jn = jnp.zeros_like(s1)
    for i in range(K):
        jn = jnp.where(rank1 == float(i), j_rows[i], jn)
    row_ref[0, 0] = jnp.exp(s1 - v1[0:1]) / z
    row_ref[0, 1] = jn
    pl_ref[0, 0] = jnp.exp(s2 - v2[0:1]).astype(pl_ref.dtype)
    pl_ref[0, 1] = rank2.astype(pl_ref.dtype)


def peer_route(q, keys, tm=512):
    T = q.shape[0]
    H, _, nkeys, hd = keys.shape
    tm = min(tm, T)
    spec = pl.BlockSpec((1, 2, nkeys, tm), lambda i, h: (h, 0, 0, i))
    return pl.pallas_call(
        _peer_route_kernel,
        grid=(T // tm, H),
        in_specs=[pl.BlockSpec((tm, 2 * hd), lambda i, h: (i, h)),
                  pl.BlockSpec((1, 2, nkeys, hd), lambda i, h: (h, 0, 0, 0))],
        out_specs=[spec, spec],
        out_shape=[jax.ShapeDtypeStruct((H, 2, nkeys, T), F32), jax.ShapeDtypeStruct((H, 2, nkeys, T), BF16)],
        compiler_params=_cparams(("parallel", "parallel")),
        name="peer_route",
    )(q, keys)


def _peer_act_kernel(h_ref, u_ref, row_ref, pl_ref, o_ref):
    nk = PEER_KEYS
    ec = u_ref.shape[0]
    j = pl.program_id(1)
    hu = _dot(u_ref[...], h_ref[...])
    zero = jnp.zeros((), o_ref.dtype)
    for al in range(ec // nk):
        a = j * (ec // nk) + al
        w = None
        for hd in range(row_ref.shape[0]):
            e1 = row_ref[hd, 0, pl.ds(a, 1), :].astype(o_ref.dtype)
            jn = row_ref[hd, 1, pl.ds(a, 1), :].astype(o_ref.dtype)
            g = jnp.where(pl_ref[hd, 1] < jn, e1 * pl_ref[hd, 0], zero)
            w = g if w is None else w + g
        act = _gelu(hu[al * nk:(al + 1) * nk, :].astype(o_ref.dtype)) * w
        o_ref[:, al * nk:(al + 1) * nk] = act.T


def peer_activations(h_t, u_tab, rows, planes, tm=512, ec=1024):
    D, T = h_t.shape
    E = u_tab.shape[0]
    tm, ec = min(tm, T), min(ec, E)
    H, _, nkeys, _ = rows.shape
    rspec = pl.BlockSpec((H, 2, nkeys, tm), lambda i, j: (0, 0, 0, i))
    return pl.pallas_call(
        _peer_act_kernel,
        grid=(T // tm, E // ec),
        in_specs=[pl.BlockSpec((D, tm), lambda i, j: (0, i)),
                  pl.BlockSpec((ec, D), lambda i, j: (j, 0)),
                  rspec, rspec],
        out_specs=pl.BlockSpec((tm, ec), lambda i, j: (i, j)),
        out_shape=jax.ShapeDtypeStruct((T, E), BF16),
        compiler_params=_cparams(("parallel", "arbitrary")),
        name="peer_activations",
    )(h_t, u_tab, rows, planes)


def kernel(x, mem, positions, norm_mix, w_in, nsa_cmp_pe, nsa_cmp_w1, nsa_cmp_w2, rwkv_mu, rwkv_w0, rwkv_w2, rwkv_a0, rwkv_a2, rwkv_g2, rwkv_k_k, rwkv_k_a, rwkv_r_k, rwkv_ln_w, rwkv_ln_b, w_proj_nsa, w_proj_rwkv, w_out, norm_xattn, norm_mem, xattn_wq, xattn_wkv, xattn_wo, norm_ffn, peer_wq, peer_keys, peer_u, peer_v, norm_final):
    B, S, D = x.shape
    T = B * S
    depth = w_in.shape[0]
    G, n, dh = NSA_GROUPS, NSA_HPG, HEAD_DIM
    nq_cols = G * n * dh
    nkv_cols = 6 * G * dh
    ngate = 3 * G * n
    R = rwkv_w0.shape[1]
    c_q, c_kv, c_g = nq_cols, nq_cols + nkv_cols, nq_cols + nkv_cols + ngate
    c_r = c_g + 3 * R + DECAY_LORA + AAA_LORA + GATE_LORA
    gd = G * dh

    pos_flat = positions.reshape(T)
    rope_tabs = _rope_tables(pos_flat)
    cend = jnp.arange(S // CMP_STRIDE) * CMP_STRIDE + (CMP_LEN - 1)
    ctab = _rope_tables(positions[:, jnp.minimum(cend, S - 1)])
    ident = _rope_tables(jnp.zeros_like(positions[:, :S // CMP_STRIDE]))
    cmp_tabs = [jnp.stack([a, b]) for a, b in zip(ctab, ident)]

    xf = x.reshape(T, D)
    for l in range(depth):
        w_t = jnp.swapaxes(w_in, 1, 2)
        pieces = [(0, c_q), (c_q + 2 * gd, gd), (c_q + 4 * gd, gd),
                  (c_q, 2 * gd), (c_q + 3 * gd, gd), (c_q + 5 * gd, gd),
                  (c_g, 3 * R), (c_r, 2 * D)]
        wp = repack_rows(w_t, l, pieces)
        g_rope = (0, c_q + 2 * gd)
        g_plain = (g_rope[1], 4 * gd)
        g_rkv = (g_plain[0] + g_plain[1], 3 * R)
        g_merge = (g_rkv[0] + g_rkv[1], 2 * D)
        w_gate = jnp.pad(w_t[l, c_kv:c_g].reshape(G, 3 * n, D), ((0, 0), (0, LANES - 3 * n), (0, 0)))
        w_gate = w_gate.reshape(G * LANES, D)
        wl = w_t[l, c_g + 3 * R:c_r]
        padl = lambda a, n_: jnp.pad(a, ((0, n_ - a.shape[0]), (0, 0)))
        w_lora = jnp.concatenate([padl(wl[:DECAY_LORA], LANES), padl(wl[DECAY_LORA:DECAY_LORA + AAA_LORA], LANES),
                                  wl[DECAY_LORA + AAA_LORA:]], axis=0)

        h = rmsnorm(xf, norm_mix[l], BF16)
        proj = functools.partial(matmul, h, w_t=True)
        qkr = proj(wp, wcols=g_rope, mode="heads_rope", extras=rope_tabs, out_dtype=BF16, seq=S, name="proj_rope")
        kvc = proj(wp, wcols=(g_plain[0], 2 * gd), mode="heads", out_dtype=BF16, seq=S, name="proj_kvc")
        vt = proj(wp, wcols=(g_plain[0] + 2 * gd, 2 * gd), mode="heads_t", out_dtype=BF16, seq=S, name="proj_vt")
        gates_nsa = proj(w_gate, mode="heads_sigmoid", out_dtype=F32, seq=S, name="proj_gate")
        z_rkv = proj(wp, wcols=g_rkv, out_dtype=F32, name="proj_rkv")
        z_lora = proj(w_lora, out_dtype=F32, name="proj_lora")
        gates_merge = proj(wp, wcols=g_merge, mode="sigmoid", out_dtype=BF16, name="proj_merge")

        pe8 = jnp.broadcast_to(nsa_cmp_pe[l].reshape(2, 1, CMP_LEN * dh), (2, 8, CMP_LEN * dh)).astype(BF16)
        cmp = nsa_compress(kvc, nsa_cmp_w1[l].astype(BF16), nsa_cmp_w2[l].astype(BF16), pe8, cmp_tabs)
        o_nsa = nsa_attention(qkr, vt, cmp, gates_nsa).reshape(T, nq_cols)

        mu = rwkv_mu[l]
        mul = mu[3 * R:]
        pad1 = lambda a, n_: jnp.pad(a, (0, n_ - a.shape[0]))
        mu_lora = jnp.concatenate([pad1(mul[:DECAY_LORA], LANES), pad1(mul[DECAY_LORA:DECAY_LORA + AAA_LORA], LANES),
                                   mul[DECAY_LORA + AAA_LORA:]]).reshape(1, -1)
        feat = rwkv_features(z_lora.reshape(B, S, -1), mu_lora)
        par = jnp.stack([mu[:R], mu[R:2 * R], mu[2 * R:3 * R], rwkv_w0[l], rwkv_a0[l], rwkv_k_k[l], rwkv_k_a[l],
                         rwkv_r_k[l].reshape(R), rwkv_ln_w[l], rwkv_ln_b[l]])
        par = jnp.pad(par, ((0, 16 - par.shape[0]), (0, 0)))
        w2p = jnp.pad(rwkv_w2[l], ((0, LANES - DECAY_LORA), (0, 0)))
        a2p = jnp.pad(rwkv_a2[l], ((0, LANES - AAA_LORA), (0, 0)))
        o_rwkv = rwkv_scan(z_rkv.reshape(B, S, 3 * R), feat, par, w2p, a2p, rwkv_g2[l]).reshape(T, R)

        merged = merge_mixers(o_nsa, o_rwkv, w_proj_nsa[l], w_proj_rwkv[l], gates_merge)
        xf = matmul(merged, w_out[l], mode="residual", extras=(xf,), out_dtype=F32, name="out_proj")

        q = rmsnorm_proj(xf, norm_xattn[l], xattn_wq[l].astype(BF16), BF16)
        kvm = rmsnorm_proj(mem.reshape(-1, D), norm_mem[l], xattn_wkv[l].astype(BF16), BF16)
        kvm = kvm.reshape(B, mem.shape[1], -1)
        xf, h, h_t = cross_attention(q, kvm, xattn_wo[l].astype(BF16), xf, norm_ffn[l], S)

        pq = matmul(h, peer_wq[l], out_dtype=F32, name="peer_q")
        rows, planes = peer_route(pq, peer_keys[l])
        act = peer_activations(h_t, peer_u[l].astype(BF16), rows, planes)
        xf = matmul(act, peer_v[l], mode="residual", extras=(xf,), out_dtype=F32,
                    tm=1024, tn=1024, tk=2048, name="peer_out")

    return rmsnorm(xf, norm_final, F32).reshape(B, S, D)
```

```python
import functools

import jax
import jax.numpy as jnp
from jax import lax
from jax.experimental import pallas as pl
from jax.experimental.pallas import tpu as pltpu

F32 = jnp.float32
BF16 = jnp.bfloat16
HI = lax.Precision.HIGHEST

LANES = 128
HEAD_DIM = 128
ROPE_DIM = HEAD_DIM // 4
ROPE_THETA = 500000.0
RMS_EPS = 1e-6
NEG_INF = -1e30

NSA_GROUPS = 4
NSA_HPG = 4
CMP_LEN = 32
CMP_STRIDE = 16
SEL_BLOCK = 64
SEL_TOP = 16
WINDOW = 512
Q_BLOCK = 128

RWKV_HEAD = 64
DECAY_LORA = 96
AAA_LORA = 96
GATE_LORA = 256
GN_EPS = 64e-5
RWKV_CHUNK = 128

XATTN_HEADS = 4
PEER_HEADS = 8
PEER_KEYS = 128
PEER_TOPK = 16

VMEM_LIMIT = 56 * 1024 * 1024


def _cparams(sem, flags=None):
    return pltpu.CompilerParams(dimension_semantics=sem, vmem_limit_bytes=VMEM_LIMIT, flags=flags)


def _gelu(x):
    c = 0.7978845608028654
    return 0.5 * x * (1.0 + jnp.tanh(x * (c + (0.044715 * c) * (x * x))))


def _softplus(x):
    return jnp.maximum(x, 0.0) + jnp.log(1.0 + jnp.exp(-jnp.abs(x)))


def _dot(a, b):
    return jnp.dot(a, b, preferred_element_type=F32)


def _dot_nt(a, b):
    return lax.dot_general(a, b, (((1,), (1,)), ((), ())), preferred_element_type=F32)


def _dotf_nt(a, b):
    return lax.dot_general(a, b, (((1,), (1,)), ((), ())), preferred_element_type=F32, precision=HI)


def _split(x):
    hi = x.astype(BF16)
    return hi, (x - hi.astype(F32)).astype(BF16)


def _exact(x):
    return x.astype(BF16), None


_DN = {"nn": (((1,), (0,)), ((), ())), "nt": (((1,), (1,)), ((), ())), "tn": (((0,), (0,)), ((), ()))}


def _dot3(a, b, form="nn"):
    d = lambda x, y: lax.dot_general(x, y, _DN[form], preferred_element_type=F32)
    out = d(a[0], b[0])
    if a[1] is not None:
        out = out + d(a[1], b[0])
    if b[1] is not None:
        out = out + d(a[0], b[1])
    return out


def _rope_apply(a, c, sa, sb):
    return a * c + pltpu.roll(a, LANES - ROPE_DIM // 2, 1) * sa + pltpu.roll(a, ROPE_DIM // 2, 1) * sb


def _rope_tables(pos):
    half = ROPE_DIM // 2
    inv = ROPE_THETA ** (-jnp.arange(half, dtype=F32) / half)
    ang = pos.astype(F32)[..., None] * inv
    cos, sin = jnp.cos(ang), jnp.sin(ang)
    shp = pos.shape + (HEAD_DIM - ROPE_DIM,)
    c = jnp.concatenate([cos, cos, jnp.ones(shp, F32)], axis=-1)
    sa = jnp.concatenate([-sin, jnp.zeros(pos.shape + (HEAD_DIM - half,), F32)], axis=-1)
    sb = jnp.concatenate([jnp.zeros(pos.shape + (half,), F32), sin, jnp.zeros(shp, F32)], axis=-1)
    return c, sa, sb


def _rmsnorm_kernel(x_ref, g_ref, o_ref):
    x = x_ref[...].astype(F32)
    ms = jnp.mean(x * x, axis=-1, keepdims=True)
    o_ref[...] = (x * lax.rsqrt(ms + RMS_EPS) * g_ref[...]).astype(o_ref.dtype)


def rmsnorm(x2d, g, out_dtype, tm=256):
    M, D = x2d.shape
    tm = min(tm, M)
    return pl.pallas_call(
        _rmsnorm_kernel,
        grid=(M // tm,),
        in_specs=[pl.BlockSpec((tm, D), lambda i: (i, 0)), pl.BlockSpec((1, D), lambda i: (0, 0))],
        out_specs=pl.BlockSpec((tm, D), lambda i: (i, 0)),
        out_shape=jax.ShapeDtypeStruct((M, D), out_dtype),
        compiler_params=_cparams(("parallel",)),
        name="rmsnorm",
    )(x2d, g.reshape(1, D).astype(F32))


def _rmsnorm_proj_kernel(x_ref, g_ref, w_ref, o_ref):
    x = x_ref[...]
    y = x * lax.rsqrt(jnp.mean(x * x, axis=-1, keepdims=True) + RMS_EPS) * g_ref[...]
    o_ref[...] = _dot(y.astype(BF16), w_ref[...]).astype(o_ref.dtype)


def rmsnorm_proj(x2d, g, w, out_dtype, tm=256):
    M, D = x2d.shape
    N = w.shape[1]
    tm = min(tm, M)
    return pl.pallas_call(
        _rmsnorm_proj_kernel,
        grid=(M // tm,),
        in_specs=[pl.BlockSpec((tm, D), lambda i: (i, 0)), pl.BlockSpec((1, D), lambda i: (0, 0)),
                  pl.BlockSpec((D, N), lambda i: (0, 0))],
        out_specs=pl.BlockSpec((tm, N), lambda i: (i, 0)),
        out_shape=jax.ShapeDtypeStruct((M, N), out_dtype),
        compiler_params=_cparams(("parallel",)),
        name="rmsnorm_proj",
    )(x2d, g.reshape(1, D).astype(F32), w)


def _mm_kernel(*refs, nk, mode, n_extra, w_t):
    x_ref, w_ref = refs[0], refs[1]
    extras = refs[2:2 + n_extra]
    o_ref = refs[2 + n_extra]

    def product():
        w = w_ref[...]
        if w.dtype != x_ref.dtype:
            w = w.astype(x_ref.dtype)
        return _dot_nt(x_ref[...], w) if w_t else _dot(x_ref[...], w)

    def epilogue(acc):
        if mode == "plain":
            o_ref[...] = acc.astype(o_ref.dtype)
        elif mode == "sigmoid":
            o_ref[...] = jax.nn.sigmoid(acc).astype(o_ref.dtype)
        elif mode == "residual":
            o_ref[...] = (extras[0][...] + acc).astype(o_ref.dtype)
        elif mode in ("heads", "heads_sigmoid", "heads_rope"):
            nh = acc.shape[1] // LANES
            for j in range(nh):
                a = acc[:, j * LANES:(j + 1) * LANES]
                if mode == "heads_sigmoid":
                    a = jax.nn.sigmoid(a)
                elif mode == "heads_rope":
                    a = _rope_apply(a, extras[0][...], extras[1][...], extras[2][...])
                o_ref[0, j] = a.astype(o_ref.dtype)
        elif mode == "heads_t":
            for j in range(acc.shape[1] // LANES):
                for c in range(acc.shape[0] // LANES):
                    o_ref[0, j, c] = acc[c * LANES:(c + 1) * LANES, j * LANES:(j + 1) * LANES].T.astype(o_ref.dtype)
        else:
            raise ValueError(mode)

    if nk == 1:
        epilogue(product())
    else:
        acc_ref = refs[-1]
        k = pl.program_id(2)

        @pl.when(k == 0)
        def _():
            acc_ref[...] = jnp.zeros_like(acc_ref)

        acc_ref[...] += product()

        @pl.when(k == nk - 1)
        def _():
            epilogue(acc_ref[...])


def matmul(x, w, *, mode="plain", extras=(), out_dtype=F32, tm=1024, tn=512, tk=None, seq=None, name="mm",
           w_t=False, wcols=None):
    M, K = x.shape
    wcol0, N = (0, w.shape[0 if w_t else 1]) if wcols is None else wcols
    tm, tn = min(tm, M), min(tn, N)
    assert wcol0 % tn == 0
    wblk0 = wcol0 // tn
    if mode.startswith("heads"):
        tm = min(tm, seq)
    tk = K if tk is None else min(tk, K)
    nk = K // tk
    assert M % tm == 0 and N % tn == 0 and K % tk == 0
    grid = (M // tm, N // tn, nk)
    if w_t:
        w_spec = pl.BlockSpec((tn, tk), lambda i, j, k: (wblk0 + j, k))
    else:
        w_spec = pl.BlockSpec((tk, tn), lambda i, j, k: (k, wblk0 + j))
    in_specs = [pl.BlockSpec((tm, tk), lambda i, j, k: (i, k)), w_spec]
    if mode == "residual":
        in_specs.append(pl.BlockSpec((tm, tn), lambda i, j, k: (i, j)))
    elif mode == "heads_rope":
        in_specs += [pl.BlockSpec((tm, LANES), lambda i, j, k: (i, 0))] * 3
    if mode == "heads_t":
        assert seq % tm == 0 and tn % LANES == 0 and tm % LANES == 0
        spb = seq // tm
        out_shape = jax.ShapeDtypeStruct((M // seq, N // LANES, seq // LANES, LANES, LANES), out_dtype)
        out_spec = pl.BlockSpec((1, tn // LANES, tm // LANES, LANES, LANES),
                                lambda i, j, k: (i // spb, j, i % spb, 0, 0))
    elif mode.startswith("heads"):
        assert seq % tm == 0 and tn % LANES == 0
        spb = seq // tm
        out_shape = jax.ShapeDtypeStruct((M // seq, N // LANES, seq, LANES), out_dtype)
        out_spec = pl.BlockSpec((1, tn // LANES, tm, LANES), lambda i, j, k: (i // spb, j, i % spb, 0))
    else:
        out_shape = jax.ShapeDtypeStruct((M, N), out_dtype)
        out_spec = pl.BlockSpec((tm, tn), lambda i, j, k: (i, j))
    scratch = [pltpu.VMEM((tm, tn), F32)] if nk > 1 else []
    return pl.pallas_call(
        functools.partial(_mm_kernel, nk=nk, mode=mode, n_extra=len(extras), w_t=w_t),
        grid=grid,
        in_specs=in_specs,
        out_specs=out_spec,
        out_shape=out_shape,
        scratch_shapes=scratch,
        compiler_params=_cparams(("parallel", "parallel", "arbitrary")),
        name=name,
    )(x, w, *extras)


SUBLANES = 8


def _repack_kernel(tab_ref, w_ref, o_ref):
    o_ref[...] = w_ref[0].astype(o_ref.dtype)


def repack_rows(wt3, l, pieces, tr=512):
    _, NC, D = wt3.shape
    assert all(r0 % SUBLANES == 0 and nrow % tr == 0 for r0, nrow in pieces)
    starts = [r0 + r for r0, nrow in pieces for r in range(0, nrow, tr)]
    tab = jnp.asarray([s // SUBLANES for s in starts], jnp.int32)
    grid_spec = pltpu.PrefetchScalarGridSpec(
        num_scalar_prefetch=1, grid=(len(starts),),
        in_specs=[pl.BlockSpec((pl.Element(1), pl.Element(tr), pl.Element(D)),
                               lambda j, t: (l, t[j] * SUBLANES, 0))],
        out_specs=pl.BlockSpec((tr, D), lambda j, t: (j, 0)))
    return pl.pallas_call(
        _repack_kernel, grid_spec=grid_spec, out_shape=jax.ShapeDtypeStruct((len(starts) * tr, D), BF16),
        compiler_params=_cparams(("arbitrary",)), name="repack_w_in",
    )(tab, wt3)


def _compress_kernel(x_ref, w1_ref, w2_ref, pe_ref, c_ref, sa_ref, sb_ref, o_ref):
    half = x_ref.shape[-1]
    x = x_ref[0, 0]
    a = _dot(x, w1_ref[0, :half, :])
    b = _dot(x, w1_ref[0, half:, :])
    pet = _dot(pe_ref[0], w1_ref[0])
    n = a.shape[0]
    hid = a + pltpu.roll(b, n - 1, 0) + pet[0:1]
    out = _dot(_gelu(hid).astype(BF16), w2_ref[0])
    out = _rope_apply(out, c_ref[0, 0], sa_ref[0, 0], sb_ref[0, 0])
    o_ref[0, 0, 0] = out.astype(o_ref.dtype)


def nsa_compress(kvp, w1, w2, pe8, tabs):
    B, _, S, dh = kvp.shape
    G = NSA_GROUPS
    nch = S // CMP_STRIDE
    xv = kvp.reshape(B, kvp.shape[1], nch, CMP_STRIDE * dh)
    hidden = w1.shape[-1]
    tab_spec = pl.BlockSpec((1, 1, nch, LANES), lambda w, b, g: (w, b, 0, 0))
    return pl.pallas_call(
        _compress_kernel,
        grid=(2, B, G),
        in_specs=[
            pl.BlockSpec((1, 1, nch, CMP_STRIDE * dh), lambda w, b, g: (b, w * G + g, 0, 0)),
            pl.BlockSpec((1, CMP_LEN * dh, hidden), lambda w, b, g: (w, 0, 0)),
            pl.BlockSpec((1, hidden, dh), lambda w, b, g: (w, 0, 0)),
            pl.BlockSpec((1, 8, CMP_LEN * dh), lambda w, b, g: (w, 0, 0)),
            tab_spec, tab_spec, tab_spec,
        ],
        out_specs=pl.BlockSpec((1, 1, 1, nch, dh), lambda w, b, g: (w, b, g, 0, 0)),
        out_shape=jax.ShapeDtypeStruct((2, B, G, nch, dh), BF16),
        compiler_params=_cparams(("parallel", "parallel", "parallel")),
        name="nsa_compress",
    )(xv, w1, w2, pe8, *tabs)


def _dot_tn(a, b):
    return lax.dot_general(a, b, (((0,), (0,)), ((), ())), preferred_element_type=F32)


NSA_SPLIT = 2


def _nsa_kernel(q_ref, ks_ref, kw_ref, vs_ref, vw_ref, kc_ref, vc_ref, gt_ref, cmap_ref, e_ref, o_ref, *, ck):
    n = NSA_HPG
    ns = NSA_SPLIT
    hps = n // ns
    sq = hps * Q_BLOCK
    streams = range(ns)
    qb = pl.program_id(2)
    t0 = qb * Q_BLOCK
    c2 = (HEAD_DIM ** -0.5) * 1.4426950408889634
    nsel = ks_ref.shape[2] // SEL_BLOCK
    q_t = [jnp.concatenate([q_ref[0, s * hps + h].T for h in range(hps)], axis=1) for s in streams]
    tq = t0 + lax.broadcasted_iota(jnp.int32, (1, Q_BLOCK), 1)

    def heads(x):
        return jnp.concatenate([x] * hps, axis=1)

    def scores(k, ok_t):
        bias = heads(jnp.where(ok_t, 0.0, NEG_INF))
        return [_dot(k, q_t[s]) + bias for s in streams]

    def values(vt_ref, tile0, ntile):
        vt = jnp.concatenate([vt_ref[0, 0, tile0 + i] for i in range(ntile)], axis=1)
        return jnp.concatenate([vt, jnp.ones(vt.shape, vt.dtype)], axis=0)

    kc = kc_ref[0, 0, 0]
    vc = vc_ref[0, 0, 0]
    ncmp = kc.shape[0]
    crow = lax.broadcasted_iota(jnp.int32, (ncmp, Q_BLOCK), 0)
    sb = scores(kc, (crow < ncmp - 1) & (crow * CMP_STRIDE + (CMP_LEN - 1) <= tq))
    p = [jnp.exp2((sb[s] - jnp.max(sb[s], axis=0, keepdims=True)) * c2) for s in streams]
    anyv = heads((tq >= CMP_LEN - 1).astype(F32))
    p = [p[s] * (anyv / jnp.sum(p[s], axis=0, keepdims=True)) for s in streams]
    o_c = [_dot_tn(vc, p[s].astype(BF16)) for s in streams]

    psum = None
    for s in streams:
        for h in range(hps):
            ph = p[s][:, h * Q_BLOCK:(h + 1) * Q_BLOCK]
            psum = ph if psum is None else psum + ph
    hi = psum.astype(BF16)
    lo = (psum - hi.astype(F32)).astype(BF16)
    imp = _dot(cmap_ref[...], hi) + _dot(cmap_ref[...], lo)
    jr = lax.broadcasted_iota(jnp.int32, (LANES, Q_BLOCK), 0)
    cur = tq // SEL_BLOCK
    imp = jnp.where(jr == 0, 1e6, imp)
    imp = jnp.where(jr == cur, 1e6, imp)
    imp = jnp.where(jr == cur - 1, 1e6, imp)
    imp = jnp.where(jr > cur, -1e6, imp)

    nrow = min(((nsel + 7) // 8) * 8, LANES)
    imp = jnp.where(jr >= nsel, -3e6, imp)[0:nrow]
    jrr = jr[0:nrow]
    rank = jnp.zeros((nrow, Q_BLOCK), F32)
    for i in range(min(nsel, LANES)):
        row = imp[i:i + 1, :]
        rank = rank + jnp.where(row > imp, 1.0, jnp.where(row == imp, jnp.where(jrr > i, 1.0, 0.0), 0.0))
    sel = jnp.where(rank < float(SEL_TOP), 1.0, 0.0)
    if nrow < LANES:
        sel = jnp.concatenate([sel, jnp.zeros((LANES - nrow, Q_BLOCK), F32)], axis=0)
    sel = sel.astype(BF16)

    n_ch = (t0 + Q_BLOCK + ck - 1) // ck
    krow = lax.broadcasted_iota(jnp.int32, (ck, Q_BLOCK), 0)

    def body(kb, carry):
        m_i, acc = carry[:ns], carry[ns:]
        k0 = pl.multiple_of(kb * ck, ck)
        kblk = ks_ref[0, 0, pl.ds(k0, ck), :]
        v1 = values(vs_ref, kb * (ck // LANES), ck // LANES)
        em = _dot(e_ref[kb], sel)
        sb = scores(kblk, (k0 + krow <= tq) & (em > 0.5))
        m_new = [jnp.maximum(m_i[s], jnp.max(sb[s], axis=0, keepdims=True)) for s in streams]
        alpha = [jnp.exp2((m_i[s] - m_new[s]) * c2) for s in streams]
        pp = [jnp.exp2(((sb[s] - m_new[s]) * c2).astype(BF16)) for s in streams]
        acc = [alpha[s] * acc[s] + _dot(v1, pp[s]) for s in streams]
        return tuple(m_new) + tuple(acc)

    init = tuple(jnp.full((1, sq), NEG_INF, F32) for _ in streams) + tuple(
        jnp.zeros((2 * HEAD_DIM, sq), F32) for _ in streams)
    acc_f = lax.fori_loop(0, n_ch, body, init)[ns:]
    o_s = [a[0:HEAD_DIM] / a[HEAD_DIM:HEAD_DIM + 1] for a in acc_f]

    wlen = min(WINDOW + Q_BLOCK, kw_ref.shape[2])
    w0 = pl.multiple_of(jnp.maximum(t0 + Q_BLOCK - wlen, 0), Q_BLOCK)
    kblk = kw_ref[0, 0, pl.ds(w0, wlen), :]
    v1 = values(vw_ref, w0 // LANES, wlen // LANES)
    kpos = w0 + lax.broadcasted_iota(jnp.int32, (wlen, Q_BLOCK), 0)
    sb = scores(kblk, (kpos <= tq) & (tq - kpos < WINDOW))
    pw = [jnp.exp2(((sb[s] - jnp.max(sb[s], axis=0, keepdims=True)) * c2).astype(BF16)) for s in streams]
    acc_w = [_dot(v1, pw[s]) for s in streams]
    o_w = [a[0:HEAD_DIM] / a[HEAD_DIM:HEAD_DIM + 1] for a in acc_w]

    gt = gt_ref[0, 0].T
    for h in range(n):
        s = h // hps
        r = slice((h % hps) * Q_BLOCK, (h % hps + 1) * Q_BLOCK)
        o = (gt[3 * h:3 * h + 1] * o_c[s][:, r] + gt[3 * h + 1:3 * h + 2] * o_s[s][:, r]
             + gt[3 * h + 2:3 * h + 3] * o_w[s][:, r])
        o_ref[0, :, h * HEAD_DIM:(h + 1) * HEAD_DIM] = o.T.astype(o_ref.dtype)


def nsa_attention(qkr, vt, cmp, gates, ck=1024):
    B, _, S, dh = qkr.shape
    G, n = NSA_GROUPS, NSA_HPG
    ncmp = cmp.shape[3]
    nsel = S // SEL_BLOCK
    ck = min(ck, S)
    r_sel, r_cmp = SEL_BLOCK // CMP_STRIDE, CMP_LEN // CMP_STRIDE
    tgt = (r_sel * jnp.arange(nsel)[:, None, None] + jnp.arange(r_sel)[None, :, None]
           - jnp.arange(r_cmp)[None, None, :])
    cmap = (jnp.arange(ncmp - 1)[:, None, None, None] == tgt[None]).sum((2, 3)).astype(F32)
    cmap = jnp.pad(cmap, ((0, 1), (0, LANES - nsel))).astype(BF16).T
    e = (jnp.arange(S)[:, None] // SEL_BLOCK == jnp.arange(LANES)[None, :]).astype(BF16)
    e3 = e.reshape(S // ck, ck, LANES)

    def head(h0):
        return pl.BlockSpec((1, 1, S, dh), lambda b, g, i: (b, h0 + g, 0, 0))

    def head_t(h0):
        return pl.BlockSpec((1, 1, S // LANES, dh, LANES), lambda b, g, i: (b, h0 + g, 0, 0, 0))

    return pl.pallas_call(
        functools.partial(_nsa_kernel, ck=ck),
        grid=(B, G, S // Q_BLOCK),
        in_specs=[
            pl.BlockSpec((1, n, Q_BLOCK, dh), lambda b, g, i: (b, g, i, 0)),
            head(16), head(20), head_t(0), head_t(G),
            pl.BlockSpec((1, 1, 1, ncmp, dh), lambda b, g, i: (0, b, g, 0, 0)),
            pl.BlockSpec((1, 1, 1, ncmp, dh), lambda b, g, i: (1, b, g, 0, 0)),
            pl.BlockSpec((1, 1, Q_BLOCK, LANES), lambda b, g, i: (b, g, i, 0)),
            pl.BlockSpec((LANES, ncmp), lambda b, g, i: (0, 0)),
            pl.BlockSpec((S // ck, ck, LANES), lambda b, g, i: (0, 0, 0)),
        ],
        out_specs=pl.BlockSpec((1, Q_BLOCK, n * dh), lambda b, g, i: (b, i, g)),
        out_shape=jax.ShapeDtypeStruct((B, S, G * n * dh), BF16),
        compiler_params=_cparams(("parallel", "parallel", "arbitrary")),
        name="nsa_attention",
    )(qkr, qkr, qkr, vt, vt, cmp, cmp, gates, cmap, e3)


def _rwkv_feat_kernel(z_ref, zp_ref, mu_ref, o_ref):
    i = pl.program_id(1)
    z = z_ref[0]
    prev = zp_ref[0, 7:8, :] * (i > 0).astype(F32)
    row = lax.broadcasted_iota(jnp.int32, z.shape, 0)
    sh = jnp.where(row == 0, prev, pltpu.roll(z, 1, 0))
    zs = z + (sh - z) * mu_ref[...]
    o_ref[0, :, 0:128] = jnp.tanh(zs[:, 0:128])
    o_ref[0, :, 128:256] = zs[:, 128:256]
    o_ref[0, :, 256:] = jax.nn.sigmoid(zs[:, 256:])


def rwkv_features(z_lora, mu_lora, tm=512):
    B, S, W = z_lora.shape
    tm = min(tm, S)
    return pl.pallas_call(
        _rwkv_feat_kernel,
        grid=(B, S // tm),
        in_specs=[pl.BlockSpec((1, tm, W), lambda b, i: (b, i, 0)),
                  pl.BlockSpec((1, 8, W), lambda b, i: (b, jnp.maximum(i * (tm // 8) - 1, 0), 0)),
                  pl.BlockSpec((1, W), lambda b, i: (0, 0))],
        out_specs=pl.BlockSpec((1, tm, W), lambda b, i: (b, i, 0)),
        out_shape=jax.ShapeDtypeStruct((B, S, W), F32),
        compiler_params=_cparams(("parallel", "arbitrary")),
        name="rwkv_features",
    )(z_lora, z_lora, mu_lora)


def _rwkv_kernel(r_ref, k_ref, v_ref, f_ref, par_ref, w2_ref, a2_ref, g2_ref, o_ref, st_ref, prev_ref):
    C = RWKV_CHUNK
    W = r_ref.shape[-1]
    npair = W // LANES
    c = pl.program_id(2)

    @pl.when(c == 0)
    def _():
        st_ref[...] = jnp.zeros_like(st_ref)
        prev_ref[...] = jnp.zeros_like(prev_ref)

    par = par_ref[...]
    mu_r, mu_k, mu_v = par[0:1], par[1:2], par[2:3]
    w0, a0, k_k, k_a, r_k, ln_w, ln_b = par[3:4], par[4:5], par[5:6], par[6:7], par[7:8], par[8:9], par[9:10]

    row = lax.broadcasted_iota(jnp.int32, (C, C), 0)
    col = lax.broadcasted_iota(jnp.int32, (C, C), 1)
    trow = lax.broadcasted_iota(jnp.int32, (C, W), 0)
    lane = lax.broadcasted_iota(jnp.int32, (C, LANES), 1)
    lr = lax.broadcasted_iota(jnp.int32, (LANES, LANES), 0)
    lc = lax.broadcasted_iota(jnp.int32, (LANES, LANES), 1)

    def shifted(z_ref, idx, mu):
        z = z_ref[0]
        sh = jnp.where(trow == 0, prev_ref[idx:idx + 1, :], pltpu.roll(z, 1, 0))
        prev_ref[idx:idx + 1, :] = z[C - 1:C, :]
        return z + (sh - z) * mu

    r = shifted(r_ref, 0, mu_r)
    k = shifted(k_ref, 1, mu_k)
    v = shifted(v_ref, 2, mu_v)
    f = f_ref[0]
    w_log = -_softplus(-(w0 + _dot3(_split(f[:, 0:128]), _split(w2_ref[...])))) - 0.5
    logd = -jnp.exp(w_log)
    a = jax.nn.sigmoid(a0 + _dot3(_split(f[:, 128:256]), _split(a2_ref[...])))
    g = _dot3(_split(f[:, 256:]), _split(g2_ref[...]))

    bd = _exact(jnp.where((lr // RWKV_HEAD) == (lc // RWKV_HEAD), 1.0, 0.0))

    def head_sum(x, passes=2):
        sp = _split if passes == 2 else (lambda z: (z.astype(BF16), None))
        return jnp.concatenate([_dot3(sp(x[:, p * LANES:(p + 1) * LANES]), bd) for p in range(npair)], axis=1)

    kk = k * k_k
    kk = kk / jnp.maximum(jnp.sqrt(head_sum(kk * kk)), 1e-12)
    k2 = k * (1.0 + (a - 1.0) * k_a)

    tri_incl = _exact(jnp.where(col <= row, 1.0, 0.0))
    cum = _dot3(tri_incl, _split(logd))
    cum_end = cum[C - 1:C, :]
    e_neg = jnp.exp(-cum)
    kka = kk * a
    a_t = -kk * jnp.exp(cum - logd)
    b_t = kka * e_neg
    k_t = k2 * e_neg
    r_t = r * jnp.exp(cum)
    e_end = jnp.exp(cum_end - cum)
    b_e = kka * e_end
    k_e = k2 * e_end
    p_end = jnp.exp(cum_end)

    strict = col < row
    incl = col <= row
    eye = jnp.where(col == row, 1.0, 0.0)
    n_sq = C.bit_length() - 2
    hpp = LANES // RWKV_HEAD
    heads = [(p, h) for p in range(npair) for h in range(hpp)]
    hd = []
    for p, h in heads:
        sl = slice(p * LANES, (p + 1) * LANES)
        mh = (lane // RWKV_HEAD) == h
        s_am = _split(jnp.where(mh, a_t[:, sl], 0.0))
        s_rm = _split(jnp.where(mh, r_t[:, sl], 0.0))
        hd.append(dict(
            sl=sl, s_am=s_am, s_rm=s_rm,
            s_vm=_split(jnp.where(mh, v[:, sl], 0.0)),
            s_bem=_split(jnp.where(mh, b_e[:, sl], 0.0)),
            s_kem=_split(jnp.where(mh, k_e[:, sl], 0.0)),
            rhs=_split(jnp.concatenate([b_t[:, sl], k_t[:, sl]], axis=0)) if h == 0 else hd[-1]["rhs"]))
    for d in hd:
        amat = _dot3(d["s_am"], d["rhs"], "nt")
        rmat = _dot_nt(d["s_rm"][0], d["rhs"][0])
        l_ab = jnp.where(strict, amat[:, 0:C], 0.0)
        d["l_ak"] = _split(jnp.where(strict, amat[:, C:2 * C], 0.0))
        d["l_rb"] = jnp.where(incl, rmat[:, 0:C], 0.0).astype(BF16)
        d["l_rk"] = jnp.where(incl, rmat[:, C:2 * C], 0.0).astype(BF16)
        d["x"] = eye + l_ab
        d["s_pw"] = _split(l_ab)
    for i in range(n_sq):
        for d in hd:
            d["s_pw"] = _split(_dot3(d["s_pw"], d["s_pw"])) if i == 0 else (_dot(d["s_pw"][0], d["s_pw"][0]).astype(BF16), None)
        for d in hd:
            d["x"] = d["x"] + (_dot3(_split(d["x"]), d["s_pw"]) if i == 0 else _dot(d["x"].astype(BF16), d["s_pw"][0]))
    cat = lambda parts, axis: tuple(None if parts[0][i] is None else jnp.concatenate([p[i] for p in parts], axis=axis)
                                    for i in range(2))
    for i, d in enumerate(hd):
        d["st"] = st_ref[i]
        d["s_st"] = _split(d["st"])
        d["t"] = _split(_dot3(cat([d["s_am"], d["l_ak"]], 1), cat([d["s_st"], d["s_vm"]], 0)))
    for d in hd:
        d["s_u"] = _split(_dot3(_split(d["x"]), d["t"]))
    for i, d in enumerate(hd):
        if i % hpp == 0:
            decay = jnp.broadcast_to(p_end[:, d["sl"]], (LANES, LANES)).T
        st_ref[i] = d["st"] * decay + _dot3(cat([d["s_bem"], d["s_kem"]], 0), cat([d["s_u"], d["s_vm"]], 0), "tn")
    ys = []
    for d in hd:
        ys.append(_dot(jnp.concatenate([d["s_rm"][0], d["l_rb"], d["l_rk"]], axis=1),
                       jnp.concatenate([d["s_st"][0], d["s_u"][0], d["s_vm"][0]], axis=0)))
    y = jnp.concatenate([sum(ys[hpp * p + 1:hpp * (p + 1)], ys[hpp * p]) for p in range(npair)], axis=1)

    inv_n = 1.0 / RWKV_HEAD
    mean = head_sum(y) * inv_n
    d = y - mean
    var = head_sum(d * d) * inv_n
    yn = d * lax.rsqrt(var + GN_EPS) * ln_w + ln_b
    bonus = head_sum(r * k2 * r_k, passes=1) * v
    o_ref[0] = ((yn + bonus) * g).astype(o_ref.dtype)


def rwkv_scan(z_rkv, feat, par, w2p, a2p, g2, pairs_per_step=8):
    B, S, R3 = z_rkv.shape
    R = R3 // 3
    W = pairs_per_step * LANES
    nblk = R // W
    C = RWKV_CHUNK
    assert S % C == 0 and R % W == 0

    def col(off):
        return pl.BlockSpec((1, C, W), lambda b, p, c: (b, c, off + p))

    return pl.pallas_call(
        _rwkv_kernel,
        grid=(B, nblk, S // C),
        in_specs=[
            col(0), col(nblk), col(2 * nblk),
            pl.BlockSpec((1, C, feat.shape[-1]), lambda b, p, c: (b, c, 0)),
            pl.BlockSpec((16, W), lambda b, p, c: (0, p)),
            pl.BlockSpec((LANES, W), lambda b, p, c: (0, p)),
            pl.BlockSpec((LANES, W), lambda b, p, c: (0, p)),
            pl.BlockSpec((GATE_LORA, W), lambda b, p, c: (0, p)),
        ],
        out_specs=pl.BlockSpec((1, C, W), lambda b, p, c: (b, c, p)),
        out_shape=jax.ShapeDtypeStruct((B, S, R), BF16),
        scratch_shapes=[pltpu.VMEM((W // RWKV_HEAD, LANES, LANES), F32), pltpu.VMEM((8, W), F32)],
        compiler_params=_cparams(("parallel", "parallel", "arbitrary")),
        name="rwkv_scan",
    )(z_rkv, z_rkv, z_rkv, feat, par, w2p, a2p, g2)


def _merge_kernel(h_ref, wa_ref, wb_ref, on_ref, or_ref, wn_ref, wr_ref, o_ref):
    h = h_ref[...]
    ga = jax.nn.sigmoid(_dot_nt(h, wa_ref[...]))
    gb = jax.nn.sigmoid(_dot_nt(h, wb_ref[...]))
    pa = _dot(on_ref[...], wn_ref[...].astype(on_ref.dtype))
    pb = _dot(or_ref[...], wr_ref[...].astype(or_ref.dtype))
    o_ref[...] = (ga * pa + gb * pb).astype(o_ref.dtype)


def merge_mixers(h, wp, gate_rows, o_nsa, o_rwkv, wn, wr, tm=512, tn=512):
    M, Kn = o_nsa.shape
    Kr = o_rwkv.shape[1]
    D = wn.shape[1]
    tm, tn = min(tm, M), min(tn, D)
    assert gate_rows[0] % tn == 0 and gate_rows[1] == 2 * D
    ba, bb = gate_rows[0] // tn, (gate_rows[0] + D) // tn
    return pl.pallas_call(
        _merge_kernel,
        grid=(M // tm, D // tn),
        in_specs=[
            pl.BlockSpec((tm, D), lambda i, j: (i, 0)),
            pl.BlockSpec((tn, D), lambda i, j: (ba + j, 0)),
            pl.BlockSpec((tn, D), lambda i, j: (bb + j, 0)),
            pl.BlockSpec((tm, Kn), lambda i, j: (i, 0)),
            pl.BlockSpec((tm, Kr), lambda i, j: (i, 0)),
            pl.BlockSpec((Kn, tn), lambda i, j: (0, j)),
            pl.BlockSpec((Kr, tn), lambda i, j: (0, j)),
        ],
        out_specs=pl.BlockSpec((tm, tn), lambda i, j: (i, j)),
        out_shape=jax.ShapeDtypeStruct((M, D), BF16),
        compiler_params=_cparams(("parallel", "parallel")),
        name="merge_mixers",
    )(h, wp, wp, o_nsa, o_rwkv, wn, wr)


def _xattn_kernel(q_ref, kv_ref, wo_ref, x_ref, g_ref, o_ref, h_ref, ht_ref):
    nh = XATTN_HEADS
    scale = HEAD_DIM ** -0.5
    outs = []
    for h in range(nh):
        q = q_ref[:, h * HEAD_DIM:(h + 1) * HEAD_DIM]
        k = kv_ref[0, :, h * HEAD_DIM:(h + 1) * HEAD_DIM]
        v = kv_ref[0, :, (nh + h) * HEAD_DIM:(nh + h + 1) * HEAD_DIM]
        s = _dot_nt(q, k) * scale
        m = jnp.max(s, axis=-1, keepdims=True)
        p = jnp.exp(s - m)
        p = p / jnp.sum(p, axis=-1, keepdims=True)
        outs.append(_dot(p.astype(BF16), v).astype(BF16))
    o = jnp.concatenate(outs, axis=-1)
    x = x_ref[...] + _dot(o, wo_ref[...])
    o_ref[...] = x
    y = x * lax.rsqrt(jnp.mean(x * x, axis=-1, keepdims=True) + RMS_EPS) * g_ref[...]
    h_ref[...] = y.astype(h_ref.dtype)
    ht_ref[...] = y.T.astype(ht_ref.dtype)


def cross_attention(q, kv, wo, x, g_next, seq, tm=256):
    T, D = x.shape
    tm = min(tm, seq)
    spb = seq // tm
    Mm = kv.shape[1]
    row = pl.BlockSpec((tm, D), lambda i: (i, 0))
    return pl.pallas_call(
        _xattn_kernel,
        grid=(T // tm,),
        in_specs=[
            pl.BlockSpec((tm, q.shape[1]), lambda i: (i, 0)),
            pl.BlockSpec((1, Mm, kv.shape[2]), lambda i: (i // spb, 0, 0)),
            pl.BlockSpec(wo.shape, lambda i: (0, 0)),
            row,
            pl.BlockSpec((1, D), lambda i: (0, 0)),
        ],
        out_specs=[row, row, pl.BlockSpec((D, tm), lambda i: (0, i))],
        out_shape=[jax.ShapeDtypeStruct((T, D), F32), jax.ShapeDtypeStruct((T, D), BF16),
                   jax.ShapeDtypeStruct((D, T), BF16)],
        compiler_params=_cparams(("parallel",)),
        name="cross_attention",
    )(q, kv, wo, x, g_next.reshape(1, D).astype(F32))


def _top16(x, iota=None, n=None):
    return _top16_many([x], iota, n)[0]


def _top16_many(xs, iota=None, n=None):
    shape = xs[0].shape
    if iota is None:
        n = shape[0]
        iota = lax.broadcasted_iota(jnp.int32, shape, 0).astype(F32)
    cur = list(xs)
    rank = [jnp.full(shape, float(PEER_TOPK), F32) for _ in xs]
    vals = [[] for _ in xs]
    for r in range(PEER_TOPK):
        m = [jnp.max(c, axis=0, keepdims=True) for c in cur]
        idx = [jnp.min(jnp.where(c == mi, iota, float(n)), axis=0, keepdims=True) for c, mi in zip(cur, m)]
        hit = [iota == i for i in idx]
        rank = [jnp.where(h, float(r), rk) for h, rk in zip(hit, rank)]
        cur = [jnp.where(h, -jnp.inf, c) for h, c in zip(hit, cur)]
        for v, mi in zip(vals, m):
            v.append(mi)
    return [(jnp.concatenate(v, axis=0), rk) for v, rk in zip(vals, rank)]


def _peer_route_kernel(q_ref, keys_ref, row_ref, pl_ref):
    K = PEER_TOPK
    q = q_ref[...]
    half = q.shape[1] // 2
    s1 = _dotf_nt(keys_ref[0, 0], q[:, :half])
    s2 = _dotf_nt(keys_ref[0, 1], q[:, half:])
    ht = q.shape[0] // 2
    halves = lambda parts: tuple(jnp.concatenate([a, b], axis=1) for a, b in zip(*parts))
    t16 = _top16_many([s1[:, :ht], s1[:, ht:], s2[:, :ht], s2[:, ht:]])
    (v1, rank1), (v2, rank2) = halves(t16[0:2]), halves(t16[2:4])
    tmq = q.shape[0]
    r8 = lax.broadcasted_iota(jnp.int32, (8, tmq), 0)
    ninf = -jnp.inf
    lo8 = v2[0:8]
    tiles = [
        (v1[0:1] + lo8, r8),
        (v1[0:1] + v2[8:16], 8 + r8),
        (v1[1:2] + lo8, 16 + r8),
        (jnp.where(r8 < 5, v1[2:3] + lo8, ninf), 32 + r8),
        (jnp.where(r8 < 7, jnp.where(r8 < 4, v1[3:4], v1[4:5]) + jnp.where(r8 < 4, lo8, pltpu.roll(lo8, 4, 0)), ninf),
         jnp.where(r8 < 4, 48 + r8, 60 + r8)),
        (jnp.where(r8 < 6, jnp.where(r8 < 2, v1[5:6], jnp.where(r8 < 4, v1[6:7], v1[7:8]))
                   + jnp.where(r8 % 2 == 0, v2[0:1], v2[1:2]), ninf),
         80 + (r8 // 2) * 16 + r8 % 2),
        (v1[8:16] + v2[0:1], (8 + r8) * 16),
    ]
    cand = jnp.concatenate([t for t, _ in tiles], axis=0)
    cidx = jnp.concatenate([i for _, i in tiles], axis=0).astype(F32)
    top_s, crank = halves(_top16_many([cand[:, :ht], cand[:, ht:]], cidx[:, :ht], K * K))
    chosen = jnp.where(crank < float(K), 1.0, 0.0)
    z = jnp.sum(jnp.exp(top_s - top_s[0:1]), axis=0, keepdims=True)
    ch = [chosen[8 * t:8 * (t + 1)] for t in range(len(tiles))]
    colsum = lambda x: jnp.sum(x, axis=0, keepdims=True)
    upper = (r8 < 4).astype(F32)
    j_rows = [colsum(ch[0]) + colsum(ch[1]), colsum(ch[2]), colsum(ch[3]),
              colsum(ch[4] * upper), colsum(ch[4] * (1.0 - upper))]
    j_rows += [ch[5][2 * m:2 * m + 1] + ch[5][2 * m + 1:2 * m + 2] for m in range(3)]
    j_rows += [ch[6][m:m + 1] for m in range(8)]
    jn = jnp.zeros_like(s1)
    for i in range(K):
        jn = jnp.where(rank1 == float(i), j_rows[i], jn)
    row_ref[0, 0] = jnp.exp(s1 - v1[0:1]) / z
    row_ref[0, 1] = jn
    pl_ref[0, 0] = jnp.exp(s2 - v2[0:1]).astype(pl_ref.dtype)
    pl_ref[0, 1] = rank2.astype(pl_ref.dtype)


def peer_route(q, keys, tm=512):
    T = q.shape[0]
    H, _, nkeys, hd = keys.shape
    tm = min(tm, T)
    spec = pl.BlockSpec((1, 2, nkeys, tm), lambda i, h: (h, 0, 0, i))
    return pl.pallas_call(
        _peer_route_kernel,
        grid=(T // tm, H),
        in_specs=[pl.BlockSpec((tm, 2 * hd), lambda i, h: (i, h)),
                  pl.BlockSpec((1, 2, nkeys, hd), lambda i, h: (h, 0, 0, 0))],
        out_specs=[spec, spec],
        out_shape=[jax.ShapeDtypeStruct((H, 2, nkeys, T), F32), jax.ShapeDtypeStruct((H, 2, nkeys, T), BF16)],
        compiler_params=_cparams(("parallel", "parallel")),
        name="peer_route",
    )(q, keys)


def _peer_act_kernel(h_ref, u_ref, row_ref, pl_ref, o_ref):
    nk = PEER_KEYS
    ec = u_ref.shape[0]
    j = pl.program_id(1)
    hu = _dot(u_ref[...], h_ref[...])
    zero = jnp.zeros((), o_ref.dtype)
    for al in range(ec // nk):
        a = j * (ec // nk) + al
        w = None
        for hd in range(row_ref.shape[0]):
            e1 = row_ref[hd, 0, pl.ds(a, 1), :].astype(o_ref.dtype)
            jn = row_ref[hd, 1, pl.ds(a, 1), :].astype(o_ref.dtype)
            g = jnp.where(pl_ref[hd, 1] < jn, e1 * pl_ref[hd, 0], zero)
            w = g if w is None else w + g
        act = _gelu(hu[al * nk:(al + 1) * nk, :].astype(o_ref.dtype)) * w
        o_ref[:, al * nk:(al + 1) * nk] = act.T


def peer_activations(h_t, u_tab, rows, planes, tm=512, ec=1024):
    D, T = h_t.shape
    E = u_tab.shape[0]
    tm, ec = min(tm, T), min(ec, E)
    H, _, nkeys, _ = rows.shape
    rspec = pl.BlockSpec((H, 2, nkeys, tm), lambda i, j: (0, 0, 0, i))
    return pl.pallas_call(
        _peer_act_kernel,
        grid=(T // tm, E // ec),
        in_specs=[pl.BlockSpec((D, tm), lambda i, j: (0, i)),
                  pl.BlockSpec((ec, D), lambda i, j: (j, 0)),
                  rspec, rspec],
        out_specs=pl.BlockSpec((tm, ec), lambda i, j: (i, j)),
        out_shape=jax.ShapeDtypeStruct((T, E), BF16),
        compiler_params=_cparams(("parallel", "arbitrary")),
        name="peer_activations",
    )(h_t, u_tab, rows, planes)


def kernel(x, mem, positions, norm_mix, w_in, nsa_cmp_pe, nsa_cmp_w1, nsa_cmp_w2, rwkv_mu, rwkv_w0, rwkv_w2, rwkv_a0, rwkv_a2, rwkv_g2, rwkv_k_k, rwkv_k_a, rwkv_r_k, rwkv_ln_w, rwkv_ln_b, w_proj_nsa, w_proj_rwkv, w_out, norm_xattn, norm_mem, xattn_wq, xattn_wkv, xattn_wo, norm_ffn, peer_wq, peer_keys, peer_u, peer_v, norm_final):
    B, S, D = x.shape
    T = B * S
    depth = w_in.shape[0]
    G, n, dh = NSA_GROUPS, NSA_HPG, HEAD_DIM
    nq_cols = G * n * dh
    nkv_cols = 6 * G * dh
    ngate = 3 * G * n
    R = rwkv_w0.shape[1]
    c_q, c_kv, c_g = nq_cols, nq_cols + nkv_cols, nq_cols + nkv_cols + ngate
    c_r = c_g + 3 * R + DECAY_LORA + AAA_LORA + GATE_LORA
    gd = G * dh

    pos_flat = positions.reshape(T)
    rope_tabs = _rope_tables(pos_flat)
    cend = jnp.arange(S // CMP_STRIDE) * CMP_STRIDE + (CMP_LEN - 1)
    ctab = _rope_tables(positions[:, jnp.minimum(cend, S - 1)])
    ident = _rope_tables(jnp.zeros_like(positions[:, :S // CMP_STRIDE]))
    cmp_tabs = [jnp.stack([a, b]) for a, b in zip(ctab, ident)]

    xf = x.reshape(T, D)
    for l in range(depth):
        w_t = jnp.swapaxes(w_in, 1, 2)
        pieces = [(0, c_q), (c_q + 2 * gd, gd), (c_q + 4 * gd, gd),
                  (c_q, 2 * gd), (c_q + 3 * gd, gd), (c_q + 5 * gd, gd),
                  (c_g, 3 * R), (c_r, 2 * D)]
        wp = repack_rows(w_t, l, pieces)
        g_rope = (0, c_q + 2 * gd)
        g_plain = (g_rope[1], 4 * gd)
        g_rkv = (g_plain[0] + g_plain[1], 3 * R)
        g_merge = (g_rkv[0] + g_rkv[1], 2 * D)
        w_gate = jnp.pad(w_t[l, c_kv:c_g].reshape(G, 3 * n, D), ((0, 0), (0, LANES - 3 * n), (0, 0)))
        w_gate = w_gate.reshape(G * LANES, D)
        wl = w_t[l, c_g + 3 * R:c_r]
        padl = lambda a, n_: jnp.pad(a, ((0, n_ - a.shape[0]), (0, 0)))
        w_lora = jnp.concatenate([padl(wl[:DECAY_LORA], LANES), padl(wl[DECAY_LORA:DECAY_LORA + AAA_LORA], LANES),
                                  wl[DECAY_LORA + AAA_LORA:]], axis=0)

        h = rmsnorm(xf, norm_mix[l], BF16)
        proj = functools.partial(matmul, h, w_t=True)
        qkr = proj(wp, wcols=g_rope, mode="heads_rope", extras=rope_tabs, out_dtype=BF16, seq=S, name="proj_rope")
        kvc = proj(wp, wcols=(g_plain[0], 2 * gd), mode="heads", out_dtype=BF16, seq=S, name="proj_kvc")
        vt = proj(wp, wcols=(g_plain[0] + 2 * gd, 2 * gd), mode="heads_t", out_dtype=BF16, seq=S, name="proj_vt")
        gates_nsa = proj(w_gate, mode="heads_sigmoid", out_dtype=F32, seq=S, name="proj_gate")
        z_rkv = proj(wp, wcols=g_rkv, out_dtype=F32, name="proj_rkv")
        z_lora = proj(w_lora, out_dtype=F32, name="proj_lora")

        pe8 = jnp.broadcast_to(nsa_cmp_pe[l].reshape(2, 1, CMP_LEN * dh), (2, 8, CMP_LEN * dh)).astype(BF16)
        cmp = nsa_compress(kvc, nsa_cmp_w1[l].astype(BF16), nsa_cmp_w2[l].astype(BF16), pe8, cmp_tabs)
        o_nsa = nsa_attention(qkr, vt, cmp, gates_nsa).reshape(T, nq_cols)

        mu = rwkv_mu[l]
        mul = mu[3 * R:]
        pad1 = lambda a, n_: jnp.pad(a, (0, n_ - a.shape[0]))
        mu_lora = jnp.concatenate([pad1(mul[:DECAY_LORA], LANES), pad1(mul[DECAY_LORA:DECAY_LORA + AAA_LORA], LANES),
                                   mul[DECAY_LORA + AAA_LORA:]]).reshape(1, -1)
        feat = rwkv_features(z_lora.reshape(B, S, -1), mu_lora)
        par = jnp.stack([mu[:R], mu[R:2 * R], mu[2 * R:3 * R], rwkv_w0[l], rwkv_a0[l], rwkv_k_k[l], rwkv_k_a[l],
                         rwkv_r_k[l].reshape(R), rwkv_ln_w[l], rwkv_ln_b[l]])
        par = jnp.pad(par, ((0, 16 - par.shape[0]), (0, 0)))
        w2p = jnp.pad(rwkv_w2[l], ((0, LANES - DECAY_LORA), (0, 0)))
        a2p = jnp.pad(rwkv_a2[l], ((0, LANES - AAA_LORA), (0, 0)))
        o_rwkv = rwkv_scan(z_rkv.reshape(B, S, 3 * R), feat, par, w2p, a2p, rwkv_g2[l]).reshape(T, R)

        merged = merge_mixers(h, wp, g_merge, o_nsa, o_rwkv, w_proj_nsa[l], w_proj_rwkv[l])
        xf = matmul(merged, w_out[l], mode="residual", extras=(xf,), out_dtype=F32, name="out_proj")

        q = rmsnorm_proj(xf, norm_xattn[l], xattn_wq[l].astype(BF16), BF16)
        kvm = rmsnorm_proj(mem.reshape(-1, D), norm_mem[l], xattn_wkv[l].astype(BF16), BF16)
        kvm = kvm.reshape(B, mem.shape[1], -1)
        xf, h, h_t = cross_attention(q, kvm, xattn_wo[l].astype(BF16), xf, norm_ffn[l], S)

        pq = matmul(h, peer_wq[l], out_dtype=F32, name="peer_q")
        rows, planes = peer_route(pq, peer_keys[l])
        act = peer_activations(h_t, peer_u[l].astype(BF16), rows, planes)
        xf = matmul(act, peer_v[l], mode="residual", extras=(xf,), out_dtype=F32,
                    tm=1024, tn=1024, tk=2048, name="peer_out")

    return rmsnorm(xf, norm_final, F32).reshape(B, S, D)
```

```python
import functools

import jax
import jax.numpy as jnp
from jax import lax
from jax.experimental import pallas as pl
from jax.experimental.pallas import tpu as pltpu

F32 = jnp.float32
BF16 = jnp.bfloat16
HI = lax.Precision.HIGHEST

LANES = 128
HEAD_DIM = 128
ROPE_DIM = HEAD_DIM // 4
ROPE_THETA = 500000.0
RMS_EPS = 1e-6
NEG_INF = -1e30

NSA_GROUPS = 4
NSA_HPG = 4
CMP_LEN = 32
CMP_STRIDE = 16
SEL_BLOCK = 64
SEL_TOP = 16
WINDOW = 512
Q_BLOCK = 128

RWKV_HEAD = 64
DECAY_LORA = 96
AAA_LORA = 96
GATE_LORA = 256
GN_EPS = 64e-5
RWKV_CHUNK = 128

XATTN_HEADS = 4
PEER_HEADS = 8
PEER_KEYS = 128
PEER_TOPK = 16

VMEM_LIMIT = 56 * 1024 * 1024


def _cparams(sem, flags=None):
    return pltpu.CompilerParams(dimension_semantics=sem, vmem_limit_bytes=VMEM_LIMIT, flags=flags)


def _gelu(x):
    c = 0.7978845608028654
    return 0.5 * x * (1.0 + jnp.tanh(x * (c + (0.044715 * c) * (x * x))))


def _softplus(x):
    return jnp.maximum(x, 0.0) + jnp.log(1.0 + jnp.exp(-jnp.abs(x)))


def _dot(a, b):
    return jnp.dot(a, b, preferred_element_type=F32)


def _dot_nt(a, b):
    return lax.dot_general(a, b, (((1,), (1,)), ((), ())), preferred_element_type=F32)


def _dotf_nt(a, b):
    return lax.dot_general(a, b, (((1,), (1,)), ((), ())), preferred_element_type=F32, precision=HI)


def _split(x):
    hi = x.astype(BF16)
    return hi, (x - hi.astype(F32)).astype(BF16)


def _exact(x):
    return x.astype(BF16), None


_DN = {"nn": (((1,), (0,)), ((), ())), "nt": (((1,), (1,)), ((), ())), "tn": (((0,), (0,)), ((), ()))}


def _dot3(a, b, form="nn"):
    d = lambda x, y: lax.dot_general(x, y, _DN[form], preferred_element_type=F32)
    out = d(a[0], b[0])
    if a[1] is not None:
        out = out + d(a[1], b[0])
    if b[1] is not None:
        out = out + d(a[0], b[1])
    return out


def _rope_apply(a, c, sa, sb):
    return a * c + pltpu.roll(a, LANES - ROPE_DIM // 2, 1) * sa + pltpu.roll(a, ROPE_DIM // 2, 1) * sb


def _rope_tables(pos):
    half = ROPE_DIM // 2
    inv = ROPE_THETA ** (-jnp.arange(half, dtype=F32) / half)
    ang = pos.astype(F32)[..., None] * inv
    cos, sin = jnp.cos(ang), jnp.sin(ang)
    shp = pos.shape + (HEAD_DIM - ROPE_DIM,)
    c = jnp.concatenate([cos, cos, jnp.ones(shp, F32)], axis=-1)
    sa = jnp.concatenate([-sin, jnp.zeros(pos.shape + (HEAD_DIM - half,), F32)], axis=-1)
    sb = jnp.concatenate([jnp.zeros(pos.shape + (half,), F32), sin, jnp.zeros(shp, F32)], axis=-1)
    return c, sa, sb


def _rmsnorm_kernel(x_ref, g_ref, o_ref):
    x = x_ref[...].astype(F32)
    ms = jnp.mean(x * x, axis=-1, keepdims=True)
    o_ref[...] = (x * lax.rsqrt(ms + RMS_EPS) * g_ref[...]).astype(o_ref.dtype)


def rmsnorm(x2d, g, out_dtype, tm=256):
    M, D = x2d.shape
    tm = min(tm, M)
    return pl.pallas_call(
        _rmsnorm_kernel,
        grid=(M // tm,),
        in_specs=[pl.BlockSpec((tm, D), lambda i: (i, 0)), pl.BlockSpec((1, D), lambda i: (0, 0))],
        out_specs=pl.BlockSpec((tm, D), lambda i: (i, 0)),
        out_shape=jax.ShapeDtypeStruct((M, D), out_dtype),
        compiler_params=_cparams(("parallel",)),
        name="rmsnorm",
    )(x2d, g.reshape(1, D).astype(F32))


def _rmsnorm_proj_kernel(x_ref, g_ref, w_ref, o_ref):
    x = x_ref[...]
    y = x * lax.rsqrt(jnp.mean(x * x, axis=-1, keepdims=True) + RMS_EPS) * g_ref[...]
    o_ref[...] = _dot(y.astype(BF16), w_ref[...]).astype(o_ref.dtype)


def rmsnorm_proj(x2d, g, w, out_dtype, tm=256):
    M, D = x2d.shape
    N = w.shape[1]
    tm = min(tm, M)
    return pl.pallas_call(
        _rmsnorm_proj_kernel,
        grid=(M // tm,),
        in_specs=[pl.BlockSpec((tm, D), lambda i: (i, 0)), pl.BlockSpec((1, D), lambda i: (0, 0)),
                  pl.BlockSpec((D, N), lambda i: (0, 0))],
        out_specs=pl.BlockSpec((tm, N), lambda i: (i, 0)),
        out_shape=jax.ShapeDtypeStruct((M, N), out_dtype),
        compiler_params=_cparams(("parallel",)),
        name="rmsnorm_proj",
    )(x2d, g.reshape(1, D).astype(F32), w)


def _mm_kernel(*refs, nk, mode, n_extra, w_t):
    x_ref, w_ref = refs[0], refs[1]
    extras = refs[2:2 + n_extra]
    o_ref = refs[2 + n_extra]

    def product():
        w = w_ref[...]
        if w.dtype != x_ref.dtype:
            w = w.astype(x_ref.dtype)
        return _dot_nt(x_ref[...], w) if w_t else _dot(x_ref[...], w)

    def epilogue(acc):
        if mode == "plain":
            o_ref[...] = acc.astype(o_ref.dtype)
        elif mode == "sigmoid":
            o_ref[...] = jax.nn.sigmoid(acc).astype(o_ref.dtype)
        elif mode == "residual":
            o_ref[...] = (extras[0][...] + acc).astype(o_ref.dtype)
        elif mode in ("heads", "heads_sigmoid", "heads_rope"):
            nh = acc.shape[1] // LANES
            for j in range(nh):
                a = acc[:, j * LANES:(j + 1) * LANES]
                if mode == "heads_sigmoid":
                    a = jax.nn.sigmoid(a)
                elif mode == "heads_rope":
                    a = _rope_apply(a, extras[0][...], extras[1][...], extras[2][...])
                o_ref[0, j] = a.astype(o_ref.dtype)
        elif mode == "heads_t":
            for j in range(acc.shape[1] // LANES):
                for c in range(acc.shape[0] // LANES):
                    o_ref[0, j, c] = acc[c * LANES:(c + 1) * LANES, j * LANES:(j + 1) * LANES].T.astype(o_ref.dtype)
        else:
            raise ValueError(mode)

    if nk == 1:
        epilogue(product())
    else:
        acc_ref = refs[-1]
        k = pl.program_id(2)

        @pl.when(k == 0)
        def _():
            acc_ref[...] = jnp.zeros_like(acc_ref)

        acc_ref[...] += product()

        @pl.when(k == nk - 1)
        def _():
            epilogue(acc_ref[...])


def matmul(x, w, *, mode="plain", extras=(), out_dtype=F32, tm=1024, tn=512, tk=None, seq=None, name="mm",
           w_t=False, wcols=None):
    M, K = x.shape
    wcol0, N = (0, w.shape[0 if w_t else 1]) if wcols is None else wcols
    tm, tn = min(tm, M), min(tn, N)
    assert wcol0 % tn == 0
    wblk0 = wcol0 // tn
    if mode.startswith("heads"):
        tm = min(tm, seq)
    tk = K if tk is None else min(tk, K)
    nk = K // tk
    assert M % tm == 0 and N % tn == 0 and K % tk == 0
    grid = (M // tm, N // tn, nk)
    if w_t:
        w_spec = pl.BlockSpec((tn, tk), lambda i, j, k: (wblk0 + j, k))
    else:
        w_spec = pl.BlockSpec((tk, tn), lambda i, j, k: (k, wblk0 + j))
    in_specs = [pl.BlockSpec((tm, tk), lambda i, j, k: (i, k)), w_spec]
    if mode == "residual":
        in_specs.append(pl.BlockSpec((tm, tn), lambda i, j, k: (i, j)))
    elif mode == "heads_rope":
        in_specs += [pl.BlockSpec((tm, LANES), lambda i, j, k: (i, 0))] * 3
    if mode == "heads_t":
        assert seq % tm == 0 and tn % LANES == 0 and tm % LANES == 0
        spb = seq // tm
        out_shape = jax.ShapeDtypeStruct((M // seq, N // LANES, seq // LANES, LANES, LANES), out_dtype)
        out_spec = pl.BlockSpec((1, tn // LANES, tm // LANES, LANES, LANES),
                                lambda i, j, k: (i // spb, j, i % spb, 0, 0))
    elif mode.startswith("heads"):
        assert seq % tm == 0 and tn % LANES == 0
        spb = seq // tm
        out_shape = jax.ShapeDtypeStruct((M // seq, N // LANES, seq, LANES), out_dtype)
        out_spec = pl.BlockSpec((1, tn // LANES, tm, LANES), lambda i, j, k: (i // spb, j, i % spb, 0))
    else:
        out_shape = jax.ShapeDtypeStruct((M, N), out_dtype)
        out_spec = pl.BlockSpec((tm, tn), lambda i, j, k: (i, j))
    scratch = [pltpu.VMEM((tm, tn), F32)] if nk > 1 else []
    return pl.pallas_call(
        functools.partial(_mm_kernel, nk=nk, mode=mode, n_extra=len(extras), w_t=w_t),
        grid=grid,
        in_specs=in_specs,
        out_specs=out_spec,
        out_shape=out_shape,
        scratch_shapes=scratch,
        compiler_params=_cparams(("parallel", "parallel", "arbitrary")),
        name=name,
    )(x, w, *extras)


SUBLANES = 8


def _repack_kernel(tab_ref, w_ref, o_ref):
    o_ref[...] = w_ref[0].astype(o_ref.dtype)


def repack_rows(wt3, l, pieces, tr=512):
    _, NC, D = wt3.shape
    assert all(r0 % SUBLANES == 0 and nrow % tr == 0 for r0, nrow in pieces)
    starts = [r0 + r for r0, nrow in pieces for r in range(0, nrow, tr)]
    tab = jnp.asarray([s // SUBLANES for s in starts], jnp.int32)
    grid_spec = pltpu.PrefetchScalarGridSpec(
        num_scalar_prefetch=1, grid=(len(starts),),
        in_specs=[pl.BlockSpec((pl.Element(1), pl.Element(tr), pl.Element(D)),
                               lambda j, t: (l, t[j] * SUBLANES, 0))],
        out_specs=pl.BlockSpec((tr, D), lambda j, t: (j, 0)))
    return pl.pallas_call(
        _repack_kernel, grid_spec=grid_spec, out_shape=jax.ShapeDtypeStruct((len(starts) * tr, D), BF16),
        compiler_params=_cparams(("arbitrary",)), name="repack_w_in",
    )(tab, wt3)


def _compress_kernel(x_ref, w1_ref, w2_ref, pe_ref, c_ref, sa_ref, sb_ref, o_ref):
    half = x_ref.shape[-1]
    x = x_ref[0, 0]
    a = _dot(x, w1_ref[0, :half, :])
    b = _dot(x, w1_ref[0, half:, :])
    pet = _dot(pe_ref[0], w1_ref[0])
    n = a.shape[0]
    hid = a + pltpu.roll(b, n - 1, 0) + pet[0:1]
    out = _dot(_gelu(hid).astype(BF16), w2_ref[0])
    out = _rope_apply(out, c_ref[0, 0], sa_ref[0, 0], sb_ref[0, 0])
    o_ref[0, 0, 0] = out.astype(o_ref.dtype)


def nsa_compress(kvp, w1, w2, pe8, tabs):
    B, _, S, dh = kvp.shape
    G = NSA_GROUPS
    nch = S // CMP_STRIDE
    xv = kvp.reshape(B, kvp.shape[1], nch, CMP_STRIDE * dh)
    hidden = w1.shape[-1]
    tab_spec = pl.BlockSpec((1, 1, nch, LANES), lambda w, b, g: (w, b, 0, 0))
    return pl.pallas_call(
        _compress_kernel,
        grid=(2, B, G),
        in_specs=[
            pl.BlockSpec((1, 1, nch, CMP_STRIDE * dh), lambda w, b, g: (b, w * G + g, 0, 0)),
            pl.BlockSpec((1, CMP_LEN * dh, hidden), lambda w, b, g: (w, 0, 0)),
            pl.BlockSpec((1, hidden, dh), lambda w, b, g: (w, 0, 0)),
            pl.BlockSpec((1, 8, CMP_LEN * dh), lambda w, b, g: (w, 0, 0)),
            tab_spec, tab_spec, tab_spec,
        ],
        out_specs=pl.BlockSpec((1, 1, 1, nch, dh), lambda w, b, g: (w, b, g, 0, 0)),
        out_shape=jax.ShapeDtypeStruct((2, B, G, nch, dh), BF16),
        compiler_params=_cparams(("parallel", "parallel", "parallel")),
        name="nsa_compress",
    )(xv, w1, w2, pe8, *tabs)


def _dot_tn(a, b):
    return lax.dot_general(a, b, (((0,), (0,)), ((), ())), preferred_element_type=F32)


NSA_SPLIT = 2


def _nsa_kernel(q_ref, ks_ref, kw_ref, vs_ref, vw_ref, kc_ref, vc_ref, gt_ref, cmap_ref, e_ref, o_ref, *, ck):
    n = NSA_HPG
    ns = NSA_SPLIT
    hps = n // ns
    sq = hps * Q_BLOCK
    streams = range(ns)
    qb = pl.program_id(2)
    t0 = qb * Q_BLOCK
    c2 = (HEAD_DIM ** -0.5) * 1.4426950408889634
    nsel = ks_ref.shape[2] // SEL_BLOCK
    q_t = [jnp.concatenate([q_ref[0, s * hps + h].T for h in range(hps)], axis=1) for s in streams]
    tq = t0 + lax.broadcasted_iota(jnp.int32, (1, Q_BLOCK), 1)

    def heads(x):
        return jnp.concatenate([x] * hps, axis=1)

    def scores(k, ok_t):
        bias = heads(jnp.where(ok_t, 0.0, NEG_INF))
        return [_dot(k, q_t[s]) + bias for s in streams]

    def values(vt_ref, tile0, ntile):
        vt = jnp.concatenate([vt_ref[0, 0, tile0 + i] for i in range(ntile)], axis=1)
        return jnp.concatenate([vt, jnp.ones(vt.shape, vt.dtype)], axis=0)

    kc = kc_ref[0, 0, 0]
    vc = vc_ref[0, 0, 0]
    ncmp = kc.shape[0]
    crow = lax.broadcasted_iota(jnp.int32, (ncmp, Q_BLOCK), 0)
    sb = scores(kc, (crow < ncmp - 1) & (crow * CMP_STRIDE + (CMP_LEN - 1) <= tq))
    p = [jnp.exp2((sb[s] - jnp.max(sb[s], axis=0, keepdims=True)) * c2) for s in streams]
    anyv = heads((tq >= CMP_LEN - 1).astype(F32))
    p = [p[s] * (anyv / jnp.sum(p[s], axis=0, keepdims=True)) for s in streams]
    o_c = [_dot_tn(vc, p[s].astype(BF16)) for s in streams]

    psum = None
    for s in streams:
        for h in range(hps):
            ph = p[s][:, h * Q_BLOCK:(h + 1) * Q_BLOCK]
            psum = ph if psum is None else psum + ph
    hi = psum.astype(BF16)
    lo = (psum - hi.astype(F32)).astype(BF16)
    imp = _dot(cmap_ref[...], hi) + _dot(cmap_ref[...], lo)
    jr = lax.broadcasted_iota(jnp.int32, (LANES, Q_BLOCK), 0)
    cur = tq // SEL_BLOCK
    imp = jnp.where(jr == 0, 1e6, imp)
    imp = jnp.where(jr == cur, 1e6, imp)
    imp = jnp.where(jr == cur - 1, 1e6, imp)
    imp = jnp.where(jr > cur, -1e6, imp)

    nrow = min(((nsel + 7) // 8) * 8, LANES)
    imp = jnp.where(jr >= nsel, -3e6, imp)[0:nrow]
    jrr = jr[0:nrow]
    rank = jnp.zeros((nrow, Q_BLOCK), F32)
    for i in range(min(nsel, LANES)):
        row = imp[i:i + 1, :]
        rank = rank + jnp.where(row > imp, 1.0, jnp.where(row == imp, jnp.where(jrr > i, 1.0, 0.0), 0.0))
    sel = jnp.where(rank < float(SEL_TOP), 1.0, 0.0)
    if nrow < LANES:
        sel = jnp.concatenate([sel, jnp.zeros((LANES - nrow, Q_BLOCK), F32)], axis=0)
    sel = sel.astype(BF16)

    n_ch = (t0 + Q_BLOCK + ck - 1) // ck
    krow = lax.broadcasted_iota(jnp.int32, (ck, Q_BLOCK), 0)

    def body(kb, carry):
        m_i, acc = carry[:ns], carry[ns:]
        k0 = pl.multiple_of(kb * ck, ck)
        kblk = ks_ref[0, 0, pl.ds(k0, ck), :]
        v1 = values(vs_ref, kb * (ck // LANES), ck // LANES)
        em = _dot(e_ref[kb], sel)
        sb = scores(kblk, (k0 + krow <= tq) & (em > 0.5))
        m_new = [jnp.maximum(m_i[s], jnp.max(sb[s], axis=0, keepdims=True)) for s in streams]
        alpha = [jnp.exp2((m_i[s] - m_new[s]) * c2) for s in streams]
        pp = [jnp.exp2(((sb[s] - m_new[s]) * c2).astype(BF16)) for s in streams]
        acc = [alpha[s] * acc[s] + _dot(v1, pp[s]) for s in streams]
        return tuple(m_new) + tuple(acc)

    init = tuple(jnp.full((1, sq), NEG_INF, F32) for _ in streams) + tuple(
        jnp.zeros((2 * HEAD_DIM, sq), F32) for _ in streams)
    acc_f = lax.fori_loop(0, n_ch, body, init)[ns:]
    o_s = [a[0:HEAD_DIM] / a[HEAD_DIM:HEAD_DIM + 1] for a in acc_f]

    wlen = min(WINDOW + Q_BLOCK, kw_ref.shape[2])
    w0 = pl.multiple_of(jnp.maximum(t0 + Q_BLOCK - wlen, 0), Q_BLOCK)
    kblk = kw_ref[0, 0, pl.ds(w0, wlen), :]
    v1 = values(vw_ref, w0 // LANES, wlen // LANES)
    kpos = w0 + lax.broadcasted_iota(jnp.int32, (wlen, Q_BLOCK), 0)
    sb = scores(kblk, (kpos <= tq) & (tq - kpos < WINDOW))
    pw = [jnp.exp2(((sb[s] - jnp.max(sb[s], axis=0, keepdims=True)) * c2).astype(BF16)) for s in streams]
    acc_w = [_dot(v1, pw[s]) for s in streams]
    o_w = [a[0:HEAD_DIM] / a[HEAD_DIM:HEAD_DIM + 1] for a in acc_w]

    gt = gt_ref[0, 0].T
    for h in range(n):
        s = h // hps
        r = slice((h % hps) * Q_BLOCK, (h % hps + 1) * Q_BLOCK)
        o = (gt[3 * h:3 * h + 1] * o_c[s][:, r] + gt[3 * h + 1:3 * h + 2] * o_s[s][:, r]
             + gt[3 * h + 2:3 * h + 3] * o_w[s][:, r])
        o_ref[0, :, h * HEAD_DIM:(h + 1) * HEAD_DIM] = o.T.astype(o_ref.dtype)


def nsa_attention(qkr, vt, cmp, gates, ck=1024):
    B, _, S, dh = qkr.shape
    G, n = NSA_GROUPS, NSA_HPG
    ncmp = cmp.shape[3]
    nsel = S // SEL_BLOCK
    ck = min(ck, S)
    r_sel, r_cmp = SEL_BLOCK // CMP_STRIDE, CMP_LEN // CMP_STRIDE
    tgt = (r_sel * jnp.arange(nsel)[:, None, None] + jnp.arange(r_sel)[None, :, None]
           - jnp.arange(r_cmp)[None, None, :])
    cmap = (jnp.arange(ncmp - 1)[:, None, None, None] == tgt[None]).sum((2, 3)).astype(F32)
    cmap = jnp.pad(cmap, ((0, 1), (0, LANES - nsel))).astype(BF16).T
    e = (jnp.arange(S)[:, None] // SEL_BLOCK == jnp.arange(LANES)[None, :]).astype(BF16)
    e3 = e.reshape(S // ck, ck, LANES)

    def head(h0):
        return pl.BlockSpec((1, 1, S, dh), lambda b, g, i: (b, h0 + g, 0, 0))

    def head_t(h0):
        return pl.BlockSpec((1, 1, S // LANES, dh, LANES), lambda b, g, i: (b, h0 + g, 0, 0, 0))

    return pl.pallas_call(
        functools.partial(_nsa_kernel, ck=ck),
        grid=(B, G, S // Q_BLOCK),
        in_specs=[
            pl.BlockSpec((1, n, Q_BLOCK, dh), lambda b, g, i: (b, g, i, 0)),
            head(16), head(20), head_t(0), head_t(G),
            pl.BlockSpec((1, 1, 1, ncmp, dh), lambda b, g, i: (0, b, g, 0, 0)),
            pl.BlockSpec((1, 1, 1, ncmp, dh), lambda b, g, i: (1, b, g, 0, 0)),
            pl.BlockSpec((1, 1, Q_BLOCK, LANES), lambda b, g, i: (b, g, i, 0)),
            pl.BlockSpec((LANES, ncmp), lambda b, g, i: (0, 0)),
            pl.BlockSpec((S // ck, ck, LANES), lambda b, g, i: (0, 0, 0)),
        ],
        out_specs=pl.BlockSpec((1, Q_BLOCK, n * dh), lambda b, g, i: (b, i, g)),
        out_shape=jax.ShapeDtypeStruct((B, S, G * n * dh), BF16),
        compiler_params=_cparams(("parallel", "parallel", "arbitrary")),
        name="nsa_attention",
    )(qkr, qkr, qkr, vt, vt, cmp, cmp, gates, cmap, e3)


def _rwkv_feat_kernel(z_ref, zp_ref, mu_ref, o_ref):
    i = pl.program_id(1)
    z = z_ref[0]
    prev = zp_ref[0, 7:8, :] * (i > 0).astype(F32)
    row = lax.broadcasted_iota(jnp.int32, z.shape, 0)
    sh = jnp.where(row == 0, prev, pltpu.roll(z, 1, 0))
    zs = z + (sh - z) * mu_ref[...]
    o_ref[0, :, 0:128] = jnp.tanh(zs[:, 0:128])
    o_ref[0, :, 128:256] = zs[:, 128:256]
    o_ref[0, :, 256:] = jax.nn.sigmoid(zs[:, 256:])


def rwkv_features(z_lora, mu_lora, tm=512):
    B, S, W = z_lora.shape
    tm = min(tm, S)
    return pl.pallas_call(
        _rwkv_feat_kernel,
        grid=(B, S // tm),
        in_specs=[pl.BlockSpec((1, tm, W), lambda b, i: (b, i, 0)),
                  pl.BlockSpec((1, 8, W), lambda b, i: (b, jnp.maximum(i * (tm // 8) - 1, 0), 0)),
                  pl.BlockSpec((1, W), lambda b, i: (0, 0))],
        out_specs=pl.BlockSpec((1, tm, W), lambda b, i: (b, i, 0)),
        out_shape=jax.ShapeDtypeStruct((B, S, W), F32),
        compiler_params=_cparams(("parallel", "arbitrary")),
        name="rwkv_features",
    )(z_lora, z_lora, mu_lora)


def _rwkv_kernel(r_ref, k_ref, v_ref, f_ref, par_ref, w2_ref, a2_ref, g2_ref, o_ref, st_ref, prev_ref):
    C = RWKV_CHUNK
    W = r_ref.shape[-1]
    npair = W // LANES
    c = pl.program_id(2)

    @pl.when(c == 0)
    def _():
        st_ref[...] = jnp.zeros_like(st_ref)
        prev_ref[...] = jnp.zeros_like(prev_ref)

    par = par_ref[...]
    mu_r, mu_k, mu_v = par[0:1], par[1:2], par[2:3]
    w0, a0, k_k, k_a, r_k, ln_w, ln_b = par[3:4], par[4:5], par[5:6], par[6:7], par[7:8], par[8:9], par[9:10]

    row = lax.broadcasted_iota(jnp.int32, (C, C), 0)
    col = lax.broadcasted_iota(jnp.int32, (C, C), 1)
    trow = lax.broadcasted_iota(jnp.int32, (C, W), 0)
    lane = lax.broadcasted_iota(jnp.int32, (C, LANES), 1)
    lr = lax.broadcasted_iota(jnp.int32, (LANES, LANES), 0)
    lc = lax.broadcasted_iota(jnp.int32, (LANES, LANES), 1)

    def shifted(z_ref, idx, mu):
        z = z_ref[0]
        sh = jnp.where(trow == 0, prev_ref[idx:idx + 1, :], pltpu.roll(z, 1, 0))
        prev_ref[idx:idx + 1, :] = z[C - 1:C, :]
        return z + (sh - z) * mu

    r = shifted(r_ref, 0, mu_r)
    k = shifted(k_ref, 1, mu_k)
    v = shifted(v_ref, 2, mu_v)
    f = f_ref[0]
    w_log = -_softplus(-(w0 + _dot3(_split(f[:, 0:128]), _split(w2_ref[...])))) - 0.5
    logd = -jnp.exp(w_log)
    a = jax.nn.sigmoid(a0 + _dot3(_split(f[:, 128:256]), _split(a2_ref[...])))
    g = _dot3(_split(f[:, 256:]), _split(g2_ref[...]))

    bd = _exact(jnp.where((lr // RWKV_HEAD) == (lc // RWKV_HEAD), 1.0, 0.0))

    def head_sum(x, passes=2):
        sp = _split if passes == 2 else (lambda z: (z.astype(BF16), None))
        return jnp.concatenate([_dot3(sp(x[:, p * LANES:(p + 1) * LANES]), bd) for p in range(npair)], axis=1)

    kk = k * k_k
    kk = kk / jnp.maximum(jnp.sqrt(head_sum(kk * kk)), 1e-12)
    k2 = k * (1.0 + (a - 1.0) * k_a)

    tri_incl = _exact(jnp.where(col <= row, 1.0, 0.0))
    cum = _dot3(tri_incl, _split(logd))
    cum_end = cum[C - 1:C, :]
    e_neg = jnp.exp(-cum)
    kka = kk * a
    a_t = -kk * jnp.exp(cum - logd)
    b_t = kka * e_neg
    k_t = k2 * e_neg
    r_t = r * jnp.exp(cum)
    e_end = jnp.exp(cum_end - cum)
    b_e = kka * e_end
    k_e = k2 * e_end
    p_end = jnp.exp(cum_end)

    strict = col < row
    incl = col <= row
    eye = jnp.where(col == row, 1.0, 0.0)
    n_sq = C.bit_length() - 2
    hpp = LANES // RWKV_HEAD
    heads = [(p, h) for p in range(npair) for h in range(hpp)]
    hd = []
    for p, h in heads:
        sl = slice(p * LANES, (p + 1) * LANES)
        mh = (lane // RWKV_HEAD) == h
        s_am = _split(jnp.where(mh, a_t[:, sl], 0.0))
        s_rm = _split(jnp.where(mh, r_t[:, sl], 0.0))
        hd.append(dict(
            sl=sl, s_am=s_am, s_rm=s_rm,
            s_vm=_split(jnp.where(mh, v[:, sl], 0.0)),
            s_bem=_split(jnp.where(mh, b_e[:, sl], 0.0)),
            s_kem=_split(jnp.where(mh, k_e[:, sl], 0.0)),
            rhs=_split(jnp.concatenate([b_t[:, sl], k_t[:, sl]], axis=0)) if h == 0 else hd[-1]["rhs"]))
    for d in hd:
        amat = _dot3(d["s_am"], d["rhs"], "nt")
        rmat = _dot_nt(d["s_rm"][0], d["rhs"][0])
        l_ab = jnp.where(strict, amat[:, 0:C], 0.0)
        d["l_ak"] = _split(jnp.where(strict, amat[:, C:2 * C], 0.0))
        d["l_rb"] = jnp.where(incl, rmat[:, 0:C], 0.0).astype(BF16)
        d["l_rk"] = jnp.where(incl, rmat[:, C:2 * C], 0.0).astype(BF16)
        d["x"] = eye + l_ab
        d["s_pw"] = _split(l_ab)
    for i in range(n_sq):
        for d in hd:
            d["s_pw"] = _split(_dot3(d["s_pw"], d["s_pw"])) if i == 0 else (_dot(d["s_pw"][0], d["s_pw"][0]).astype(BF16), None)
        for d in hd:
            d["x"] = d["x"] + (_dot3(_split(d["x"]), d["s_pw"]) if i == 0 else _dot(d["x"].astype(BF16), d["s_pw"][0]))
    cat = lambda parts, axis: tuple(None if parts[0][i] is None else jnp.concatenate([p[i] for p in parts], axis=axis)
                                    for i in range(2))
    for i, d in enumerate(hd):
        d["st"] = st_ref[i]
        d["s_st"] = _split(d["st"])
        d["t"] = _split(_dot3(cat([d["s_am"], d["l_ak"]], 1), cat([d["s_st"], d["s_vm"]], 0)))
    for d in hd:
        d["s_u"] = _split(_dot3(_split(d["x"]), d["t"]))
    for i, d in enumerate(hd):
        if i % hpp == 0:
            decay = jnp.broadcast_to(p_end[:, d["sl"]], (LANES, LANES)).T
        st_ref[i] = d["st"] * decay + _dot3(cat([d["s_bem"], d["s_kem"]], 0), cat([d["s_u"], d["s_vm"]], 0), "tn")
    ys = []
    for d in hd:
        ys.append(_dot(jnp.concatenate([d["s_rm"][0], d["l_rb"], d["l_rk"]], axis=1),
                       jnp.concatenate([d["s_st"][0], d["s_u"][0], d["s_vm"][0]], axis=0)))
    y = jnp.concatenate([sum(ys[hpp * p + 1:hpp * (p + 1)], ys[hpp * p]) for p in range(npair)], axis=1)

    inv_n = 1.0 / RWKV_HEAD
    mean = head_sum(y) * inv_n
    d = y - mean
    var = head_sum(d * d) * inv_n
    yn = d * lax.rsqrt(var + GN_EPS) * ln_w + ln_b
    bonus = head_sum(r * k2 * r_k, passes=1) * v
    o_ref[0] = ((yn + bonus) * g).astype(o_ref.dtype)


def rwkv_scan(z_rkv, feat, par, w2p, a2p, g2, pairs_per_step=8):
    B, S, R3 = z_rkv.shape
    R = R3 // 3
    W = pairs_per_step * LANES
    nblk = R // W
    C = RWKV_CHUNK
    assert S % C == 0 and R % W == 0

    def col(off):
        return pl.BlockSpec((1, C, W), lambda b, p, c: (b, c, off + p))

    return pl.pallas_call(
        _rwkv_kernel,
        grid=(B, nblk, S // C),
        in_specs=[
            col(0), col(nblk), col(2 * nblk),
            pl.BlockSpec((1, C, feat.shape[-1]), lambda b, p, c: (b, c, 0)),
            pl.BlockSpec((16, W), lambda b, p, c: (0, p)),
            pl.BlockSpec((LANES, W), lambda b, p, c: (0, p)),
            pl.BlockSpec((LANES, W), lambda b, p, c: (0, p)),
            pl.BlockSpec((GATE_LORA, W), lambda b, p, c: (0, p)),
        ],
        out_specs=pl.BlockSpec((1, C, W), lambda b, p, c: (b, c, p)),
        out_shape=jax.ShapeDtypeStruct((B, S, R), BF16),
        scratch_shapes=[pltpu.VMEM((W // RWKV_HEAD, LANES, LANES), F32), pltpu.VMEM((8, W), F32)],
        compiler_params=_cparams(("parallel", "parallel", "arbitrary")),
        name="rwkv_scan",
    )(z_rkv, z_rkv, z_rkv, feat, par, w2p, a2p, g2)


def _merge_kernel(h_ref, wa_ref, wb_ref, on_ref, or_ref, wn_ref, wr_ref, o_ref):
    h = h_ref[...]
    ga = jax.nn.sigmoid(_dot_nt(h, wa_ref[...]))
    gb = jax.nn.sigmoid(_dot_nt(h, wb_ref[...]))
    pa = _dot(on_ref[...], wn_ref[...].astype(on_ref.dtype))
    pb = _dot(or_ref[...], wr_ref[...].astype(or_ref.dtype))
    o_ref[...] = (ga * pa + gb * pb).astype(o_ref.dtype)


def merge_mixers(h, wp, gate_rows, o_nsa, o_rwkv, wn, wr, tm=1024, tn=256):
    M, Kn = o_nsa.shape
    Kr = o_rwkv.shape[1]
    D = wn.shape[1]
    tm, tn = min(tm, M), min(tn, D)
    assert gate_rows[0] % tn == 0 and gate_rows[1] == 2 * D
    ba, bb = gate_rows[0] // tn, (gate_rows[0] + D) // tn
    return pl.pallas_call(
        _merge_kernel,
        grid=(M // tm, D // tn),
        in_specs=[
            pl.BlockSpec((tm, D), lambda i, j: (i, 0)),
            pl.BlockSpec((tn, D), lambda i, j: (ba + j, 0)),
            pl.BlockSpec((tn, D), lambda i, j: (bb + j, 0)),
            pl.BlockSpec((tm, Kn), lambda i, j: (i, 0)),
            pl.BlockSpec((tm, Kr), lambda i, j: (i, 0)),
            pl.BlockSpec((Kn, tn), lambda i, j: (0, j)),
            pl.BlockSpec((Kr, tn), lambda i, j: (0, j)),
        ],
        out_specs=pl.BlockSpec((tm, tn), lambda i, j: (i, j)),
        out_shape=jax.ShapeDtypeStruct((M, D), BF16),
        compiler_params=_cparams(("parallel", "parallel")),
        name="merge_mixers",
    )(h, wp, wp, o_nsa, o_rwkv, wn, wr)


def _xattn_kernel(q_ref, kv_ref, wo_ref, x_ref, g_ref, o_ref, h_ref, ht_ref):
    nh = XATTN_HEADS
    scale = HEAD_DIM ** -0.5
    outs = []
    for h in range(nh):
        q = q_ref[:, h * HEAD_DIM:(h + 1) * HEAD_DIM]
        k = kv_ref[0, :, h * HEAD_DIM:(h + 1) * HEAD_DIM]
        v = kv_ref[0, :, (nh + h) * HEAD_DIM:(nh + h + 1) * HEAD_DIM]
        s = _dot_nt(q, k) * scale
        m = jnp.max(s, axis=-1, keepdims=True)
        p = jnp.exp(s - m)
        p = p / jnp.sum(p, axis=-1, keepdims=True)
        outs.append(_dot(p.astype(BF16), v).astype(BF16))
    o = jnp.concatenate(outs, axis=-1)
    x = x_ref[...] + _dot(o, wo_ref[...])
    o_ref[...] = x
    y = x * lax.rsqrt(jnp.mean(x * x, axis=-1, keepdims=True) + RMS_EPS) * g_ref[...]
    h_ref[...] = y.astype(h_ref.dtype)
    ht_ref[...] = y.T.astype(ht_ref.dtype)


def cross_attention(q, kv, wo, x, g_next, seq, tm=256):
    T, D = x.shape
    tm = min(tm, seq)
    spb = seq // tm
    Mm = kv.shape[1]
    row = pl.BlockSpec((tm, D), lambda i: (i, 0))
    return pl.pallas_call(
        _xattn_kernel,
        grid=(T // tm,),
        in_specs=[
            pl.BlockSpec((tm, q.shape[1]), lambda i: (i, 0)),
            pl.BlockSpec((1, Mm, kv.shape[2]), lambda i: (i // spb, 0, 0)),
            pl.BlockSpec(wo.shape, lambda i: (0, 0)),
            row,
            pl.BlockSpec((1, D), lambda i: (0, 0)),
        ],
        out_specs=[row, row, pl.BlockSpec((D, tm), lambda i: (0, i))],
        out_shape=[jax.ShapeDtypeStruct((T, D), F32), jax.ShapeDtypeStruct((T, D), BF16),
                   jax.ShapeDtypeStruct((D, T), BF16)],
        compiler_params=_cparams(("parallel",)),
        name="cross_attention",
    )(q, kv, wo, x, g_next.reshape(1, D).astype(F32))


def _top16(x, iota=None, n=None):
    return _top16_many([x], iota, n)[0]


def _top16_many(xs, iota=None, n=None):
    shape = xs[0].shape
    if iota is None:
        n = shape[0]
        iota = lax.broadcasted_iota(jnp.int32, shape, 0).astype(F32)
    cur = list(xs)
    rank = [jnp.full(shape, float(PEER_TOPK), F32) for _ in xs]
    vals = [[] for _ in xs]
    for r in range(PEER_TOPK):
        m = [jnp.max(c, axis=0, keepdims=True) for c in cur]
        idx = [jnp.min(jnp.where(c == mi, iota, float(n)), axis=0, keepdims=True) for c, mi in zip(cur, m)]
        hit = [iota == i for i in idx]
        rank = [jnp.where(h, float(r), rk) for h, rk in zip(hit, rank)]
        cur = [jnp.where(h, -jnp.inf, c) for h, c in zip(hit, cur)]
        for v, mi in zip(vals, m):
            v.append(mi)
    return [(jnp.concatenate(v, axis=0), rk) for v, rk in zip(vals, rank)]


def _peer_route_kernel(q_ref, keys_ref, row_ref, pl_ref):
    K = PEER_TOPK
    q = q_ref[...]
    half = q.shape[1] // 2
    s1 = _dotf_nt(keys_ref[0, 0], q[:, :half])
    s2 = _dotf_nt(keys_ref[0, 1], q[:, half:])
    ht = q.shape[0] // 2
    halves = lambda parts: tuple(jnp.concatenate([a, b], axis=1) for a, b in zip(*parts))
    t16 = _top16_many([s1[:, :ht], s1[:, ht:], s2[:, :ht], s2[:, ht:]])
    (v1, rank1), (v2, rank2) = halves(t16[0:2]), halves(t16[2:4])
    tmq = q.shape[0]
    r8 = lax.broadcasted_iota(jnp.int32, (8, tmq), 0)
    ninf = -jnp.inf
    lo8 = v2[0:8]
    tiles = [
        (v1[0:1] + lo8, r8),
        (v1[0:1] + v2[8:16], 8 + r8),
        (v1[1:2] + lo8, 16 + r8),
        (jnp.where(r8 < 5, v1[2:3] + lo8, ninf), 32 + r8),
        (jnp.where(r8 < 7, jnp.where(r8 < 4, v1[3:4], v1[4:5]) + jnp.where(r8 < 4, lo8, pltpu.roll(lo8, 4, 0)), ninf),
         jnp.where(r8 < 4, 48 + r8, 60 + r8)),
        (jnp.where(r8 < 6, jnp.where(r8 < 2, v1[5:6], jnp.where(r8 < 4, v1[6:7], v1[7:8]))
                   + jnp.where(r8 % 2 == 0, v2[0:1], v2[1:2]), ninf),
         80 + (r8 // 2) * 16 + r8 % 2),
        (v1[8:16] + v2[0:1], (8 + r8) * 16),
    ]
    cand = jnp.concatenate([t for t, _ in tiles], axis=0)
    cidx = jnp.concatenate([i for _, i in tiles], axis=0).astype(F32)
    top_s, crank = halves(_top16_many([cand[:, :ht], cand[:, ht:]], cidx[:, :ht], K * K))
    chosen = jnp.where(crank < float(K), 1.0, 0.0)
    z = jnp.sum(jnp.exp(top_s - top_s[0:1]), axis=0, keepdims=True)
    ch = [chosen[8 * t:8 * (t + 1)] for t in range(len(tiles))]
    colsum = lambda x: jnp.sum(x, axis=0, keepdims=True)
    upper = (r8 < 4).astype(F32)
    j_rows = [colsum(ch[0]) + colsum(ch[1]), colsum(ch[2]), colsum(ch[3]),
              colsum(ch[4] * upper), colsum(ch[4] * (1.0 - upper))]
    j_rows += [ch[5][2 * m:2 * m + 1] + ch[5][2 * m + 1:2 * m + 2] for m in range(3)]
    j_rows += [ch[6][m:m + 1] for m in range(8)]
    jn = jnp.zeros_like(s1)
    for i in range(K):
        jn = jnp.where(rank1 == float(i), j_rows[i], jn)
    row_ref[0, 0] = jnp.exp(s1 - v1[0:1]) / z
    row_ref[0, 1] = jn
    pl_ref[0, 0] = jnp.exp(s2 - v2[0:1]).astype(pl_ref.dtype)
    pl_ref[0, 1] = rank2.astype(pl_ref.dtype)


def peer_route(q, keys, tm=512):
    T = q.shape[0]
    H, _, nkeys, hd = keys.shape
    tm = min(tm, T)
    spec = pl.BlockSpec((1, 2, nkeys, tm), lambda i, h: (h, 0, 0, i))
    return pl.pallas_call(
        _peer_route_kernel,
        grid=(T // tm, H),
        in_specs=[pl.BlockSpec((tm, 2 * hd), lambda i, h: (i, h)),
                  pl.BlockSpec((1, 2, nkeys, hd), lambda i, h: (h, 0, 0, 0))],
        out_specs=[spec, spec],
        out_shape=[jax.ShapeDtypeStruct((H, 2, nkeys, T), F32), jax.ShapeDtypeStruct((H, 2, nkeys, T), BF16)],
        compiler_params=_cparams(("parallel", "parallel")),
        name="peer_route",
    )(q, keys)


def _peer_act_kernel(h_ref, u_ref, row_ref, pl_ref, o_ref):
    nk = PEER_KEYS
    ec = u_ref.shape[0]
    j = pl.program_id(1)
    hu = _dot(u_ref[...], h_ref[...])
    zero = jnp.zeros((), o_ref.dtype)
    for al in range(ec // nk):
        a = j * (ec // nk) + al
        w = None
        for hd in range(row_ref.shape[0]):
            e1 = row_ref[hd, 0, pl.ds(a, 1), :].astype(o_ref.dtype)
            jn = row_ref[hd, 1, pl.ds(a, 1), :].astype(o_ref.dtype)
            g = jnp.where(pl_ref[hd, 1] < jn, e1 * pl_ref[hd, 0], zero)
            w = g if w is None else w + g
        act = _gelu(hu[al * nk:(al + 1) * nk, :].astype(o_ref.dtype)) * w
        o_ref[:, al * nk:(al + 1) * nk] = act.T


def peer_activations(h_t, u_tab, rows, planes, tm=512, ec=1024):
    D, T = h_t.shape
    E = u_tab.shape[0]
    tm, ec = min(tm, T), min(ec, E)
    H, _, nkeys, _ = rows.shape
    rspec = pl.BlockSpec((H, 2, nkeys, tm), lambda i, j: (0, 0, 0, i))
    return pl.pallas_call(
        _peer_act_kernel,
        grid=(T // tm, E // ec),
        in_specs=[pl.BlockSpec((D, tm), lambda i, j: (0, i)),
                  pl.BlockSpec((ec, D), lambda i, j: (j, 0)),
                  rspec, rspec],
        out_specs=pl.BlockSpec((tm, ec), lambda i, j: (i, j)),
        out_shape=jax.ShapeDtypeStruct((T, E), BF16),
        compiler_params=_cparams(("parallel", "arbitrary")),
        name="peer_activations",
    )(h_t, u_tab, rows, planes)


def kernel(x, mem, positions, norm_mix, w_in, nsa_cmp_pe, nsa_cmp_w1, nsa_cmp_w2, rwkv_mu, rwkv_w0, rwkv_w2, rwkv_a0, rwkv_a2, rwkv_g2, rwkv_k_k, rwkv_k_a, rwkv_r_k, rwkv_ln_w, rwkv_ln_b, w_proj_nsa, w_proj_rwkv, w_out, norm_xattn, norm_mem, xattn_wq, xattn_wkv, xattn_wo, norm_ffn, peer_wq, peer_keys, peer_u, peer_v, norm_final):
    B, S, D = x.shape
    T = B * S
    depth = w_in.shape[0]
    G, n, dh = NSA_GROUPS, NSA_HPG, HEAD_DIM
    nq_cols = G * n * dh
    nkv_cols = 6 * G * dh
    ngate = 3 * G * n
    R = rwkv_w0.shape[1]
    c_q, c_kv, c_g = nq_cols, nq_cols + nkv_cols, nq_cols + nkv_cols + ngate
    c_r = c_g + 3 * R + DECAY_LORA + AAA_LORA + GATE_LORA
    gd = G * dh

    pos_flat = positions.reshape(T)
    rope_tabs = _rope_tables(pos_flat)
    cend = jnp.arange(S // CMP_STRIDE) * CMP_STRIDE + (CMP_LEN - 1)
    ctab = _rope_tables(positions[:, jnp.minimum(cend, S - 1)])
    ident = _rope_tables(jnp.zeros_like(positions[:, :S // CMP_STRIDE]))
    cmp_tabs = [jnp.stack([a, b]) for a, b in zip(ctab, ident)]

    xf = x.reshape(T, D)
    for l in range(depth):
        w_t = jnp.swapaxes(w_in, 1, 2)
        pieces = [(0, c_q), (c_q + 2 * gd, gd), (c_q + 4 * gd, gd),
                  (c_q, 2 * gd), (c_q + 3 * gd, gd), (c_q + 5 * gd, gd),
                  (c_g, 3 * R), (c_r, 2 * D)]
        wp = repack_rows(w_t, l, pieces)
        g_rope = (0, c_q + 2 * gd)
        g_plain = (g_rope[1], 4 * gd)
        g_rkv = (g_plain[0] + g_plain[1], 3 * R)
        g_merge = (g_rkv[0] + g_rkv[1], 2 * D)
        w_gate = jnp.pad(w_t[l, c_kv:c_g].reshape(G, 3 * n, D), ((0, 0), (0, LANES - 3 * n), (0, 0)))
        w_gate = w_gate.reshape(G * LANES, D)
        wl = w_t[l, c_g + 3 * R:c_r]
        padl = lambda a, n_: jnp.pad(a, ((0, n_ - a.shape[0]), (0, 0)))
        w_lora = jnp.concatenate([padl(wl[:DECAY_LORA], LANES), padl(wl[DECAY_LORA:DECAY_LORA + AAA_LORA], LANES),
                                  wl[DECAY_LORA + AAA_LORA:]], axis=0)

        h = rmsnorm(xf, norm_mix[l], BF16)
        proj = functools.partial(matmul, h, w_t=True)
        qkr = proj(wp, wcols=g_rope, mode="heads_rope", extras=rope_tabs, out_dtype=BF16, seq=S, name="proj_rope")
        kvc = proj(wp, wcols=(g_plain[0], 2 * gd), mode="heads", out_dtype=BF16, seq=S, name="proj_kvc")
        vt = proj(wp, wcols=(g_plain[0] + 2 * gd, 2 * gd), mode="heads_t", out_dtype=BF16, seq=S, name="proj_vt")
        gates_nsa = proj(w_gate, mode="heads_sigmoid", out_dtype=F32, seq=S, name="proj_gate")
        z_rkv = proj(wp, wcols=g_rkv, out_dtype=F32, name="proj_rkv")
        z_lora = proj(w_lora, out_dtype=F32, name="proj_lora")

        pe8 = jnp.broadcast_to(nsa_cmp_pe[l].reshape(2, 1, CMP_LEN * dh), (2, 8, CMP_LEN * dh)).astype(BF16)
        cmp = nsa_compress(kvc, nsa_cmp_w1[l].astype(BF16), nsa_cmp_w2[l].astype(BF16), pe8, cmp_tabs)
        o_nsa = nsa_attention(qkr, vt, cmp, gates_nsa).reshape(T, nq_cols)

        mu = rwkv_mu[l]
        mul = mu[3 * R:]
        pad1 = lambda a, n_: jnp.pad(a, (0, n_ - a.shape[0]))
        mu_lora = jnp.concatenate([pad1(mul[:DECAY_LORA], LANES), pad1(mul[DECAY_LORA:DECAY_LORA + AAA_LORA], LANES),
                                   mul[DECAY_LORA + AAA_LORA:]]).reshape(1, -1)
        feat = rwkv_features(z_lora.reshape(B, S, -1), mu_lora)
        par = jnp.stack([mu[:R], mu[R:2 * R], mu[2 * R:3 * R], rwkv_w0[l], rwkv_a0[l], rwkv_k_k[l], rwkv_k_a[l],
                         rwkv_r_k[l].reshape(R), rwkv_ln_w[l], rwkv_ln_b[l]])
        par = jnp.pad(par, ((0, 16 - par.shape[0]), (0, 0)))
        w2p = jnp.pad(rwkv_w2[l], ((0, LANES - DECAY_LORA), (0, 0)))
        a2p = jnp.pad(rwkv_a2[l], ((0, LANES - AAA_LORA), (0, 0)))
        o_rwkv = rwkv_scan(z_rkv.reshape(B, S, 3 * R), feat, par, w2p, a2p, rwkv_g2[l]).reshape(T, R)

        merged = merge_mixers(h, wp, g_merge, o_nsa, o_rwkv, w_proj_nsa[l], w_proj_rwkv[l])
        xf = matmul(merged, w_out[l], mode="residual", extras=(xf,), out_dtype=F32, name="out_proj")

        q = rmsnorm_proj(xf, norm_xattn[l], xattn_wq[l].astype(BF16), BF16)
        kvm = rmsnorm_proj(mem.reshape(-1, D), norm_mem[l], xattn_wkv[l].astype(BF16), BF16)
        kvm = kvm.reshape(B, mem.shape[1], -1)
        xf, h, h_t = cross_attention(q, kvm, xattn_wo[l].astype(BF16), xf, norm_ffn[l], S)

        pq = matmul(h, peer_wq[l], out_dtype=F32, name="peer_q")
        rows, planes = peer_route(pq, peer_keys[l])
        act = peer_activations(h_t, peer_u[l].astype(BF16), rows, planes)
        xf = matmul(act, peer_v[l], mode="residual", extras=(xf,), out_dtype=F32,
                    tm=1024, tn=1024, tk=2048, name="peer_out")

    return rmsnorm(xf, norm_final, F32).reshape(B, S, D)
```
